```python
import math
import jax, jax.numpy as jnp
from jax import lax
import numpy as np

D_MODEL = 2048
BATCH = 4
SEQ = 2048
DEPTH = 2

CTX_LEN = 256
GRID_W = 64
N_EVEN = (DEPTH + 1) // 2
N_ODD = DEPTH // 2
F32 = jnp.float32
EPS = 1e-6

A_HEAD_DIM = 128
A_WIDTH = D_MODEL // 2
A_HEADS = A_WIDTH // A_HEAD_DIM
A_CHUNK = 64
B_WIDTH = D_MODEL - A_WIDTH
CONV_W = 3
AB_SPLITS = [A_WIDTH, 2 * A_WIDTH, 3 * A_WIDTH, 4 * A_WIDTH, 5 * A_WIDTH,
             5 * A_WIDTH + B_WIDTH, 5 * A_WIDTH + 2 * B_WIDTH]
AB_IN_WIDTH = 5 * A_WIDTH + 3 * B_WIDTH
HY_ORDER = 2
HY_EMB = 33
HY_BANDS = (HY_EMB - 1) // 2
HY_FILTER_HIDDEN = 64
HY_DECAY_TARGET = 1e-2
HY_FAST_PCT = 0.3
HY_SLOW_PCT = 1.5
D_FF = 4 * D_MODEL

kernel_name = "hybrid_hgrn2_shortconv_hyena_dit"


def rmsnorm(x, g):
    xf = x.astype(F32)
    xf = xf * lax.rsqrt(jnp.mean(xf * xf, axis=-1, keepdims=True) + EPS)
    return xf.astype(x.dtype) * g


def conv3_seq(u, w):
    up = jnp.pad(u, ((0, 0), (1, 1), (0, 0)))
    return w[0] * up[:, :-2] + w[1] * up[:, 1:-1] + w[2] * up[:, 2:]


def conv3_grid(u, w):
    b, l, ch = u.shape
    rows = l // GRID_W
    return conv3_seq(u.reshape(b * rows, GRID_W, ch), w).reshape(b, l, ch)


def heads(t):
    return t.reshape(t.shape[:-1] + (A_HEADS, A_HEAD_DIM))


def rev(t):
    return jnp.flip(t, axis=1)


def hgrn_gates(z, lb):
    f = lb + (1.0 - lb) * jax.nn.sigmoid(z.astype(F32))
    return jnp.log(f), 1.0 - f


def hgrn_state(logf, k, v):
    b = jnp.cumsum(logf, axis=1)
    return jnp.einsum("blhk,blhv->bhkv", k * jnp.exp(b[:, -1:] - b), v)


def hgrn_scan(q, logf, k, v, s0):
    bsz, l, h, dh = q.shape
    n = l // A_CHUNK

    def chunks(t):
        return t.reshape(bsz, n, A_CHUNK, h, dh).transpose(1, 0, 3, 2, 4)

    past_in_chunk = jnp.tril(jnp.ones((A_CHUNK, A_CHUNK), dtype=bool))[:, :, None]

    def step(s, inp):
        qc, gc, kc, vc = inp
        b = jnp.cumsum(gc, axis=2)
        o_inter = jnp.einsum("bhtk,bhkv->bhtv", qc * jnp.exp(b), s)
        diff = jnp.where(past_in_chunk, b[:, :, :, None, :] - b[:, :, None, :, :], -jnp.inf)
        att = jnp.einsum("bhtk,bhsk,bhtsk->bhts", qc, kc, jnp.exp(diff))
        o_intra = jnp.einsum("bhts,bhsv->bhtv", att, vc)
        b_last = b[:, :, -1:, :]
        s_new = jnp.exp(b_last[:, :, 0, :])[..., None] * s + jnp.einsum(
            "bhsk,bhsv->bhkv", kc * jnp.exp(b_last - b), vc)
        return s_new, o_inter + o_intra

    s_fin, o = lax.scan(step, s0, (chunks(q), chunks(logf), chunks(k), chunks(v)))
    return o.transpose(1, 0, 3, 2, 4).reshape(bsz, l, h, dh), s_fin


def hgrn_bidir(q, zf_f, zf_b, v, lb, s0_f, s0_b):
    lg_f, k_f = hgrn_gates(zf_f, lb[0])
    lg_b, k_b = hgrn_gates(zf_b, lb[1])
    q, v = heads(q.astype(F32)), heads(v.astype(F32))
    o_f, s_f = hgrn_scan(q, heads(lg_f), heads(k_f), v, s0_f)
    o_b, s_b = hgrn_scan(rev(q), rev(heads(lg_b)), rev(heads(k_b)), rev(v), s0_b)
    return o_f + rev(o_b), s_f, s_b


def hgrn_context_states(zf_f, zf_b, v, lb):
    lg_f, k_f = hgrn_gates(zf_f, lb[0])
    lg_b, k_b = hgrn_gates(zf_b, lb[1])
    v = heads(v.astype(F32))
    s_f = hgrn_state(heads(lg_f), heads(k_f), v)
    s_b = hgrn_state(rev(heads(lg_b)), rev(heads(k_b)), rev(v))
    return s_f, s_b


def hgrn_readout(o, g, gnorm_g):
    on = rmsnorm(o, gnorm_g.reshape(A_HEADS, A_HEAD_DIM))
    return on.reshape(g.shape).astype(g.dtype) * jax.nn.silu(g)


def hgrn_conv_mixer(h, hc, w_in, conv_w, gnorm_g, w_out, lb, ctx_out):
    zf_f, zf_b, v, q, g, u, gb, gc = jnp.split(h @ w_in, AB_SPLITS, axis=-1)
    if ctx_out:
        czf_f, czf_b, cv, cq, cg, cu, cgb, cgc = jnp.split(hc @ w_in, AB_SPLITS, axis=-1)
        zero = jnp.zeros((hc.shape[0], A_HEADS, A_HEAD_DIM, A_HEAD_DIM), F32)
        oc, s_f, s_b = hgrn_bidir(cq, czf_f, czf_b, cv, lb, zero, zero)
        yc = jnp.concatenate([hgrn_readout(oc, cg, gnorm_g),
                              cgb * conv3_seq(cgc * cu, conv_w)], axis=-1) @ w_out
    else:
        czf_f, czf_b, cv = jnp.split(hc @ w_in[:, :3 * A_WIDTH], 3, axis=-1)
        s_f, s_b = hgrn_context_states(czf_f, czf_b, cv, lb)
        yc = None
    o, _, _ = hgrn_bidir(q, zf_f, zf_b, v, lb, s_f, s_b)
    y = jnp.concatenate([hgrn_readout(o, g, gnorm_g),
                         gb * conv3_grid(gc * u, conv_w)], axis=-1) @ w_out
    return y, yc


def hyena_filters(l, fw1, fb1, fw2, fb2, fw3, fb3, fw4, freq):
    pos = jnp.arange(l, dtype=F32)
    t = jnp.linspace(0.0, 1.0, l, dtype=F32)
    w = 2.0 * math.pi * pos / l
    bands = jnp.linspace(1e-4, HY_BANDS - 1, HY_BANDS, dtype=F32)
    ang = w[:, None] * bands[None, :]
    z = jnp.concatenate([t[:, None], jnp.cos(ang), -jnp.sin(ang)], axis=-1)
    freq = freq.astype(F32)
    hdn = jnp.sin(freq * (z @ fw1.astype(F32) + fb1.astype(F32)))
    hdn = jnp.sin(freq * (hdn @ fw2.astype(F32) + fb2.astype(F32)))
    hdn = jnp.sin(freq * (hdn @ fw3.astype(F32) + fb3.astype(F32)))
    filt = (hdn @ fw4.astype(F32)).reshape(l, HY_ORDER, 2, D_MODEL)
    max_decay = math.log(HY_DECAY_TARGET) / HY_FAST_PCT
    min_decay = math.log(HY_DECAY_TARGET) / HY_SLOW_PCT
    deltas = jnp.abs(jnp.linspace(min_decay, max_decay, D_MODEL, dtype=F32))
    window = jnp.exp(-t[:, None] * deltas[None, :])
    return filt * window[:, None, None, :]


def long_conv(z, filt, skip):
    l = z.shape[1]
    fw, bw = filt[:, 0], filt[:, 1]
    taps = jnp.concatenate([(fw[0] + bw[0])[None], fw[1:], jnp.zeros_like(fw[:1]), bw[:0:-1]], axis=0)
    taps = taps / jnp.sum(jnp.abs(taps), axis=0, keepdims=True)
    zf = z.astype(F32)
    y = jnp.fft.irfft(jnp.fft.rfft(zf, n=2 * l, axis=1) * jnp.fft.rfft(taps, n=2 * l, axis=0)[None],
                      n=2 * l, axis=1)[:, :l]
    return (y + zf * skip.astype(F32)).astype(z.dtype)


def hyena_mixer(h, in_w, short_w, out_w, fparams, skip, conv_fn):
    p = conv_fn(h @ in_w, short_w)
    x1, x2, z = jnp.split(p, 3, axis=-1)
    filt = hyena_filters(h.shape[1], *fparams)
    for o, gate in enumerate((x1, x2)):
        z = gate * long_conv(z, filt[:, o], skip[o])
    return z @ out_w


def sq_relu_mlp(h, w1, w2):
    return jnp.square(jax.nn.relu(h @ w1)) @ w2


def setup_inputs(seed: int = 0) -> dict:
    key = jax.random.key(seed)
    ks = jax.random.split(key, 32)
    D = D_MODEL

    def nrm(k, shape, scale):
        return scale * jax.random.normal(k, shape, F32)

    return {
        "x": nrm(ks[0], (BATCH, SEQ, D), 1.0),
        "c": nrm(ks[1], (BATCH, D), 1.0),
        "ctx": nrm(ks[2], (BATCH, CTX_LEN, D), 1.0),
        "c_ctx": nrm(ks[3], (D,), 1.0),
        "ada_w": nrm(ks[4], (DEPTH, D, 6 * D), D ** -0.5),
        "ada_b": nrm(ks[5], (DEPTH, 6 * D), 0.02),
        "norm_g": 1.0 + nrm(ks[6], (DEPTH, 2, D), 0.02),
        "lb_logits": nrm(ks[7], (2, DEPTH + 1, A_WIDTH), 0.5),
        "ab_w_in": nrm(ks[8], (N_EVEN, D, AB_IN_WIDTH), D ** -0.5),
        "ab_conv_w": nrm(ks[9], (N_EVEN, CONV_W, B_WIDTH), CONV_W ** -0.5),
        "ab_gnorm_g": 1.0 + nrm(ks[10], (N_EVEN, A_WIDTH), 0.02),
        "ab_w_out": nrm(ks[11], (N_EVEN, D, D), D ** -0.5),
        "hy_in_w": nrm(ks[12], (N_ODD, D, 3 * D), D ** -0.5),
        "hy_short_w": nrm(ks[13], (N_ODD, CONV_W, 3 * D), CONV_W ** -0.5),
        "hy_out_w": nrm(ks[14], (N_ODD, D, D), D ** -0.5),
        "hy_fw1": nrm(ks[15], (N_ODD, HY_EMB, HY_FILTER_HIDDEN), HY_EMB ** -0.5),
        "hy_fb1": nrm(ks[16], (N_ODD, HY_FILTER_HIDDEN), 0.02),
        "hy_fw2": nrm(ks[17], (N_ODD, HY_FILTER_HIDDEN, HY_FILTER_HIDDEN), HY_FILTER_HIDDEN ** -0.5),
        "hy_fb2": nrm(ks[18], (N_ODD, HY_FILTER_HIDDEN), 0.02),
        "hy_fw3": nrm(ks[19], (N_ODD, HY_FILTER_HIDDEN, HY_FILTER_HIDDEN), HY_FILTER_HIDDEN ** -0.5),
        "hy_fb3": nrm(ks[20], (N_ODD, HY_FILTER_HIDDEN), 0.02),
        "hy_fw4": nrm(ks[21], (N_ODD, HY_FILTER_HIDDEN, HY_ORDER * 2 * D), HY_FILTER_HIDDEN ** -0.5),
        "hy_freq": 1.0 + nrm(ks[22], (N_ODD, HY_FILTER_HIDDEN), 0.02),
        "hy_skip": nrm(ks[23], (N_ODD, HY_ORDER, D), 0.1),
        "mlp_w1": nrm(ks[24], (DEPTH, D, D_FF), D ** -0.5),
        "mlp_w2": nrm(ks[25], (DEPTH, D_FF, D), D_FF ** -0.5),
        "final_g": 1.0 + nrm(ks[26], (D,), 0.02),
    }


def reference(x, c, ctx, c_ctx, ada_w, ada_b, norm_g, lb_logits, ab_w_in, ab_conv_w, ab_gnorm_g,
              ab_w_out, hy_in_w, hy_short_w, hy_out_w, hy_fw1, hy_fb1, hy_fw2, hy_fb2, hy_fw3,
              hy_fb3, hy_fw4, hy_freq, hy_skip, mlp_w1, mlp_w2, final_g):
    lb_table = jnp.cumsum(jax.nn.softmax(lb_logits.astype(F32), axis=1), axis=1)
    silu_c = jax.nn.silu(c)
    silu_cc = jax.nn.silu(c_ctx)
    xc = ctx
    for l in range(DEPTH):
        even = l % 2 == 0
        ctx_out = any(j % 2 == 0 for j in range(l + 1, DEPTH))
        i = l // 2
        sh1, sc1, g1, sh2, sc2, g2 = [m[:, None, :] for m in
                                       jnp.split(silu_c @ ada_w[l] + ada_b[l], 6, axis=-1)]
        h = rmsnorm(x, norm_g[l, 0]) * (1.0 + sc1) + sh1
        hc, cmod = None, None
        if even or ctx_out:
            n_mod = 6 if ctx_out else 2
            cmod = jnp.split(silu_cc @ ada_w[l][:, :n_mod * D_MODEL] + ada_b[l][:n_mod * D_MODEL], n_mod)
            hc = rmsnorm(xc, norm_g[l, 0]) * (1.0 + cmod[1]) + cmod[0]
        if even:
            y, yc = hgrn_conv_mixer(h, hc, ab_w_in[i], ab_conv_w[i], ab_gnorm_g[i], ab_w_out[i],
                                    lb_table[:, l], ctx_out)
        else:
            fparams = (hy_fw1[i], hy_fb1[i], hy_fw2[i], hy_fb2[i], hy_fw3[i], hy_fb3[i], hy_fw4[i], hy_freq[i])
            y = hyena_mixer(h, hy_in_w[i], hy_short_w[i], hy_out_w[i], fparams, hy_skip[i], conv3_grid)
            yc = hyena_mixer(hc, hy_in_w[i], hy_short_w[i], hy_out_w[i], fparams, hy_skip[i],
                             conv3_seq) if ctx_out else None
        x = x + g1 * y
        x = x + g2 * sq_relu_mlp(rmsnorm(x, norm_g[l, 1]) * (1.0 + sc2) + sh2, mlp_w1[l], mlp_w2[l])
        if ctx_out:
            xc = xc + cmod[2] * yc
            xc = xc + cmod[5] * sq_relu_mlp(rmsnorm(xc, norm_g[l, 1]) * (1.0 + cmod[4]) + cmod[3],
                                            mlp_w1[l], mlp_w2[l])
    return rmsnorm(x, final_g)
```

```python
import functools
import math

import numpy as np
import jax
import jax.numpy as jnp
from jax import lax
from jax.experimental import pallas as pl
from jax.experimental.pallas import tpu as pltpu

F32 = jnp.float32
BF16 = jnp.bfloat16

D_MODEL = 2048
BATCH = 4
SEQ = 2048
CTX_LEN = 256
GRID_W = 64
EPS = 1e-6
A_HEAD_DIM = 128
A_WIDTH = D_MODEL // 2
A_HEADS = A_WIDTH // A_HEAD_DIM
B_WIDTH = D_MODEL - A_WIDTH
AB_IN_WIDTH = 5 * A_WIDTH + 3 * B_WIDTH
HY_EMB = 33
HY_BANDS = (HY_EMB - 1) // 2
HY_HIDDEN = 64
HY_DECAY_TARGET = 1e-2
HY_FAST_PCT = 0.3
HY_SLOW_PCT = 1.5
D_FF = 4 * D_MODEL
TOKENS = BATCH * SEQ

SCAN_CHUNK = 64
SCAN_LEVELS = (1, 2, 4, 8, 16, 32)

VMEM_LIMIT_BYTES = 56 * 1024 * 1024


def _params(*sem):
    return pltpu.CompilerParams(dimension_semantics=sem, vmem_limit_bytes=VMEM_LIMIT_BYTES)


def _dot(a, b):
    return jnp.dot(a, b, preferred_element_type=F32)


def _dot_nt(a, b):
    return lax.dot_general(a, b, (((1,), (1,)), ((), ())), preferred_element_type=F32)


def _dot_tn(a, b):
    return lax.dot_general(a, b, (((0,), (0,)), ((), ())), preferred_element_type=F32)


def _dot_f32(a, b):
    return jnp.dot(a, b, precision=lax.Precision.HIGHEST, preferred_element_type=F32)


def _ada_kernel(c_ref, w_ref, b_ref, o_ref):
    c = c_ref[...]
    a = (c * jax.nn.sigmoid(c)).astype(BF16)
    o_ref[...] = _dot(a, w_ref[...].astype(BF16)) + b_ref[...]


def _ada(cond, ada_w, ada_b):
    depth, d, n = ada_w.shape
    tn = 1024
    return pl.pallas_call(
        _ada_kernel,
        grid=(depth, n // tn),
        in_specs=[
            pl.BlockSpec((8, d), lambda l, j: (0, 0)),
            pl.BlockSpec((None, d, tn), lambda l, j: (l, 0, j)),
            pl.BlockSpec((None, 1, tn), lambda l, j: (l, 0, j)),
        ],
        out_specs=pl.BlockSpec((None, 8, tn), lambda l, j: (l, 0, j)),
        out_shape=jax.ShapeDtypeStruct((depth, 8, n), F32),
        compiler_params=_params("arbitrary", "arbitrary"),
        name="ada_mod",
    )(cond, ada_w, ada_b.reshape(depth, 1, n))


def _prep_kernel(x_ref, g_ref, sc_ref, sh_ref, o_ref):
    x = x_ref[...]
    ms = jnp.mean(x * x, axis=-1, keepdims=True)
    xn = x * lax.rsqrt(ms + EPS) * g_ref[...]
    o_ref[...] = (xn * (1.0 + sc_ref[...]) + sh_ref[...]).astype(o_ref.dtype)


def _prep(x, g, sc, sh, out_dtype, ts):
    b, l, d = x.shape
    return pl.pallas_call(
        _prep_kernel,
        grid=(b, l // ts),
        in_specs=[
            pl.BlockSpec((None, ts, d), lambda i, j: (i, j, 0)),
            pl.BlockSpec((1, d), lambda i, j: (0, 0)),
            pl.BlockSpec((None, 1, d), lambda i, j: (i, 0, 0)),
            pl.BlockSpec((None, 1, d), lambda i, j: (i, 0, 0)),
        ],
        out_specs=pl.BlockSpec((None, ts, d), lambda i, j: (i, j, 0)),
        out_shape=jax.ShapeDtypeStruct((b, l, d), out_dtype),
        compiler_params=_params("arbitrary", "arbitrary"),
        name="norm_mod",
    )(x, g.reshape(1, d), sc.reshape(b, 1, d), sh.reshape(b, 1, d))


def _mm_kernel(*refs, n_a, k_sizes, epi):
    a_refs = refs[:n_a]
    w_ref = refs[n_a]
    rest = refs[n_a + 1:]
    acc = None
    off = 0
    for a_ref, k in zip(a_refs, k_sizes):
        part = _dot(a_ref[...], w_ref[off:off + k, :])
        acc = part if acc is None else acc + part
        off += k
    epi(acc, *rest)


def _mm(a_list, w, *, tm, tn, n_cols, epi, extras, extra_specs, out_shapes, out_specs, name):
    m = a_list[0].shape[0]
    k_sizes = tuple(a.shape[1] for a in a_list)
    k_total = sum(k_sizes)
    assert w.shape[0] == k_total and m % tm == 0 and n_cols % tn == 0
    in_specs = [pl.BlockSpec((tm, k), lambda j, i: (i, 0)) for k in k_sizes]
    in_specs.append(pl.BlockSpec((k_total, tn), lambda j, i: (0, j)))
    in_specs.extend(extra_specs)
    return pl.pallas_call(
        functools.partial(_mm_kernel, n_a=len(a_list), k_sizes=k_sizes, epi=epi),
        grid=(n_cols // tn, m // tm),
        in_specs=in_specs,
        out_specs=out_specs,
        out_shape=out_shapes,
        compiler_params=_params("arbitrary", "arbitrary"),
        name=name,
    )(*a_list, w, *extras)


def _epi_store(acc, o_ref):
    o_ref[...] = acc.astype(o_ref.dtype)


def _epi_relu2(acc, o_ref):
    r = jnp.maximum(acc, 0.0)
    o_ref[...] = (r * r).astype(o_ref.dtype)


def _epi_residual(acc, x_ref, gate_ref, o_ref):
    o_ref[...] = x_ref[...] + gate_ref[...] * acc


def _epi_conv3_grid(acc, w_ref, o_ref):
    rows = lax.broadcasted_iota(jnp.int32, acc.shape, 0) % GRID_W
    prev = jnp.where(rows == 0, 0.0, pltpu.roll(acc, 1, 0))
    nxt = jnp.where(rows == GRID_W - 1, 0.0, pltpu.roll(acc, acc.shape[0] - 1, 0))
    w = w_ref[...]
    o_ref[...] = (w[0:1] * prev + w[1:2] * acc + w[2:3] * nxt).astype(o_ref.dtype)


def _mm_plain(a, w, n_cols, out_dtype, epi, tm, tn, name):
    m = a.shape[0]
    return _mm([a], w, tm=tm, tn=tn, n_cols=n_cols, epi=epi, extras=(), extra_specs=(),
               out_shapes=jax.ShapeDtypeStruct((m, n_cols), out_dtype),
               out_specs=pl.BlockSpec((tm, tn), lambda j, i: (i, j)), name=name)


def _mm_residual(a_list, w, x, gate, tm, tn, name):
    m, n = x.shape
    tiles_per_batch = SEQ // tm
    return _mm(a_list, w, tm=tm, tn=tn, n_cols=n, epi=_epi_residual,
               extras=(x, gate.reshape(BATCH, 1, n)),
               extra_specs=(pl.BlockSpec((tm, tn), lambda j, i: (i, j)),
                            pl.BlockSpec((None, 1, tn), lambda j, i: (i // tiles_per_batch, 0, j))),
               out_shapes=jax.ShapeDtypeStruct((m, n), F32),
               out_specs=pl.BlockSpec((tm, tn), lambda j, i: (i, j)), name=name)


def _gconv_kernel(u_ref, gb_ref, gc_ref, w_ref, o_ref):
    t = gc_ref[...] * u_ref[...]
    rows = lax.broadcasted_iota(jnp.int32, t.shape, 0) % GRID_W
    prev = jnp.where(rows == 0, 0.0, pltpu.roll(t, 1, 0))
    nxt = jnp.where(rows == GRID_W - 1, 0.0, pltpu.roll(t, t.shape[0] - 1, 0))
    w = w_ref[...]
    o_ref[...] = (gb_ref[...] * (w[0:1] * prev + w[1:2] * t + w[2:3] * nxt)).astype(o_ref.dtype)


def _gconv(p, conv_w):
    tm, tn = 512, 512
    nb = B_WIDTH // tn
    base = 5 * A_WIDTH // tn
    return pl.pallas_call(
        _gconv_kernel,
        grid=(TOKENS // tm, nb),
        in_specs=[
            pl.BlockSpec((tm, tn), lambda i, j: (i, base + j)),
            pl.BlockSpec((tm, tn), lambda i, j: (i, base + nb + j)),
            pl.BlockSpec((tm, tn), lambda i, j: (i, base + 2 * nb + j)),
            pl.BlockSpec((3, tn), lambda i, j: (0, j)),
        ],
        out_specs=pl.BlockSpec((tm, tn), lambda i, j: (i, j)),
        out_shape=jax.ShapeDtypeStruct((TOKENS, B_WIDTH), BF16),
        compiler_params=_params("arbitrary", "arbitrary"),
        name="gated_conv",
    )(p, p, p, conv_w)


def _scan_constants():
    c = SCAN_CHUNK

    def ranges_to_matrix(ranges):
        mat = np.zeros((c, c), np.float32)
        for r, (lo, hi) in enumerate(ranges):
            if hi >= lo:
                mat[r, lo:hi + 1] = 1.0
        return mat

    blocks = [ranges_to_matrix([(0, t) for t in range(c)]),
              ranges_to_matrix([(t + 1, c - 1) for t in range(c)])]
    later_masks, block_masks = [], []
    for w in SCAN_LEVELS:
        ranges, later = [], np.zeros((c,), np.float32)
        for r in range(c):
            first_later = (r // (2 * w)) * 2 * w + w
            if r >= first_later:
                ranges.append((first_later, r))
                later[r] = 1.0
            else:
                ranges.append((r + 1, first_later - 1))
        blocks.append(ranges_to_matrix(ranges))
        later_masks.append(np.broadcast_to(later[:, None], (c, A_HEAD_DIM)).copy())
        blk = np.arange(c) // (2 * w)
        block_masks.append((blk[:, None] == blk[None, :]).astype(np.float32)
                           * later[:, None] * (1.0 - later[None, :]))
    fwd_sum = np.concatenate(blocks, axis=0)
    later = np.stack(later_masks)
    pair = np.stack(block_masks)
    bwd_sum = np.concatenate([blk[::-1, ::-1] for blk in blocks], axis=0)
    sums = np.stack([fwd_sum, bwd_sum])
    laters = np.stack([later, later[:, ::-1]])
    pairs = np.stack([pair, pair[:, ::-1, ::-1]])
    return (jnp.asarray(sums, BF16), jnp.asarray(np.ascontiguousarray(laters), F32),
            jnp.asarray(np.ascontiguousarray(pairs), F32))


def _scan_kernel(zf_ref, zb_ref, v_ref, q_ref, g_ref, czf_ref, czb_ref, cv_ref, lbl_ref, gn_ref,
                 sums_ref, later_ref, pair_ref, o_ref, of_ref, ob_ref, st_ref, *, layer):
    c = SCAN_CHUNK
    n_chunks = SEQ // c
    n_ctx = CTX_LEN // c
    z_refs = (zf_ref, zb_ref)
    cz_refs = (czf_ref, czb_ref)
    out_refs = (of_ref, ob_ref)

    logit_rows = [lbl_ref[:, k, :] for k in range(lbl_ref.shape[1])]
    top = functools.reduce(jnp.maximum, logit_rows)
    exps = [jnp.exp(r - top) for r in logit_rows]
    lbs = sum(exps[:layer + 1]) / sum(exps)

    eye = (lax.broadcasted_iota(jnp.int32, (c, c), 0)
           == lax.broadcasted_iota(jnp.int32, (c, c), 1)).astype(F32)

    def chunk(d, z, q, v, want_out):
        lb = lbs[d:d + 1]
        f = lb + (1.0 - lb) * jax.nn.sigmoid(z)
        logf = jnp.log(f)
        kk = 1.0 - f
        hi = logf.astype(BF16)
        r1 = logf - hi.astype(F32)
        mid = r1.astype(BF16)
        lo = (r1 - mid.astype(F32)).astype(BF16)
        e3 = _dot(sums_ref[d], jnp.concatenate([hi, mid, lo], axis=1))
        run = e3[:, :A_HEAD_DIM] + e3[:, A_HEAD_DIM:2 * A_HEAD_DIM] + e3[:, 2 * A_HEAD_DIM:]
        cum = run[0:c]
        rem = run[c:2 * c]
        last = cum[c - 1:c] if d == 0 else cum[0:1]
        state = st_ref[d]
        vb = v.astype(BF16)
        kd = (kk * jnp.exp(rem)).astype(BF16)
        st_ref[d] = state * jnp.exp(last) + _dot_tn(vb, kd)
        if not want_out:
            return None
        o = _dot_nt((q * jnp.exp(cum)).astype(BF16), state.astype(BF16))
        att = eye * _dot_nt(q.astype(BF16), kk.astype(BF16))
        for lvl in range(len(SCAN_LEVELS)):
            x = jnp.exp(run[(2 + lvl) * c:(3 + lvl) * c])
            later = later_ref[d, lvl]
            qt = (q * x * later).astype(BF16)
            kt = (kk * x * (1.0 - later)).astype(BF16)
            att = att + pair_ref[d, lvl] * _dot_nt(qt, kt)
        return o + _dot(att.astype(BF16), vb)

    st_ref[...] = jnp.zeros_like(st_ref)

    def ctx_body(i, carry):
        for d in range(2):
            ci = i if d == 0 else n_ctx - 1 - i
            rows = pl.ds(pl.multiple_of(ci * c, c), c)
            chunk(d, cz_refs[d][rows, :], None, cv_ref[rows, :], False)
        return carry

    lax.fori_loop(0, n_ctx, ctx_body, 0)

    def seq_body(i, carry):
        for d in range(2):
            ci = i if d == 0 else n_chunks - 1 - i
            rows = pl.ds(pl.multiple_of(ci * c, c), c)
            out_refs[d][rows, :] = chunk(d, z_refs[d][rows, :], q_ref[rows, :], v_ref[rows, :], True)
        return carry

    lax.fori_loop(0, n_chunks, seq_body, 0)

    rb = 256
    gn = gn_ref[...]

    def read_body(i, carry):
        rows = pl.ds(pl.multiple_of(i * rb, rb), rb)
        o = of_ref[rows, :] + ob_ref[rows, :]
        ms = jnp.mean(o * o, axis=-1, keepdims=True)
        on = o * lax.rsqrt(ms + EPS) * gn
        g = g_ref[rows, :]
        o_ref[rows, :] = (on * (g * jax.nn.sigmoid(g))).astype(o_ref.dtype)
        return carry

    lax.fori_loop(0, SEQ // rb, read_body, 0)


def _hgrn_scan(p, cp, lb_logits, gnorm_g, layer):
    sums, laters, pairs = _scan_constants()
    h = A_HEADS
    hd = A_HEAD_DIM
    n_lb = lb_logits.shape[1]

    def col(k):
        return pl.BlockSpec((SEQ, hd), lambda b, j, k=k: (b, k * h + j))

    def ccol(k):
        return pl.BlockSpec((CTX_LEN, hd), lambda b, j, k=k: (b, k * h + j))

    def whole(arr):
        nd = arr.ndim
        return pl.BlockSpec(arr.shape, lambda b, j, nd=nd: (0,) * nd)

    return pl.pallas_call(
        functools.partial(_scan_kernel, layer=layer),
        grid=(BATCH, h),
        in_specs=[col(0), col(1), col(2), col(3), col(4), ccol(0), ccol(1), ccol(2),
                  pl.BlockSpec((2, n_lb, hd), lambda b, j: (0, 0, j)),
                  pl.BlockSpec((1, hd), lambda b, j: (0, j)),
                  whole(sums), whole(laters), whole(pairs)],
        out_specs=pl.BlockSpec((SEQ, hd), lambda b, j: (b, j)),
        out_shape=jax.ShapeDtypeStruct((TOKENS, A_WIDTH), BF16),
        scratch_shapes=[pltpu.VMEM((SEQ, hd), F32), pltpu.VMEM((SEQ, hd), F32),
                        pltpu.VMEM((2, hd, hd), F32)],
        compiler_params=_params("arbitrary", "arbitrary"),
        name="hgrn_scan",
    )(p, p, p, p, p, cp, cp, cp, lb_logits, gnorm_g.reshape(1, A_WIDTH), sums, laters, pairs)


def _filter_kernel(z_ref, w1_ref, b1_ref, w2_ref, b2_ref, w3_ref, b3_ref, fr_ref, w4f_ref, w4b_ref,
                   sum_ref, diff_ref, hid_ref, *, tc):
    o = pl.program_id(0)
    j = pl.program_id(1)

    @pl.when((o == 0) & (j == 0))
    def _():
        fr = fr_ref[...]
        h = jnp.sin(fr * (_dot_f32(z_ref[...], w1_ref[...]) + b1_ref[...]))
        h = jnp.sin(fr * (_dot_f32(h, w2_ref[...]) + b2_ref[...]))
        hid_ref[...] = jnp.sin(fr * (_dot_f32(h, w3_ref[...]) + b3_ref[...]))

    hid = hid_ref[...]
    rows = lax.broadcasted_iota(jnp.int32, (SEQ, tc), 0)
    chan = lax.broadcasted_iota(jnp.int32, (SEQ, tc), 1) + j * tc
    t = rows.astype(F32) * (1.0 / (SEQ - 1))
    max_decay = math.log(HY_DECAY_TARGET) / HY_FAST_PCT
    min_decay = math.log(HY_DECAY_TARGET) / HY_SLOW_PCT
    deltas = jnp.abs(min_decay + chan.astype(F32) * ((max_decay - min_decay) / (D_MODEL - 1)))
    window = jnp.exp(-t * deltas)
    fw = _dot_f32(hid, w4f_ref[...]) * window
    bw = _dot_f32(hid, w4b_ref[...]) * window
    first = rows == 0
    a = fw + jnp.where(first, bw, 0.0)
    bb = jnp.where(first, 0.0, bw)
    inv = 1.0 / jnp.sum(jnp.abs(a) + jnp.abs(bb), axis=0, keepdims=True)
    sum_ref[...] = ((a + bb) * inv).astype(sum_ref.dtype)
    diff_ref[...] = ((a - bb) * inv).astype(diff_ref.dtype)


def _hyena_filter_taps(fw1, fb1, fw2, fb2, fw3, fb3, fw4, freq):
    l = SEQ
    pos = jnp.arange(l, dtype=F32)
    t = jnp.linspace(0.0, 1.0, l, dtype=F32)
    w = 2.0 * math.pi * pos / l
    bands = jnp.linspace(1e-4, HY_BANDS - 1, HY_BANDS, dtype=F32)
    ang = w[:, None] * bands[None, :]
    z = jnp.concatenate([t[:, None], jnp.cos(ang), -jnp.sin(ang)], axis=-1)
    tc = 256
    nj = D_MODEL // tc
    hh = 128

    def pad(a, rows, cols):
        return jnp.pad(a, ((0, rows - a.shape[0]), (0, cols - a.shape[1])))

    def small(shape):
        return pl.BlockSpec(shape, lambda o, j: (0, 0))

    out_sds = jax.ShapeDtypeStruct((l, 2 * D_MODEL), BF16)
    out_spec = pl.BlockSpec((l, tc), lambda o, j: (0, o * nj + j))
    fw4p = pad(fw4, hh, fw4.shape[1])
    return pl.pallas_call(
        functools.partial(_filter_kernel, tc=tc),
        grid=(2, nj),
        in_specs=[small((l, hh)), small((hh, hh)), small((1, hh)), small((hh, hh)), small((1, hh)),
                  small((hh, hh)), small((1, hh)), small((1, hh)),
                  pl.BlockSpec((hh, tc), lambda o, j: (0, 2 * o * nj + j)),
                  pl.BlockSpec((hh, tc), lambda o, j: (0, (2 * o + 1) * nj + j))],
        out_specs=(out_spec, out_spec),
        out_shape=(out_sds, out_sds),
        scratch_shapes=[pltpu.VMEM((l, hh), F32)],
        compiler_params=_params("arbitrary", "arbitrary"),
        name="hyena_filter",
    )(pad(z, l, hh), pad(fw1, hh, hh), pad(fb1[None, :], 1, hh), pad(fw2, hh, hh), pad(fb2[None, :], 1, hh),
      pad(fw3, hh, hh), pad(fb3[None, :], 1, hh), pad(freq[None, :], 1, hh), fw4p, fw4p)


def _dft_matrices():
    l = SEQ
    period = 4 * l

    def trig(n):
        ang = (n % period).astype(F32) * (math.pi / (2 * l))
        return jnp.cos(ang), jnp.sin(ang)

    lanes = 128
    groups = l // lanes
    row = jnp.arange(l, dtype=jnp.int32)[:, None]
    lo = jnp.arange(lanes, dtype=jnp.int32)[None, :]
    hi = jnp.arange(groups, dtype=jnp.int32)[None, :]

    def combine(c_hi, s_hi, c_lo, s_lo, scale):
        cm = (c_hi[:, :, None] * c_lo[:, None, :] - s_hi[:, :, None] * s_lo[:, None, :]).reshape(l, l)
        sm = (s_hi[:, :, None] * c_lo[:, None, :] + c_hi[:, :, None] * s_lo[:, None, :]).reshape(l, l)
        return (cm * scale).astype(BF16), (sm * scale).astype(BF16)

    odd = 2 * row + 1
    c_hi, s_hi = trig(odd * (lanes * hi))
    c_lo, s_lo = trig(odd * lo)
    c_fm, s_fm = combine(c_hi, s_hi, c_lo, s_lo, 1.0)
    c_hi, s_hi = trig(row * (2 * lanes * hi))
    c_lo, s_lo = trig(row * (2 * lo + 1))
    c_mf, s_mf = combine(c_hi, s_hi, c_lo, s_lo, 1.0 / l)
    return c_fm, s_fm, c_mf, s_mf


def _dft_fwd_kernel(c_ref, s_ref, z_ref, p_ref, q_ref, u_ref, v_ref):
    z = z_ref[...]
    a = _dot(c_ref[...], z)
    b = _dot(s_ref[...], z)
    p = p_ref[...]
    q = q_ref[...]
    u_ref[...] = (a * p - b * q).astype(u_ref.dtype)
    v_ref[...] = (a * q + b * p).astype(v_ref.dtype)


def _dft_fwd(c_fm, s_fm, z_arr, z_col0, pq_p, pq_q, order):
    tm, tn = 1024, 512
    ni, nj = SEQ // tm, D_MODEL // tn
    zc = z_col0 // tn
    out_sds = jax.ShapeDtypeStruct((TOKENS, D_MODEL), BF16)
    out_spec = pl.BlockSpec((tm, tn), lambda j, i, b: (b * ni + i, j))
    return pl.pallas_call(
        _dft_fwd_kernel,
        grid=(nj, ni, BATCH),
        in_specs=[pl.BlockSpec((tm, SEQ), lambda j, i, b: (i, 0)),
                  pl.BlockSpec((tm, SEQ), lambda j, i, b: (i, 0)),
                  pl.BlockSpec((SEQ, tn), lambda j, i, b: (b, zc + j)),
                  pl.BlockSpec((tm, tn), lambda j, i, b: (i, order * nj + j)),
                  pl.BlockSpec((tm, tn), lambda j, i, b: (i, order * nj + j))],
        out_specs=(out_spec, out_spec),
        out_shape=(out_sds, out_sds),
        compiler_params=_params("arbitrary", "arbitrary", "arbitrary"),
        name="dft_analysis",
    )(c_fm, s_fm, z_arr, pq_p, pq_q)


def _dft_inv_kernel(c_ref, s_ref, u_ref, v_ref, z_ref, gate_ref, skip_ref, o_ref):
    y = _dot(c_ref[...], u_ref[...]) + _dot(s_ref[...], v_ref[...])
    y = y + z_ref[...].astype(F32) * skip_ref[...]
    o_ref[...] = (gate_ref[...].astype(F32) * y).astype(o_ref.dtype)


def _dft_inv(c_mf, s_mf, u, v, z_arr, z_col0, gate_arr, gate_col0, skip):
    tm, tn = 1024, 512
    ni, nj = SEQ // tm, D_MODEL // tn
    zc, gc = z_col0 // tn, gate_col0 // tn
    return pl.pallas_call(
        _dft_inv_kernel,
        grid=(nj, BATCH, ni),
        in_specs=[pl.BlockSpec((tm, SEQ), lambda j, b, i: (i, 0)),
                  pl.BlockSpec((tm, SEQ), lambda j, b, i: (i, 0)),
                  pl.BlockSpec((SEQ, tn), lambda j, b, i: (b, j)),
                  pl.BlockSpec((SEQ, tn), lambda j, b, i: (b, j)),
                  pl.BlockSpec((tm, tn), lambda j, b, i: (b * ni + i, zc + j)),
                  pl.BlockSpec((tm, tn), lambda j, b, i: (b * ni + i, gc + j)),
                  pl.BlockSpec((1, tn), lambda j, b, i: (0, j))],
        out_specs=pl.BlockSpec((tm, tn), lambda j, b, i: (b * ni + i, j)),
        out_shape=jax.ShapeDtypeStruct((TOKENS, D_MODEL), BF16),
        compiler_params=_params("arbitrary", "arbitrary", "arbitrary"),
        name="dft_synthesis",
    )(c_mf, s_mf, u, v, z_arr, gate_arr, skip.reshape(1, D_MODEL))


def _mlp_block(x2d, mod, l, norm_g, w1, w2):
    sh2, sc2, g2 = (mod[l, :BATCH, k * D_MODEL:(k + 1) * D_MODEL] for k in (3, 4, 5))
    h = _prep(x2d.reshape(BATCH, SEQ, D_MODEL), norm_g[l, 1], sc2, sh2, BF16, 512).reshape(TOKENS, D_MODEL)
    hid = _mm_plain(h, w1.astype(BF16), D_FF, BF16, _epi_relu2, 1024, 1024, "mlp_up")
    return _mm_residual([hid], w2.astype(BF16), x2d, g2, 512, 512, "mlp_down")


def kernel(x, c, ctx, c_ctx, ada_w, ada_b, norm_g, lb_logits, ab_w_in, ab_conv_w, ab_gnorm_g, ab_w_out,
           hy_in_w, hy_short_w, hy_out_w, hy_fw1, hy_fb1, hy_fw2, hy_fb2, hy_fw3, hy_fb3, hy_fw4, hy_freq,
           hy_skip, mlp_w1, mlp_w2, final_g):
    d = D_MODEL
    cond = jnp.concatenate([c, c_ctx[None, :], jnp.zeros((3, d), F32)], axis=0)
    mod = _ada(cond, ada_w, ada_b)
    x2d = x.reshape(TOKENS, d)

    sh1, sc1, g1 = (mod[0, :BATCH, k * d:(k + 1) * d] for k in (0, 1, 2))
    h = _prep(x, norm_g[0, 0], sc1, sh1, BF16, 512).reshape(TOKENS, d)
    csh = jnp.broadcast_to(mod[0, BATCH:BATCH + 1, 0:d], (BATCH, d))
    csc = jnp.broadcast_to(mod[0, BATCH:BATCH + 1, d:2 * d], (BATCH, d))
    hc = _prep(ctx, norm_g[0, 0], csc, csh, BF16, CTX_LEN).reshape(BATCH * CTX_LEN, d)
    w_in = ab_w_in[0].astype(BF16)
    p = _mm_plain(h, w_in, AB_IN_WIDTH, F32, _epi_store, 1024, 1024, "ab_in_proj")
    cp = _mm_plain(hc, w_in, 3 * A_WIDTH, F32, _epi_store, 1024, 1024, "ab_ctx_proj")
    mix_a = _hgrn_scan(p, cp, lb_logits, ab_gnorm_g[0], 0)
    mix_b = _gconv(p, ab_conv_w[0])
    x2d = _mm_residual([mix_a, mix_b], ab_w_out[0].astype(BF16), x2d, g1, 1024, 1024, "ab_out_proj")
    x2d = _mlp_block(x2d, mod, 0, norm_g, mlp_w1[0], mlp_w2[0])

    sh1, sc1, g1 = (mod[1, :BATCH, k * d:(k + 1) * d] for k in (0, 1, 2))
    h = _prep(x2d.reshape(BATCH, SEQ, d), norm_g[1, 0], sc1, sh1, BF16, 512).reshape(TOKENS, d)
    pc = _mm([h], hy_in_w[0].astype(BF16), tm=1024, tn=1024, n_cols=3 * d, epi=_epi_conv3_grid,
             extras=(hy_short_w[0],), extra_specs=(pl.BlockSpec((3, 1024), lambda j, i: (0, j)),),
             out_shapes=jax.ShapeDtypeStruct((TOKENS, 3 * d), BF16),
             out_specs=pl.BlockSpec((1024, 1024), lambda j, i: (i, j)), name="hy_in_proj")
    taps_sum, taps_diff = _hyena_filter_taps(hy_fw1[0], hy_fb1[0], hy_fw2[0], hy_fb2[0], hy_fw3[0],
                                             hy_fb3[0], hy_fw4[0], hy_freq[0])
    c_fm, s_fm, c_mf, s_mf = _dft_matrices()
    filt_p = _mm_plain(c_fm, taps_sum, 2 * d, F32, _epi_store, 1024, 1024, "filter_dft_cos")
    filt_q = _mm_plain(s_fm, taps_diff, 2 * d, F32, _epi_store, 1024, 1024, "filter_dft_sin")
    u, v = _dft_fwd(c_fm, s_fm, pc, 2 * d, filt_p, filt_q, 0)
    z = _dft_inv(c_mf, s_mf, u, v, pc, 2 * d, pc, 0, hy_skip[0, 0])
    u, v = _dft_fwd(c_fm, s_fm, z, 0, filt_p, filt_q, 1)
    z = _dft_inv(c_mf, s_mf, u, v, z, 0, pc, d, hy_skip[0, 1])
    x2d = _mm_residual([z], hy_out_w[0].astype(BF16), x2d, g1, 1024, 1024, "hy_out_proj")
    x2d = _mlp_block(x2d, mod, 1, norm_g, mlp_w1[1], mlp_w2[1])

    zeros = jnp.zeros((BATCH, d), F32)
    return _prep(x2d.reshape(BATCH, SEQ, d), final_g, zeros, zeros, F32, 512)
```

```python
import functools
import math

import numpy as np
import jax
import jax.numpy as jnp
from jax import lax
from jax.experimental import pallas as pl
from jax.experimental.pallas import tpu as pltpu

F32 = jnp.float32
BF16 = jnp.bfloat16

D_MODEL = 2048
BATCH = 4
SEQ = 2048
CTX_LEN = 256
GRID_W = 64
EPS = 1e-6
A_HEAD_DIM = 128
A_WIDTH = D_MODEL // 2
A_HEADS = A_WIDTH // A_HEAD_DIM
B_WIDTH = D_MODEL - A_WIDTH
AB_IN_WIDTH = 5 * A_WIDTH + 3 * B_WIDTH
HY_EMB = 33
HY_BANDS = (HY_EMB - 1) // 2
HY_HIDDEN = 64
HY_DECAY_TARGET = 1e-2
HY_FAST_PCT = 0.3
HY_SLOW_PCT = 1.5
D_FF = 4 * D_MODEL
TOKENS = BATCH * SEQ

SCAN_CHUNK = 64
SCAN_LEVELS = (1, 2, 4, 8, 16, 32)

VMEM_LIMIT_BYTES = 56 * 1024 * 1024


def _params(*sem):
    return pltpu.CompilerParams(dimension_semantics=sem, vmem_limit_bytes=VMEM_LIMIT_BYTES)


def _dot(a, b):
    return jnp.dot(a, b, preferred_element_type=F32)


def _dot_nt(a, b):
    return lax.dot_general(a, b, (((1,), (1,)), ((), ())), preferred_element_type=F32)


def _dot_tn(a, b):
    return lax.dot_general(a, b, (((0,), (0,)), ((), ())), preferred_element_type=F32)


def _dot_f32(a, b):
    return jnp.dot(a, b, precision=lax.Precision.HIGHEST, preferred_element_type=F32)


def _ada_kernel(c_ref, w_ref, b_ref, o_ref):
    c = c_ref[...]
    a = (c * jax.nn.sigmoid(c)).astype(BF16)
    o_ref[...] = _dot(a, w_ref[...].astype(BF16)) + b_ref[...]


def _ada(cond, ada_w, ada_b):
    depth, d, n = ada_w.shape
    tn = 1024
    return pl.pallas_call(
        _ada_kernel,
        grid=(depth, n // tn),
        in_specs=[
            pl.BlockSpec((8, d), lambda l, j: (0, 0)),
            pl.BlockSpec((None, d, tn), lambda l, j: (l, 0, j)),
            pl.BlockSpec((None, 1, tn), lambda l, j: (l, 0, j)),
        ],
        out_specs=pl.BlockSpec((None, 8, tn), lambda l, j: (l, 0, j)),
        out_shape=jax.ShapeDtypeStruct((depth, 8, n), F32),
        compiler_params=_params("arbitrary", "arbitrary"),
        name="ada_mod",
    )(cond, ada_w, ada_b.reshape(depth, 1, n))


def _prep_kernel(x_ref, g_ref, sc_ref, sh_ref, o_ref):
    x = x_ref[...]
    ms = jnp.mean(x * x, axis=-1, keepdims=True)
    xn = x * lax.rsqrt(ms + EPS) * g_ref[...]
    o_ref[...] = (xn * (1.0 + sc_ref[...]) + sh_ref[...]).astype(o_ref.dtype)


def _prep(x, g, sc, sh, out_dtype, ts):
    b, l, d = x.shape
    return pl.pallas_call(
        _prep_kernel,
        grid=(b, l // ts),
        in_specs=[
            pl.BlockSpec((None, ts, d), lambda i, j: (i, j, 0)),
            pl.BlockSpec((1, d), lambda i, j: (0, 0)),
            pl.BlockSpec((None, 1, d), lambda i, j: (i, 0, 0)),
            pl.BlockSpec((None, 1, d), lambda i, j: (i, 0, 0)),
        ],
        out_specs=pl.BlockSpec((None, ts, d), lambda i, j: (i, j, 0)),
        out_shape=jax.ShapeDtypeStruct((b, l, d), out_dtype),
        compiler_params=_params("arbitrary", "arbitrary"),
        name="norm_mod",
    )(x, g.reshape(1, d), sc.reshape(b, 1, d), sh.reshape(b, 1, d))


def _mm_kernel(*refs, n_a, k_sizes, epi, cast_w):
    a_refs = refs[:n_a]
    w_ref = refs[n_a]
    rest = refs[n_a + 1:]
    if cast_w:
        w_ref, rest = rest[-1], rest[:-1]

        @pl.when(pl.program_id(1) == 0)
        def _():
            w_ref[...] = refs[n_a][...].astype(BF16)

    acc = None
    off = 0
    for a_ref, k in zip(a_refs, k_sizes):
        part = _dot(a_ref[...], w_ref[off:off + k, :])
        acc = part if acc is None else acc + part
        off += k
    epi(acc, *rest)


def _mm(a_list, w, *, tm, tn, n_cols, epi, extras, extra_specs, out_shapes, out_specs, name,
        w_layer=None, w_single_buffer=False):
    m = a_list[0].shape[0]
    k_sizes = tuple(a.shape[1] for a in a_list)
    k_total = sum(k_sizes)
    cast_w = w_layer is not None
    assert w.shape[-2] == k_total and m % tm == 0 and n_cols % tn == 0
    in_specs = [pl.BlockSpec((tm, k), lambda j, i: (i, 0)) for k in k_sizes]
    w_mode = dict(pipeline_mode=pl.Buffered(1)) if w_single_buffer else {}
    if cast_w:
        in_specs.append(pl.BlockSpec((None, k_total, tn), lambda j, i: (w_layer, 0, j), **w_mode))
    else:
        in_specs.append(pl.BlockSpec((k_total, tn), lambda j, i: (0, j), **w_mode))
    in_specs.extend(extra_specs)
    return pl.pallas_call(
        functools.partial(_mm_kernel, n_a=len(a_list), k_sizes=k_sizes, epi=epi, cast_w=cast_w),
        grid=(n_cols // tn, m // tm),
        in_specs=in_specs,
        out_specs=out_specs,
        out_shape=out_shapes,
        scratch_shapes=[pltpu.VMEM((k_total, tn), BF16)] if cast_w else [],
        compiler_params=_params("arbitrary", "arbitrary"),
        name=name,
    )(*a_list, w, *extras)


def _epi_store(acc, o_ref):
    o_ref[...] = acc.astype(o_ref.dtype)


def _epi_relu2(acc, o_ref):
    r = jnp.maximum(acc, 0.0)
    o_ref[...] = (r * r).astype(o_ref.dtype)


def _epi_residual(acc, x_ref, gate_ref, o_ref):
    o_ref[...] = x_ref[...] + gate_ref[...] * acc


def _epi_conv3_grid(acc, w_ref, o_ref):
    rows = lax.broadcasted_iota(jnp.int32, acc.shape, 0) % GRID_W
    prev = jnp.where(rows == 0, 0.0, pltpu.roll(acc, 1, 0))
    nxt = jnp.where(rows == GRID_W - 1, 0.0, pltpu.roll(acc, acc.shape[0] - 1, 0))
    w = w_ref[...]
    o_ref[...] = (w[0:1] * prev + w[1:2] * acc + w[2:3] * nxt).astype(o_ref.dtype)


def _mm_plain(a, w, n_cols, out_dtype, epi, tm, tn, name, w_layer=None):
    m = a.shape[0]
    return _mm([a], w, tm=tm, tn=tn, n_cols=n_cols, epi=epi, extras=(), extra_specs=(),
               out_shapes=jax.ShapeDtypeStruct((m, n_cols), out_dtype),
               out_specs=pl.BlockSpec((tm, tn), lambda j, i: (i, j)), name=name, w_layer=w_layer)


def _mm_residual(a_list, w, w_layer, x, gate, tm, tn, name, w_single_buffer=False):
    m, n = x.shape
    tiles_per_batch = SEQ // tm
    return _mm(a_list, w, tm=tm, tn=tn, n_cols=n, epi=_epi_residual,
               extras=(x, gate.reshape(BATCH, 1, n)),
               extra_specs=(pl.BlockSpec((tm, tn), lambda j, i: (i, j)),
                            pl.BlockSpec((None, 1, tn), lambda j, i: (i // tiles_per_batch, 0, j))),
               out_shapes=jax.ShapeDtypeStruct((m, n), F32),
               out_specs=pl.BlockSpec((tm, tn), lambda j, i: (i, j)), name=name,
               w_layer=w_layer, w_single_buffer=w_single_buffer)


def _gconv_kernel(u_ref, gb_ref, gc_ref, w_ref, o_ref):
    t = gc_ref[...] * u_ref[...]
    rows = lax.broadcasted_iota(jnp.int32, t.shape, 0) % GRID_W
    prev = jnp.where(rows == 0, 0.0, pltpu.roll(t, 1, 0))
    nxt = jnp.where(rows == GRID_W - 1, 0.0, pltpu.roll(t, t.shape[0] - 1, 0))
    w = w_ref[...]
    o_ref[...] = (gb_ref[...] * (w[0:1] * prev + w[1:2] * t + w[2:3] * nxt)).astype(o_ref.dtype)


def _gconv(p, conv_w):
    tm, tn = 512, 512
    nb = B_WIDTH // tn
    base = 5 * A_WIDTH // tn
    return pl.pallas_call(
        _gconv_kernel,
        grid=(TOKENS // tm, nb),
        in_specs=[
            pl.BlockSpec((tm, tn), lambda i, j: (i, base + j)),
            pl.BlockSpec((tm, tn), lambda i, j: (i, base + nb + j)),
            pl.BlockSpec((tm, tn), lambda i, j: (i, base + 2 * nb + j)),
            pl.BlockSpec((3, tn), lambda i, j: (0, j)),
        ],
        out_specs=pl.BlockSpec((tm, tn), lambda i, j: (i, j)),
        out_shape=jax.ShapeDtypeStruct((TOKENS, B_WIDTH), BF16),
        compiler_params=_params("arbitrary", "arbitrary"),
        name="gated_conv",
    )(p, p, p, conv_w)


SUBLANES = 8


def _row_groups(x):
    return [x[k:k + SUBLANES] for k in range(0, x.shape[0], SUBLANES)]


def _cumsum_groups(groups, sub):
    out, carry = [], None
    for g in groups:
        for s in (1, 2, 4):
            g = g + jnp.where(sub >= s, pltpu.roll(g, s, 0), 0.0)
        if carry is not None:
            g = g + carry
        out.append(g)
        carry = g[SUBLANES - 1:SUBLANES]
    return out


def _hold_boundary(cum, w, k, sub):
    shape = cum[k].shape
    if w >= SUBLANES // 2:
        r = (k * SUBLANES // (2 * w)) * 2 * w + w - 1
        return jnp.broadcast_to(cum[r // SUBLANES][r % SUBLANES:r % SUBLANES + 1], shape)
    assert w == 2
    return jnp.where(sub < 4, jnp.broadcast_to(cum[k][1:2], shape), jnp.broadcast_to(cum[k][5:6], shape))


def _scan_kernel(zf_ref, zb_ref, v_ref, q_ref, g_ref, czf_ref, czb_ref, cv_ref, lbl_ref, gn_ref,
                 o_ref, of_ref, ob_ref, st_ref, qs_ref, inc_ref, dec_ref, kl_ref, lv_ref, att_ref, *, layer):
    c = SCAN_CHUNK
    n_chunks = SEQ // c
    n_ctx = CTX_LEN // c
    z_refs = (zf_ref, zb_ref)
    cz_refs = (czf_ref, czb_ref)
    out_refs = (of_ref, ob_ref)

    logit_rows = [lbl_ref[:, k, :] for k in range(lbl_ref.shape[1])]
    top = functools.reduce(jnp.maximum, logit_rows)
    exps = [jnp.exp(r - top) for r in logit_rows]
    lbs = sum(exps[:layer + 1]) / sum(exps)

    sub = lax.broadcasted_iota(jnp.int32, (SUBLANES, A_HEAD_DIM), 0)
    n_groups = c // SUBLANES

    def rows(groups):
        return jnp.concatenate(groups, axis=0)

    def chunk(d, z, q, v, want_out):
        lb = lbs[d:d + 1]
        f = lb + (1.0 - lb) * jax.nn.sigmoid(z)
        kk = 1.0 - f
        fg, kg = _row_groups(f), _row_groups(kk)
        lg = [jnp.log2(g) for g in fg]
        cum = _cumsum_groups(lg, sub)
        total = cum[-1][SUBLANES - 1:SUBLANES]
        if d == 0:
            pos = cum
            q_in = [jnp.exp2(g) for g in cum]
            k_out = [jnp.exp2(total - g) for g in cum]
        else:
            pos = [a - b for a, b in zip(cum, lg)]
            q_in = [jnp.exp2(total - g) for g in pos]
            k_out = [jnp.exp2(g) for g in pos]
        k_leave = rows([a * b for a, b in zip(kg, k_out)]).astype(BF16)
        decay = jnp.exp2(total)
        if not want_out:
            return k_leave, decay, None, None, None
        qg = _row_groups(q)
        o_diag = jnp.sum(q * kk, axis=1, keepdims=True) * v
        upper, lower = (qg, kg) if d == 0 else (kg, qg)
        operands = []
        for w in SCAN_LEVELS:
            y = []
            for k in range(n_groups):
                if w == 1:
                    odd = (sub & 1) != 0
                    y.append(jnp.where(odd, qg[k] * fg[k], kg[k]) if d == 0
                             else jnp.where(odd, kg[k], qg[k] * fg[k]))
                    continue
                gap = pos[k] - _hold_boundary(cum, w, k, sub)
                if w >= SUBLANES:
                    bit_set = (k * SUBLANES) & w != 0
                    y.append(upper[k] * jnp.exp2(gap) if bit_set else lower[k] * jnp.exp2(-gap))
                else:
                    bit = (sub & w) != 0
                    y.append(jnp.where(bit, upper[k], lower[k]) * jnp.exp2(jnp.where(bit, gap, -gap)))
            operands.append(rows(y).astype(BF16))
        qs = rows([a * b for a, b in zip(qg, q_in)]).astype(BF16)
        return k_leave, decay, qs, o_diag, operands

    def within_chunk(d, operands):
        t_idx = lax.broadcasted_iota(jnp.int32, (c, c), 0)
        s_idx = lax.broadcasted_iota(jnp.int32, (c, c), 1)
        split = t_idx ^ s_idx
        att = None
        for w, yb in reversed(list(zip(SCAN_LEVELS, operands))):
            scores = _dot_nt(yb, yb)
            att = scores if att is None else jnp.where(split < 2 * w, scores, att)
        return jnp.where(t_idx > s_idx if d == 0 else t_idx < s_idx, att, 0.0)

    st_ref[...] = jnp.zeros_like(st_ref)

    def ctx_body(i, carry):
        for d in range(2):
            ci = i if d == 0 else n_ctx - 1 - i
            rows_i = pl.ds(pl.multiple_of(ci * c, c), c)
            k_leave, decay, _, _, _ = chunk(d, cz_refs[d][rows_i, :], None, None, False)
            st_ref[d] = st_ref[d] * decay + _dot_tn(cv_ref[rows_i, :].astype(BF16), k_leave)
        return carry

    lax.fori_loop(0, n_ctx, ctx_body, 0)

    def prepare_body(i, carry):
        for d in range(2):
            rows_i = pl.ds(pl.multiple_of(i * c, c), c)
            k_leave, decay, qs, o_diag, operands = chunk(d, z_refs[d][rows_i, :], q_ref[rows_i, :],
                                                         v_ref[rows_i, :], True)
            kl_ref[d, rows_i, :] = k_leave
            dec_ref[d, i] = decay
            qs_ref[d, rows_i, :] = qs
            out_refs[d][rows_i, :] = o_diag
            for lvl, yb in enumerate(operands):
                lv_ref[d, lvl, rows_i, :] = yb
        return carry

    lax.fori_loop(0, n_chunks, prepare_body, 0, unroll=2)

    def scores_body(i, carry):
        rows_i = pl.ds(pl.multiple_of(i * c, c), c)
        vb = v_ref[rows_i, :].astype(BF16)
        for d in range(2):
            inc_ref[d, i] = _dot_tn(vb, kl_ref[d, rows_i, :])
            operands = [lv_ref[d, lvl, rows_i, :] for lvl in range(len(SCAN_LEVELS))]
            att_ref[d, rows_i, :] = within_chunk(d, operands).astype(BF16)
        return carry

    lax.fori_loop(0, n_chunks, scores_body, 0, unroll=4)

    def carry_body(i, carry):
        for d in range(2):
            ci = i if d == 0 else n_chunks - 1 - i
            rows_i = pl.ds(pl.multiple_of(ci * c, c), c)
            state = st_ref[d]
            out_refs[d][rows_i, :] += (_dot(att_ref[d, rows_i, :], v_ref[rows_i, :].astype(BF16))
                                       + _dot_nt(qs_ref[d, rows_i, :], state.astype(BF16)))
            st_ref[d] = state * dec_ref[d, ci] + inc_ref[d, ci]
        return carry

    lax.fori_loop(0, n_chunks, carry_body, 0, unroll=4)

    rb = 256
    gn = gn_ref[...]

    def read_body(i, carry):
        rows = pl.ds(pl.multiple_of(i * rb, rb), rb)
        o = of_ref[rows, :] + ob_ref[rows, :]
        ms = jnp.mean(o * o, axis=-1, keepdims=True)
        on = o * lax.rsqrt(ms + EPS) * gn
        g = g_ref[rows, :]
        o_ref[rows, :] = (on * (g * jax.nn.sigmoid(g))).astype(o_ref.dtype)
        return carry

    lax.fori_loop(0, SEQ // rb, read_body, 0)


def _hgrn_scan(p, cp, lb_logits, gnorm_g, layer):
    h = A_HEADS
    hd = A_HEAD_DIM
    n_lb = lb_logits.shape[1]

    def col(k):
        return pl.BlockSpec((SEQ, hd), lambda b, j, k=k: (b, k * h + j))

    def ccol(k):
        return pl.BlockSpec((CTX_LEN, hd), lambda b, j, k=k: (b, k * h + j))

    return pl.pallas_call(
        functools.partial(_scan_kernel, layer=layer),
        grid=(BATCH, h),
        in_specs=[col(0), col(1), col(2), col(3), col(4), ccol(0), ccol(1), ccol(2),
                  pl.BlockSpec((2, n_lb, hd), lambda b, j: (0, 0, j)),
                  pl.BlockSpec((1, hd), lambda b, j: (0, j))],
        out_specs=pl.BlockSpec((SEQ, hd), lambda b, j: (b, j)),
        out_shape=jax.ShapeDtypeStruct((TOKENS, A_WIDTH), BF16),
        scratch_shapes=[pltpu.VMEM((SEQ, hd), F32), pltpu.VMEM((SEQ, hd), F32),
                        pltpu.VMEM((2, hd, hd), F32), pltpu.VMEM((2, SEQ, hd), BF16),
                        pltpu.VMEM((2, SEQ // SCAN_CHUNK, hd, hd), F32),
                        pltpu.VMEM((2, SEQ // SCAN_CHUNK, 1, hd), F32),
                        pltpu.VMEM((2, SEQ, hd), BF16),
                        pltpu.VMEM((2, len(SCAN_LEVELS), SEQ, hd), BF16),
                        pltpu.VMEM((2, SEQ, SCAN_CHUNK), BF16)],
        compiler_params=_params("arbitrary", "arbitrary"),
        name="hgrn_scan",
    )(p, p, p, p, p, cp, cp, cp, lb_logits, gnorm_g.reshape(1, A_WIDTH))


def _filter_kernel(z_ref, w1_ref, b1_ref, w2_ref, b2_ref, w3_ref, b3_ref, fr_ref, w4f_ref, w4b_ref,
                   sum_ref, diff_ref, hid_ref, *, tc):
    o = pl.program_id(0)
    j = pl.program_id(1)

    @pl.when((o == 0) & (j == 0))
    def _():
        fr = fr_ref[...]
        h = jnp.sin(fr * (_dot_f32(z_ref[...], w1_ref[...]) + b1_ref[...]))
        h = jnp.sin(fr * (_dot_f32(h, w2_ref[...]) + b2_ref[...]))
        hid_ref[...] = jnp.sin(fr * (_dot_f32(h, w3_ref[...]) + b3_ref[...]))

    hid = hid_ref[...]
    rows = lax.broadcasted_iota(jnp.int32, (SEQ, tc), 0)
    chan = lax.broadcasted_iota(jnp.int32, (SEQ, tc), 1) + j * tc
    t = rows.astype(F32) * (1.0 / (SEQ - 1))
    max_decay = math.log(HY_DECAY_TARGET) / HY_FAST_PCT
    min_decay = math.log(HY_DECAY_TARGET) / HY_SLOW_PCT
    deltas = jnp.abs(min_decay + chan.astype(F32) * ((max_decay - min_decay) / (D_MODEL - 1)))
    window = jnp.exp(-t * deltas)
    fw = _dot_f32(hid, w4f_ref[...]) * window
    bw = _dot_f32(hid, w4b_ref[...]) * window
    first = rows == 0
    a = fw + jnp.where(first, bw, 0.0)
    bb = jnp.where(first, 0.0, bw)
    inv = 1.0 / jnp.sum(jnp.abs(a) + jnp.abs(bb), axis=0, keepdims=True)
    sum_ref[...] = ((a + bb) * inv).astype(sum_ref.dtype)
    diff_ref[...] = ((a - bb) * inv).astype(diff_ref.dtype)


def _hyena_filter_taps(fw1, fb1, fw2, fb2, fw3, fb3, fw4, freq):
    l = SEQ
    pos = jnp.arange(l, dtype=F32)
    t = jnp.linspace(0.0, 1.0, l, dtype=F32)
    w = 2.0 * math.pi * pos / l
    bands = jnp.linspace(1e-4, HY_BANDS - 1, HY_BANDS, dtype=F32)
    ang = w[:, None] * bands[None, :]
    z = jnp.concatenate([t[:, None], jnp.cos(ang), -jnp.sin(ang)], axis=-1)
    tc = 256
    nj = D_MODEL // tc
    hh = 128

    def pad(a, rows, cols):
        return jnp.pad(a, ((0, rows - a.shape[0]), (0, cols - a.shape[1])))

    def small(shape):
        return pl.BlockSpec(shape, lambda o, j: (0, 0))

    out_sds = jax.ShapeDtypeStruct((l, 2 * D_MODEL), BF16)
    out_spec = pl.BlockSpec((l, tc), lambda o, j: (0, o * nj + j))
    fw4p = pad(fw4, hh, fw4.shape[1])
    return pl.pallas_call(
        functools.partial(_filter_kernel, tc=tc),
        grid=(2, nj),
        in_specs=[small((l, hh)), small((hh, hh)), small((1, hh)), small((hh, hh)), small((1, hh)),
                  small((hh, hh)), small((1, hh)), small((1, hh)),
                  pl.BlockSpec((hh, tc), lambda o, j: (0, 2 * o * nj + j)),
                  pl.BlockSpec((hh, tc), lambda o, j: (0, (2 * o + 1) * nj + j))],
        out_specs=(out_spec, out_spec),
        out_shape=(out_sds, out_sds),
        scratch_shapes=[pltpu.VMEM((l, hh), F32)],
        compiler_params=_params("arbitrary", "arbitrary"),
        name="hyena_filter",
    )(pad(z, l, hh), pad(fw1, hh, hh), pad(fb1[None, :], 1, hh), pad(fw2, hh, hh), pad(fb2[None, :], 1, hh),
      pad(fw3, hh, hh), pad(fb3[None, :], 1, hh), pad(freq[None, :], 1, hh), fw4p, fw4p)


def _dft_matrices():
    l = SEQ
    period = 4 * l

    def trig(n):
        ang = (n % period).astype(F32) * (math.pi / (2 * l))
        return jnp.cos(ang), jnp.sin(ang)

    lanes = 128
    groups = l // lanes
    row = jnp.arange(l, dtype=jnp.int32)[:, None]
    lo = jnp.arange(lanes, dtype=jnp.int32)[None, :]
    hi = jnp.arange(groups, dtype=jnp.int32)[None, :]

    def combine(c_hi, s_hi, c_lo, s_lo, scale):
        cm = (c_hi[:, :, None] * c_lo[:, None, :] - s_hi[:, :, None] * s_lo[:, None, :]).reshape(l, l)
        sm = (s_hi[:, :, None] * c_lo[:, None, :] + c_hi[:, :, None] * s_lo[:, None, :]).reshape(l, l)
        return (cm * scale).astype(BF16), (sm * scale).astype(BF16)

    odd = 2 * row + 1
    c_hi, s_hi = trig(odd * (lanes * hi))
    c_lo, s_lo = trig(odd * lo)
    c_fm, s_fm = combine(c_hi, s_hi, c_lo, s_lo, 1.0)
    c_hi, s_hi = trig(row * (2 * lanes * hi))
    c_lo, s_lo = trig(row * (2 * lo + 1))
    c_mf, s_mf = combine(c_hi, s_hi, c_lo, s_lo, 1.0 / l)
    return c_fm, s_fm, c_mf, s_mf


def _dft_fwd_kernel(c_ref, s_ref, z_ref, p_ref, q_ref, u_ref, v_ref):
    z = z_ref[...]
    a = _dot(c_ref[...], z)
    b = _dot(s_ref[...], z)
    p = p_ref[...]
    q = q_ref[...]
    u_ref[...] = (a * p - b * q).astype(u_ref.dtype)
    v_ref[...] = (a * q + b * p).astype(v_ref.dtype)


def _dft_fwd(c_fm, s_fm, z_arr, z_col0, pq_p, pq_q, order):
    tm, tn = 1024, 512
    ni, nj = SEQ // tm, D_MODEL // tn
    zc = z_col0 // tn
    out_sds = jax.ShapeDtypeStruct((TOKENS, D_MODEL), BF16)
    out_spec = pl.BlockSpec((tm, tn), lambda j, i, b: (b * ni + i, j))
    return pl.pallas_call(
        _dft_fwd_kernel,
        grid=(nj, ni, BATCH),
        in_specs=[pl.BlockSpec((tm, SEQ), lambda j, i, b: (i, 0)),
                  pl.BlockSpec((tm, SEQ), lambda j, i, b: (i, 0)),
                  pl.BlockSpec((SEQ, tn), lambda j, i, b: (b, zc + j)),
                  pl.BlockSpec((tm, tn), lambda j, i, b: (i, order * nj + j)),
                  pl.BlockSpec((tm, tn), lambda j, i, b: (i, order * nj + j))],
        out_specs=(out_spec, out_spec),
        out_shape=(out_sds, out_sds),
        compiler_params=_params("arbitrary", "arbitrary", "arbitrary"),
        name="dft_analysis",
    )(c_fm, s_fm, z_arr, pq_p, pq_q)


def _dft_inv_kernel(c_ref, s_ref, u_ref, v_ref, z_ref, gate_ref, skip_ref, o_ref):
    y = _dot(c_ref[...], u_ref[...]) + _dot(s_ref[...], v_ref[...])
    y = y + z_ref[...].astype(F32) * skip_ref[...]
    o_ref[...] = (gate_ref[...].astype(F32) * y).astype(o_ref.dtype)


def _dft_inv(c_mf, s_mf, u, v, z_arr, z_col0, gate_arr, gate_col0, skip):
    tm, tn = 1024, 512
    ni, nj = SEQ // tm, D_MODEL // tn
    zc, gc = z_col0 // tn, gate_col0 // tn
    return pl.pallas_call(
        _dft_inv_kernel,
        grid=(nj, BATCH, ni),
        in_specs=[pl.BlockSpec((tm, SEQ), lambda j, b, i: (i, 0)),
                  pl.BlockSpec((tm, SEQ), lambda j, b, i: (i, 0)),
                  pl.BlockSpec((SEQ, tn), lambda j, b, i: (b, j)),
                  pl.BlockSpec((SEQ, tn), lambda j, b, i: (b, j)),
                  pl.BlockSpec((tm, tn), lambda j, b, i: (b * ni + i, zc + j)),
                  pl.BlockSpec((tm, tn), lambda j, b, i: (b * ni + i, gc + j)),
                  pl.BlockSpec((1, tn), lambda j, b, i: (0, j))],
        out_specs=pl.BlockSpec((tm, tn), lambda j, b, i: (b * ni + i, j)),
        out_shape=jax.ShapeDtypeStruct((TOKENS, D_MODEL), BF16),
        compiler_params=_params("arbitrary", "arbitrary", "arbitrary"),
        name="dft_synthesis",
    )(c_mf, s_mf, u, v, z_arr, gate_arr, skip.reshape(1, D_MODEL))


def _mlp_block(x2d, mod, l, norm_g, w1, w2):
    sh2, sc2, g2 = (mod[l, :BATCH, k * D_MODEL:(k + 1) * D_MODEL] for k in (3, 4, 5))
    h = _prep(x2d.reshape(BATCH, SEQ, D_MODEL), norm_g[l, 1], sc2, sh2, BF16, 512).reshape(TOKENS, D_MODEL)
    hid = _mm_plain(h, w1, D_FF, BF16, _epi_relu2, 1024, 1024, "mlp_up", w_layer=l)
    return _mm_residual([hid], w2, l, x2d, g2, 512, 512, "mlp_down", w_single_buffer=True)


def kernel(x, c, ctx, c_ctx, ada_w, ada_b, norm_g, lb_logits, ab_w_in, ab_conv_w, ab_gnorm_g, ab_w_out,
           hy_in_w, hy_short_w, hy_out_w, hy_fw1, hy_fb1, hy_fw2, hy_fb2, hy_fw3, hy_fb3, hy_fw4, hy_freq,
           hy_skip, mlp_w1, mlp_w2, final_g):
    d = D_MODEL
    cond = jnp.concatenate([c, c_ctx[None, :], jnp.zeros((3, d), F32)], axis=0)
    mod = _ada(cond, ada_w, ada_b)
    x2d = x.reshape(TOKENS, d)

    sh1, sc1, g1 = (mod[0, :BATCH, k * d:(k + 1) * d] for k in (0, 1, 2))
    h = _prep(x, norm_g[0, 0], sc1, sh1, BF16, 512).reshape(TOKENS, d)
    csh = jnp.broadcast_to(mod[0, BATCH:BATCH + 1, 0:d], (BATCH, d))
    csc = jnp.broadcast_to(mod[0, BATCH:BATCH + 1, d:2 * d], (BATCH, d))
    hc = _prep(ctx, norm_g[0, 0], csc, csh, BF16, CTX_LEN).reshape(BATCH * CTX_LEN, d)
    p = _mm_plain(h, ab_w_in, AB_IN_WIDTH, F32, _epi_store, 1024, 1024, "ab_in_proj", w_layer=0)
    cp = _mm_plain(hc, ab_w_in, 3 * A_WIDTH, F32, _epi_store, 1024, 1024, "ab_ctx_proj", w_layer=0)
    mix_a = _hgrn_scan(p, cp, lb_logits, ab_gnorm_g[0], 0)
    mix_b = _gconv(p, ab_conv_w[0])
    x2d = _mm_residual([mix_a, mix_b], ab_w_out, 0, x2d, g1, 1024, 1024, "ab_out_proj")
    x2d = _mlp_block(x2d, mod, 0, norm_g, mlp_w1, mlp_w2)

    sh1, sc1, g1 = (mod[1, :BATCH, k * d:(k + 1) * d] for k in (0, 1, 2))
    h = _prep(x2d.reshape(BATCH, SEQ, d), norm_g[1, 0], sc1, sh1, BF16, 512).reshape(TOKENS, d)
    pc = _mm([h], hy_in_w, w_layer=0, tm=1024, tn=1024, n_cols=3 * d, epi=_epi_conv3_grid,
             extras=(hy_short_w[0],), extra_specs=(pl.BlockSpec((3, 1024), lambda j, i: (0, j)),),
             out_shapes=jax.ShapeDtypeStruct((TOKENS, 3 * d), BF16),
             out_specs=pl.BlockSpec((1024, 1024), lambda j, i: (i, j)), name="hy_in_proj")
    taps_sum, taps_diff = _hyena_filter_taps(hy_fw1[0], hy_fb1[0], hy_fw2[0], hy_fb2[0], hy_fw3[0],
                                             hy_fb3[0], hy_fw4[0], hy_freq[0])
    c_fm, s_fm, c_mf, s_mf = _dft_matrices()
    filt_p = _mm_plain(c_fm, taps_sum, 2 * d, F32, _epi_store, 1024, 1024, "filter_dft_cos")
    filt_q = _mm_plain(s_fm, taps_diff, 2 * d, F32, _epi_store, 1024, 1024, "filter_dft_sin")
    u, v = _dft_fwd(c_fm, s_fm, pc, 2 * d, filt_p, filt_q, 0)
    z = _dft_inv(c_mf, s_mf, u, v, pc, 2 * d, pc, 0, hy_skip[0, 0])
    u, v = _dft_fwd(c_fm, s_fm, z, 0, filt_p, filt_q, 1)
    z = _dft_inv(c_mf, s_mf, u, v, z, 0, pc, d, hy_skip[0, 1])
    x2d = _mm_residual([z], hy_out_w, 0, x2d, g1, 1024, 1024, "hy_out_proj")
    x2d = _mlp_block(x2d, mod, 1, norm_g, mlp_w1, mlp_w2)

    zeros = jnp.zeros((BATCH, d), F32)
    return _prep(x2d.reshape(BATCH, SEQ, d), final_g, zeros, zeros, F32, 512)
```

```python
import functools
import math

import numpy as np
import jax
import jax.numpy as jnp
from jax import lax
from jax.experimental import pallas as pl
from jax.experimental.pallas import tpu as pltpu

F32 = jnp.float32
BF16 = jnp.bfloat16

D_MODEL = 2048
BATCH = 4
SEQ = 2048
CTX_LEN = 256
GRID_W = 64
EPS = 1e-6
A_HEAD_DIM = 128
A_WIDTH = D_MODEL // 2
A_HEADS = A_WIDTH // A_HEAD_DIM
B_WIDTH = D_MODEL - A_WIDTH
AB_IN_WIDTH = 5 * A_WIDTH + 3 * B_WIDTH
HY_EMB = 33
HY_BANDS = (HY_EMB - 1) // 2
HY_HIDDEN = 64
HY_DECAY_TARGET = 1e-2
HY_FAST_PCT = 0.3
HY_SLOW_PCT = 1.5
D_FF = 4 * D_MODEL
TOKENS = BATCH * SEQ

SCAN_CHUNK = 64
SCAN_LEVELS = (1, 2, 4, 8, 16, 32)

VMEM_LIMIT_BYTES = 56 * 1024 * 1024


def _params(*sem):
    return pltpu.CompilerParams(dimension_semantics=sem, vmem_limit_bytes=VMEM_LIMIT_BYTES)


def _dot(a, b):
    return jnp.dot(a, b, preferred_element_type=F32)


def _dot_nt(a, b):
    return lax.dot_general(a, b, (((1,), (1,)), ((), ())), preferred_element_type=F32)


def _dot_tn(a, b):
    return lax.dot_general(a, b, (((0,), (0,)), ((), ())), preferred_element_type=F32)


def _dot_f32(a, b):
    return jnp.dot(a, b, precision=lax.Precision.HIGHEST, preferred_element_type=F32)


def _ada_kernel(c_ref, w_ref, b_ref, o_ref):
    c = c_ref[...]
    a = (c * jax.nn.sigmoid(c)).astype(BF16)
    o_ref[...] = _dot(a, w_ref[...].astype(BF16)) + b_ref[...]


def _ada(cond, ada_w, ada_b):
    depth, d, n = ada_w.shape
    tn = 1024
    return pl.pallas_call(
        _ada_kernel,
        grid=(depth, n // tn),
        in_specs=[
            pl.BlockSpec((8, d), lambda l, j: (0, 0)),
            pl.BlockSpec((None, d, tn), lambda l, j: (l, 0, j)),
            pl.BlockSpec((None, 1, tn), lambda l, j: (l, 0, j)),
        ],
        out_specs=pl.BlockSpec((None, 8, tn), lambda l, j: (l, 0, j)),
        out_shape=jax.ShapeDtypeStruct((depth, 8, n), F32),
        compiler_params=_params("arbitrary", "arbitrary"),
        name="ada_mod",
    )(cond, ada_w, ada_b.reshape(depth, 1, n))


def _prep_kernel(x_ref, g_ref, sc_ref, sh_ref, o_ref):
    x = x_ref[...]
    ms = jnp.mean(x * x, axis=-1, keepdims=True)
    xn = x * lax.rsqrt(ms + EPS) * g_ref[...]
    o_ref[...] = (xn * (1.0 + sc_ref[...]) + sh_ref[...]).astype(o_ref.dtype)


def _prep(x, g, sc, sh, out_dtype, ts):
    b, l, d = x.shape
    return pl.pallas_call(
        _prep_kernel,
        grid=(b, l // ts),
        in_specs=[
            pl.BlockSpec((None, ts, d), lambda i, j: (i, j, 0)),
            pl.BlockSpec((1, d), lambda i, j: (0, 0)),
            pl.BlockSpec((None, 1, d), lambda i, j: (i, 0, 0)),
            pl.BlockSpec((None, 1, d), lambda i, j: (i, 0, 0)),
        ],
        out_specs=pl.BlockSpec((None, ts, d), lambda i, j: (i, j, 0)),
        out_shape=jax.ShapeDtypeStruct((b, l, d), out_dtype),
        compiler_params=_params("arbitrary", "arbitrary"),
        name="norm_mod",
    )(x, g.reshape(1, d), sc.reshape(b, 1, d), sh.reshape(b, 1, d))


def _mm_kernel(*refs, n_a, k_sizes, epi, cast_w):
    a_refs = refs[:n_a]
    w_ref = refs[n_a]
    rest = refs[n_a + 1:]
    if cast_w:
        w_ref, rest = rest[-1], rest[:-1]

        @pl.when(pl.program_id(1) == 0)
        def _():
            w_ref[...] = refs[n_a][...].astype(BF16)

    acc = None
    off = 0
    for a_ref, k in zip(a_refs, k_sizes):
        part = _dot(a_ref[...], w_ref[off:off + k, :])
        acc = part if acc is None else acc + part
        off += k
    epi(acc, *rest)


def _mm(a_list, w, *, tm, tn, n_cols, epi, extras, extra_specs, out_shapes, out_specs, name,
        w_layer=None, w_single_buffer=False):
    m = a_list[0].shape[0]
    k_sizes = tuple(a.shape[1] for a in a_list)
    k_total = sum(k_sizes)
    cast_w = w_layer is not None
    assert w.shape[-2] == k_total and m % tm == 0 and n_cols % tn == 0
    in_specs = [pl.BlockSpec((tm, k), lambda j, i: (i, 0)) for k in k_sizes]
    w_mode = dict(pipeline_mode=pl.Buffered(1)) if w_single_buffer else {}
    if cast_w:
        in_specs.append(pl.BlockSpec((None, k_total, tn), lambda j, i: (w_layer, 0, j), **w_mode))
    else:
        in_specs.append(pl.BlockSpec((k_total, tn), lambda j, i: (0, j), **w_mode))
    in_specs.extend(extra_specs)
    return pl.pallas_call(
        functools.partial(_mm_kernel, n_a=len(a_list), k_sizes=k_sizes, epi=epi, cast_w=cast_w),
        grid=(n_cols // tn, m // tm),
        in_specs=in_specs,
        out_specs=out_specs,
        out_shape=out_shapes,
        scratch_shapes=[pltpu.VMEM((k_total, tn), BF16)] if cast_w else [],
        compiler_params=_params("arbitrary", "arbitrary"),
        name=name,
    )(*a_list, w, *extras)


def _epi_store(acc, o_ref):
    o_ref[...] = acc.astype(o_ref.dtype)


def _epi_relu2(acc, o_ref):
    r = jnp.maximum(acc, 0.0)
    o_ref[...] = (r * r).astype(o_ref.dtype)


def _epi_residual(acc, x_ref, gate_ref, o_ref):
    o_ref[...] = x_ref[...] + gate_ref[...] * acc


def _epi_conv3_grid(acc, w_ref, o_ref):
    rows = lax.broadcasted_iota(jnp.int32, acc.shape, 0) % GRID_W
    prev = jnp.where(rows == 0, 0.0, pltpu.roll(acc, 1, 0))
    nxt = jnp.where(rows == GRID_W - 1, 0.0, pltpu.roll(acc, acc.shape[0] - 1, 0))
    w = w_ref[...]
    o_ref[...] = (w[0:1] * prev + w[1:2] * acc + w[2:3] * nxt).astype(o_ref.dtype)


def _mm_plain(a, w, n_cols, out_dtype, epi, tm, tn, name, w_layer=None):
    m = a.shape[0]
    return _mm([a], w, tm=tm, tn=tn, n_cols=n_cols, epi=epi, extras=(), extra_specs=(),
               out_shapes=jax.ShapeDtypeStruct((m, n_cols), out_dtype),
               out_specs=pl.BlockSpec((tm, tn), lambda j, i: (i, j)), name=name, w_layer=w_layer)


def _mm_residual(a_list, w, w_layer, x, gate, tm, tn, name, w_single_buffer=False):
    m, n = x.shape
    tiles_per_batch = SEQ // tm
    return _mm(a_list, w, tm=tm, tn=tn, n_cols=n, epi=_epi_residual,
               extras=(x, gate.reshape(BATCH, 1, n)),
               extra_specs=(pl.BlockSpec((tm, tn), lambda j, i: (i, j)),
                            pl.BlockSpec((None, 1, tn), lambda j, i: (i // tiles_per_batch, 0, j))),
               out_shapes=jax.ShapeDtypeStruct((m, n), F32),
               out_specs=pl.BlockSpec((tm, tn), lambda j, i: (i, j)), name=name,
               w_layer=w_layer, w_single_buffer=w_single_buffer)


def _gconv_kernel(u_ref, gb_ref, gc_ref, w_ref, o_ref):
    t = gc_ref[...] * u_ref[...]
    rows = lax.broadcasted_iota(jnp.int32, t.shape, 0) % GRID_W
    prev = jnp.where(rows == 0, 0.0, pltpu.roll(t, 1, 0))
    nxt = jnp.where(rows == GRID_W - 1, 0.0, pltpu.roll(t, t.shape[0] - 1, 0))
    w = w_ref[...]
    o_ref[...] = (gb_ref[...] * (w[0:1] * prev + w[1:2] * t + w[2:3] * nxt)).astype(o_ref.dtype)


def _gconv(p, conv_w):
    tm, tn = 512, 512
    nb = B_WIDTH // tn
    base = 5 * A_WIDTH // tn
    return pl.pallas_call(
        _gconv_kernel,
        grid=(TOKENS // tm, nb),
        in_specs=[
            pl.BlockSpec((tm, tn), lambda i, j: (i, base + j)),
            pl.BlockSpec((tm, tn), lambda i, j: (i, base + nb + j)),
            pl.BlockSpec((tm, tn), lambda i, j: (i, base + 2 * nb + j)),
            pl.BlockSpec((3, tn), lambda i, j: (0, j)),
        ],
        out_specs=pl.BlockSpec((tm, tn), lambda i, j: (i, j)),
        out_shape=jax.ShapeDtypeStruct((TOKENS, B_WIDTH), BF16),
        compiler_params=_params("arbitrary", "arbitrary"),
        name="gated_conv",
    )(p, p, p, conv_w)


SUBLANES = 8
LANES = 128


def _row_groups(x):
    return [x[k:k + SUBLANES] for k in range(0, x.shape[0], SUBLANES)]


def _cumsum_groups(groups, sub):
    out, carry = [], None
    for g in groups:
        for s in (1, 2, 4):
            g = g + jnp.where(sub >= s, pltpu.roll(g, s, 0), 0.0)
        if carry is not None:
            g = g + carry
        out.append(g)
        carry = g[SUBLANES - 1:SUBLANES]
    return out


def _hold_boundary(cum, w, k, sub):
    shape = cum[k].shape
    if w >= SUBLANES // 2:
        r = (k * SUBLANES // (2 * w)) * 2 * w + w - 1
        return jnp.broadcast_to(cum[r // SUBLANES][r % SUBLANES:r % SUBLANES + 1], shape)
    assert w == 2
    return jnp.where(sub < 4, jnp.broadcast_to(cum[k][1:2], shape), jnp.broadcast_to(cum[k][5:6], shape))


def _scan_kernel(zf_ref, zb_ref, v_ref, q_ref, g_ref, czf_ref, czb_ref, cv_ref, lbl_ref, gn_ref,
                 o_ref, of_ref, ob_ref, st_ref, qs_ref, inc_ref, dec_ref, kl_ref, lv_ref, att_ref, *, layer):
    c = SCAN_CHUNK
    n_chunks = SEQ // c
    n_ctx = CTX_LEN // c
    z_refs = (zf_ref, zb_ref)
    cz_refs = (czf_ref, czb_ref)
    out_refs = (of_ref, ob_ref)

    logit_rows = [lbl_ref[:, k, :] for k in range(lbl_ref.shape[1])]
    top = functools.reduce(jnp.maximum, logit_rows)
    exps = [jnp.exp(r - top) for r in logit_rows]
    lbs = sum(exps[:layer + 1]) / sum(exps)

    sub = lax.broadcasted_iota(jnp.int32, (SUBLANES, A_HEAD_DIM), 0)
    n_groups = c // SUBLANES

    def rows(groups):
        return jnp.concatenate(groups, axis=0)

    def chunk(d, z, q, v, want_out):
        lb = lbs[d:d + 1]
        f = lb + (1.0 - lb) * jax.nn.sigmoid(z)
        kk = 1.0 - f
        fg, kg = _row_groups(f), _row_groups(kk)
        lg = [jnp.log2(g) for g in fg]
        cum = _cumsum_groups(lg, sub)
        total = cum[-1][SUBLANES - 1:SUBLANES]
        if d == 0:
            pos = cum
            q_in = [jnp.exp2(g) for g in cum]
            k_out = [jnp.exp2(total - g) for g in cum]
        else:
            pos = [a - b for a, b in zip(cum, lg)]
            q_in = [jnp.exp2(total - g) for g in pos]
            k_out = [jnp.exp2(g) for g in pos]
        k_leave = rows([a * b for a, b in zip(kg, k_out)]).astype(BF16)
        decay = jnp.exp2(total)
        if not want_out:
            return k_leave, decay, None, None, None
        qg = _row_groups(q)
        o_diag = jnp.sum(q * kk, axis=1, keepdims=True) * v
        upper, lower = (qg, kg) if d == 0 else (kg, qg)
        operands = []
        for w in SCAN_LEVELS:
            y = []
            for k in range(n_groups):
                if w == 1:
                    odd = (sub & 1) != 0
                    y.append(jnp.where(odd, qg[k] * fg[k], kg[k]) if d == 0
                             else jnp.where(odd, kg[k], qg[k] * fg[k]))
                    continue
                gap = pos[k] - _hold_boundary(cum, w, k, sub)
                if w >= SUBLANES:
                    bit_set = (k * SUBLANES) & w != 0
                    y.append(upper[k] * jnp.exp2(gap) if bit_set else lower[k] * jnp.exp2(-gap))
                else:
                    bit = (sub & w) != 0
                    y.append(jnp.where(bit, upper[k], lower[k]) * jnp.exp2(jnp.where(bit, gap, -gap)))
            operands.append(rows(y).astype(BF16))
        qs = rows([a * b for a, b in zip(qg, q_in)]).astype(BF16)
        return k_leave, decay, qs, o_diag, operands

    def within_chunk(d, operands):
        t_idx = lax.broadcasted_iota(jnp.int32, (c, c), 0)
        s_idx = lax.broadcasted_iota(jnp.int32, (c, c), 1)
        split = t_idx ^ s_idx
        att = None
        for w, yb in reversed(list(zip(SCAN_LEVELS, operands))):
            scores = _dot_nt(yb, yb)
            att = scores if att is None else jnp.where(split < 2 * w, scores, att)
        return jnp.where(t_idx > s_idx if d == 0 else t_idx < s_idx, att, 0.0)

    st_ref[...] = jnp.zeros_like(st_ref)

    def ctx_body(i, carry):
        for d in range(2):
            ci = i if d == 0 else n_ctx - 1 - i
            rows_i = pl.ds(pl.multiple_of(ci * c, c), c)
            k_leave, decay, _, _, _ = chunk(d, cz_refs[d][rows_i, :], None, None, False)
            st_ref[d] = st_ref[d] * decay + _dot_tn(cv_ref[rows_i, :].astype(BF16), k_leave)
        return carry

    lax.fori_loop(0, n_ctx, ctx_body, 0)

    def chunk_rows(ci):
        start = ci * c
        return pl.ds(start if isinstance(ci, int) else pl.multiple_of(start, c), c)

    def prepare(ci):
        rows_i = chunk_rows(ci)
        for d in range(2):
            k_leave, decay, qs, o_diag, operands = chunk(d, z_refs[d][rows_i, :], q_ref[rows_i, :],
                                                         v_ref[rows_i, :], True)
            kl_ref[d, rows_i, :] = k_leave
            dec_ref[d, ci] = decay
            qs_ref[d, rows_i, :] = qs
            out_refs[d][rows_i, :] = o_diag
            for lvl, yb in enumerate(operands):
                lv_ref[d, lvl, rows_i, :] = yb

    def scores(ci):
        rows_i = chunk_rows(ci)
        vb = v_ref[rows_i, :].astype(BF16)
        for d in range(2):
            inc_ref[d, ci] = _dot_tn(vb, kl_ref[d, rows_i, :])
            operands = [lv_ref[d, lvl, rows_i, :] for lvl in range(len(SCAN_LEVELS))]
            att_ref[d, rows_i, :] = within_chunk(d, operands).astype(BF16)

    pair = 2
    for j in range(pair):
        prepare(j)

    def skewed_body(i, carry):
        for j in range(pair):
            scores((i - 1) * pair + j)
        for j in range(pair):
            prepare(i * pair + j)
        return carry

    lax.fori_loop(1, n_chunks // pair, skewed_body, 0)
    for j in range(pair):
        scores(n_chunks - pair + j)

    def carry_body(i, carry):
        for d in range(2):
            ci = i if d == 0 else n_chunks - 1 - i
            rows_i = pl.ds(pl.multiple_of(ci * c, c), c)
            state = st_ref[d]
            out_refs[d][rows_i, :] += (_dot(att_ref[d, rows_i, :], v_ref[rows_i, :].astype(BF16))
                                       + _dot_nt(qs_ref[d, rows_i, :], state.astype(BF16)))
            st_ref[d] = state * dec_ref[d, ci] + inc_ref[d, ci]
        return carry

    lax.fori_loop(0, n_chunks, carry_body, 0, unroll=4)

    rb = 256
    gn = gn_ref[...]

    def read_body(i, carry):
        rows = pl.ds(pl.multiple_of(i * rb, rb), rb)
        o = of_ref[rows, :] + ob_ref[rows, :]
        ms = jnp.mean(o * o, axis=-1, keepdims=True)
        on = o * lax.rsqrt(ms + EPS) * gn
        g = g_ref[rows, :]
        o_ref[rows, :] = (on * (g * jax.nn.sigmoid(g))).astype(o_ref.dtype)
        return carry

    lax.fori_loop(0, SEQ // rb, read_body, 0)


def _hgrn_scan(p, cp, lb_logits, gnorm_g, layer):
    h = A_HEADS
    hd = A_HEAD_DIM
    n_lb = lb_logits.shape[1]

    def col(k):
        return pl.BlockSpec((SEQ, hd), lambda b, j, k=k: (b, k * h + j))

    def ccol(k):
        return pl.BlockSpec((CTX_LEN, hd), lambda b, j, k=k: (b, k * h + j))

    return pl.pallas_call(
        functools.partial(_scan_kernel, layer=layer),
        grid=(BATCH, h),
        in_specs=[col(0), col(1), col(2), col(3), col(4), ccol(0), ccol(1), ccol(2),
                  pl.BlockSpec((2, n_lb, hd), lambda b, j: (0, 0, j)),
                  pl.BlockSpec((1, hd), lambda b, j: (0, j))],
        out_specs=pl.BlockSpec((SEQ, hd), lambda b, j: (b, j)),
        out_shape=jax.ShapeDtypeStruct((TOKENS, A_WIDTH), BF16),
        scratch_shapes=[pltpu.VMEM((SEQ, hd), F32), pltpu.VMEM((SEQ, hd), F32),
                        pltpu.VMEM((2, hd, hd), F32), pltpu.VMEM((2, SEQ, hd), BF16),
                        pltpu.VMEM((2, SEQ // SCAN_CHUNK, hd, hd), F32),
                        pltpu.VMEM((2, SEQ // SCAN_CHUNK, 1, hd), F32),
                        pltpu.VMEM((2, SEQ, hd), BF16),
                        pltpu.VMEM((2, len(SCAN_LEVELS), SEQ, hd), BF16),
                        pltpu.VMEM((2, SEQ, SCAN_CHUNK), BF16)],
        compiler_params=_params("arbitrary", "arbitrary"),
        name="hgrn_scan",
    )(p, p, p, p, p, cp, cp, cp, lb_logits, gnorm_g.reshape(1, A_WIDTH))


def _filter_kernel(z_ref, w1_ref, b1_ref, w2_ref, b2_ref, w3_ref, b3_ref, fr_ref, w4f_ref, w4b_ref,
                   sum_ref, diff_ref, hid_ref, *, tc):
    o = pl.program_id(0)
    j = pl.program_id(1)

    @pl.when((o == 0) & (j == 0))
    def _():
        fr = fr_ref[...]
        h = jnp.sin(fr * (_dot_f32(z_ref[...], w1_ref[...]) + b1_ref[...]))
        h = jnp.sin(fr * (_dot_f32(h, w2_ref[...]) + b2_ref[...]))
        hid_ref[...] = jnp.sin(fr * (_dot_f32(h, w3_ref[...]) + b3_ref[...]))

    hid = hid_ref[...]
    rows = lax.broadcasted_iota(jnp.int32, (SEQ, tc), 0)
    chan = lax.broadcasted_iota(jnp.int32, (SEQ, tc), 1) + j * tc
    lag = jnp.where(rows < SEQ // 2, 2 * rows, 2 * rows - (SEQ - 1))
    t = lag.astype(F32) * (1.0 / (SEQ - 1))
    max_decay = math.log(HY_DECAY_TARGET) / HY_FAST_PCT
    min_decay = math.log(HY_DECAY_TARGET) / HY_SLOW_PCT
    deltas = jnp.abs(min_decay + chan.astype(F32) * ((max_decay - min_decay) / (D_MODEL - 1)))
    window = jnp.exp(-t * deltas)
    fw = _dot_f32(hid, w4f_ref[...]) * window
    bw = _dot_f32(hid, w4b_ref[...]) * window
    first = rows == 0
    a = fw + jnp.where(first, bw, 0.0)
    bb = jnp.where(first, 0.0, bw)
    inv = 1.0 / jnp.sum(jnp.abs(a) + jnp.abs(bb), axis=0, keepdims=True)
    sum_ref[...] = ((a + bb) * inv).astype(sum_ref.dtype)
    diff_ref[...] = ((a - bb) * inv).astype(diff_ref.dtype)


def _hyena_filter_taps(fw1, fb1, fw2, fb2, fw3, fb3, fw4, freq):
    l = SEQ
    pos = jnp.arange(l, dtype=F32)
    t = jnp.linspace(0.0, 1.0, l, dtype=F32)
    w = 2.0 * math.pi * pos / l
    bands = jnp.linspace(1e-4, HY_BANDS - 1, HY_BANDS, dtype=F32)
    ang = w[:, None] * bands[None, :]
    z = jnp.concatenate([t[:, None], jnp.cos(ang), -jnp.sin(ang)], axis=-1)
    z = jnp.concatenate([z[0::2], z[1::2]], axis=0)
    tc = 256
    nj = D_MODEL // tc
    hh = 128

    def pad(a, rows, cols):
        return jnp.pad(a, ((0, rows - a.shape[0]), (0, cols - a.shape[1])))

    def small(shape):
        return pl.BlockSpec(shape, lambda o, j: (0, 0))

    out_sds = jax.ShapeDtypeStruct((l, 2 * D_MODEL), BF16)
    out_spec = pl.BlockSpec((l, tc), lambda o, j: (0, o * nj + j))
    fw4p = pad(fw4, hh, fw4.shape[1])
    return pl.pallas_call(
        functools.partial(_filter_kernel, tc=tc),
        grid=(2, nj),
        in_specs=[small((l, hh)), small((hh, hh)), small((1, hh)), small((hh, hh)), small((1, hh)),
                  small((hh, hh)), small((1, hh)), small((1, hh)),
                  pl.BlockSpec((hh, tc), lambda o, j: (0, 2 * o * nj + j)),
                  pl.BlockSpec((hh, tc), lambda o, j: (0, (2 * o + 1) * nj + j))],
        out_specs=(out_spec, out_spec),
        out_shape=(out_sds, out_sds),
        scratch_shapes=[pltpu.VMEM((l, hh), F32)],
        compiler_params=_params("arbitrary", "arbitrary"),
        name="hyena_filter",
    )(pad(z, l, hh), pad(fw1, hh, hh), pad(fb1[None, :], 1, hh), pad(fw2, hh, hh), pad(fb2[None, :], 1, hh),
      pad(fw3, hh, hh), pad(fb3[None, :], 1, hh), pad(freq[None, :], 1, hh), fw4p, fw4p)


def _dft_matrices():
    l = SEQ
    period = 4 * l

    def trig(n):
        ang = (n % period).astype(F32) * (math.pi / (2 * l))
        return jnp.cos(ang), jnp.sin(ang)

    lanes = 128
    groups = l // lanes
    row = jnp.arange(l, dtype=jnp.int32)[:, None]
    lo = jnp.arange(lanes, dtype=jnp.int32)[None, :]
    hi = jnp.arange(groups, dtype=jnp.int32)[None, :]

    def combine(c_hi, s_hi, c_lo, s_lo, scale):
        cm = (c_hi[:, :, None] * c_lo[:, None, :] - s_hi[:, :, None] * s_lo[:, None, :]).reshape(l, l)
        sm = (s_hi[:, :, None] * c_lo[:, None, :] + c_hi[:, :, None] * s_lo[:, None, :]).reshape(l, l)
        return (cm * scale).astype(BF16), (sm * scale).astype(BF16)

    odd = 2 * row + 1
    c_hi, s_hi = trig(odd * (lanes * hi))
    c_lo, s_lo = trig(odd * lo)
    c_fm, s_fm = combine(c_hi, s_hi, c_lo, s_lo, 1.0)
    c_hi, s_hi = trig(row * (2 * lanes * hi))
    c_lo, s_lo = trig(row * (2 * lo + 1))
    c_mf, s_mf = combine(c_hi, s_hi, c_lo, s_lo, 1.0 / l)
    return c_fm, s_fm, c_mf, s_mf


def _dft_fwd_kernel(c_ref, s_ref, z_ref, p_ref, q_ref, u_ref, v_ref):
    z = z_ref[...]
    a = _dot(c_ref[...], z)
    b = _dot(s_ref[...], z)
    p = p_ref[...]
    q = q_ref[...]
    u_ref[...] = (a * p - b * q).astype(u_ref.dtype)
    v_ref[...] = (a * q + b * p).astype(v_ref.dtype)


def _dft_fwd(c_fm, s_fm, z_arr, z_col0, pq_p, pq_q, order):
    tm, tn = 1024, 512
    ni, nj = SEQ // tm, D_MODEL // tn
    zc = z_col0 // tn
    out_sds = jax.ShapeDtypeStruct((TOKENS, D_MODEL), BF16)
    out_spec = pl.BlockSpec((tm, tn), lambda j, i, b: (b * ni + i, j))
    return pl.pallas_call(
        _dft_fwd_kernel,
        grid=(nj, ni, BATCH),
        in_specs=[pl.BlockSpec((tm, SEQ), lambda j, i, b: (i, 0)),
                  pl.BlockSpec((tm, SEQ), lambda j, i, b: (i, 0)),
                  pl.BlockSpec((SEQ, tn), lambda j, i, b: (b, zc + j)),
                  pl.BlockSpec((tm, tn), lambda j, i, b: (i, order * nj + j)),
                  pl.BlockSpec((tm, tn), lambda j, i, b: (i, order * nj + j))],
        out_specs=(out_spec, out_spec),
        out_shape=(out_sds, out_sds),
        compiler_params=_params("arbitrary", "arbitrary", "arbitrary"),
        name="dft_analysis",
    )(c_fm, s_fm, z_arr, pq_p, pq_q)


def _dft_inv_kernel(c_ref, s_ref, u_ref, v_ref, z_ref, gate_ref, skip_ref, o_ref):
    y = _dot(c_ref[...], u_ref[...]) + _dot(s_ref[...], v_ref[...])
    y = y + z_ref[...].astype(F32) * skip_ref[...]
    o_ref[...] = (gate_ref[...].astype(F32) * y).astype(o_ref.dtype)


def _dft_inv(c_mf, s_mf, u, v, z_arr, z_col0, gate_arr, gate_col0, skip):
    tm, tn = 1024, 512
    ni, nj = SEQ // tm, D_MODEL // tn
    zc, gc = z_col0 // tn, gate_col0 // tn
    return pl.pallas_call(
        _dft_inv_kernel,
        grid=(nj, BATCH, ni),
        in_specs=[pl.BlockSpec((tm, SEQ), lambda j, b, i: (i, 0)),
                  pl.BlockSpec((tm, SEQ), lambda j, b, i: (i, 0)),
                  pl.BlockSpec((SEQ, tn), lambda j, b, i: (b, j)),
                  pl.BlockSpec((SEQ, tn), lambda j, b, i: (b, j)),
                  pl.BlockSpec((tm, tn), lambda j, b, i: (b * ni + i, zc + j)),
                  pl.BlockSpec((tm, tn), lambda j, b, i: (b * ni + i, gc + j)),
                  pl.BlockSpec((1, tn), lambda j, b, i: (0, j))],
        out_specs=pl.BlockSpec((tm, tn), lambda j, b, i: (b * ni + i, j)),
        out_shape=jax.ShapeDtypeStruct((TOKENS, D_MODEL), BF16),
        compiler_params=_params("arbitrary", "arbitrary", "arbitrary"),
        name="dft_synthesis",
    )(c_mf, s_mf, u, v, z_arr, gate_arr, skip.reshape(1, D_MODEL))


HALF = SEQ // 2


def _trig_table(a_rows, b_hi, b_lo, scale):
    period = 4 * SEQ

    def trig(n):
        ang = (n % period).astype(F32) * (math.pi / (2 * SEQ))
        return jnp.cos(ang), jnp.sin(ang)

    rows = a_rows.shape[0]
    c_hi, s_hi = trig(a_rows[:, None] * b_hi[None, :])
    c_lo, s_lo = trig(a_rows[:, None] * b_lo[None, :])
    cm = (c_hi[:, :, None] * c_lo[:, None, :] - s_hi[:, :, None] * s_lo[:, None, :]).reshape(rows, -1)
    sm = (s_hi[:, :, None] * c_lo[:, None, :] + c_hi[:, :, None] * s_lo[:, None, :]).reshape(rows, -1)
    return (cm * scale).astype(BF16), (sm * scale).astype(BF16)


def _dft_half_tables():
    lanes = 128
    idx = jnp.arange(HALF, dtype=jnp.int32)
    hi = jnp.arange(HALF // lanes, dtype=jnp.int32) * (2 * lanes)
    lo = jnp.arange(lanes, dtype=jnp.int32)
    analysis, synthesis = [], []
    for par in (0, 1):
        analysis.append(_trig_table(2 * idx + 1, hi, 2 * lo + par, 1.0))
        synthesis.append(_trig_table(2 * idx + par, hi, 2 * lo + 1, 1.0 / SEQ))
    (cee, see), (ceo, seo) = analysis
    (tce, tse), (tco, tso) = synthesis
    return (cee, ceo, see, seo), (tce, tse, tco, tso)


def _filter_dft2_kernel(cee_ref, ceo_ref, see_ref, seo_ref, fs_ref, fd_ref, p_ref, pm_ref, q_ref, qm_ref):
    pe = _dot(cee_ref[...], fs_ref[0:HALF, :])
    po = _dot(ceo_ref[...], fs_ref[HALF:SEQ, :])
    qe = _dot(see_ref[...], fd_ref[0:HALF, :])
    qo = _dot(seo_ref[...], fd_ref[HALF:SEQ, :])
    p_ref[...] = pe + po
    pm_ref[...] = pe - po
    q_ref[...] = qe + qo
    qm_ref[...] = qo - qe


def _filter_dft2(tables, taps_sum, taps_diff):
    tm, tn = 512, 512
    n_cols = taps_sum.shape[1]
    mat = pl.BlockSpec((tm, HALF), lambda j, i: (i, 0))
    tap = pl.BlockSpec((SEQ, tn), lambda j, i: (0, j))
    out_spec = pl.BlockSpec((tm, tn), lambda j, i: (i, j))
    out_sds = jax.ShapeDtypeStruct((HALF, n_cols), F32)
    return pl.pallas_call(
        _filter_dft2_kernel,
        grid=(n_cols // tn, HALF // tm),
        in_specs=[mat, mat, mat, mat, tap, tap],
        out_specs=(out_spec,) * 4,
        out_shape=(out_sds,) * 4,
        compiler_params=_params("arbitrary", "arbitrary"),
        name="filter_dft",
    )(*tables, taps_sum, taps_diff)


def _dft2_fwd_kernel(cee_ref, ceo_ref, see_ref, seo_ref, z_ref, p_ref, pm_ref, q_ref, qm_ref,
                     ue_ref, ve_ref, uo_ref, vo_ref, zs_ref, *, slab):
    panels = zs_ref.shape[0]
    for k in range(panels):
        zs_ref[k] = z_ref[:, k * LANES:(k + 1) * LANES].astype(F32)
    ze = jnp.concatenate([zs_ref[k, pl.ds(0, HALF, stride=2), :] for k in range(panels)], axis=1).astype(BF16)
    zo = jnp.concatenate([zs_ref[k, pl.ds(1, HALF, stride=2), :] for k in range(panels)], axis=1).astype(BF16)
    for r in range(0, HALF, slab):
        rows = slice(r, r + slab)
        ec = _dot(cee_ref[rows, :], ze)
        oc = _dot(ceo_ref[rows, :], zo)
        es = _dot(see_ref[rows, :], ze)
        os_ = _dot(seo_ref[rows, :], zo)
        a, am, b, bm = ec + oc, ec - oc, es + os_, os_ - es
        p, pm, q, qm = p_ref[rows, :], pm_ref[rows, :], q_ref[rows, :], qm_ref[rows, :]
        u, v = a * p - b * q, a * q + b * p
        um, vm = am * pm - bm * qm, am * qm + bm * pm
        ue_ref[rows, :] = (u + um).astype(ue_ref.dtype)
        uo_ref[rows, :] = (u - um).astype(uo_ref.dtype)
        ve_ref[rows, :] = (v - vm).astype(ve_ref.dtype)
        vo_ref[rows, :] = (v + vm).astype(vo_ref.dtype)


def _dft2_fwd(tables, z_arr, z_col0, spectra, order):
    tn = 512
    nj = D_MODEL // tn
    zc = z_col0 // tn
    mat = pl.BlockSpec((HALF, HALF), lambda j, b: (0, 0), pipeline_mode=pl.Buffered(1))
    spec = pl.BlockSpec((HALF, tn), lambda j, b: (0, order * nj + j))
    out_sds = jax.ShapeDtypeStruct((BATCH * HALF, D_MODEL), BF16)
    out_spec = pl.BlockSpec((HALF, tn), lambda j, b: (b, j))
    return pl.pallas_call(
        functools.partial(_dft2_fwd_kernel, slab=256),
        grid=(nj, BATCH),
        in_specs=[mat, mat, mat, mat, pl.BlockSpec((SEQ, tn), lambda j, b: (b, zc + j)),
                  spec, spec, spec, spec],
        out_specs=(out_spec,) * 4,
        out_shape=(out_sds,) * 4,
        scratch_shapes=[pltpu.VMEM((tn // LANES, SEQ, LANES), F32)],
        compiler_params=_params("arbitrary", "arbitrary"),
        name="dft_analysis",
    )(*tables, z_arr, *spectra)


def _dft2_inv_kernel(tce_ref, tse_ref, tco_ref, tso_ref, ue_ref, ve_ref, uo_ref, vo_ref,
                     z_ref, gate_ref, skip_ref, o_ref, ys_ref):
    panels, rows, _ = ys_ref.shape
    y_even = _dot(tce_ref[...], ue_ref[...]) + _dot(tse_ref[...], ve_ref[...])
    y_odd = _dot(tco_ref[...], uo_ref[...]) + _dot(tso_ref[...], vo_ref[...])
    for k in range(panels):
        ys_ref[k, pl.ds(0, rows // 2, stride=2), :] = y_even[:, k * LANES:(k + 1) * LANES]
        ys_ref[k, pl.ds(1, rows // 2, stride=2), :] = y_odd[:, k * LANES:(k + 1) * LANES]
    y = jnp.concatenate([ys_ref[k] for k in range(panels)], axis=1)
    y = y + z_ref[...].astype(F32) * skip_ref[...]
    o_ref[...] = (gate_ref[...].astype(F32) * y).astype(o_ref.dtype)


def _dft2_inv(tables, uv, z_arr, z_col0, gate_arr, gate_col0, skip):
    tm, tn = 1024, 512
    ni, nj = SEQ // tm, D_MODEL // tn
    zc, gc = z_col0 // tn, gate_col0 // tn
    mat = pl.BlockSpec((tm // 2, HALF), lambda j, b, i: (i, 0))
    coef = pl.BlockSpec((HALF, tn), lambda j, b, i: (b, j))
    return pl.pallas_call(
        _dft2_inv_kernel,
        grid=(nj, BATCH, ni),
        in_specs=[mat, mat, mat, mat, coef, coef, coef, coef,
                  pl.BlockSpec((tm, tn), lambda j, b, i: (b * ni + i, zc + j)),
                  pl.BlockSpec((tm, tn), lambda j, b, i: (b * ni + i, gc + j)),
                  pl.BlockSpec((1, tn), lambda j, b, i: (0, j))],
        out_specs=pl.BlockSpec((tm, tn), lambda j, b, i: (b * ni + i, j)),
        out_shape=jax.ShapeDtypeStruct((TOKENS, D_MODEL), BF16),
        scratch_shapes=[pltpu.VMEM((tn // LANES, tm, LANES), F32)],
        compiler_params=_params("arbitrary", "arbitrary", "arbitrary"),
        name="dft_synthesis",
    )(*tables, *uv, z_arr, gate_arr, skip.reshape(1, D_MODEL))


def _mlp_block(x2d, mod, l, norm_g, w1, w2):
    sh2, sc2, g2 = (mod[l, :BATCH, k * D_MODEL:(k + 1) * D_MODEL] for k in (3, 4, 5))
    h = _prep(x2d.reshape(BATCH, SEQ, D_MODEL), norm_g[l, 1], sc2, sh2, BF16, 512).reshape(TOKENS, D_MODEL)
    hid = _mm_plain(h, w1, D_FF, BF16, _epi_relu2, 1024, 1024, "mlp_up", w_layer=l)
    return _mm_residual([hid], w2, l, x2d, g2, 512, 512, "mlp_down", w_single_buffer=True)


def kernel(x, c, ctx, c_ctx, ada_w, ada_b, norm_g, lb_logits, ab_w_in, ab_conv_w, ab_gnorm_g, ab_w_out,
           hy_in_w, hy_short_w, hy_out_w, hy_fw1, hy_fb1, hy_fw2, hy_fb2, hy_fw3, hy_fb3, hy_fw4, hy_freq,
           hy_skip, mlp_w1, mlp_w2, final_g):
    d = D_MODEL
    cond = jnp.concatenate([c, c_ctx[None, :], jnp.zeros((3, d), F32)], axis=0)
    mod = _ada(cond, ada_w, ada_b)
    x2d = x.reshape(TOKENS, d)

    sh1, sc1, g1 = (mod[0, :BATCH, k * d:(k + 1) * d] for k in (0, 1, 2))
    h = _prep(x, norm_g[0, 0], sc1, sh1, BF16, 512).reshape(TOKENS, d)
    csh = jnp.broadcast_to(mod[0, BATCH:BATCH + 1, 0:d], (BATCH, d))
    csc = jnp.broadcast_to(mod[0, BATCH:BATCH + 1, d:2 * d], (BATCH, d))
    hc = _prep(ctx, norm_g[0, 0], csc, csh, BF16, CTX_LEN).reshape(BATCH * CTX_LEN, d)
    p = _mm_plain(h, ab_w_in, AB_IN_WIDTH, F32, _epi_store, 1024, 1024, "ab_in_proj", w_layer=0)
    cp = _mm_plain(hc, ab_w_in, 3 * A_WIDTH, F32, _epi_store, 1024, 1024, "ab_ctx_proj", w_layer=0)
    mix_a = _hgrn_scan(p, cp, lb_logits, ab_gnorm_g[0], 0)
    mix_b = _gconv(p, ab_conv_w[0])
    x2d = _mm_residual([mix_a, mix_b], ab_w_out, 0, x2d, g1, 1024, 1024, "ab_out_proj")
    x2d = _mlp_block(x2d, mod, 0, norm_g, mlp_w1, mlp_w2)

    sh1, sc1, g1 = (mod[1, :BATCH, k * d:(k + 1) * d] for k in (0, 1, 2))
    h = _prep(x2d.reshape(BATCH, SEQ, d), norm_g[1, 0], sc1, sh1, BF16, 512).reshape(TOKENS, d)
    pc = _mm([h], hy_in_w, w_layer=0, tm=1024, tn=1024, n_cols=3 * d, epi=_epi_conv3_grid,
             extras=(hy_short_w[0],), extra_specs=(pl.BlockSpec((3, 1024), lambda j, i: (0, j)),),
             out_shapes=jax.ShapeDtypeStruct((TOKENS, 3 * d), BF16),
             out_specs=pl.BlockSpec((1024, 1024), lambda j, i: (i, j)), name="hy_in_proj")
    taps_sum, taps_diff = _hyena_filter_taps(hy_fw1[0], hy_fb1[0], hy_fw2[0], hy_fb2[0], hy_fw3[0],
                                             hy_fb3[0], hy_fw4[0], hy_freq[0])
    analysis, synthesis = _dft_half_tables()
    spectra = _filter_dft2(analysis, taps_sum, taps_diff)
    uv = _dft2_fwd(analysis, pc, 2 * d, spectra, 0)
    z = _dft2_inv(synthesis, uv, pc, 2 * d, pc, 0, hy_skip[0, 0])
    uv = _dft2_fwd(analysis, z, 0, spectra, 1)
    z = _dft2_inv(synthesis, uv, z, 0, pc, d, hy_skip[0, 1])
    x2d = _mm_residual([z], hy_out_w, 0, x2d, g1, 1024, 1024, "hy_out_proj")
    x2d = _mlp_block(x2d, mod, 1, norm_g, mlp_w1, mlp_w2)

    zeros = jnp.zeros((BATCH, d), F32)
    return _prep(x2d.reshape(BATCH, SEQ, d), final_g, zeros, zeros, F32, 512)
```

```python
import functools
import math

import numpy as np
import jax
import jax.numpy as jnp
from jax import lax
from jax.experimental import pallas as pl
from jax.experimental.pallas import tpu as pltpu

F32 = jnp.float32
BF16 = jnp.bfloat16

D_MODEL = 2048
BATCH = 4
SEQ = 2048
CTX_LEN = 256
GRID_W = 64
EPS = 1e-6
A_HEAD_DIM = 128
A_WIDTH = D_MODEL // 2
A_HEADS = A_WIDTH // A_HEAD_DIM
B_WIDTH = D_MODEL - A_WIDTH
AB_IN_WIDTH = 5 * A_WIDTH + 3 * B_WIDTH
HY_EMB = 33
HY_BANDS = (HY_EMB - 1) // 2
HY_HIDDEN = 64
HY_DECAY_TARGET = 1e-2
HY_FAST_PCT = 0.3
HY_SLOW_PCT = 1.5
D_FF = 4 * D_MODEL
TOKENS = BATCH * SEQ

SCAN_CHUNK = 64
SCAN_LEVELS = (1, 2, 4, 8, 16, 32)

VMEM_LIMIT_BYTES = 56 * 1024 * 1024


def _params(*sem):
    return pltpu.CompilerParams(dimension_semantics=sem, vmem_limit_bytes=VMEM_LIMIT_BYTES)


def _dot(a, b):
    return jnp.dot(a, b, preferred_element_type=F32)


def _dot_nt(a, b):
    return lax.dot_general(a, b, (((1,), (1,)), ((), ())), preferred_element_type=F32)


def _dot_tn(a, b):
    return lax.dot_general(a, b, (((0,), (0,)), ((), ())), preferred_element_type=F32)


def _dot_f32(a, b):
    return jnp.dot(a, b, precision=lax.Precision.HIGHEST, preferred_element_type=F32)


def _ada_kernel(c_ref, w_ref, b_ref, o_ref):
    c = c_ref[...]
    a = (c * jax.nn.sigmoid(c)).astype(BF16)
    o_ref[...] = _dot(a, w_ref[...].astype(BF16)) + b_ref[...]


def _ada(cond, ada_w, ada_b):
    depth, d, n = ada_w.shape
    tn = 1024
    return pl.pallas_call(
        _ada_kernel,
        grid=(depth, n // tn),
        in_specs=[
            pl.BlockSpec((8, d), lambda l, j: (0, 0)),
            pl.BlockSpec((None, d, tn), lambda l, j: (l, 0, j)),
            pl.BlockSpec((None, 1, tn), lambda l, j: (l, 0, j)),
        ],
        out_specs=pl.BlockSpec((None, 8, tn), lambda l, j: (l, 0, j)),
        out_shape=jax.ShapeDtypeStruct((depth, 8, n), F32),
        compiler_params=_params("arbitrary", "arbitrary"),
        name="ada_mod",
    )(cond, ada_w, ada_b.reshape(depth, 1, n))


def _prep_kernel(x_ref, g_ref, sc_ref, sh_ref, o_ref):
    x = x_ref[...]
    ms = jnp.mean(x * x, axis=-1, keepdims=True)
    xn = x * lax.rsqrt(ms + EPS) * g_ref[...]
    o_ref[...] = (xn * (1.0 + sc_ref[...]) + sh_ref[...]).astype(o_ref.dtype)


def _prep(x, g, sc, sh, out_dtype, ts):
    b, l, d = x.shape
    return pl.pallas_call(
        _prep_kernel,
        grid=(b, l // ts),
        in_specs=[
            pl.BlockSpec((None, ts, d), lambda i, j: (i, j, 0)),
            pl.BlockSpec((1, d), lambda i, j: (0, 0)),
            pl.BlockSpec((None, 1, d), lambda i, j: (i, 0, 0)),
            pl.BlockSpec((None, 1, d), lambda i, j: (i, 0, 0)),
        ],
        out_specs=pl.BlockSpec((None, ts, d), lambda i, j: (i, j, 0)),
        out_shape=jax.ShapeDtypeStruct((b, l, d), out_dtype),
        compiler_params=_params("arbitrary", "arbitrary"),
        name="norm_mod",
    )(x, g.reshape(1, d), sc.reshape(b, 1, d), sh.reshape(b, 1, d))


def _mm_kernel(*refs, n_a, k_sizes, epi, cast_w):
    a_refs = refs[:n_a]
    w_ref = refs[n_a]
    rest = refs[n_a + 1:]
    if cast_w:
        w_ref, rest = rest[-1], rest[:-1]

        @pl.when(pl.program_id(1) == 0)
        def _():
            w_ref[...] = refs[n_a][...].astype(BF16)

    acc = None
    off = 0
    for a_ref, k in zip(a_refs, k_sizes):
        part = _dot(a_ref[...], w_ref[off:off + k, :])
        acc = part if acc is None else acc + part
        off += k
    epi(acc, *rest)


def _mm(a_list, w, *, tm, tn, n_cols, epi, extras, extra_specs, out_shapes, out_specs, name,
        w_layer=None, w_single_buffer=False):
    m = a_list[0].shape[0]
    k_sizes = tuple(a.shape[1] for a in a_list)
    k_total = sum(k_sizes)
    cast_w = w_layer is not None
    assert w.shape[-2] == k_total and m % tm == 0 and n_cols % tn == 0
    in_specs = [pl.BlockSpec((tm, k), lambda j, i: (i, 0)) for k in k_sizes]
    w_mode = dict(pipeline_mode=pl.Buffered(1)) if w_single_buffer else {}
    if cast_w:
        in_specs.append(pl.BlockSpec((None, k_total, tn), lambda j, i: (w_layer, 0, j), **w_mode))
    else:
        in_specs.append(pl.BlockSpec((k_total, tn), lambda j, i: (0, j), **w_mode))
    in_specs.extend(extra_specs)
    return pl.pallas_call(
        functools.partial(_mm_kernel, n_a=len(a_list), k_sizes=k_sizes, epi=epi, cast_w=cast_w),
        grid=(n_cols // tn, m // tm),
        in_specs=in_specs,
        out_specs=out_specs,
        out_shape=out_shapes,
        scratch_shapes=[pltpu.VMEM((k_total, tn), BF16)] if cast_w else [],
        compiler_params=_params("arbitrary", "arbitrary"),
        name=name,
    )(*a_list, w, *extras)


def _epi_store(acc, o_ref):
    o_ref[...] = acc.astype(o_ref.dtype)


def _epi_relu2(acc, o_ref):
    r = jnp.maximum(acc, 0.0)
    o_ref[...] = (r * r).astype(o_ref.dtype)


def _epi_residual(acc, x_ref, gate_ref, o_ref):
    o_ref[...] = x_ref[...] + gate_ref[...] * acc


def _epi_conv3_grid(acc, w_ref, o_ref):
    rows = lax.broadcasted_iota(jnp.int32, acc.shape, 0) % GRID_W
    prev = jnp.where(rows == 0, 0.0, pltpu.roll(acc, 1, 0))
    nxt = jnp.where(rows == GRID_W - 1, 0.0, pltpu.roll(acc, acc.shape[0] - 1, 0))
    w = w_ref[...]
    o_ref[...] = (w[0:1] * prev + w[1:2] * acc + w[2:3] * nxt).astype(o_ref.dtype)


def _mm_plain(a, w, n_cols, out_dtype, epi, tm, tn, name, w_layer=None):
    m = a.shape[0]
    return _mm([a], w, tm=tm, tn=tn, n_cols=n_cols, epi=epi, extras=(), extra_specs=(),
               out_shapes=jax.ShapeDtypeStruct((m, n_cols), out_dtype),
               out_specs=pl.BlockSpec((tm, tn), lambda j, i: (i, j)), name=name, w_layer=w_layer)


def _mm_residual(a_list, w, w_layer, x, gate, tm, tn, name, w_single_buffer=False):
    m, n = x.shape
    tiles_per_batch = SEQ // tm
    return _mm(a_list, w, tm=tm, tn=tn, n_cols=n, epi=_epi_residual,
               extras=(x, gate.reshape(BATCH, 1, n)),
               extra_specs=(pl.BlockSpec((tm, tn), lambda j, i: (i, j)),
                            pl.BlockSpec((None, 1, tn), lambda j, i: (i // tiles_per_batch, 0, j))),
               out_shapes=jax.ShapeDtypeStruct((m, n), F32),
               out_specs=pl.BlockSpec((tm, tn), lambda j, i: (i, j)), name=name,
               w_layer=w_layer, w_single_buffer=w_single_buffer)


def _gconv_kernel(u_ref, gb_ref, gc_ref, w_ref, o_ref):
    t = gc_ref[...] * u_ref[...]
    rows = lax.broadcasted_iota(jnp.int32, t.shape, 0) % GRID_W
    prev = jnp.where(rows == 0, 0.0, pltpu.roll(t, 1, 0))
    nxt = jnp.where(rows == GRID_W - 1, 0.0, pltpu.roll(t, t.shape[0] - 1, 0))
    w = w_ref[...]
    o_ref[...] = (gb_ref[...] * (w[0:1] * prev + w[1:2] * t + w[2:3] * nxt)).astype(o_ref.dtype)


def _gconv(p, conv_w):
    tm, tn = 512, 512
    nb = B_WIDTH // tn
    base = 5 * A_WIDTH // tn
    return pl.pallas_call(
        _gconv_kernel,
        grid=(TOKENS // tm, nb),
        in_specs=[
            pl.BlockSpec((tm, tn), lambda i, j: (i, base + j)),
            pl.BlockSpec((tm, tn), lambda i, j: (i, base + nb + j)),
            pl.BlockSpec((tm, tn), lambda i, j: (i, base + 2 * nb + j)),
            pl.BlockSpec((3, tn), lambda i, j: (0, j)),
        ],
        out_specs=pl.BlockSpec((tm, tn), lambda i, j: (i, j)),
        out_shape=jax.ShapeDtypeStruct((TOKENS, B_WIDTH), BF16),
        compiler_params=_params("arbitrary", "arbitrary"),
        name="gated_conv",
    )(p, p, p, conv_w)


SUBLANES = 8
LANES = 128


def _row_groups(x):
    return [x[k:k + SUBLANES] for k in range(0, x.shape[0], SUBLANES)]


def _cumsum_groups(groups, sub):
    out, carry = [], None
    for g in groups:
        for s in (1, 2, 4):
            g = g + jnp.where(sub >= s, pltpu.roll(g, s, 0), 0.0)
        if carry is not None:
            g = g + carry
        out.append(g)
        carry = g[SUBLANES - 1:SUBLANES]
    return out


def _hold_boundary(cum, w, k, sub):
    shape = cum[k].shape
    if w >= SUBLANES // 2:
        r = (k * SUBLANES // (2 * w)) * 2 * w + w - 1
        return jnp.broadcast_to(cum[r // SUBLANES][r % SUBLANES:r % SUBLANES + 1], shape)
    assert w == 2
    return jnp.where(sub < 4, jnp.broadcast_to(cum[k][1:2], shape), jnp.broadcast_to(cum[k][5:6], shape))


def _scan_kernel(zf_ref, zb_ref, v_ref, q_ref, g_ref, czf_ref, czb_ref, cv_ref, lbl_ref, gn_ref,
                 o_ref, of_ref, ob_ref, st_ref, qs_ref, inc_ref, dec_ref, kl_ref, lv_ref, att_ref, *, layer):
    c = SCAN_CHUNK
    n_chunks = SEQ // c
    n_ctx = CTX_LEN // c
    z_refs = (zf_ref, zb_ref)
    cz_refs = (czf_ref, czb_ref)
    out_refs = (of_ref, ob_ref)

    logit_rows = [lbl_ref[:, k, :] for k in range(lbl_ref.shape[1])]
    top = functools.reduce(jnp.maximum, logit_rows)
    exps = [jnp.exp(r - top) for r in logit_rows]
    lbs = sum(exps[:layer + 1]) / sum(exps)

    sub = lax.broadcasted_iota(jnp.int32, (SUBLANES, A_HEAD_DIM), 0)
    n_groups = c // SUBLANES

    def rows(groups):
        return jnp.concatenate(groups, axis=0)

    def chunk(d, z, q, v, want_out):
        lb = lbs[d:d + 1]
        f = lb + (1.0 - lb) * jax.nn.sigmoid(z)
        kk = 1.0 - f
        fg, kg = _row_groups(f), _row_groups(kk)
        lg = [jnp.log2(g) for g in fg]
        cum = _cumsum_groups(lg, sub)
        total = cum[-1][SUBLANES - 1:SUBLANES]
        if d == 0:
            pos = cum
            q_in = [jnp.exp2(g) for g in cum]
            k_out = [jnp.exp2(total - g) for g in cum]
        else:
            pos = [a - b for a, b in zip(cum, lg)]
            q_in = [jnp.exp2(total - g) for g in pos]
            k_out = [jnp.exp2(g) for g in pos]
        k_leave = rows([a * b for a, b in zip(kg, k_out)]).astype(BF16)
        decay = jnp.exp2(total)
        if not want_out:
            return k_leave, decay, None, None, None
        qg = _row_groups(q)
        o_diag = jnp.sum(q * kk, axis=1, keepdims=True) * v
        upper, lower = (qg, kg) if d == 0 else (kg, qg)
        operands = []
        for w in SCAN_LEVELS:
            y = []
            for k in range(n_groups):
                if w == 1:
                    odd = (sub & 1) != 0
                    y.append(jnp.where(odd, qg[k] * fg[k], kg[k]) if d == 0
                             else jnp.where(odd, kg[k], qg[k] * fg[k]))
                    continue
                gap = pos[k] - _hold_boundary(cum, w, k, sub)
                if w >= SUBLANES:
                    bit_set = (k * SUBLANES) & w != 0
                    y.append(upper[k] * jnp.exp2(gap) if bit_set else lower[k] * jnp.exp2(-gap))
                else:
                    bit = (sub & w) != 0
                    y.append(jnp.where(bit, upper[k], lower[k]) * jnp.exp2(jnp.where(bit, gap, -gap)))
            operands.append(rows(y).astype(BF16))
        qs = rows([a * b for a, b in zip(qg, q_in)]).astype(BF16)
        return k_leave, decay, qs, o_diag, operands

    def within_chunk(d, operands):
        t_idx = lax.broadcasted_iota(jnp.int32, (c, c), 0)
        s_idx = lax.broadcasted_iota(jnp.int32, (c, c), 1)
        split = t_idx ^ s_idx
        att = None
        for w, yb in reversed(list(zip(SCAN_LEVELS, operands))):
            scores = _dot_nt(yb, yb)
            att = scores if att is None else jnp.where(split < 2 * w, scores, att)
        return jnp.where(t_idx > s_idx if d == 0 else t_idx < s_idx, att, 0.0)

    st_ref[...] = jnp.zeros_like(st_ref)

    def ctx_body(i, carry):
        for d in range(2):
            ci = i if d == 0 else n_ctx - 1 - i
            rows_i = pl.ds(pl.multiple_of(ci * c, c), c)
            k_leave, decay, _, _, _ = chunk(d, cz_refs[d][rows_i, :], None, None, False)
            st_ref[d] = st_ref[d] * decay + _dot_tn(cv_ref[rows_i, :].astype(BF16), k_leave)
        return carry

    lax.fori_loop(0, n_ctx, ctx_body, 0)

    def chunk_rows(ci):
        start = ci * c
        return pl.ds(start if isinstance(ci, int) else pl.multiple_of(start, c), c)

    def prepare(ci):
        rows_i = chunk_rows(ci)
        for d in range(2):
            k_leave, decay, qs, o_diag, operands = chunk(d, z_refs[d][rows_i, :], q_ref[rows_i, :],
                                                         v_ref[rows_i, :], True)
            kl_ref[d, rows_i, :] = k_leave
            dec_ref[d, ci] = decay
            qs_ref[d, rows_i, :] = qs
            out_refs[d][rows_i, :] = o_diag
            for lvl, yb in enumerate(operands):
                lv_ref[d, lvl, rows_i, :] = yb

    def scores(ci):
        rows_i = chunk_rows(ci)
        vb = v_ref[rows_i, :].astype(BF16)
        for d in range(2):
            inc_ref[d, ci] = _dot_tn(vb, kl_ref[d, rows_i, :])
            operands = [lv_ref[d, lvl, rows_i, :] for lvl in range(len(SCAN_LEVELS))]
            att_ref[d, rows_i, :] = within_chunk(d, operands).astype(BF16)

    pair = 2
    for j in range(pair):
        prepare(j)

    def skewed_body(i, carry):
        for j in range(pair):
            scores((i - 1) * pair + j)
        for j in range(pair):
            prepare(i * pair + j)
        return carry

    lax.fori_loop(1, n_chunks // pair, skewed_body, 0)
    for j in range(pair):
        scores(n_chunks - pair + j)

    def carry_body(i, carry):
        for d in range(2):
            ci = i if d == 0 else n_chunks - 1 - i
            rows_i = pl.ds(pl.multiple_of(ci * c, c), c)
            state = st_ref[d]
            out_refs[d][rows_i, :] += (_dot(att_ref[d, rows_i, :], v_ref[rows_i, :].astype(BF16))
                                       + _dot_nt(qs_ref[d, rows_i, :], state.astype(BF16)))
            st_ref[d] = state * dec_ref[d, ci] + inc_ref[d, ci]
        return carry

    lax.fori_loop(0, n_chunks, carry_body, 0, unroll=4)

    rb = 256
    gn = gn_ref[...]

    def read_body(i, carry):
        rows = pl.ds(pl.multiple_of(i * rb, rb), rb)
        o = of_ref[rows, :] + ob_ref[rows, :]
        ms = jnp.mean(o * o, axis=-1, keepdims=True)
        on = o * lax.rsqrt(ms + EPS) * gn
        g = g_ref[rows, :]
        o_ref[rows, :] = (on * (g * jax.nn.sigmoid(g))).astype(o_ref.dtype)
        return carry

    lax.fori_loop(0, SEQ // rb, read_body, 0)


def _hgrn_scan(p, cp, lb_logits, gnorm_g, layer):
    h = A_HEADS
    hd = A_HEAD_DIM
    n_lb = lb_logits.shape[1]

    def col(k):
        return pl.BlockSpec((SEQ, hd), lambda b, j, k=k: (b, k * h + j))

    def ccol(k):
        return pl.BlockSpec((CTX_LEN, hd), lambda b, j, k=k: (b, k * h + j))

    return pl.pallas_call(
        functools.partial(_scan_kernel, layer=layer),
        grid=(BATCH, h),
        in_specs=[col(0), col(1), col(2), col(3), col(4), ccol(0), ccol(1), ccol(2),
                  pl.BlockSpec((2, n_lb, hd), lambda b, j: (0, 0, j)),
                  pl.BlockSpec((1, hd), lambda b, j: (0, j))],
        out_specs=pl.BlockSpec((SEQ, hd), lambda b, j: (b, j)),
        out_shape=jax.ShapeDtypeStruct((TOKENS, A_WIDTH), BF16),
        scratch_shapes=[pltpu.VMEM((SEQ, hd), F32), pltpu.VMEM((SEQ, hd), F32),
                        pltpu.VMEM((2, hd, hd), F32), pltpu.VMEM((2, SEQ, hd), BF16),
                        pltpu.VMEM((2, SEQ // SCAN_CHUNK, hd, hd), F32),
                        pltpu.VMEM((2, SEQ // SCAN_CHUNK, 1, hd), F32),
                        pltpu.VMEM((2, SEQ, hd), BF16),
                        pltpu.VMEM((2, len(SCAN_LEVELS), SEQ, hd), BF16),
                        pltpu.VMEM((2, SEQ, SCAN_CHUNK), BF16)],
        compiler_params=_params("arbitrary", "arbitrary"),
        name="hgrn_scan",
    )(p, p, p, p, p, cp, cp, cp, lb_logits, gnorm_g.reshape(1, A_WIDTH))


def _split_bf16(x):
    hi = x.astype(BF16)
    return hi, (x - hi.astype(F32)).astype(BF16)


def _filter_kernel(z_ref, w1_ref, b1_ref, w2_ref, b2_ref, w3_ref, b3_ref, fr_ref,
                   w4f_top_ref, w4f_bot_ref, w4b_top_ref, w4b_bot_ref, sum_ref, diff_ref, hid_ref, *, tc):
    o = pl.program_id(0)
    j = pl.program_id(1)

    @pl.when((o == 0) & (j == 0))
    def _():
        fr = fr_ref[...]
        h = jnp.sin(fr * (_dot_f32(z_ref[...], w1_ref[...]) + b1_ref[...]))
        h = jnp.sin(fr * (_dot_f32(h, w2_ref[...]) + b2_ref[...]))
        h = jnp.sin(fr * (_dot_f32(h, w3_ref[...]) + b3_ref[...]))
        hi, lo = _split_bf16(h)
        hid_ref[...] = jnp.concatenate([hi, lo, hi], axis=1)

    def project(top_ref, bot_ref):
        halves = []
        for w_ref in (top_ref, bot_ref):
            w_hi, w_lo = _split_bf16(w_ref[...])
            halves.append(_dot(hid_ref[...], jnp.concatenate([w_hi, w_hi, w_lo], axis=0)))
        return jnp.concatenate(halves, axis=0)

    rows = lax.broadcasted_iota(jnp.int32, (SEQ, tc), 0)
    chan = lax.broadcasted_iota(jnp.int32, (SEQ, tc), 1) + j * tc
    lag = jnp.where(rows < SEQ // 2, 2 * rows, 2 * rows - (SEQ - 1))
    t = lag.astype(F32) * (1.0 / (SEQ - 1))
    max_decay = math.log(HY_DECAY_TARGET) / HY_FAST_PCT
    min_decay = math.log(HY_DECAY_TARGET) / HY_SLOW_PCT
    deltas = jnp.abs(min_decay + chan.astype(F32) * ((max_decay - min_decay) / (D_MODEL - 1)))
    window = jnp.exp(-t * deltas)
    fw = project(w4f_top_ref, w4f_bot_ref) * window
    bw = project(w4b_top_ref, w4b_bot_ref) * window
    first = rows == 0
    a = fw + jnp.where(first, bw, 0.0)
    bb = jnp.where(first, 0.0, bw)
    inv = 1.0 / jnp.sum(jnp.abs(a) + jnp.abs(bb), axis=0, keepdims=True)
    sum_ref[...] = ((a + bb) * inv).astype(sum_ref.dtype)
    diff_ref[...] = ((a - bb) * inv).astype(diff_ref.dtype)


def _hyena_filter_taps(fw1, fb1, fw2, fb2, fw3, fb3, fw4, freq):
    l = SEQ
    pos = jnp.arange(l, dtype=F32)
    t = jnp.linspace(0.0, 1.0, l, dtype=F32)
    w = 2.0 * math.pi * pos / l
    bands = jnp.linspace(1e-4, HY_BANDS - 1, HY_BANDS, dtype=F32)
    ang = w[:, None] * bands[None, :]
    z = jnp.concatenate([t[:, None], jnp.cos(ang), -jnp.sin(ang)], axis=-1)
    z = jnp.concatenate([z[0::2], z[1::2]], axis=0)
    tc = 256
    nj = D_MODEL // tc
    hw = LANES // 2

    def pad(a, rows, cols):
        return jnp.pad(a, ((0, rows - a.shape[0]), (0, cols - a.shape[1])))

    def twice(a):
        a = pad(a, hw, hw)
        zero = jnp.zeros_like(a)
        return jnp.concatenate([jnp.concatenate([a, zero], axis=1), jnp.concatenate([zero, a], axis=1)], axis=0)

    def row2(v):
        v = pad(v[None, :], 1, hw)
        return jnp.concatenate([v, v], axis=1)

    def small(shape):
        return pl.BlockSpec(shape, lambda o, j: (0, 0))

    z = pad(z, l, hw)
    z2 = jnp.concatenate([z[:l // 2], z[l // 2:]], axis=1)
    w4_top = pad(fw4, LANES, fw4.shape[1])
    w4_bot = jnp.concatenate([jnp.zeros((hw, fw4.shape[1]), F32), pad(fw4, hw, fw4.shape[1])], axis=0)
    out_sds = jax.ShapeDtypeStruct((l, 2 * D_MODEL), BF16)
    out_spec = pl.BlockSpec((l, tc), lambda o, j: (0, o * nj + j))
    side_f = pl.BlockSpec((LANES, tc), lambda o, j: (0, 2 * o * nj + j))
    side_b = pl.BlockSpec((LANES, tc), lambda o, j: (0, (2 * o + 1) * nj + j))
    sq, vec = small((LANES, LANES)), small((1, LANES))
    return pl.pallas_call(
        functools.partial(_filter_kernel, tc=tc),
        grid=(2, nj),
        in_specs=[small((l // 2, LANES)), sq, vec, sq, vec, sq, vec, vec, side_f, side_f, side_b, side_b],
        out_specs=(out_spec, out_spec),
        out_shape=(out_sds, out_sds),
        scratch_shapes=[pltpu.VMEM((l // 2, 3 * LANES), BF16)],
        compiler_params=_params("arbitrary", "arbitrary"),
        name="hyena_filter",
    )(z2, twice(fw1), row2(fb1), twice(fw2), row2(fb2), twice(fw3), row2(fb3), row2(freq),
      w4_top, w4_bot, w4_top, w4_bot)


def _dft_matrices():
    l = SEQ
    period = 4 * l

    def trig(n):
        ang = (n % period).astype(F32) * (math.pi / (2 * l))
        return jnp.cos(ang), jnp.sin(ang)

    lanes = 128
    groups = l // lanes
    row = jnp.arange(l, dtype=jnp.int32)[:, None]
    lo = jnp.arange(lanes, dtype=jnp.int32)[None, :]
    hi = jnp.arange(groups, dtype=jnp.int32)[None, :]

    def combine(c_hi, s_hi, c_lo, s_lo, scale):
        cm = (c_hi[:, :, None] * c_lo[:, None, :] - s_hi[:, :, None] * s_lo[:, None, :]).reshape(l, l)
        sm = (s_hi[:, :, None] * c_lo[:, None, :] + c_hi[:, :, None] * s_lo[:, None, :]).reshape(l, l)
        return (cm * scale).astype(BF16), (sm * scale).astype(BF16)

    odd = 2 * row + 1
    c_hi, s_hi = trig(odd * (lanes * hi))
    c_lo, s_lo = trig(odd * lo)
    c_fm, s_fm = combine(c_hi, s_hi, c_lo, s_lo, 1.0)
    c_hi, s_hi = trig(row * (2 * lanes * hi))
    c_lo, s_lo = trig(row * (2 * lo + 1))
    c_mf, s_mf = combine(c_hi, s_hi, c_lo, s_lo, 1.0 / l)
    return c_fm, s_fm, c_mf, s_mf


def _dft_fwd_kernel(c_ref, s_ref, z_ref, p_ref, q_ref, u_ref, v_ref):
    z = z_ref[...]
    a = _dot(c_ref[...], z)
    b = _dot(s_ref[...], z)
    p = p_ref[...]
    q = q_ref[...]
    u_ref[...] = (a * p - b * q).astype(u_ref.dtype)
    v_ref[...] = (a * q + b * p).astype(v_ref.dtype)


def _dft_fwd(c_fm, s_fm, z_arr, z_col0, pq_p, pq_q, order):
    tm, tn = 1024, 512
    ni, nj = SEQ // tm, D_MODEL // tn
    zc = z_col0 // tn
    out_sds = jax.ShapeDtypeStruct((TOKENS, D_MODEL), BF16)
    out_spec = pl.BlockSpec((tm, tn), lambda j, i, b: (b * ni + i, j))
    return pl.pallas_call(
        _dft_fwd_kernel,
        grid=(nj, ni, BATCH),
        in_specs=[pl.BlockSpec((tm, SEQ), lambda j, i, b: (i, 0)),
                  pl.BlockSpec((tm, SEQ), lambda j, i, b: (i, 0)),
                  pl.BlockSpec((SEQ, tn), lambda j, i, b: (b, zc + j)),
                  pl.BlockSpec((tm, tn), lambda j, i, b: (i, order * nj + j)),
                  pl.BlockSpec((tm, tn), lambda j, i, b: (i, order * nj + j))],
        out_specs=(out_spec, out_spec),
        out_shape=(out_sds, out_sds),
        compiler_params=_params("arbitrary", "arbitrary", "arbitrary"),
        name="dft_analysis",
    )(c_fm, s_fm, z_arr, pq_p, pq_q)


def _dft_inv_kernel(c_ref, s_ref, u_ref, v_ref, z_ref, gate_ref, skip_ref, o_ref):
    y = _dot(c_ref[...], u_ref[...]) + _dot(s_ref[...], v_ref[...])
    y = y + z_ref[...].astype(F32) * skip_ref[...]
    o_ref[...] = (gate_ref[...].astype(F32) * y).astype(o_ref.dtype)


def _dft_inv(c_mf, s_mf, u, v, z_arr, z_col0, gate_arr, gate_col0, skip):
    tm, tn = 1024, 512
    ni, nj = SEQ // tm, D_MODEL // tn
    zc, gc = z_col0 // tn, gate_col0 // tn
    return pl.pallas_call(
        _dft_inv_kernel,
        grid=(nj, BATCH, ni),
        in_specs=[pl.BlockSpec((tm, SEQ), lambda j, b, i: (i, 0)),
                  pl.BlockSpec((tm, SEQ), lambda j, b, i: (i, 0)),
                  pl.BlockSpec((SEQ, tn), lambda j, b, i: (b, j)),
                  pl.BlockSpec((SEQ, tn), lambda j, b, i: (b, j)),
                  pl.BlockSpec((tm, tn), lambda j, b, i: (b * ni + i, zc + j)),
                  pl.BlockSpec((tm, tn), lambda j, b, i: (b * ni + i, gc + j)),
                  pl.BlockSpec((1, tn), lambda j, b, i: (0, j))],
        out_specs=pl.BlockSpec((tm, tn), lambda j, b, i: (b * ni + i, j)),
        out_shape=jax.ShapeDtypeStruct((TOKENS, D_MODEL), BF16),
        compiler_params=_params("arbitrary", "arbitrary", "arbitrary"),
        name="dft_synthesis",
    )(c_mf, s_mf, u, v, z_arr, gate_arr, skip.reshape(1, D_MODEL))


HALF = SEQ // 2


def _trig_table(a_rows, b_hi, b_lo, scale):
    period = 4 * SEQ

    def trig(n):
        ang = (n % period).astype(F32) * (math.pi / (2 * SEQ))
        return jnp.cos(ang), jnp.sin(ang)

    rows = a_rows.shape[0]
    c_hi, s_hi = trig(a_rows[:, None] * b_hi[None, :])
    c_lo, s_lo = trig(a_rows[:, None] * b_lo[None, :])
    cm = (c_hi[:, :, None] * c_lo[:, None, :] - s_hi[:, :, None] * s_lo[:, None, :]).reshape(rows, -1)
    sm = (s_hi[:, :, None] * c_lo[:, None, :] + c_hi[:, :, None] * s_lo[:, None, :]).reshape(rows, -1)
    return (cm * scale).astype(BF16), (sm * scale).astype(BF16)


def _dft_half_tables():
    lanes = 128
    idx = jnp.arange(HALF, dtype=jnp.int32)
    hi = jnp.arange(HALF // lanes, dtype=jnp.int32) * (2 * lanes)
    lo = jnp.arange(lanes, dtype=jnp.int32)
    analysis, synthesis = [], []
    for par in (0, 1):
        analysis.append(_trig_table(2 * idx + 1, hi, 2 * lo + par, 1.0))
        synthesis.append(_trig_table(2 * idx + par, hi, 2 * lo + 1, 1.0 / SEQ))
    (cee, see), (ceo, seo) = analysis
    (tce, tse), (tco, tso) = synthesis
    return (cee, ceo, see, seo), (tce, tse, tco, tso)


def _filter_dft2_kernel(cee_ref, ceo_ref, see_ref, seo_ref, fs_ref, fd_ref, p_ref, pm_ref, q_ref, qm_ref):
    pe = _dot(cee_ref[...], fs_ref[0:HALF, :])
    po = _dot(ceo_ref[...], fs_ref[HALF:SEQ, :])
    qe = _dot(see_ref[...], fd_ref[0:HALF, :])
    qo = _dot(seo_ref[...], fd_ref[HALF:SEQ, :])
    p_ref[...] = pe + po
    pm_ref[...] = pe - po
    q_ref[...] = qe + qo
    qm_ref[...] = qo - qe


def _filter_dft2(tables, taps_sum, taps_diff):
    tm, tn = HALF, 256
    n_cols = taps_sum.shape[1]
    mat = pl.BlockSpec((tm, HALF), lambda j, i: (i, 0), pipeline_mode=pl.Buffered(1))
    tap = pl.BlockSpec((SEQ, tn), lambda j, i: (0, j))
    out_spec = pl.BlockSpec((tm, tn), lambda j, i: (i, j))
    out_sds = jax.ShapeDtypeStruct((HALF, n_cols), F32)
    return pl.pallas_call(
        _filter_dft2_kernel,
        grid=(n_cols // tn, HALF // tm),
        in_specs=[mat, mat, mat, mat, tap, tap],
        out_specs=(out_spec,) * 4,
        out_shape=(out_sds,) * 4,
        compiler_params=_params("arbitrary", "arbitrary"),
        name="filter_dft",
    )(*tables, taps_sum, taps_diff)


def _dft2_fwd_kernel(cee_ref, ceo_ref, see_ref, seo_ref, z_ref, p_ref, pm_ref, q_ref, qm_ref,
                     ue_ref, ve_ref, uo_ref, vo_ref, zs_ref, *, slab):
    panels = zs_ref.shape[0]
    for k in range(panels):
        zs_ref[k] = z_ref[:, k * LANES:(k + 1) * LANES].astype(F32)
    ze = jnp.concatenate([zs_ref[k, pl.ds(0, HALF, stride=2), :] for k in range(panels)], axis=1).astype(BF16)
    zo = jnp.concatenate([zs_ref[k, pl.ds(1, HALF, stride=2), :] for k in range(panels)], axis=1).astype(BF16)
    for r in range(0, HALF, slab):
        rows = slice(r, r + slab)
        ec = _dot(cee_ref[rows, :], ze)
        oc = _dot(ceo_ref[rows, :], zo)
        es = _dot(see_ref[rows, :], ze)
        os_ = _dot(seo_ref[rows, :], zo)
        a, am, b, bm = ec + oc, ec - oc, es + os_, os_ - es
        p, pm, q, qm = p_ref[rows, :], pm_ref[rows, :], q_ref[rows, :], qm_ref[rows, :]
        u, v = a * p - b * q, a * q + b * p
        um, vm = am * pm - bm * qm, am * qm + bm * pm
        ue_ref[rows, :] = (u + um).astype(ue_ref.dtype)
        uo_ref[rows, :] = (u - um).astype(uo_ref.dtype)
        ve_ref[rows, :] = (v - vm).astype(ve_ref.dtype)
        vo_ref[rows, :] = (v + vm).astype(vo_ref.dtype)


def _dft2_fwd(tables, z_arr, z_col0, spectra, order):
    tn = 512
    nj = D_MODEL // tn
    zc = z_col0 // tn
    mat = pl.BlockSpec((HALF, HALF), lambda j, b: (0, 0), pipeline_mode=pl.Buffered(1))
    spec = pl.BlockSpec((HALF, tn), lambda j, b: (0, order * nj + j))
    out_sds = jax.ShapeDtypeStruct((BATCH * HALF, D_MODEL), BF16)
    out_spec = pl.BlockSpec((HALF, tn), lambda j, b: (b, j))
    return pl.pallas_call(
        functools.partial(_dft2_fwd_kernel, slab=256),
        grid=(nj, BATCH),
        in_specs=[mat, mat, mat, mat, pl.BlockSpec((SEQ, tn), lambda j, b: (b, zc + j)),
                  spec, spec, spec, spec],
        out_specs=(out_spec,) * 4,
        out_shape=(out_sds,) * 4,
        scratch_shapes=[pltpu.VMEM((tn // LANES, SEQ, LANES), F32)],
        compiler_params=_params("arbitrary", "arbitrary"),
        name="dft_analysis",
    )(*tables, z_arr, *spectra)


def _dft2_inv_kernel(tce_ref, tse_ref, tco_ref, tso_ref, ue_ref, ve_ref, uo_ref, vo_ref,
                     z_ref, gate_ref, skip_ref, o_ref, ys_ref):
    panels, rows, _ = ys_ref.shape
    y_even = _dot(tce_ref[...], ue_ref[...]) + _dot(tse_ref[...], ve_ref[...])
    y_odd = _dot(tco_ref[...], uo_ref[...]) + _dot(tso_ref[...], vo_ref[...])
    for k in range(panels):
        ys_ref[k, pl.ds(0, rows // 2, stride=2), :] = y_even[:, k * LANES:(k + 1) * LANES]
        ys_ref[k, pl.ds(1, rows // 2, stride=2), :] = y_odd[:, k * LANES:(k + 1) * LANES]
    y = jnp.concatenate([ys_ref[k] for k in range(panels)], axis=1)
    y = y + z_ref[...].astype(F32) * skip_ref[...]
    o_ref[...] = (gate_ref[...].astype(F32) * y).astype(o_ref.dtype)


def _dft2_inv(tables, uv, z_arr, z_col0, gate_arr, gate_col0, skip):
    tm, tn = SEQ, 512
    ni, nj = SEQ // tm, D_MODEL // tn
    zc, gc = z_col0 // tn, gate_col0 // tn
    mat = pl.BlockSpec((tm // 2, HALF), lambda j, b, i: (i, 0), pipeline_mode=pl.Buffered(1))
    coef = pl.BlockSpec((HALF, tn), lambda j, b, i: (b, j))
    return pl.pallas_call(
        _dft2_inv_kernel,
        grid=(nj, BATCH, ni),
        in_specs=[mat, mat, mat, mat, coef, coef, coef, coef,
                  pl.BlockSpec((tm, tn), lambda j, b, i: (b * ni + i, zc + j)),
                  pl.BlockSpec((tm, tn), lambda j, b, i: (b * ni + i, gc + j)),
                  pl.BlockSpec((1, tn), lambda j, b, i: (0, j))],
        out_specs=pl.BlockSpec((tm, tn), lambda j, b, i: (b * ni + i, j)),
        out_shape=jax.ShapeDtypeStruct((TOKENS, D_MODEL), BF16),
        scratch_shapes=[pltpu.VMEM((tn // LANES, tm, LANES), F32)],
        compiler_params=_params("arbitrary", "arbitrary", "arbitrary"),
        name="dft_synthesis",
    )(*tables, *uv, z_arr, gate_arr, skip.reshape(1, D_MODEL))


def _mlp_block(x2d, mod, l, norm_g, w1, w2):
    sh2, sc2, g2 = (mod[l, :BATCH, k * D_MODEL:(k + 1) * D_MODEL] for k in (3, 4, 5))
    h = _prep(x2d.reshape(BATCH, SEQ, D_MODEL), norm_g[l, 1], sc2, sh2, BF16, 1024).reshape(TOKENS, D_MODEL)
    hid = _mm_plain(h, w1, D_FF, BF16, _epi_relu2, 1024, 1024, "mlp_up", w_layer=l)
    return _mm_residual([hid], w2, l, x2d, g2, 512, 512, "mlp_down", w_single_buffer=True)


def kernel(x, c, ctx, c_ctx, ada_w, ada_b, norm_g, lb_logits, ab_w_in, ab_conv_w, ab_gnorm_g, ab_w_out,
           hy_in_w, hy_short_w, hy_out_w, hy_fw1, hy_fb1, hy_fw2, hy_fb2, hy_fw3, hy_fb3, hy_fw4, hy_freq,
           hy_skip, mlp_w1, mlp_w2, final_g):
    d = D_MODEL
    cond = jnp.concatenate([c, c_ctx[None, :], jnp.zeros((3, d), F32)], axis=0)
    mod = _ada(cond, ada_w, ada_b)
    x2d = x.reshape(TOKENS, d)

    sh1, sc1, g1 = (mod[0, :BATCH, k * d:(k + 1) * d] for k in (0, 1, 2))
    h = _prep(x, norm_g[0, 0], sc1, sh1, BF16, 1024).reshape(TOKENS, d)
    csh = jnp.broadcast_to(mod[0, BATCH:BATCH + 1, 0:d], (BATCH, d))
    csc = jnp.broadcast_to(mod[0, BATCH:BATCH + 1, d:2 * d], (BATCH, d))
    hc = _prep(ctx, norm_g[0, 0], csc, csh, BF16, CTX_LEN).reshape(BATCH * CTX_LEN, d)
    p = _mm_plain(h, ab_w_in, AB_IN_WIDTH, F32, _epi_store, 1024, 1024, "ab_in_proj", w_layer=0)
    cp = _mm_plain(hc, ab_w_in, 3 * A_WIDTH, F32, _epi_store, 1024, 1024, "ab_ctx_proj", w_layer=0)
    mix_a = _hgrn_scan(p, cp, lb_logits, ab_gnorm_g[0], 0)
    mix_b = _gconv(p, ab_conv_w[0])
    x2d = _mm_residual([mix_a, mix_b], ab_w_out, 0, x2d, g1, 1024, 1024, "ab_out_proj")
    x2d = _mlp_block(x2d, mod, 0, norm_g, mlp_w1, mlp_w2)

    sh1, sc1, g1 = (mod[1, :BATCH, k * d:(k + 1) * d] for k in (0, 1, 2))
    h = _prep(x2d.reshape(BATCH, SEQ, d), norm_g[1, 0], sc1, sh1, BF16, 1024).reshape(TOKENS, d)
    pc = _mm([h], hy_in_w, w_layer=0, tm=1024, tn=1024, n_cols=3 * d, epi=_epi_conv3_grid,
             extras=(hy_short_w[0],), extra_specs=(pl.BlockSpec((3, 1024), lambda j, i: (0, j)),),
             out_shapes=jax.ShapeDtypeStruct((TOKENS, 3 * d), BF16),
             out_specs=pl.BlockSpec((1024, 1024), lambda j, i: (i, j)), name="hy_in_proj")
    taps_sum, taps_diff = _hyena_filter_taps(hy_fw1[0], hy_fb1[0], hy_fw2[0], hy_fb2[0], hy_fw3[0],
                                             hy_fb3[0], hy_fw4[0], hy_freq[0])
    analysis, synthesis = _dft_half_tables()
    spectra = _filter_dft2(analysis, taps_sum, taps_diff)
    uv = _dft2_fwd(analysis, pc, 2 * d, spectra, 0)
    z = _dft2_inv(synthesis, uv, pc, 2 * d, pc, 0, hy_skip[0, 0])
    uv = _dft2_fwd(analysis, z, 0, spectra, 1)
    z = _dft2_inv(synthesis, uv, z, 0, pc, d, hy_skip[0, 1])
    x2d = _mm_residual([z], hy_out_w, 0, x2d, g1, 1024, 1024, "hy_out_proj")
    x2d = _mlp_block(x2d, mod, 1, norm_g, mlp_w1, mlp_w2)

    zeros = jnp.zeros((BATCH, d), F32)
    return _prep(x2d.reshape(BATCH, SEQ, d), final_g, zeros, zeros, F32, 1024)
```

```python
import functools
import math

import numpy as np
import jax
import jax.numpy as jnp
from jax import lax
from jax.experimental import pallas as pl
from jax.experimental.pallas import tpu as pltpu

F32 = jnp.float32
BF16 = jnp.bfloat16

D_MODEL = 2048
BATCH = 4
SEQ = 2048
CTX_LEN = 256
GRID_W = 64
EPS = 1e-6
A_HEAD_DIM = 128
A_WIDTH = D_MODEL // 2
A_HEADS = A_WIDTH // A_HEAD_DIM
B_WIDTH = D_MODEL - A_WIDTH
AB_IN_WIDTH = 5 * A_WIDTH + 3 * B_WIDTH
HY_EMB = 33
HY_BANDS = (HY_EMB - 1) // 2
HY_HIDDEN = 64
HY_DECAY_TARGET = 1e-2
HY_FAST_PCT = 0.3
HY_SLOW_PCT = 1.5
D_FF = 4 * D_MODEL
TOKENS = BATCH * SEQ

SCAN_CHUNK = 64
SCAN_LEVELS = (1, 2, 4, 8, 16, 32)

VMEM_LIMIT_BYTES = 56 * 1024 * 1024


def _params(*sem):
    return pltpu.CompilerParams(dimension_semantics=sem, vmem_limit_bytes=VMEM_LIMIT_BYTES)


def _dot(a, b):
    return jnp.dot(a, b, preferred_element_type=F32)


def _dot_nt(a, b):
    return lax.dot_general(a, b, (((1,), (1,)), ((), ())), preferred_element_type=F32)


def _dot_tn(a, b):
    return lax.dot_general(a, b, (((0,), (0,)), ((), ())), preferred_element_type=F32)


def _dot_f32(a, b):
    return jnp.dot(a, b, precision=lax.Precision.HIGHEST, preferred_element_type=F32)


def _ada_kernel(c_ref, w_ref, b_ref, o_ref):
    c = c_ref[...]
    a = (c * jax.nn.sigmoid(c)).astype(BF16)
    o_ref[...] = _dot(a, w_ref[...].astype(BF16)) + b_ref[...]


def _ada(cond, ada_w, ada_b):
    depth, d, n = ada_w.shape
    tn = 1024
    return pl.pallas_call(
        _ada_kernel,
        grid=(depth, n // tn),
        in_specs=[
            pl.BlockSpec((8, d), lambda l, j: (0, 0)),
            pl.BlockSpec((None, d, tn), lambda l, j: (l, 0, j)),
            pl.BlockSpec((None, 1, tn), lambda l, j: (l, 0, j)),
        ],
        out_specs=pl.BlockSpec((None, 8, tn), lambda l, j: (l, 0, j)),
        out_shape=jax.ShapeDtypeStruct((depth, 8, n), F32),
        compiler_params=_params("arbitrary", "arbitrary"),
        name="ada_mod",
    )(cond, ada_w, ada_b.reshape(depth, 1, n))


def _prep_kernel(x_ref, g_ref, sc_ref, sh_ref, o_ref):
    x = x_ref[...]
    ms = jnp.mean(x * x, axis=-1, keepdims=True)
    xn = x * lax.rsqrt(ms + EPS) * g_ref[...]
    o_ref[...] = (xn * (1.0 + sc_ref[...]) + sh_ref[...]).astype(o_ref.dtype)


def _prep(x, g, sc, sh, out_dtype, ts):
    b, l, d = x.shape
    return pl.pallas_call(
        _prep_kernel,
        grid=(b, l // ts),
        in_specs=[
            pl.BlockSpec((None, ts, d), lambda i, j: (i, j, 0)),
            pl.BlockSpec((1, d), lambda i, j: (0, 0)),
            pl.BlockSpec((None, 1, d), lambda i, j: (i, 0, 0)),
            pl.BlockSpec((None, 1, d), lambda i, j: (i, 0, 0)),
        ],
        out_specs=pl.BlockSpec((None, ts, d), lambda i, j: (i, j, 0)),
        out_shape=jax.ShapeDtypeStruct((b, l, d), out_dtype),
        compiler_params=_params("arbitrary", "arbitrary"),
        name="norm_mod",
    )(x, g.reshape(1, d), sc.reshape(b, 1, d), sh.reshape(b, 1, d))


def _mm_kernel(*refs, n_a, k_sizes, epi, cast_w):
    a_refs = refs[:n_a]
    w_ref = refs[n_a]
    rest = refs[n_a + 1:]
    if cast_w:
        w_ref, rest = rest[-1], rest[:-1]

        @pl.when(pl.program_id(1) == 0)
        def _():
            w_ref[...] = refs[n_a][...].astype(BF16)

    acc = None
    off = 0
    for a_ref, k in zip(a_refs, k_sizes):
        part = _dot(a_ref[...], w_ref[off:off + k, :])
        acc = part if acc is None else acc + part
        off += k
    epi(acc, *rest)


def _mm(a_list, w, *, tm, tn, n_cols, epi, extras, extra_specs, out_shapes, out_specs, name,
        w_layer=None, w_single_buffer=False):
    m = a_list[0].shape[0]
    k_sizes = tuple(a.shape[1] for a in a_list)
    k_total = sum(k_sizes)
    cast_w = w_layer is not None
    assert w.shape[-2] == k_total and m % tm == 0 and n_cols % tn == 0
    in_specs = [pl.BlockSpec((tm, k), lambda j, i: (i, 0)) for k in k_sizes]
    w_mode = dict(pipeline_mode=pl.Buffered(1)) if w_single_buffer else {}
    if cast_w:
        in_specs.append(pl.BlockSpec((None, k_total, tn), lambda j, i: (w_layer, 0, j), **w_mode))
    else:
        in_specs.append(pl.BlockSpec((k_total, tn), lambda j, i: (0, j), **w_mode))
    in_specs.extend(extra_specs)
    return pl.pallas_call(
        functools.partial(_mm_kernel, n_a=len(a_list), k_sizes=k_sizes, epi=epi, cast_w=cast_w),
        grid=(n_cols // tn, m // tm),
        in_specs=in_specs,
        out_specs=out_specs,
        out_shape=out_shapes,
        scratch_shapes=[pltpu.VMEM((k_total, tn), BF16)] if cast_w else [],
        compiler_params=_params("arbitrary", "arbitrary"),
        name=name,
    )(*a_list, w, *extras)


def _epi_store(acc, o_ref):
    o_ref[...] = acc.astype(o_ref.dtype)


def _epi_relu2(acc, o_ref):
    r = jnp.maximum(acc, 0.0)
    o_ref[...] = (r * r).astype(o_ref.dtype)


def _epi_residual(acc, x_ref, gate_ref, o_ref):
    o_ref[...] = x_ref[...] + gate_ref[...] * acc


def _epi_conv3_grid(acc, w_ref, o_ref):
    rows = lax.broadcasted_iota(jnp.int32, acc.shape, 0) % GRID_W
    prev = jnp.where(rows == 0, 0.0, pltpu.roll(acc, 1, 0))
    nxt = jnp.where(rows == GRID_W - 1, 0.0, pltpu.roll(acc, acc.shape[0] - 1, 0))
    w = w_ref[...]
    o_ref[...] = (w[0:1] * prev + w[1:2] * acc + w[2:3] * nxt).astype(o_ref.dtype)


def _mm_plain(a, w, n_cols, out_dtype, epi, tm, tn, name, w_layer=None):
    m = a.shape[0]
    return _mm([a], w, tm=tm, tn=tn, n_cols=n_cols, epi=epi, extras=(), extra_specs=(),
               out_shapes=jax.ShapeDtypeStruct((m, n_cols), out_dtype),
               out_specs=pl.BlockSpec((tm, tn), lambda j, i: (i, j)), name=name, w_layer=w_layer)


def _mm_residual(a_list, w, w_layer, x, gate, tm, tn, name, w_single_buffer=False):
    m, n = x.shape
    tiles_per_batch = SEQ // tm
    return _mm(a_list, w, tm=tm, tn=tn, n_cols=n, epi=_epi_residual,
               extras=(x, gate.reshape(BATCH, 1, n)),
               extra_specs=(pl.BlockSpec((tm, tn), lambda j, i: (i, j)),
                            pl.BlockSpec((None, 1, tn), lambda j, i: (i // tiles_per_batch, 0, j))),
               out_shapes=jax.ShapeDtypeStruct((m, n), F32),
               out_specs=pl.BlockSpec((tm, tn), lambda j, i: (i, j)), name=name,
               w_layer=w_layer, w_single_buffer=w_single_buffer)


def _gconv_kernel(u_ref, gb_ref, gc_ref, w_ref, o_ref):
    t = gc_ref[...] * u_ref[...]
    rows = lax.broadcasted_iota(jnp.int32, t.shape, 0) % GRID_W
    prev = jnp.where(rows == 0, 0.0, pltpu.roll(t, 1, 0))
    nxt = jnp.where(rows == GRID_W - 1, 0.0, pltpu.roll(t, t.shape[0] - 1, 0))
    w = w_ref[...]
    o_ref[...] = (gb_ref[...] * (w[0:1] * prev + w[1:2] * t + w[2:3] * nxt)).astype(o_ref.dtype)


def _gconv(p, conv_w):
    tm, tn = 512, 512
    nb = B_WIDTH // tn
    base = 5 * A_WIDTH // tn
    return pl.pallas_call(
        _gconv_kernel,
        grid=(TOKENS // tm, nb),
        in_specs=[
            pl.BlockSpec((tm, tn), lambda i, j: (i, base + j)),
            pl.BlockSpec((tm, tn), lambda i, j: (i, base + nb + j)),
            pl.BlockSpec((tm, tn), lambda i, j: (i, base + 2 * nb + j)),
            pl.BlockSpec((3, tn), lambda i, j: (0, j)),
        ],
        out_specs=pl.BlockSpec((tm, tn), lambda i, j: (i, j)),
        out_shape=jax.ShapeDtypeStruct((TOKENS, B_WIDTH), BF16),
        compiler_params=_params("arbitrary", "arbitrary"),
        name="gated_conv",
    )(p, p, p, conv_w)


SUBLANES = 8
LANES = 128


def _row_groups(x):
    return [x[k:k + SUBLANES] for k in range(0, x.shape[0], SUBLANES)]


def _cumsum_groups(groups, sub):
    out, carry = [], None
    for g in groups:
        for s in (1, 2, 4):
            g = g + jnp.where(sub >= s, pltpu.roll(g, s, 0), 0.0)
        if carry is not None:
            g = g + carry
        out.append(g)
        carry = g[SUBLANES - 1:SUBLANES]
    return out


def _hold_boundary(cum, w, k, sub):
    shape = cum[k].shape
    if w >= SUBLANES // 2:
        r = (k * SUBLANES // (2 * w)) * 2 * w + w - 1
        return jnp.broadcast_to(cum[r // SUBLANES][r % SUBLANES:r % SUBLANES + 1], shape)
    assert w == 2
    return jnp.where(sub < 4, jnp.broadcast_to(cum[k][1:2], shape), jnp.broadcast_to(cum[k][5:6], shape))


def _scan_kernel(zf_ref, zb_ref, v_ref, q_ref, g_ref, czf_ref, czb_ref, cv_ref, lbl_ref, gn_ref,
                 o_ref, of_ref, ob_ref, st_ref, qs_ref, inc_ref, dec_ref, kl_ref, lv_ref, att_ref, *, layer):
    c = SCAN_CHUNK
    n_chunks = SEQ // c
    n_ctx = CTX_LEN // c
    z_refs = (zf_ref, zb_ref)
    cz_refs = (czf_ref, czb_ref)
    out_refs = (of_ref, ob_ref)

    logit_rows = [lbl_ref[:, k, :] for k in range(lbl_ref.shape[1])]
    top = functools.reduce(jnp.maximum, logit_rows)
    exps = [jnp.exp(r - top) for r in logit_rows]
    lbs = sum(exps[:layer + 1]) / sum(exps)

    sub = lax.broadcasted_iota(jnp.int32, (SUBLANES, A_HEAD_DIM), 0)
    n_groups = c // SUBLANES

    def rows(groups):
        return jnp.concatenate(groups, axis=0)

    def chunk(d, z, q, v, want_out):
        lb = lbs[d:d + 1]
        f = lb + (1.0 - lb) * jax.nn.sigmoid(z)
        kk = 1.0 - f
        fg, kg = _row_groups(f), _row_groups(kk)
        lg = [jnp.log2(g) for g in fg]
        cum = _cumsum_groups(lg, sub)
        total = cum[-1][SUBLANES - 1:SUBLANES]
        if d == 0:
            pos = cum
            q_in = [jnp.exp2(g) for g in cum]
            k_out = [jnp.exp2(total - g) for g in cum]
        else:
            pos = [a - b for a, b in zip(cum, lg)]
            q_in = [jnp.exp2(total - g) for g in pos]
            k_out = [jnp.exp2(g) for g in pos]
        k_leave = rows([a * b for a, b in zip(kg, k_out)]).astype(BF16)
        decay = jnp.exp2(total)
        if not want_out:
            return k_leave, decay, None, None, None
        qg = _row_groups(q)
        o_diag = jnp.sum(q * kk, axis=1, keepdims=True) * v
        upper, lower = (qg, kg) if d == 0 else (kg, qg)
        operands = []
        for w in SCAN_LEVELS:
            y = []
            for k in range(n_groups):
                if w == 1:
                    odd = (sub & 1) != 0
                    y.append(jnp.where(odd, qg[k] * fg[k], kg[k]) if d == 0
                             else jnp.where(odd, kg[k], qg[k] * fg[k]))
                    continue
                gap = pos[k] - _hold_boundary(cum, w, k, sub)
                if w >= SUBLANES:
                    bit_set = (k * SUBLANES) & w != 0
                    y.append(upper[k] * jnp.exp2(gap) if bit_set else lower[k] * jnp.exp2(-gap))
                else:
                    bit = (sub & w) != 0
                    y.append(jnp.where(bit, upper[k], lower[k]) * jnp.exp2(jnp.where(bit, gap, -gap)))
            operands.append(rows(y).astype(BF16))
        qs = rows([a * b for a, b in zip(qg, q_in)]).astype(BF16)
        return k_leave, decay, qs, o_diag, operands

    def within_chunk(d, operands):
        t_idx = lax.broadcasted_iota(jnp.int32, (c, c), 0)
        s_idx = lax.broadcasted_iota(jnp.int32, (c, c), 1)
        split = t_idx ^ s_idx
        att = None
        for w, yb in reversed(list(zip(SCAN_LEVELS, operands))):
            scores = _dot_nt(yb, yb)
            att = scores if att is None else jnp.where(split < 2 * w, scores, att)
        return jnp.where(t_idx > s_idx if d == 0 else t_idx < s_idx, att, 0.0)

    st_ref[...] = jnp.zeros_like(st_ref)

    def ctx_body(i, carry):
        for d in range(2):
            ci = i if d == 0 else n_ctx - 1 - i
            rows_i = pl.ds(pl.multiple_of(ci * c, c), c)
            k_leave, decay, _, _, _ = chunk(d, cz_refs[d][rows_i, :], None, None, False)
            st_ref[d] = st_ref[d] * decay + _dot_tn(cv_ref[rows_i, :].astype(BF16), k_leave)
        return carry

    lax.fori_loop(0, n_ctx, ctx_body, 0)

    def chunk_rows(ci):
        start = ci * c
        return pl.ds(start if isinstance(ci, int) else pl.multiple_of(start, c), c)

    def prepare(ci):
        rows_i = chunk_rows(ci)
        for d in range(2):
            k_leave, decay, qs, o_diag, operands = chunk(d, z_refs[d][rows_i, :], q_ref[rows_i, :],
                                                         v_ref[rows_i, :], True)
            kl_ref[d, rows_i, :] = k_leave
            dec_ref[d, ci] = decay
            qs_ref[d, rows_i, :] = qs
            out_refs[d][rows_i, :] = o_diag
            for lvl, yb in enumerate(operands):
                lv_ref[d, lvl, rows_i, :] = yb

    def scores(ci):
        rows_i = chunk_rows(ci)
        vb = v_ref[rows_i, :].astype(BF16)
        for d in range(2):
            inc_ref[d, ci] = _dot_tn(vb, kl_ref[d, rows_i, :])
            operands = [lv_ref[d, lvl, rows_i, :] for lvl in range(len(SCAN_LEVELS))]
            att_ref[d, rows_i, :] = within_chunk(d, operands).astype(BF16)

    pair = 2
    for j in range(pair):
        prepare(j)

    def skewed_body(i, carry):
        for j in range(pair):
            scores((i - 1) * pair + j)
        for j in range(pair):
            prepare(i * pair + j)
        return carry

    lax.fori_loop(1, n_chunks // pair, skewed_body, 0)
    for j in range(pair):
        scores(n_chunks - pair + j)

    def carry_body(i, carry):
        for d in range(2):
            ci = i if d == 0 else n_chunks - 1 - i
            rows_i = pl.ds(pl.multiple_of(ci * c, c), c)
            state = st_ref[d]
            out_refs[d][rows_i, :] += (_dot(att_ref[d, rows_i, :], v_ref[rows_i, :].astype(BF16))
                                       + _dot_nt(qs_ref[d, rows_i, :], state.astype(BF16)))
            st_ref[d] = state * dec_ref[d, ci] + inc_ref[d, ci]
        return carry

    lax.fori_loop(0, n_chunks, carry_body, 0, unroll=4)

    rb = 256
    gn = gn_ref[...]

    def read_body(i, carry):
        rows = pl.ds(pl.multiple_of(i * rb, rb), rb)
        o = of_ref[rows, :] + ob_ref[rows, :]
        ms = jnp.mean(o * o, axis=-1, keepdims=True)
        on = o * lax.rsqrt(ms + EPS) * gn
        g = g_ref[rows, :]
        o_ref[rows, :] = (on * (g * jax.nn.sigmoid(g))).astype(o_ref.dtype)
        return carry

    lax.fori_loop(0, SEQ // rb, read_body, 0)


def _hgrn_scan(p, cp, lb_logits, gnorm_g, layer):
    h = A_HEADS
    hd = A_HEAD_DIM
    n_lb = lb_logits.shape[1]

    def col(k):
        return pl.BlockSpec((SEQ, hd), lambda b, j, k=k: (b, k * h + j))

    def ccol(k):
        return pl.BlockSpec((CTX_LEN, hd), lambda b, j, k=k: (b, k * h + j))

    return pl.pallas_call(
        functools.partial(_scan_kernel, layer=layer),
        grid=(BATCH, h),
        in_specs=[col(0), col(1), col(2), col(3), col(4), ccol(0), ccol(1), ccol(2),
                  pl.BlockSpec((2, n_lb, hd), lambda b, j: (0, 0, j)),
                  pl.BlockSpec((1, hd), lambda b, j: (0, j))],
        out_specs=pl.BlockSpec((SEQ, hd), lambda b, j: (b, j)),
        out_shape=jax.ShapeDtypeStruct((TOKENS, A_WIDTH), BF16),
        scratch_shapes=[pltpu.VMEM((SEQ, hd), F32), pltpu.VMEM((SEQ, hd), F32),
                        pltpu.VMEM((2, hd, hd), F32), pltpu.VMEM((2, SEQ, hd), BF16),
                        pltpu.VMEM((2, SEQ // SCAN_CHUNK, hd, hd), F32),
                        pltpu.VMEM((2, SEQ // SCAN_CHUNK, 1, hd), F32),
                        pltpu.VMEM((2, SEQ, hd), BF16),
                        pltpu.VMEM((2, len(SCAN_LEVELS), SEQ, hd), BF16),
                        pltpu.VMEM((2, SEQ, SCAN_CHUNK), BF16)],
        compiler_params=_params("arbitrary", "arbitrary"),
        name="hgrn_scan",
    )(p, p, p, p, p, cp, cp, cp, lb_logits, gnorm_g.reshape(1, A_WIDTH))


def _split_bf16(x):
    hi = x.astype(BF16)
    return hi, (x - hi.astype(F32)).astype(BF16)


def _filter_kernel(z_ref, w1_ref, b1_ref, w2_ref, b2_ref, w3_ref, b3_ref, fr_ref,
                   w4f_top_ref, w4f_bot_ref, w4b_top_ref, w4b_bot_ref, sum_ref, diff_ref, hid_ref, *, tc):
    o = pl.program_id(0)
    j = pl.program_id(1)

    @pl.when((o == 0) & (j == 0))
    def _():
        fr = fr_ref[...]
        h = jnp.sin(fr * (_dot_f32(z_ref[...], w1_ref[...]) + b1_ref[...]))
        h = jnp.sin(fr * (_dot_f32(h, w2_ref[...]) + b2_ref[...]))
        h = jnp.sin(fr * (_dot_f32(h, w3_ref[...]) + b3_ref[...]))
        hi, lo = _split_bf16(h)
        hid_ref[...] = jnp.concatenate([hi, lo, hi], axis=1)

    def project(top_ref, bot_ref):
        halves = []
        for w_ref in (top_ref, bot_ref):
            w_hi, w_lo = _split_bf16(w_ref[...])
            halves.append(_dot(hid_ref[...], jnp.concatenate([w_hi, w_hi, w_lo], axis=0)))
        return jnp.concatenate(halves, axis=0)

    rows = lax.broadcasted_iota(jnp.int32, (SEQ, tc), 0)
    chan = lax.broadcasted_iota(jnp.int32, (SEQ, tc), 1) + j * tc
    lag = jnp.where(rows < SEQ // 2, 2 * rows, 2 * rows - (SEQ - 1))
    t = lag.astype(F32) * (1.0 / (SEQ - 1))
    max_decay = math.log(HY_DECAY_TARGET) / HY_FAST_PCT
    min_decay = math.log(HY_DECAY_TARGET) / HY_SLOW_PCT
    deltas = jnp.abs(min_decay + chan.astype(F32) * ((max_decay - min_decay) / (D_MODEL - 1)))
    window = jnp.exp(-t * deltas)
    fw = project(w4f_top_ref, w4f_bot_ref) * window
    bw = project(w4b_top_ref, w4b_bot_ref) * window
    first = rows == 0
    a = fw + jnp.where(first, bw, 0.0)
    bb = jnp.where(first, 0.0, bw)
    inv = 1.0 / jnp.sum(jnp.abs(a) + jnp.abs(bb), axis=0, keepdims=True)
    sum_ref[...] = ((a + bb) * inv).astype(sum_ref.dtype)
    diff_ref[...] = ((a - bb) * inv).astype(diff_ref.dtype)


def _hyena_filter_taps(fw1, fb1, fw2, fb2, fw3, fb3, fw4, freq):
    l = SEQ
    pos = jnp.arange(l, dtype=F32)
    t = jnp.linspace(0.0, 1.0, l, dtype=F32)
    w = 2.0 * math.pi * pos / l
    bands = jnp.linspace(1e-4, HY_BANDS - 1, HY_BANDS, dtype=F32)
    ang = w[:, None] * bands[None, :]
    z = jnp.concatenate([t[:, None], jnp.cos(ang), -jnp.sin(ang)], axis=-1)
    z = jnp.concatenate([z[0::2], z[1::2]], axis=0)
    tc = 256
    nj = D_MODEL // tc
    hw = LANES // 2

    def pad(a, rows, cols):
        return jnp.pad(a, ((0, rows - a.shape[0]), (0, cols - a.shape[1])))

    def twice(a):
        a = pad(a, hw, hw)
        zero = jnp.zeros_like(a)
        return jnp.concatenate([jnp.concatenate([a, zero], axis=1), jnp.concatenate([zero, a], axis=1)], axis=0)

    def row2(v):
        v = pad(v[None, :], 1, hw)
        return jnp.concatenate([v, v], axis=1)

    def small(shape):
        return pl.BlockSpec(shape, lambda o, j: (0, 0))

    z = pad(z, l, hw)
    z2 = jnp.concatenate([z[:l // 2], z[l // 2:]], axis=1)
    w4_top = pad(fw4, LANES, fw4.shape[1])
    w4_bot = jnp.concatenate([jnp.zeros((hw, fw4.shape[1]), F32), pad(fw4, hw, fw4.shape[1])], axis=0)
    out_sds = jax.ShapeDtypeStruct((l, 2 * D_MODEL), BF16)
    out_spec = pl.BlockSpec((l, tc), lambda o, j: (0, o * nj + j))
    side_f = pl.BlockSpec((LANES, tc), lambda o, j: (0, 2 * o * nj + j))
    side_b = pl.BlockSpec((LANES, tc), lambda o, j: (0, (2 * o + 1) * nj + j))
    sq, vec = small((LANES, LANES)), small((1, LANES))
    return pl.pallas_call(
        functools.partial(_filter_kernel, tc=tc),
        grid=(2, nj),
        in_specs=[small((l // 2, LANES)), sq, vec, sq, vec, sq, vec, vec, side_f, side_f, side_b, side_b],
        out_specs=(out_spec, out_spec),
        out_shape=(out_sds, out_sds),
        scratch_shapes=[pltpu.VMEM((l // 2, 3 * LANES), BF16)],
        compiler_params=_params("arbitrary", "arbitrary"),
        name="hyena_filter",
    )(z2, twice(fw1), row2(fb1), twice(fw2), row2(fb2), twice(fw3), row2(fb3), row2(freq),
      w4_top, w4_bot, w4_top, w4_bot)


def _dft_matrices():
    l = SEQ
    period = 4 * l

    def trig(n):
        ang = (n % period).astype(F32) * (math.pi / (2 * l))
        return jnp.cos(ang), jnp.sin(ang)

    lanes = 128
    groups = l // lanes
    row = jnp.arange(l, dtype=jnp.int32)[:, None]
    lo = jnp.arange(lanes, dtype=jnp.int32)[None, :]
    hi = jnp.arange(groups, dtype=jnp.int32)[None, :]

    def combine(c_hi, s_hi, c_lo, s_lo, scale):
        cm = (c_hi[:, :, None] * c_lo[:, None, :] - s_hi[:, :, None] * s_lo[:, None, :]).reshape(l, l)
        sm = (s_hi[:, :, None] * c_lo[:, None, :] + c_hi[:, :, None] * s_lo[:, None, :]).reshape(l, l)
        return (cm * scale).astype(BF16), (sm * scale).astype(BF16)

    odd = 2 * row + 1
    c_hi, s_hi = trig(odd * (lanes * hi))
    c_lo, s_lo = trig(odd * lo)
    c_fm, s_fm = combine(c_hi, s_hi, c_lo, s_lo, 1.0)
    c_hi, s_hi = trig(row * (2 * lanes * hi))
    c_lo, s_lo = trig(row * (2 * lo + 1))
    c_mf, s_mf = combine(c_hi, s_hi, c_lo, s_lo, 1.0 / l)
    return c_fm, s_fm, c_mf, s_mf


def _dft_fwd_kernel(c_ref, s_ref, z_ref, p_ref, q_ref, u_ref, v_ref):
    z = z_ref[...]
    a = _dot(c_ref[...], z)
    b = _dot(s_ref[...], z)
    p = p_ref[...]
    q = q_ref[...]
    u_ref[...] = (a * p - b * q).astype(u_ref.dtype)
    v_ref[...] = (a * q + b * p).astype(v_ref.dtype)


def _dft_fwd(c_fm, s_fm, z_arr, z_col0, pq_p, pq_q, order):
    tm, tn = 1024, 512
    ni, nj = SEQ // tm, D_MODEL // tn
    zc = z_col0 // tn
    out_sds = jax.ShapeDtypeStruct((TOKENS, D_MODEL), BF16)
    out_spec = pl.BlockSpec((tm, tn), lambda j, i, b: (b * ni + i, j))
    return pl.pallas_call(
        _dft_fwd_kernel,
        grid=(nj, ni, BATCH),
        in_specs=[pl.BlockSpec((tm, SEQ), lambda j, i, b: (i, 0)),
                  pl.BlockSpec((tm, SEQ), lambda j, i, b: (i, 0)),
                  pl.BlockSpec((SEQ, tn), lambda j, i, b: (b, zc + j)),
                  pl.BlockSpec((tm, tn), lambda j, i, b: (i, order * nj + j)),
                  pl.BlockSpec((tm, tn), lambda j, i, b: (i, order * nj + j))],
        out_specs=(out_spec, out_spec),
        out_shape=(out_sds, out_sds),
        compiler_params=_params("arbitrary", "arbitrary", "arbitrary"),
        name="dft_analysis",
    )(c_fm, s_fm, z_arr, pq_p, pq_q)


def _dft_inv_kernel(c_ref, s_ref, u_ref, v_ref, z_ref, gate_ref, skip_ref, o_ref):
    y = _dot(c_ref[...], u_ref[...]) + _dot(s_ref[...], v_ref[...])
    y = y + z_ref[...].astype(F32) * skip_ref[...]
    o_ref[...] = (gate_ref[...].astype(F32) * y).astype(o_ref.dtype)


def _dft_inv(c_mf, s_mf, u, v, z_arr, z_col0, gate_arr, gate_col0, skip):
    tm, tn = 1024, 512
    ni, nj = SEQ // tm, D_MODEL // tn
    zc, gc = z_col0 // tn, gate_col0 // tn
    return pl.pallas_call(
        _dft_inv_kernel,
        grid=(nj, BATCH, ni),
        in_specs=[pl.BlockSpec((tm, SEQ), lambda j, b, i: (i, 0)),
                  pl.BlockSpec((tm, SEQ), lambda j, b, i: (i, 0)),
                  pl.BlockSpec((SEQ, tn), lambda j, b, i: (b, j)),
                  pl.BlockSpec((SEQ, tn), lambda j, b, i: (b, j)),
                  pl.BlockSpec((tm, tn), lambda j, b, i: (b * ni + i, zc + j)),
                  pl.BlockSpec((tm, tn), lambda j, b, i: (b * ni + i, gc + j)),
                  pl.BlockSpec((1, tn), lambda j, b, i: (0, j))],
        out_specs=pl.BlockSpec((tm, tn), lambda j, b, i: (b * ni + i, j)),
        out_shape=jax.ShapeDtypeStruct((TOKENS, D_MODEL), BF16),
        compiler_params=_params("arbitrary", "arbitrary", "arbitrary"),
        name="dft_synthesis",
    )(c_mf, s_mf, u, v, z_arr, gate_arr, skip.reshape(1, D_MODEL))


HALF = SEQ // 2


def _trig_table(a_rows, b_hi, b_lo, scale):
    period = 4 * SEQ

    def trig(n):
        ang = (n % period).astype(F32) * (math.pi / (2 * SEQ))
        return jnp.cos(ang), jnp.sin(ang)

    rows = a_rows.shape[0]
    c_hi, s_hi = trig(a_rows[:, None] * b_hi[None, :])
    c_lo, s_lo = trig(a_rows[:, None] * b_lo[None, :])
    cm = (c_hi[:, :, None] * c_lo[:, None, :] - s_hi[:, :, None] * s_lo[:, None, :]).reshape(rows, -1)
    sm = (s_hi[:, :, None] * c_lo[:, None, :] + c_hi[:, :, None] * s_lo[:, None, :]).reshape(rows, -1)
    return (cm * scale).astype(BF16), (sm * scale).astype(BF16)


def _dft_half_tables():
    lanes = 128
    idx = jnp.arange(HALF, dtype=jnp.int32)
    hi = jnp.arange(HALF // lanes, dtype=jnp.int32) * (2 * lanes)
    lo = jnp.arange(lanes, dtype=jnp.int32)
    analysis, synthesis = [], []
    for par in (0, 1):
        analysis.append(_trig_table(2 * idx + 1, hi, 2 * lo + par, 1.0))
        synthesis.append(_trig_table(2 * idx + par, hi, 2 * lo + 1, 1.0 / SEQ))
    (cee, see), (ceo, seo) = analysis
    (tce, tse), (tco, tso) = synthesis
    return (cee, ceo, see, seo), (tce, tse, tco, tso)


def _filter_dft2_kernel(cee_ref, ceo_ref, see_ref, seo_ref, fs_ref, fd_ref, p_ref, pm_ref, q_ref, qm_ref):
    pe = _dot(cee_ref[...], fs_ref[0:HALF, :])
    po = _dot(ceo_ref[...], fs_ref[HALF:SEQ, :])
    qe = _dot(see_ref[...], fd_ref[0:HALF, :])
    qo = _dot(seo_ref[...], fd_ref[HALF:SEQ, :])
    p_ref[...] = pe + po
    pm_ref[...] = pe - po
    q_ref[...] = qe + qo
    qm_ref[...] = qo - qe


def _filter_dft2(tables, taps_sum, taps_diff):
    tm, tn = HALF, 256
    n_cols = taps_sum.shape[1]
    mat = pl.BlockSpec((tm, HALF), lambda j, i: (i, 0), pipeline_mode=pl.Buffered(1))
    tap = pl.BlockSpec((SEQ, tn), lambda j, i: (0, j))
    out_spec = pl.BlockSpec((tm, tn), lambda j, i: (i, j))
    out_sds = jax.ShapeDtypeStruct((HALF, n_cols), F32)
    return pl.pallas_call(
        _filter_dft2_kernel,
        grid=(n_cols // tn, HALF // tm),
        in_specs=[mat, mat, mat, mat, tap, tap],
        out_specs=(out_spec,) * 4,
        out_shape=(out_sds,) * 4,
        compiler_params=_params("arbitrary", "arbitrary"),
        name="filter_dft",
    )(*tables, taps_sum, taps_diff)


def _dft2_fwd_kernel(cee_ref, ceo_ref, see_ref, seo_ref, z_ref, p_ref, pm_ref, q_ref, qm_ref,
                     ue_ref, ve_ref, uo_ref, vo_ref, zs_ref, *, slab):
    panels = zs_ref.shape[0]
    for k in range(panels):
        zs_ref[k] = z_ref[:, k * LANES:(k + 1) * LANES].astype(F32)
    ze = jnp.concatenate([zs_ref[k, pl.ds(0, HALF, stride=2), :] for k in range(panels)], axis=1).astype(BF16)
    zo = jnp.concatenate([zs_ref[k, pl.ds(1, HALF, stride=2), :] for k in range(panels)], axis=1).astype(BF16)
    for r in range(0, HALF, slab):
        rows = slice(r, r + slab)
        ec = _dot(cee_ref[rows, :], ze)
        oc = _dot(ceo_ref[rows, :], zo)
        es = _dot(see_ref[rows, :], ze)
        os_ = _dot(seo_ref[rows, :], zo)
        a, am, b, bm = ec + oc, ec - oc, es + os_, os_ - es
        p, pm, q, qm = p_ref[rows, :], pm_ref[rows, :], q_ref[rows, :], qm_ref[rows, :]
        u, v = a * p - b * q, a * q + b * p
        um, vm = am * pm - bm * qm, am * qm + bm * pm
        ue_ref[rows, :] = (u + um).astype(ue_ref.dtype)
        uo_ref[rows, :] = (u - um).astype(uo_ref.dtype)
        ve_ref[rows, :] = (v - vm).astype(ve_ref.dtype)
        vo_ref[rows, :] = (v + vm).astype(vo_ref.dtype)


def _dft2_fwd(tables, z_arr, z_col0, spectra, order):
    tn = 512
    nj = D_MODEL // tn
    zc = z_col0 // tn
    mat = pl.BlockSpec((HALF, HALF), lambda j, b: (0, 0), pipeline_mode=pl.Buffered(1))
    spec = pl.BlockSpec((HALF, tn), lambda j, b: (0, order * nj + j))
    out_sds = jax.ShapeDtypeStruct((BATCH * HALF, D_MODEL), BF16)
    out_spec = pl.BlockSpec((HALF, tn), lambda j, b: (b, j))
    return pl.pallas_call(
        functools.partial(_dft2_fwd_kernel, slab=256),
        grid=(nj, BATCH),
        in_specs=[mat, mat, mat, mat, pl.BlockSpec((SEQ, tn), lambda j, b: (b, zc + j)),
                  spec, spec, spec, spec],
        out_specs=(out_spec,) * 4,
        out_shape=(out_sds,) * 4,
        scratch_shapes=[pltpu.VMEM((tn // LANES, SEQ, LANES), F32)],
        compiler_params=_params("arbitrary", "arbitrary"),
        name="dft_analysis",
    )(*tables, z_arr, *spectra)


def _dft2_inv_kernel(tce_ref, tse_ref, tco_ref, tso_ref, ue_ref, ve_ref, uo_ref, vo_ref,
                     z_ref, gate_ref, skip_ref, o_ref, ys_ref):
    panels, rows, _ = ys_ref.shape
    y_even = _dot(tce_ref[...], ue_ref[...]) + _dot(tse_ref[...], ve_ref[...])
    y_odd = _dot(tco_ref[...], uo_ref[...]) + _dot(tso_ref[...], vo_ref[...])
    for k in range(panels):
        ys_ref[k, pl.ds(0, rows // 2, stride=2), :] = y_even[:, k * LANES:(k + 1) * LANES]
        ys_ref[k, pl.ds(1, rows // 2, stride=2), :] = y_odd[:, k * LANES:(k + 1) * LANES]
    y = jnp.concatenate([ys_ref[k] for k in range(panels)], axis=1)
    y = y + z_ref[...].astype(F32) * skip_ref[...]
    o_ref[...] = (gate_ref[...].astype(F32) * y).astype(o_ref.dtype)


def _dft2_inv(tables, uv, z_arr, z_col0, gate_arr, gate_col0, skip):
    tm, tn = SEQ, 512
    ni, nj = SEQ // tm, D_MODEL // tn
    zc, gc = z_col0 // tn, gate_col0 // tn
    mat = pl.BlockSpec((tm // 2, HALF), lambda j, b, i: (i, 0), pipeline_mode=pl.Buffered(1))
    coef = pl.BlockSpec((HALF, tn), lambda j, b, i: (b, j))
    return pl.pallas_call(
        _dft2_inv_kernel,
        grid=(nj, BATCH, ni),
        in_specs=[mat, mat, mat, mat, coef, coef, coef, coef,
                  pl.BlockSpec((tm, tn), lambda j, b, i: (b * ni + i, zc + j)),
                  pl.BlockSpec((tm, tn), lambda j, b, i: (b * ni + i, gc + j)),
                  pl.BlockSpec((1, tn), lambda j, b, i: (0, j))],
        out_specs=pl.BlockSpec((tm, tn), lambda j, b, i: (b * ni + i, j)),
        out_shape=jax.ShapeDtypeStruct((TOKENS, D_MODEL), BF16),
        scratch_shapes=[pltpu.VMEM((tn // LANES, tm, LANES), F32)],
        compiler_params=_params("arbitrary", "arbitrary", "arbitrary"),
        name="dft_synthesis",
    )(*tables, *uv, z_arr, gate_arr, skip.reshape(1, D_MODEL))


QUARTER = SEQ // 4


def _dft_quarter_tables():
    idx = jnp.arange(QUARTER, dtype=jnp.int32)
    lo = jnp.arange(LANES, dtype=jnp.int32)
    groups = jnp.arange(QUARTER // LANES, dtype=jnp.int32)
    ana_c, ana_s, syn_c, syn_s = [], [], [], []
    for r in range(4):
        c, s = _trig_table(2 * idx + 1, groups * (4 * LANES), 4 * lo + r, 1.0)
        ana_c.append(c)
        ana_s.append(s)
        c, s = _trig_table(4 * idx + r, groups * (2 * LANES), 2 * lo + 1, 1.0 / SEQ)
        syn_c.append(c)
        syn_s.append(s)
    fmap = jnp.concatenate([idx, HALF + idx])
    hi2 = jnp.arange(HALF // LANES, dtype=jnp.int32) * (2 * LANES)
    (cee, see), (ceo, seo) = (_trig_table(2 * fmap + 1, hi2, 2 * lo + par, 1.0) for par in (0, 1))
    return jnp.stack(ana_c + ana_s), jnp.stack(syn_c + syn_s), (cee, ceo, see, seo)


def _dft4_fwd_kernel(tab_ref, z_ref, p_ref, pm_ref, q_ref, qm_ref, x_ref, zs_ref, *, slab):
    panels = zs_ref.shape[0]
    for k in range(panels):
        zs_ref[k] = z_ref[:, k * LANES:(k + 1) * LANES].astype(F32)
    zr = [jnp.concatenate([zs_ref[k, pl.ds(r, QUARTER, stride=4), :] for k in range(panels)],
                          axis=1).astype(BF16) for r in range(4)]
    for f0 in range(0, QUARTER, slab):
        rows = slice(f0, f0 + slab)
        mirror = slice(QUARTER + f0, QUARTER + f0 + slab)
        c = [_dot(tab_ref[r, rows, :], zr[r]) for r in range(4)]
        s = [_dot(tab_ref[4 + r, rows, :], zr[r]) for r in range(4)]
        e, o, em, om = c[0] + c[2], c[1] + c[3], c[0] - c[2], c[1] - c[3]
        se, so, sem, som = s[0] + s[2], s[1] + s[3], s[0] - s[2], s[1] - s[3]
        a = (e + o, em + som, em - som, e - o)
        b = (se + so, om - sem, sem + om, so - se)
        pk = (p_ref[rows, :], pm_ref[mirror, :], p_ref[mirror, :], pm_ref[rows, :])
        qk = (q_ref[rows, :], qm_ref[mirror, :], q_ref[mirror, :], qm_ref[rows, :])
        u = [a[k] * pk[k] - b[k] * qk[k] for k in range(4)]
        v = [a[k] * qk[k] + b[k] * pk[k] for k in range(4)]
        up, um, wp, wm = u[0] + u[3], u[0] - u[3], u[1] + u[2], u[1] - u[2]
        vp, vm, yp, ym = v[0] + v[3], v[0] - v[3], v[1] + v[2], v[1] - v[2]
        outs = (up + wp, um + yp, up - wp, um - yp,
                vm - ym, vp + wm, vm + ym, vp - wm)
        for n, val in enumerate(outs):
            x_ref[n, rows, :] = val.astype(x_ref.dtype)


def _dft4_fwd(tab, z_arr, z_col0, spectra, order):
    tn = 512
    nj = D_MODEL // tn
    zc = z_col0 // tn
    spec = pl.BlockSpec((HALF, tn), lambda j, b: (0, order * nj + j))
    return pl.pallas_call(
        functools.partial(_dft4_fwd_kernel, slab=256),
        grid=(nj, BATCH),
        in_specs=[pl.BlockSpec(tab.shape, lambda j, b: (0, 0, 0), pipeline_mode=pl.Buffered(1)),
                  pl.BlockSpec((SEQ, tn), lambda j, b: (b, zc + j)), spec, spec, spec, spec],
        out_specs=pl.BlockSpec((8, QUARTER, tn), lambda j, b: (0, b, j)),
        out_shape=jax.ShapeDtypeStruct((8, BATCH * QUARTER, D_MODEL), BF16),
        scratch_shapes=[pltpu.VMEM((tn // LANES, SEQ, LANES), F32)],
        compiler_params=_params("arbitrary", "arbitrary"),
        name="dft_analysis",
    )(tab, z_arr, *spectra)


def _dft4_inv_kernel(tab_ref, x_ref, z_ref, gate_ref, skip_ref, o_ref, ys_ref):
    panels = ys_ref.shape[0]
    for r in range(4):
        y_r = _dot(tab_ref[r], x_ref[r]) + _dot(tab_ref[4 + r], x_ref[4 + r])
        for k in range(panels):
            ys_ref[k, pl.ds(r, QUARTER, stride=4), :] = y_r[:, k * LANES:(k + 1) * LANES]
    y = jnp.concatenate([ys_ref[k] for k in range(panels)], axis=1)
    y = y + z_ref[...].astype(F32) * skip_ref[...]
    o_ref[...] = (gate_ref[...].astype(F32) * y).astype(o_ref.dtype)


def _dft4_inv(tab, x, z_arr, z_col0, gate_arr, gate_col0, skip):
    tn = 512
    nj = D_MODEL // tn
    zc, gc = z_col0 // tn, gate_col0 // tn
    return pl.pallas_call(
        _dft4_inv_kernel,
        grid=(nj, BATCH),
        in_specs=[pl.BlockSpec(tab.shape, lambda j, b: (0, 0, 0), pipeline_mode=pl.Buffered(1)),
                  pl.BlockSpec((8, QUARTER, tn), lambda j, b: (0, b, j)),
                  pl.BlockSpec((SEQ, tn), lambda j, b: (b, zc + j)),
                  pl.BlockSpec((SEQ, tn), lambda j, b: (b, gc + j)),
                  pl.BlockSpec((1, tn), lambda j, b: (0, j))],
        out_specs=pl.BlockSpec((SEQ, tn), lambda j, b: (b, j)),
        out_shape=jax.ShapeDtypeStruct((TOKENS, D_MODEL), BF16),
        scratch_shapes=[pltpu.VMEM((tn // LANES, SEQ, LANES), F32)],
        compiler_params=_params("arbitrary", "arbitrary"),
        name="dft_synthesis",
    )(tab, x, z_arr, gate_arr, skip.reshape(1, D_MODEL))


def _mlp_block(x2d, mod, l, norm_g, w1, w2):
    sh2, sc2, g2 = (mod[l, :BATCH, k * D_MODEL:(k + 1) * D_MODEL] for k in (3, 4, 5))
    h = _prep(x2d.reshape(BATCH, SEQ, D_MODEL), norm_g[l, 1], sc2, sh2, BF16, 1024).reshape(TOKENS, D_MODEL)
    hid = _mm_plain(h, w1, D_FF, BF16, _epi_relu2, 1024, 1024, "mlp_up", w_layer=l)
    return _mm_residual([hid], w2, l, x2d, g2, 512, 512, "mlp_down", w_single_buffer=True)


def kernel(x, c, ctx, c_ctx, ada_w, ada_b, norm_g, lb_logits, ab_w_in, ab_conv_w, ab_gnorm_g, ab_w_out,
           hy_in_w, hy_short_w, hy_out_w, hy_fw1, hy_fb1, hy_fw2, hy_fb2, hy_fw3, hy_fb3, hy_fw4, hy_freq,
           hy_skip, mlp_w1, mlp_w2, final_g):
    d = D_MODEL
    cond = jnp.concatenate([c, c_ctx[None, :], jnp.zeros((3, d), F32)], axis=0)
    mod = _ada(cond, ada_w, ada_b)
    x2d = x.reshape(TOKENS, d)

    sh1, sc1, g1 = (mod[0, :BATCH, k * d:(k + 1) * d] for k in (0, 1, 2))
    h = _prep(x, norm_g[0, 0], sc1, sh1, BF16, 1024).reshape(TOKENS, d)
    csh = jnp.broadcast_to(mod[0, BATCH:BATCH + 1, 0:d], (BATCH, d))
    csc = jnp.broadcast_to(mod[0, BATCH:BATCH + 1, d:2 * d], (BATCH, d))
    hc = _prep(ctx, norm_g[0, 0], csc, csh, BF16, CTX_LEN).reshape(BATCH * CTX_LEN, d)
    p = _mm_plain(h, ab_w_in, AB_IN_WIDTH, F32, _epi_store, 1024, 1024, "ab_in_proj", w_layer=0)
    cp = _mm_plain(hc, ab_w_in, 3 * A_WIDTH, F32, _epi_store, 1024, 1024, "ab_ctx_proj", w_layer=0)
    mix_a = _hgrn_scan(p, cp, lb_logits, ab_gnorm_g[0], 0)
    mix_b = _gconv(p, ab_conv_w[0])
    x2d = _mm_residual([mix_a, mix_b], ab_w_out, 0, x2d, g1, 1024, 1024, "ab_out_proj")
    x2d = _mlp_block(x2d, mod, 0, norm_g, mlp_w1, mlp_w2)

    sh1, sc1, g1 = (mod[1, :BATCH, k * d:(k + 1) * d] for k in (0, 1, 2))
    h = _prep(x2d.reshape(BATCH, SEQ, d), norm_g[1, 0], sc1, sh1, BF16, 1024).reshape(TOKENS, d)
    pc = _mm([h], hy_in_w, w_layer=0, tm=1024, tn=1024, n_cols=3 * d, epi=_epi_conv3_grid,
             extras=(hy_short_w[0],), extra_specs=(pl.BlockSpec((3, 1024), lambda j, i: (0, j)),),
             out_shapes=jax.ShapeDtypeStruct((TOKENS, 3 * d), BF16),
             out_specs=pl.BlockSpec((1024, 1024), lambda j, i: (i, j)), name="hy_in_proj")
    taps_sum, taps_diff = _hyena_filter_taps(hy_fw1[0], hy_fb1[0], hy_fw2[0], hy_fb2[0], hy_fw3[0],
                                             hy_fb3[0], hy_fw4[0], hy_freq[0])
    analysis, synthesis, filter_tables = _dft_quarter_tables()
    spectra = _filter_dft2(filter_tables, taps_sum, taps_diff)
    coef = _dft4_fwd(analysis, pc, 2 * d, spectra, 0)
    z = _dft4_inv(synthesis, coef, pc, 2 * d, pc, 0, hy_skip[0, 0])
    coef = _dft4_fwd(analysis, z, 0, spectra, 1)
    z = _dft4_inv(synthesis, coef, z, 0, pc, d, hy_skip[0, 1])
    x2d = _mm_residual([z], hy_out_w, 0, x2d, g1, 1024, 1024, "hy_out_proj")
    x2d = _mlp_block(x2d, mod, 1, norm_g, mlp_w1, mlp_w2)

    zeros = jnp.zeros((BATCH, d), F32)
    return _prep(x2d.reshape(BATCH, SEQ, d), final_g, zeros, zeros, F32, 1024)
```

```python
import functools
import math

import numpy as np
import jax
import jax.numpy as jnp
from jax import lax
from jax.experimental import pallas as pl
from jax.experimental.pallas import tpu as pltpu

F32 = jnp.float32
BF16 = jnp.bfloat16

D_MODEL = 2048
BATCH = 4
SEQ = 2048
CTX_LEN = 256
GRID_W = 64
EPS = 1e-6
A_HEAD_DIM = 128
A_WIDTH = D_MODEL // 2
A_HEADS = A_WIDTH // A_HEAD_DIM
B_WIDTH = D_MODEL - A_WIDTH
AB_IN_WIDTH = 5 * A_WIDTH + 3 * B_WIDTH
HY_EMB = 33
HY_BANDS = (HY_EMB - 1) // 2
HY_DECAY_TARGET = 1e-2
HY_FAST_PCT = 0.3
HY_SLOW_PCT = 1.5
D_FF = 4 * D_MODEL
TOKENS = BATCH * SEQ
HALF = SEQ // 2
QUARTER = SEQ // 4

SCAN_CHUNK = 64
SCAN_LEVELS = (1, 2, 4, 8, 16, 32)

SUBLANES = 8
LANES = 128

VMEM_LIMIT_BYTES = 56 * 1024 * 1024


def _params(*sem):
    return pltpu.CompilerParams(dimension_semantics=sem, vmem_limit_bytes=VMEM_LIMIT_BYTES)


def _dot(a, b):
    return jnp.dot(a, b, preferred_element_type=F32)


def _dot_nt(a, b):
    return lax.dot_general(a, b, (((1,), (1,)), ((), ())), preferred_element_type=F32)


def _dot_tn(a, b):
    return lax.dot_general(a, b, (((0,), (0,)), ((), ())), preferred_element_type=F32)


def _dot_f32(a, b):
    return jnp.dot(a, b, precision=lax.Precision.HIGHEST, preferred_element_type=F32)


def _conv3_rows(t, w, group):
    rows = lax.broadcasted_iota(jnp.int32, t.shape, 0) % group
    prev = jnp.where(rows == 0, 0.0, pltpu.roll(t, 1, 0))
    nxt = jnp.where(rows == group - 1, 0.0, pltpu.roll(t, t.shape[0] - 1, 0))
    return w[0:1] * prev + w[1:2] * t + w[2:3] * nxt


def _ada_kernel(c_ref, w_ref, b_ref, o_ref):
    c = c_ref[...]
    a = (c * jax.nn.sigmoid(c)).astype(BF16)
    o_ref[...] = _dot(a, w_ref[...].astype(BF16)) + b_ref[...]


def _ada(cond, ada_w, ada_b):
    depth, d, n = ada_w.shape
    tn = 1024
    return pl.pallas_call(
        _ada_kernel,
        grid=(depth, n // tn),
        in_specs=[
            pl.BlockSpec((8, d), lambda l, j: (0, 0)),
            pl.BlockSpec((None, d, tn), lambda l, j: (l, 0, j)),
            pl.BlockSpec((None, 1, tn), lambda l, j: (l, 0, j)),
        ],
        out_specs=pl.BlockSpec((None, 8, tn), lambda l, j: (l, 0, j)),
        out_shape=jax.ShapeDtypeStruct((depth, 8, n), F32),
        compiler_params=_params("arbitrary", "arbitrary"),
        name="ada_mod",
    )(cond, ada_w, ada_b.reshape(depth, 1, n))


def _prep_kernel(x_ref, g_ref, sc_ref, sh_ref, o_ref):
    x = x_ref[...]
    ms = jnp.mean(x * x, axis=-1, keepdims=True)
    xn = x * lax.rsqrt(ms + EPS) * g_ref[...]
    o_ref[...] = (xn * (1.0 + sc_ref[...]) + sh_ref[...]).astype(o_ref.dtype)


def _prep(x, g, sc, sh, out_dtype, ts):
    b, l, d = x.shape
    return pl.pallas_call(
        _prep_kernel,
        grid=(b, l // ts),
        in_specs=[
            pl.BlockSpec((None, ts, d), lambda i, j: (i, j, 0)),
            pl.BlockSpec((1, d), lambda i, j: (0, 0)),
            pl.BlockSpec((None, 1, d), lambda i, j: (i, 0, 0)),
            pl.BlockSpec((None, 1, d), lambda i, j: (i, 0, 0)),
        ],
        out_specs=pl.BlockSpec((None, ts, d), lambda i, j: (i, j, 0)),
        out_shape=jax.ShapeDtypeStruct((b, l, d), out_dtype),
        compiler_params=_params("arbitrary", "arbitrary"),
        name="norm_mod",
    )(x, g.reshape(1, d), sc.reshape(b, 1, d), sh.reshape(b, 1, d))


def _mm_kernel(*refs, n_a, k_sizes, epi, cast_w):
    a_refs = refs[:n_a]
    w_ref = refs[n_a]
    rest = refs[n_a + 1:]
    if cast_w:
        w_ref, rest = rest[-1], rest[:-1]

        @pl.when(pl.program_id(1) == 0)
        def _():
            w_ref[...] = refs[n_a][...].astype(BF16)

    acc = None
    off = 0
    for a_ref, k in zip(a_refs, k_sizes):
        part = _dot(a_ref[...], w_ref[off:off + k, :])
        acc = part if acc is None else acc + part
        off += k
    epi(acc, *rest)


def _mm(a_list, w, *, tm, tn, n_cols, epi, extras, extra_specs, out_shapes, out_specs, name,
        w_layer=None, w_single_buffer=False):
    m = a_list[0].shape[0]
    k_sizes = tuple(a.shape[1] for a in a_list)
    k_total = sum(k_sizes)
    cast_w = w_layer is not None
    assert w.shape[-2] == k_total and m % tm == 0 and n_cols % tn == 0
    in_specs = [pl.BlockSpec((tm, k), lambda j, i: (i, 0)) for k in k_sizes]
    w_mode = dict(pipeline_mode=pl.Buffered(1)) if w_single_buffer else {}
    if cast_w:
        in_specs.append(pl.BlockSpec((None, k_total, tn), lambda j, i: (w_layer, 0, j), **w_mode))
    else:
        in_specs.append(pl.BlockSpec((k_total, tn), lambda j, i: (0, j), **w_mode))
    in_specs.extend(extra_specs)
    return pl.pallas_call(
        functools.partial(_mm_kernel, n_a=len(a_list), k_sizes=k_sizes, epi=epi, cast_w=cast_w),
        grid=(n_cols // tn, m // tm),
        in_specs=in_specs,
        out_specs=out_specs,
        out_shape=out_shapes,
        scratch_shapes=[pltpu.VMEM((k_total, tn), BF16)] if cast_w else [],
        compiler_params=_params("arbitrary", "arbitrary"),
        name=name,
    )(*a_list, w, *extras)


def _epi_store(acc, o_ref):
    o_ref[...] = acc.astype(o_ref.dtype)


def _epi_relu2(acc, o_ref):
    r = jnp.maximum(acc, 0.0)
    o_ref[...] = (r * r).astype(o_ref.dtype)


def _epi_residual(acc, x_ref, gate_ref, o_ref):
    o_ref[...] = x_ref[...] + gate_ref[...] * acc


def _epi_conv3_grid(acc, w_ref, o_ref):
    o_ref[...] = _conv3_rows(acc, w_ref[...], GRID_W).astype(o_ref.dtype)


def _mm_plain(a, w, n_cols, out_dtype, epi, tm, tn, name, w_layer=None):
    m = a.shape[0]
    return _mm([a], w, tm=tm, tn=tn, n_cols=n_cols, epi=epi, extras=(), extra_specs=(),
               out_shapes=jax.ShapeDtypeStruct((m, n_cols), out_dtype),
               out_specs=pl.BlockSpec((tm, tn), lambda j, i: (i, j)), name=name, w_layer=w_layer)


def _mm_residual(a_list, w, w_layer, x, gate, tm, tn, name, w_single_buffer=False):
    m, n = x.shape
    tiles_per_batch = SEQ // tm
    return _mm(a_list, w, tm=tm, tn=tn, n_cols=n, epi=_epi_residual,
               extras=(x, gate.reshape(BATCH, 1, n)),
               extra_specs=(pl.BlockSpec((tm, tn), lambda j, i: (i, j)),
                            pl.BlockSpec((None, 1, tn), lambda j, i: (i // tiles_per_batch, 0, j))),
               out_shapes=jax.ShapeDtypeStruct((m, n), F32),
               out_specs=pl.BlockSpec((tm, tn), lambda j, i: (i, j)), name=name,
               w_layer=w_layer, w_single_buffer=w_single_buffer)


def _row_groups(x):
    return [x[k:k + SUBLANES] for k in range(0, x.shape[0], SUBLANES)]


def _cumsum_groups(groups, sub):
    out, carry = [], None
    for g in groups:
        for s in (1, 2, 4):
            g = g + jnp.where(sub >= s, pltpu.roll(g, s, 0), 0.0)
        if carry is not None:
            g = g + carry
        out.append(g)
        carry = g[SUBLANES - 1:SUBLANES]
    return out


def _hold_boundary(cum, w, k, sub):
    shape = cum[k].shape
    if w >= SUBLANES // 2:
        r = (k * SUBLANES // (2 * w)) * 2 * w + w - 1
        return jnp.broadcast_to(cum[r // SUBLANES][r % SUBLANES:r % SUBLANES + 1], shape)
    assert w == 2
    return jnp.where(sub < 4, jnp.broadcast_to(cum[k][1:2], shape), jnp.broadcast_to(cum[k][5:6], shape))


def _scan_kernel(zf_ref, zb_ref, v_ref, q_ref, g_ref, u_ref, gb_ref, gc_ref, czf_ref, czb_ref, cv_ref,
                 lbl_ref, gn_ref, cw_ref, o_ref, oc_ref,
                 of_ref, ob_ref, st_ref, qs_ref, inc_ref, dec_ref, kl_ref, lv_ref, att_ref, *, layer):
    c = SCAN_CHUNK
    n_chunks = SEQ // c
    n_ctx = CTX_LEN // c
    z_refs = (zf_ref, zb_ref)
    cz_refs = (czf_ref, czb_ref)
    out_refs = (of_ref, ob_ref)

    logit_rows = [lbl_ref[:, k, :] for k in range(lbl_ref.shape[1])]
    top = functools.reduce(jnp.maximum, logit_rows)
    exps = [jnp.exp(r - top) for r in logit_rows]
    lbs = sum(exps[:layer + 1]) / sum(exps)

    sub = lax.broadcasted_iota(jnp.int32, (SUBLANES, A_HEAD_DIM), 0)
    n_groups = c // SUBLANES

    def rows(groups):
        return jnp.concatenate(groups, axis=0)

    def chunk(d, z, q, v, want_out):
        lb = lbs[d:d + 1]
        f = lb + (1.0 - lb) * jax.nn.sigmoid(z)
        kk = 1.0 - f
        fg, kg = _row_groups(f), _row_groups(kk)
        lg = [jnp.log2(g) for g in fg]
        cum = _cumsum_groups(lg, sub)
        total = cum[-1][SUBLANES - 1:SUBLANES]
        if d == 0:
            pos = cum
            q_in = [jnp.exp2(g) for g in cum]
            k_out = [jnp.exp2(total - g) for g in cum]
        else:
            pos = [a - b for a, b in zip(cum, lg)]
            q_in = [jnp.exp2(total - g) for g in pos]
            k_out = [jnp.exp2(g) for g in pos]
        k_leave = rows([a * b for a, b in zip(kg, k_out)]).astype(BF16)
        decay = jnp.exp2(total)
        if not want_out:
            return k_leave, decay, None, None, None
        qg = _row_groups(q)
        o_diag = jnp.sum(q * kk, axis=1, keepdims=True) * v
        upper, lower = (qg, kg) if d == 0 else (kg, qg)
        operands = []
        for w in SCAN_LEVELS:
            y = []
            for k in range(n_groups):
                if w == 1:
                    odd = (sub & 1) != 0
                    y.append(jnp.where(odd, qg[k] * fg[k], kg[k]) if d == 0
                             else jnp.where(odd, kg[k], qg[k] * fg[k]))
                    continue
                gap = pos[k] - _hold_boundary(cum, w, k, sub)
                if w >= SUBLANES:
                    bit_set = (k * SUBLANES) & w != 0
                    y.append(upper[k] * jnp.exp2(gap) if bit_set else lower[k] * jnp.exp2(-gap))
                else:
                    bit = (sub & w) != 0
                    y.append(jnp.where(bit, upper[k], lower[k]) * jnp.exp2(jnp.where(bit, gap, -gap)))
            operands.append(rows(y).astype(BF16))
        qs = rows([a * b for a, b in zip(qg, q_in)]).astype(BF16)
        return k_leave, decay, qs, o_diag, operands

    def within_chunk(d, operands):
        t_idx = lax.broadcasted_iota(jnp.int32, (c, c), 0)
        s_idx = lax.broadcasted_iota(jnp.int32, (c, c), 1)
        split = t_idx ^ s_idx
        att = None
        for w, yb in reversed(list(zip(SCAN_LEVELS, operands))):
            scores = _dot_nt(yb, yb)
            att = scores if att is None else jnp.where(split < 2 * w, scores, att)
        return jnp.where(t_idx > s_idx if d == 0 else t_idx < s_idx, att, 0.0)

    st_ref[...] = jnp.zeros_like(st_ref)

    def ctx_body(i, carry):
        for d in range(2):
            ci = i if d == 0 else n_ctx - 1 - i
            rows_i = pl.ds(pl.multiple_of(ci * c, c), c)
            k_leave, decay, _, _, _ = chunk(d, cz_refs[d][rows_i, :], None, None, False)
            st_ref[d] = st_ref[d] * decay + _dot_tn(cv_ref[rows_i, :].astype(BF16), k_leave)
        return carry

    lax.fori_loop(0, n_ctx, ctx_body, 0, unroll=2)

    def chunk_rows(ci):
        start = ci * c
        return pl.ds(start if isinstance(ci, int) else pl.multiple_of(start, c), c)

    def prepare(ci):
        rows_i = chunk_rows(ci)
        for d in range(2):
            k_leave, decay, qs, o_diag, operands = chunk(d, z_refs[d][rows_i, :], q_ref[rows_i, :],
                                                         v_ref[rows_i, :], True)
            kl_ref[d, rows_i, :] = k_leave
            dec_ref[d, ci] = decay
            qs_ref[d, rows_i, :] = qs
            out_refs[d][rows_i, :] = o_diag
            for lvl, yb in enumerate(operands):
                lv_ref[d, lvl, rows_i, :] = yb

    def scores(ci):
        rows_i = chunk_rows(ci)
        vb = v_ref[rows_i, :].astype(BF16)
        for d in range(2):
            inc_ref[d, ci] = _dot_tn(vb, kl_ref[d, rows_i, :])
            operands = [lv_ref[d, lvl, rows_i, :] for lvl in range(len(SCAN_LEVELS))]
            att_ref[d, rows_i, :] = within_chunk(d, operands).astype(BF16)

    pair = 2
    for j in range(pair):
        prepare(j)

    def skewed_body(i, carry):
        for j in range(pair):
            scores((i - 1) * pair + j)
        for j in range(pair):
            prepare(i * pair + j)
        return carry

    lax.fori_loop(1, n_chunks // pair, skewed_body, 0)
    for j in range(pair):
        scores(n_chunks - pair + j)

    def carry_body(i, carry):
        for d in range(2):
            ci = i if d == 0 else n_chunks - 1 - i
            rows_i = pl.ds(pl.multiple_of(ci * c, c), c)
            state = st_ref[d]
            out_refs[d][rows_i, :] += (_dot(att_ref[d, rows_i, :], v_ref[rows_i, :].astype(BF16))
                                       + _dot_nt(qs_ref[d, rows_i, :], state.astype(BF16)))
            st_ref[d] = state * dec_ref[d, ci] + inc_ref[d, ci]
        return carry

    lax.fori_loop(0, n_chunks, carry_body, 0, unroll=8)

    rb = 256
    gn = gn_ref[...]
    cw = cw_ref[...]

    def read_body(i, carry):
        rows_i = pl.ds(pl.multiple_of(i * rb, rb), rb)
        o = of_ref[rows_i, :] + ob_ref[rows_i, :]
        ms = jnp.mean(o * o, axis=-1, keepdims=True)
        on = o * lax.rsqrt(ms + EPS) * gn
        g = g_ref[rows_i, :]
        o_ref[rows_i, :] = (on * (g * jax.nn.sigmoid(g))).astype(o_ref.dtype)
        conv = _conv3_rows(gc_ref[rows_i, :] * u_ref[rows_i, :], cw, GRID_W)
        oc_ref[rows_i, :] = (gb_ref[rows_i, :] * conv).astype(oc_ref.dtype)
        return carry

    lax.fori_loop(0, SEQ // rb, read_body, 0)


def _even_mixer(p, cp, lb_logits, gnorm_g, conv_w, layer):
    h = A_HEADS
    hd = A_HEAD_DIM
    n_lb = lb_logits.shape[1]

    def col(k):
        return pl.BlockSpec((SEQ, hd), lambda b, j, k=k: (b, k * h + j))

    def ccol(k):
        return pl.BlockSpec((CTX_LEN, hd), lambda b, j, k=k: (b, k * h + j))

    out_spec = pl.BlockSpec((SEQ, hd), lambda b, j: (b, j))
    return pl.pallas_call(
        functools.partial(_scan_kernel, layer=layer),
        grid=(BATCH, h),
        in_specs=[col(0), col(1), col(2), col(3), col(4), col(5), col(6), col(7),
                  ccol(0), ccol(1), ccol(2),
                  pl.BlockSpec((2, n_lb, hd), lambda b, j: (0, 0, j)),
                  pl.BlockSpec((1, hd), lambda b, j: (0, j)),
                  pl.BlockSpec((3, hd), lambda b, j: (0, j))],
        out_specs=(out_spec, out_spec),
        out_shape=(jax.ShapeDtypeStruct((TOKENS, A_WIDTH), BF16),
                   jax.ShapeDtypeStruct((TOKENS, B_WIDTH), BF16)),
        scratch_shapes=[pltpu.VMEM((SEQ, hd), F32), pltpu.VMEM((SEQ, hd), F32),
                        pltpu.VMEM((2, hd, hd), F32), pltpu.VMEM((2, SEQ, hd), BF16),
                        pltpu.VMEM((2, SEQ // SCAN_CHUNK, hd, hd), F32),
                        pltpu.VMEM((2, SEQ // SCAN_CHUNK, 1, hd), F32),
                        pltpu.VMEM((2, SEQ, hd), BF16),
                        pltpu.VMEM((2, len(SCAN_LEVELS), SEQ, hd), BF16),
                        pltpu.VMEM((2, SEQ, SCAN_CHUNK), BF16)],
        compiler_params=_params("arbitrary", "arbitrary"),
        name="hgrn_scan",
    )(p, p, p, p, p, p, p, p, cp, cp, cp, lb_logits, gnorm_g.reshape(1, A_WIDTH), conv_w)


def _split_bf16(x):
    hi = x.astype(BF16)
    return hi, (x - hi.astype(F32)).astype(BF16)


def _filter_kernel(z_ref, w1_ref, b1_ref, w2_ref, b2_ref, w3_ref, b3_ref, fr_ref,
                   w4f_top_ref, w4f_bot_ref, w4b_top_ref, w4b_bot_ref, sum_ref, diff_ref, hid_ref, *, tc):
    o = pl.program_id(0)
    j = pl.program_id(1)

    @pl.when((o == 0) & (j == 0))
    def _():
        fr = fr_ref[...]
        h = jnp.sin(fr * (_dot_f32(z_ref[...], w1_ref[...]) + b1_ref[...]))
        h = jnp.sin(fr * (_dot_f32(h, w2_ref[...]) + b2_ref[...]))
        h = jnp.sin(fr * (_dot_f32(h, w3_ref[...]) + b3_ref[...]))
        hi, lo = _split_bf16(h)
        hid_ref[...] = jnp.concatenate([hi, lo, hi], axis=1)

    def project(top_ref, bot_ref):
        halves = []
        for w_ref in (top_ref, bot_ref):
            w_hi, w_lo = _split_bf16(w_ref[...])
            halves.append(_dot(hid_ref[...], jnp.concatenate([w_hi, w_hi, w_lo], axis=0)))
        return jnp.concatenate(halves, axis=0)

    rows = lax.broadcasted_iota(jnp.int32, (SEQ, tc), 0)
    chan = lax.broadcasted_iota(jnp.int32, (SEQ, tc), 1) + j * tc
    lag = jnp.where(rows < HALF, 2 * rows, 2 * rows - (SEQ - 1))
    t = lag.astype(F32) * (1.0 / (SEQ - 1))
    max_decay = math.log(HY_DECAY_TARGET) / HY_FAST_PCT
    min_decay = math.log(HY_DECAY_TARGET) / HY_SLOW_PCT
    deltas = jnp.abs(min_decay + chan.astype(F32) * ((max_decay - min_decay) / (D_MODEL - 1)))
    window = jnp.exp(-t * deltas)
    fw = project(w4f_top_ref, w4f_bot_ref) * window
    bw = project(w4b_top_ref, w4b_bot_ref) * window
    first = rows == 0
    a = fw + jnp.where(first, bw, 0.0)
    bb = jnp.where(first, 0.0, bw)
    inv = 1.0 / jnp.sum(jnp.abs(a) + jnp.abs(bb), axis=0, keepdims=True)
    sum_ref[...] = ((a + bb) * inv).astype(sum_ref.dtype)
    diff_ref[...] = ((a - bb) * inv).astype(diff_ref.dtype)


@functools.lru_cache(maxsize=None)
def _filter_positions():
    l = SEQ
    t = np.linspace(0.0, 1.0, l, dtype=np.float32)
    w = (2.0 * math.pi * np.arange(l, dtype=np.float32) / l).astype(np.float32)
    bands = np.linspace(1e-4, HY_BANDS - 1, HY_BANDS, dtype=np.float32)
    ang = w[:, None] * bands[None, :]
    z = np.concatenate([t[:, None], np.cos(ang), -np.sin(ang)], axis=-1).astype(np.float32)
    z = np.concatenate([z[0::2], z[1::2]], axis=0)
    z = np.pad(z, ((0, 0), (0, LANES // 2 - HY_EMB)))
    return np.concatenate([z[:l // 2], z[l // 2:]], axis=1)


def _hyena_filter_taps(fw1, fb1, fw2, fb2, fw3, fb3, fw4, freq):
    l = SEQ
    tc = 256
    nj = D_MODEL // tc
    hw = LANES // 2

    def pad(a, rows, cols):
        return jnp.pad(a, ((0, rows - a.shape[0]), (0, cols - a.shape[1])))

    def twice(a):
        a = pad(a, hw, hw)
        zero = jnp.zeros_like(a)
        return jnp.concatenate([jnp.concatenate([a, zero], axis=1), jnp.concatenate([zero, a], axis=1)], axis=0)

    def row2(v):
        v = pad(v[None, :], 1, hw)
        return jnp.concatenate([v, v], axis=1)

    def small(shape):
        return pl.BlockSpec(shape, lambda o, j: (0, 0))

    w4_top = pad(fw4, LANES, fw4.shape[1])
    w4_bot = jnp.concatenate([jnp.zeros((hw, fw4.shape[1]), F32), pad(fw4, hw, fw4.shape[1])], axis=0)
    out_sds = jax.ShapeDtypeStruct((l, 2 * D_MODEL), BF16)
    out_spec = pl.BlockSpec((l, tc), lambda o, j: (0, o * nj + j))
    side_f = pl.BlockSpec((LANES, tc), lambda o, j: (0, 2 * o * nj + j))
    side_b = pl.BlockSpec((LANES, tc), lambda o, j: (0, (2 * o + 1) * nj + j))
    sq, vec = small((LANES, LANES)), small((1, LANES))
    return pl.pallas_call(
        functools.partial(_filter_kernel, tc=tc),
        grid=(2, nj),
        in_specs=[small((l // 2, LANES)), sq, vec, sq, vec, sq, vec, vec, side_f, side_f, side_b, side_b],
        out_specs=(out_spec, out_spec),
        out_shape=(out_sds, out_sds),
        scratch_shapes=[pltpu.VMEM((l // 2, 3 * LANES), BF16)],
        compiler_params=_params("arbitrary", "arbitrary"),
        name="hyena_filter",
    )(jnp.asarray(_filter_positions()), twice(fw1), row2(fb1), twice(fw2), row2(fb2), twice(fw3), row2(fb3),
      row2(freq), w4_top, w4_bot, w4_top, w4_bot)


def _trig_table(a_rows, b_cols, scale):
    n = (a_rows[:, None].astype(np.int64) * b_cols[None, :].astype(np.int64)) % (4 * SEQ)
    ang = n.astype(np.float64) * (math.pi / (2 * SEQ))
    return (np.cos(ang) * scale).astype(np.float32), (np.sin(ang) * scale).astype(np.float32)


@functools.lru_cache(maxsize=None)
def _dft_tables():
    idx = np.arange(QUARTER)
    ana_c, ana_s, syn_c, syn_s = [], [], [], []
    for r in range(4):
        c, s = _trig_table(2 * idx + 1, 4 * idx + r, 1.0)
        ana_c.append(c)
        ana_s.append(s)
        c, s = _trig_table(4 * idx + r, 2 * idx + 1, 1.0 / SEQ)
        syn_c.append(c)
        syn_s.append(s)
    fmap = np.concatenate([idx, HALF + idx])
    lag = np.arange(HALF)
    cee, see = _trig_table(2 * fmap + 1, 2 * lag, 1.0)
    ceo, seo = _trig_table(2 * fmap + 1, 2 * lag + 1, 1.0)
    return np.stack(ana_c + ana_s), np.stack(syn_c + syn_s), (cee, ceo, see, seo)


def _filter_dft_kernel(cee_ref, ceo_ref, see_ref, seo_ref, fs_ref, fd_ref, p_ref, pm_ref, q_ref, qm_ref):
    pe = _dot(cee_ref[...], fs_ref[0:HALF, :])
    po = _dot(ceo_ref[...], fs_ref[HALF:SEQ, :])
    qe = _dot(see_ref[...], fd_ref[0:HALF, :])
    qo = _dot(seo_ref[...], fd_ref[HALF:SEQ, :])
    p_ref[...] = pe + po
    pm_ref[...] = pe - po
    q_ref[...] = qe + qo
    qm_ref[...] = qo - qe


def _filter_dft(tables, taps_sum, taps_diff):
    tm, tn = HALF, 256
    n_cols = taps_sum.shape[1]
    mat = pl.BlockSpec((tm, HALF), lambda j, i: (i, 0), pipeline_mode=pl.Buffered(1))
    tap = pl.BlockSpec((SEQ, tn), lambda j, i: (0, j))
    out_spec = pl.BlockSpec((tm, tn), lambda j, i: (i, j))
    out_sds = jax.ShapeDtypeStruct((HALF, n_cols), F32)
    return pl.pallas_call(
        _filter_dft_kernel,
        grid=(n_cols // tn, HALF // tm),
        in_specs=[mat, mat, mat, mat, tap, tap],
        out_specs=(out_spec,) * 4,
        out_shape=(out_sds,) * 4,
        compiler_params=_params("arbitrary", "arbitrary"),
        name="filter_dft",
    )(*tables, taps_sum, taps_diff)


def _dft4_fwd_kernel(tab_ref, z_ref, p_ref, pm_ref, q_ref, qm_ref, x_ref, zs_ref, *, slab):
    panels = zs_ref.shape[0]
    for k in range(panels):
        zs_ref[k] = z_ref[:, k * LANES:(k + 1) * LANES].astype(F32)
    zr = [jnp.concatenate([zs_ref[k, pl.ds(r, QUARTER, stride=4), :] for k in range(panels)],
                          axis=1).astype(BF16) for r in range(4)]
    for f0 in range(0, QUARTER, slab):
        rows = slice(f0, f0 + slab)
        mirror = slice(QUARTER + f0, QUARTER + f0 + slab)
        c = [_dot(tab_ref[r, rows, :], zr[r]) for r in range(4)]
        s = [_dot(tab_ref[4 + r, rows, :], zr[r]) for r in range(4)]
        e, o, em, om = c[0] + c[2], c[1] + c[3], c[0] - c[2], c[1] - c[3]
        se, so, sem, som = s[0] + s[2], s[1] + s[3], s[0] - s[2], s[1] - s[3]
        a = (e + o, em + som, em - som, e - o)
        b = (se + so, om - sem, sem + om, so - se)
        pk = (p_ref[rows, :], pm_ref[mirror, :], p_ref[mirror, :], pm_ref[rows, :])
        qk = (q_ref[rows, :], qm_ref[mirror, :], q_ref[mirror, :], qm_ref[rows, :])
        u = [a[k] * pk[k] - b[k] * qk[k] for k in range(4)]
        v = [a[k] * qk[k] + b[k] * pk[k] for k in range(4)]
        up, um, wp, wm = u[0] + u[3], u[0] - u[3], u[1] + u[2], u[1] - u[2]
        vp, vm, yp, ym = v[0] + v[3], v[0] - v[3], v[1] + v[2], v[1] - v[2]
        outs = (up + wp, um + yp, up - wp, um - yp,
                vm - ym, vp + wm, vm + ym, vp - wm)
        for n, val in enumerate(outs):
            x_ref[n, rows, :] = val.astype(x_ref.dtype)


def _dft4_fwd(tab, z_arr, z_col0, spectra, order):
    tn = 512
    nj = D_MODEL // tn
    zc = z_col0 // tn
    spec = pl.BlockSpec((HALF, tn), lambda j, b: (0, order * nj + j))
    return pl.pallas_call(
        functools.partial(_dft4_fwd_kernel, slab=256),
        grid=(nj, BATCH),
        in_specs=[pl.BlockSpec(tab.shape, lambda j, b: (0, 0, 0), pipeline_mode=pl.Buffered(1)),
                  pl.BlockSpec((SEQ, tn), lambda j, b: (b, zc + j)), spec, spec, spec, spec],
        out_specs=pl.BlockSpec((8, QUARTER, tn), lambda j, b: (0, b, j)),
        out_shape=jax.ShapeDtypeStruct((8, BATCH * QUARTER, D_MODEL), BF16),
        scratch_shapes=[pltpu.VMEM((tn // LANES, SEQ, LANES), F32)],
        compiler_params=_params("arbitrary", "arbitrary"),
        name="dft_analysis",
    )(tab, z_arr, *spectra)


def _dft4_inv_kernel(tab_ref, x_ref, z_ref, gate_ref, skip_ref, o_ref, ys_ref):
    panels = ys_ref.shape[0]
    for r in range(4):
        y_r = _dot(tab_ref[r], x_ref[r]) + _dot(tab_ref[4 + r], x_ref[4 + r])
        for k in range(panels):
            ys_ref[k, pl.ds(r, QUARTER, stride=4), :] = y_r[:, k * LANES:(k + 1) * LANES]
    y = jnp.concatenate([ys_ref[k] for k in range(panels)], axis=1)
    y = y + z_ref[...].astype(F32) * skip_ref[...]
    o_ref[...] = (gate_ref[...].astype(F32) * y).astype(o_ref.dtype)


def _dft4_inv(tab, x, z_arr, z_col0, gate_arr, gate_col0, skip):
    tn = 512
    nj = D_MODEL // tn
    zc, gc = z_col0 // tn, gate_col0 // tn
    return pl.pallas_call(
        _dft4_inv_kernel,
        grid=(nj, BATCH),
        in_specs=[pl.BlockSpec(tab.shape, lambda j, b: (0, 0, 0), pipeline_mode=pl.Buffered(1)),
                  pl.BlockSpec((8, QUARTER, tn), lambda j, b: (0, b, j)),
                  pl.BlockSpec((SEQ, tn), lambda j, b: (b, zc + j)),
                  pl.BlockSpec((SEQ, tn), lambda j, b: (b, gc + j)),
                  pl.BlockSpec((1, tn), lambda j, b: (0, j))],
        out_specs=pl.BlockSpec((SEQ, tn), lambda j, b: (b, j)),
        out_shape=jax.ShapeDtypeStruct((TOKENS, D_MODEL), BF16),
        scratch_shapes=[pltpu.VMEM((tn // LANES, SEQ, LANES), F32)],
        compiler_params=_params("arbitrary", "arbitrary"),
        name="dft_synthesis",
    )(tab, x, z_arr, gate_arr, skip.reshape(1, D_MODEL))


def _mlp_block(x2d, mod, l, norm_g, w1, w2):
    sh2, sc2, g2 = (mod[l, :BATCH, k * D_MODEL:(k + 1) * D_MODEL] for k in (3, 4, 5))
    h = _prep(x2d.reshape(BATCH, SEQ, D_MODEL), norm_g[l, 1], sc2, sh2, BF16, 1024).reshape(TOKENS, D_MODEL)
    hid = _mm_plain(h, w1, D_FF, BF16, _epi_relu2, 1024, 1024, "mlp_up", w_layer=l)
    return _mm_residual([hid], w2, l, x2d, g2, 512, 512, "mlp_down", w_single_buffer=True)


def kernel(x, c, ctx, c_ctx, ada_w, ada_b, norm_g, lb_logits, ab_w_in, ab_conv_w, ab_gnorm_g, ab_w_out,
           hy_in_w, hy_short_w, hy_out_w, hy_fw1, hy_fb1, hy_fw2, hy_fb2, hy_fw3, hy_fb3, hy_fw4, hy_freq,
           hy_skip, mlp_w1, mlp_w2, final_g):
    d = D_MODEL
    cond = jnp.concatenate([c, c_ctx[None, :], jnp.zeros((3, d), F32)], axis=0)
    mod = _ada(cond, ada_w, ada_b)
    x2d = x.reshape(TOKENS, d)

    sh1, sc1, g1 = (mod[0, :BATCH, k * d:(k + 1) * d] for k in (0, 1, 2))
    h = _prep(x, norm_g[0, 0], sc1, sh1, BF16, 1024).reshape(TOKENS, d)
    csh = jnp.broadcast_to(mod[0, BATCH:BATCH + 1, 0:d], (BATCH, d))
    csc = jnp.broadcast_to(mod[0, BATCH:BATCH + 1, d:2 * d], (BATCH, d))
    hc = _prep(ctx, norm_g[0, 0], csc, csh, BF16, CTX_LEN).reshape(BATCH * CTX_LEN, d)
    p = _mm_plain(h, ab_w_in, AB_IN_WIDTH, F32, _epi_store, 1024, 1024, "ab_in_proj", w_layer=0)
    cp = _mm_plain(hc, ab_w_in, 3 * A_WIDTH, F32, _epi_store, 1024, 1024, "ab_ctx_proj", w_layer=0)
    mix_a, mix_b = _even_mixer(p, cp, lb_logits, ab_gnorm_g[0], ab_conv_w[0], 0)
    x2d = _mm_residual([mix_a, mix_b], ab_w_out, 0, x2d, g1, 1024, 1024, "ab_out_proj")
    x2d = _mlp_block(x2d, mod, 0, norm_g, mlp_w1, mlp_w2)

    sh1, sc1, g1 = (mod[1, :BATCH, k * d:(k + 1) * d] for k in (0, 1, 2))
    h = _prep(x2d.reshape(BATCH, SEQ, d), norm_g[1, 0], sc1, sh1, BF16, 1024).reshape(TOKENS, d)
    pc = _mm([h], hy_in_w, w_layer=0, tm=1024, tn=1024, n_cols=3 * d, epi=_epi_conv3_grid,
             extras=(hy_short_w[0],), extra_specs=(pl.BlockSpec((3, 1024), lambda j, i: (0, j)),),
             out_shapes=jax.ShapeDtypeStruct((TOKENS, 3 * d), BF16),
             out_specs=pl.BlockSpec((1024, 1024), lambda j, i: (i, j)), name="hy_in_proj")
    taps_sum, taps_diff = _hyena_filter_taps(hy_fw1[0], hy_fb1[0], hy_fw2[0], hy_fb2[0], hy_fw3[0],
                                             hy_fb3[0], hy_fw4[0], hy_freq[0])
    analysis, synthesis, filter_tables = jax.tree.map(lambda t: jnp.asarray(t).astype(BF16), _dft_tables())
    spectra = _filter_dft(filter_tables, taps_sum, taps_diff)
    coef = _dft4_fwd(analysis, pc, 2 * d, spectra, 0)
    z = _dft4_inv(synthesis, coef, pc, 2 * d, pc, 0, hy_skip[0, 0])
    coef = _dft4_fwd(analysis, z, 0, spectra, 1)
    z = _dft4_inv(synthesis, coef, z, 0, pc, d, hy_skip[0, 1])
    x2d = _mm_residual([z], hy_out_w, 0, x2d, g1, 1024, 1024, "hy_out_proj")
    x2d = _mlp_block(x2d, mod, 1, norm_g, mlp_w1, mlp_w2)

    zeros = jnp.zeros((BATCH, d), F32)
    return _prep(x2d.reshape(BATCH, SEQ, d), final_g, zeros, zeros, F32, 1024)
```

```python
import functools
import math

import numpy as np
import jax
import jax.numpy as jnp
from jax import lax
from jax.experimental import pallas as pl
from jax.experimental.pallas import tpu as pltpu

F32 = jnp.float32
BF16 = jnp.bfloat16

D_MODEL = 2048
BATCH = 4
SEQ = 2048
CTX_LEN = 256
GRID_W = 64
EPS = 1e-6
A_HEAD_DIM = 128
A_WIDTH = D_MODEL // 2
A_HEADS = A_WIDTH // A_HEAD_DIM
B_WIDTH = D_MODEL - A_WIDTH
AB_IN_WIDTH = 5 * A_WIDTH + 3 * B_WIDTH
HY_EMB = 33
HY_BANDS = (HY_EMB - 1) // 2
HY_DECAY_TARGET = 1e-2
HY_FAST_PCT = 0.3
HY_SLOW_PCT = 1.5
D_FF = 4 * D_MODEL
TOKENS = BATCH * SEQ
HALF = SEQ // 2
QUARTER = SEQ // 4

SCAN_CHUNK = 64
SCAN_LEVELS = (1, 2, 4, 8, 16, 32)

SUBLANES = 8
LANES = 128

VMEM_LIMIT_BYTES = 56 * 1024 * 1024


def _params(*sem):
    return pltpu.CompilerParams(dimension_semantics=sem, vmem_limit_bytes=VMEM_LIMIT_BYTES)


def _dot(a, b):
    return jnp.dot(a, b, preferred_element_type=F32)


def _dot_nt(a, b):
    return lax.dot_general(a, b, (((1,), (1,)), ((), ())), preferred_element_type=F32)


def _dot_tn(a, b):
    return lax.dot_general(a, b, (((0,), (0,)), ((), ())), preferred_element_type=F32)


def _dot_f32(a, b):
    return jnp.dot(a, b, precision=lax.Precision.HIGHEST, preferred_element_type=F32)


def _conv3_rows(t, w, group):
    rows = lax.broadcasted_iota(jnp.int32, t.shape, 0) % group
    prev = jnp.where(rows == 0, 0.0, pltpu.roll(t, 1, 0))
    nxt = jnp.where(rows == group - 1, 0.0, pltpu.roll(t, t.shape[0] - 1, 0))
    return w[0:1] * prev + w[1:2] * t + w[2:3] * nxt


def _prep_kernel(x_ref, g_ref, sc_ref, sh_ref, o_ref):
    x = x_ref[...]
    ms = jnp.mean(x * x, axis=-1, keepdims=True)
    xn = x * lax.rsqrt(ms + EPS) * g_ref[...]
    o_ref[...] = (xn * (1.0 + sc_ref[...]) + sh_ref[...]).astype(o_ref.dtype)


def _prep(x, g, sc, sh, out_dtype, ts):
    b, l, d = x.shape
    return pl.pallas_call(
        _prep_kernel,
        grid=(b, l // ts),
        in_specs=[
            pl.BlockSpec((None, ts, d), lambda i, j: (i, j, 0)),
            pl.BlockSpec((1, d), lambda i, j: (0, 0)),
            pl.BlockSpec((None, 1, d), lambda i, j: (i, 0, 0)),
            pl.BlockSpec((None, 1, d), lambda i, j: (i, 0, 0)),
        ],
        out_specs=pl.BlockSpec((None, ts, d), lambda i, j: (i, j, 0)),
        out_shape=jax.ShapeDtypeStruct((b, l, d), out_dtype),
        compiler_params=_params("arbitrary", "arbitrary"),
        name="norm_mod",
    )(x, g.reshape(1, d), sc.reshape(b, 1, d), sh.reshape(b, 1, d))


def _mm_kernel(*refs, n_a, k_sizes, epi, cast_w):
    a_refs = refs[:n_a]
    w_ref = refs[n_a]
    rest = refs[n_a + 1:]
    if cast_w:
        w_ref, rest = rest[-1], rest[:-1]

        @pl.when(pl.program_id(1) == 0)
        def _():
            w_ref[...] = refs[n_a][...].astype(BF16)

    acc = None
    off = 0
    for a_ref, k in zip(a_refs, k_sizes):
        part = _dot(a_ref[...], w_ref[off:off + k, :])
        acc = part if acc is None else acc + part
        off += k
    epi(acc, *rest)


def _mm(a_list, w, *, tm, tn, n_cols, epi, extras, extra_specs, out_shapes, out_specs, name,
        w_layer=None, w_single_buffer=False):
    m = a_list[0].shape[0]
    k_sizes = tuple(a.shape[1] for a in a_list)
    k_total = sum(k_sizes)
    cast_w = w_layer is not None
    assert w.shape[-2] == k_total and m % tm == 0 and n_cols % tn == 0
    in_specs = [pl.BlockSpec((tm, k), lambda j, i: (i, 0)) for k in k_sizes]
    w_mode = dict(pipeline_mode=pl.Buffered(1)) if w_single_buffer else {}
    if cast_w:
        in_specs.append(pl.BlockSpec((None, k_total, tn), lambda j, i: (w_layer, 0, j), **w_mode))
    else:
        in_specs.append(pl.BlockSpec((k_total, tn), lambda j, i: (0, j), **w_mode))
    in_specs.extend(extra_specs)
    return pl.pallas_call(
        functools.partial(_mm_kernel, n_a=len(a_list), k_sizes=k_sizes, epi=epi, cast_w=cast_w),
        grid=(n_cols // tn, m // tm),
        in_specs=in_specs,
        out_specs=out_specs,
        out_shape=out_shapes,
        scratch_shapes=[pltpu.VMEM((k_total, tn), BF16)] if cast_w else [],
        compiler_params=_params("arbitrary", "arbitrary"),
        name=name,
    )(*a_list, w, *extras)


def _epi_store(acc, o_ref):
    o_ref[...] = acc.astype(o_ref.dtype)


def _epi_relu2(acc, o_ref):
    r = jnp.maximum(acc, 0.0)
    o_ref[...] = (r * r).astype(o_ref.dtype)


def _epi_residual(acc, x_ref, gate_ref, o_ref):
    o_ref[...] = x_ref[...] + gate_ref[...] * acc


def _epi_conv3_grid(acc, w_ref, o_ref):
    o_ref[...] = _conv3_rows(acc, w_ref[...], GRID_W).astype(o_ref.dtype)


def _mm_plain(a, w, n_cols, out_dtype, epi, tm, tn, name, w_layer=None):
    m = a.shape[0]
    return _mm([a], w, tm=tm, tn=tn, n_cols=n_cols, epi=epi, extras=(), extra_specs=(),
               out_shapes=jax.ShapeDtypeStruct((m, n_cols), out_dtype),
               out_specs=pl.BlockSpec((tm, tn), lambda j, i: (i, j)), name=name, w_layer=w_layer)


def _mm_residual(a_list, w, w_layer, x, gate, tm, tn, name, w_single_buffer=False):
    m, n = x.shape
    tiles_per_batch = SEQ // tm
    return _mm(a_list, w, tm=tm, tn=tn, n_cols=n, epi=_epi_residual,
               extras=(x, gate.reshape(BATCH, 1, n)),
               extra_specs=(pl.BlockSpec((tm, tn), lambda j, i: (i, j)),
                            pl.BlockSpec((None, 1, tn), lambda j, i: (i // tiles_per_batch, 0, j))),
               out_shapes=jax.ShapeDtypeStruct((m, n), F32),
               out_specs=pl.BlockSpec((tm, tn), lambda j, i: (i, j)), name=name,
               w_layer=w_layer, w_single_buffer=w_single_buffer)


def _row_groups(x):
    return [x[k:k + SUBLANES] for k in range(0, x.shape[0], SUBLANES)]


def _cumsum_groups(groups, sub):
    out, carry = [], None
    for g in groups:
        for s in (1, 2, 4):
            g = g + jnp.where(sub >= s, pltpu.roll(g, s, 0), 0.0)
        if carry is not None:
            g = g + carry
        out.append(g)
        carry = g[SUBLANES - 1:SUBLANES]
    return out


def _hold_boundary(cum, w, k, sub):
    shape = cum[k].shape
    if w >= SUBLANES // 2:
        r = (k * SUBLANES // (2 * w)) * 2 * w + w - 1
        return jnp.broadcast_to(cum[r // SUBLANES][r % SUBLANES:r % SUBLANES + 1], shape)
    assert w == 2
    return jnp.where(sub < 4, jnp.broadcast_to(cum[k][1:2], shape), jnp.broadcast_to(cum[k][5:6], shape))


def _scan_kernel(zf_ref, zb_ref, v_ref, q_ref, g_ref, u_ref, gb_ref, gc_ref, czf_ref, czb_ref, cv_ref,
                 lbl_ref, gn_ref, cw_ref, o_ref, oc_ref,
                 of_ref, ob_ref, st_ref, qs_ref, inc_ref, dec_ref, kl_ref, lv_ref, att_ref, *, layer):
    c = SCAN_CHUNK
    n_chunks = SEQ // c
    n_ctx = CTX_LEN // c
    z_refs = (zf_ref, zb_ref)
    cz_refs = (czf_ref, czb_ref)
    out_refs = (of_ref, ob_ref)

    logit_rows = [lbl_ref[:, k, :] for k in range(lbl_ref.shape[1])]
    top = functools.reduce(jnp.maximum, logit_rows)
    exps = [jnp.exp(r - top) for r in logit_rows]
    lbs = sum(exps[:layer + 1]) / sum(exps)

    sub = lax.broadcasted_iota(jnp.int32, (SUBLANES, A_HEAD_DIM), 0)
    n_groups = c // SUBLANES

    def rows(groups):
        return jnp.concatenate(groups, axis=0)

    def chunk(d, z, q, v, want_out):
        lb = lbs[d:d + 1]
        f = lb + (1.0 - lb) * jax.nn.sigmoid(z)
        kk = 1.0 - f
        fg, kg = _row_groups(f), _row_groups(kk)
        lg = [jnp.log2(g) for g in fg]
        cum = _cumsum_groups(lg, sub)
        total = cum[-1][SUBLANES - 1:SUBLANES]
        if d == 0:
            pos = cum
            q_in = [jnp.exp2(g) for g in cum]
            k_out = [jnp.exp2(total - g) for g in cum]
        else:
            pos = [a - b for a, b in zip(cum, lg)]
            q_in = [jnp.exp2(total - g) for g in pos]
            k_out = [jnp.exp2(g) for g in pos]
        k_leave = rows([a * b for a, b in zip(kg, k_out)]).astype(BF16)
        decay = jnp.exp2(total)
        if not want_out:
            return k_leave, decay, None, None, None
        qg = _row_groups(q)
        o_diag = jnp.sum(q * kk, axis=1, keepdims=True) * v
        upper, lower = (qg, kg) if d == 0 else (kg, qg)
        operands = []
        for w in SCAN_LEVELS:
            y = []
            for k in range(n_groups):
                if w == 1:
                    odd = (sub & 1) != 0
                    y.append(jnp.where(odd, qg[k] * fg[k], kg[k]) if d == 0
                             else jnp.where(odd, kg[k], qg[k] * fg[k]))
                    continue
                gap = pos[k] - _hold_boundary(cum, w, k, sub)
                if w >= SUBLANES:
                    bit_set = (k * SUBLANES) & w != 0
                    y.append(upper[k] * jnp.exp2(gap) if bit_set else lower[k] * jnp.exp2(-gap))
                else:
                    bit = (sub & w) != 0
                    y.append(jnp.where(bit, upper[k], lower[k]) * jnp.exp2(jnp.where(bit, gap, -gap)))
            operands.append(rows(y).astype(BF16))
        qs = rows([a * b for a, b in zip(qg, q_in)]).astype(BF16)
        return k_leave, decay, qs, o_diag, operands

    def within_chunk(d, operands):
        t_idx = lax.broadcasted_iota(jnp.int32, (c, c), 0)
        s_idx = lax.broadcasted_iota(jnp.int32, (c, c), 1)
        split = t_idx ^ s_idx
        att = None
        for w, yb in reversed(list(zip(SCAN_LEVELS, operands))):
            scores = _dot_nt(yb, yb)
            att = scores if att is None else jnp.where(split < 2 * w, scores, att)
        return jnp.where(t_idx > s_idx if d == 0 else t_idx < s_idx, att, 0.0)

    st_ref[...] = jnp.zeros_like(st_ref)

    def ctx_body(i, carry):
        for d in range(2):
            ci = i if d == 0 else n_ctx - 1 - i
            rows_i = pl.ds(pl.multiple_of(ci * c, c), c)
            k_leave, decay, _, _, _ = chunk(d, cz_refs[d][rows_i, :], None, None, False)
            st_ref[d] = st_ref[d] * decay + _dot_tn(cv_ref[rows_i, :].astype(BF16), k_leave)
        return carry

    lax.fori_loop(0, n_ctx, ctx_body, 0, unroll=2)

    def chunk_rows(ci):
        start = ci * c
        return pl.ds(start if isinstance(ci, int) else pl.multiple_of(start, c), c)

    def prepare(ci):
        rows_i = chunk_rows(ci)
        for d in range(2):
            k_leave, decay, qs, o_diag, operands = chunk(d, z_refs[d][rows_i, :], q_ref[rows_i, :],
                                                         v_ref[rows_i, :], True)
            kl_ref[d, rows_i, :] = k_leave
            dec_ref[d, ci] = decay
            qs_ref[d, rows_i, :] = qs
            out_refs[d][rows_i, :] = o_diag
            for lvl, yb in enumerate(operands):
                lv_ref[d, lvl, rows_i, :] = yb

    def scores(ci):
        rows_i = chunk_rows(ci)
        vb = v_ref[rows_i, :].astype(BF16)
        for d in range(2):
            inc_ref[d, ci] = _dot_tn(vb, kl_ref[d, rows_i, :])
            operands = [lv_ref[d, lvl, rows_i, :] for lvl in range(len(SCAN_LEVELS))]
            att_ref[d, rows_i, :] = within_chunk(d, operands).astype(BF16)

    pair = 2
    for j in range(pair):
        prepare(j)

    def skewed_body(i, carry):
        for j in range(pair):
            scores((i - 1) * pair + j)
        for j in range(pair):
            prepare(i * pair + j)
        return carry

    lax.fori_loop(1, n_chunks // pair, skewed_body, 0)
    for j in range(pair):
        scores(n_chunks - pair + j)

    def carry_body(i, carry):
        for d in range(2):
            ci = i if d == 0 else n_chunks - 1 - i
            rows_i = pl.ds(pl.multiple_of(ci * c, c), c)
            state = st_ref[d]
            out_refs[d][rows_i, :] += (_dot(att_ref[d, rows_i, :], v_ref[rows_i, :].astype(BF16))
                                       + _dot_nt(qs_ref[d, rows_i, :], state.astype(BF16)))
            st_ref[d] = state * dec_ref[d, ci] + inc_ref[d, ci]
        return carry

    lax.fori_loop(0, n_chunks, carry_body, 0, unroll=8)

    rb = 256
    gn = gn_ref[...]
    cw = cw_ref[...]

    def read_body(i, carry):
        rows_i = pl.ds(pl.multiple_of(i * rb, rb), rb)
        o = of_ref[rows_i, :] + ob_ref[rows_i, :]
        ms = jnp.mean(o * o, axis=-1, keepdims=True)
        on = o * lax.rsqrt(ms + EPS) * gn
        g = g_ref[rows_i, :]
        o_ref[rows_i, :] = (on * (g * jax.nn.sigmoid(g))).astype(o_ref.dtype)
        conv = _conv3_rows(gc_ref[rows_i, :] * u_ref[rows_i, :], cw, GRID_W)
        oc_ref[rows_i, :] = (gb_ref[rows_i, :] * conv).astype(oc_ref.dtype)
        return carry

    lax.fori_loop(0, SEQ // rb, read_body, 0)


def _even_mixer(p, cp, lb_logits, gnorm_g, conv_w, layer):
    h = A_HEADS
    hd = A_HEAD_DIM
    n_lb = lb_logits.shape[1]

    def col(k):
        return pl.BlockSpec((SEQ, hd), lambda b, j, k=k: (b, k * h + j))

    def ccol(k):
        return pl.BlockSpec((CTX_LEN, hd), lambda b, j, k=k: (b, k * h + j))

    out_spec = pl.BlockSpec((SEQ, hd), lambda b, j: (b, j))
    return pl.pallas_call(
        functools.partial(_scan_kernel, layer=layer),
        grid=(BATCH, h),
        in_specs=[col(0), col(1), col(2), col(3), col(4), col(5), col(6), col(7),
                  ccol(0), ccol(1), ccol(2),
                  pl.BlockSpec((2, n_lb, hd), lambda b, j: (0, 0, j)),
                  pl.BlockSpec((1, hd), lambda b, j: (0, j)),
                  pl.BlockSpec((3, hd), lambda b, j: (0, j))],
        out_specs=(out_spec, out_spec),
        out_shape=(jax.ShapeDtypeStruct((TOKENS, A_WIDTH), BF16),
                   jax.ShapeDtypeStruct((TOKENS, B_WIDTH), BF16)),
        scratch_shapes=[pltpu.VMEM((SEQ, hd), F32), pltpu.VMEM((SEQ, hd), F32),
                        pltpu.VMEM((2, hd, hd), F32), pltpu.VMEM((2, SEQ, hd), BF16),
                        pltpu.VMEM((2, SEQ // SCAN_CHUNK, hd, hd), F32),
                        pltpu.VMEM((2, SEQ // SCAN_CHUNK, 1, hd), F32),
                        pltpu.VMEM((2, SEQ, hd), BF16),
                        pltpu.VMEM((2, len(SCAN_LEVELS), SEQ, hd), BF16),
                        pltpu.VMEM((2, SEQ, SCAN_CHUNK), BF16)],
        compiler_params=_params("arbitrary", "arbitrary"),
        name="hgrn_scan",
    )(p, p, p, p, p, p, p, p, cp, cp, cp, lb_logits, gnorm_g.reshape(1, A_WIDTH), conv_w)


def _split_bf16(x):
    hi = x.astype(BF16)
    return hi, (x - hi.astype(F32)).astype(BF16)


def _filter_kernel(z_ref, w1_ref, b1_ref, w2_ref, b2_ref, w3_ref, b3_ref, fr_ref,
                   w4f_top_ref, w4f_bot_ref, w4b_top_ref, w4b_bot_ref, cond_ref, aw_ref, ab_ref,
                   sum_ref, diff_ref, mod_ref, hid_ref, *, tc):
    o = pl.program_id(0)
    j = pl.program_id(1)

    cond = cond_ref[...]
    mod_ref[...] = _dot((cond * jax.nn.sigmoid(cond)).astype(BF16), aw_ref[...].astype(BF16)) + ab_ref[...]

    @pl.when((o == 0) & (j == 0))
    def _():
        fr = fr_ref[...]
        h = jnp.sin(fr * (_dot_f32(z_ref[...], w1_ref[...]) + b1_ref[...]))
        h = jnp.sin(fr * (_dot_f32(h, w2_ref[...]) + b2_ref[...]))
        h = jnp.sin(fr * (_dot_f32(h, w3_ref[...]) + b3_ref[...]))
        hi, lo = _split_bf16(h)
        hid_ref[...] = jnp.concatenate([hi, lo, hi], axis=1)

    def project(top_ref, bot_ref):
        halves = []
        for w_ref in (top_ref, bot_ref):
            w_hi, w_lo = _split_bf16(w_ref[...])
            halves.append(_dot(hid_ref[...], jnp.concatenate([w_hi, w_hi, w_lo], axis=0)))
        return jnp.concatenate(halves, axis=0)

    rows = lax.broadcasted_iota(jnp.int32, (SEQ, tc), 0)
    chan = lax.broadcasted_iota(jnp.int32, (SEQ, tc), 1) + j * tc
    lag = jnp.where(rows < HALF, 2 * rows, 2 * rows - (SEQ - 1))
    t = lag.astype(F32) * (1.0 / (SEQ - 1))
    max_decay = math.log(HY_DECAY_TARGET) / HY_FAST_PCT
    min_decay = math.log(HY_DECAY_TARGET) / HY_SLOW_PCT
    deltas = jnp.abs(min_decay + chan.astype(F32) * ((max_decay - min_decay) / (D_MODEL - 1)))
    window = jnp.exp(-t * deltas)
    fw = project(w4f_top_ref, w4f_bot_ref) * window
    bw = project(w4b_top_ref, w4b_bot_ref) * window
    first = rows == 0
    a = fw + jnp.where(first, bw, 0.0)
    bb = jnp.where(first, 0.0, bw)
    inv = 1.0 / jnp.sum(jnp.abs(a) + jnp.abs(bb), axis=0, keepdims=True)
    sum_ref[...] = ((a + bb) * inv).astype(sum_ref.dtype)
    diff_ref[...] = ((a - bb) * inv).astype(diff_ref.dtype)


@functools.lru_cache(maxsize=None)
def _filter_positions():
    l = SEQ
    t = np.linspace(0.0, 1.0, l, dtype=np.float32)
    w = (2.0 * math.pi * np.arange(l, dtype=np.float32) / l).astype(np.float32)
    bands = np.linspace(1e-4, HY_BANDS - 1, HY_BANDS, dtype=np.float32)
    ang = w[:, None] * bands[None, :]
    z = np.concatenate([t[:, None], np.cos(ang), -np.sin(ang)], axis=-1).astype(np.float32)
    z = np.concatenate([z[0::2], z[1::2]], axis=0)
    z = np.pad(z, ((0, 0), (0, LANES // 2 - HY_EMB)))
    return np.concatenate([z[:l // 2], z[l // 2:]], axis=1)


def _filter_taps_and_modulation(fw1, fb1, fw2, fb2, fw3, fb3, fw4, freq, cond, ada_w, ada_b):
    l = SEQ
    tc = 256
    nj = D_MODEL // tc
    depth, d, n_mod = ada_w.shape
    assert depth == 2 and n_mod % (nj * LANES) == 0
    ta = n_mod // nj
    hw = LANES // 2

    def pad(a, rows, cols):
        return jnp.pad(a, ((0, rows - a.shape[0]), (0, cols - a.shape[1])))

    def twice(a):
        a = pad(a, hw, hw)
        zero = jnp.zeros_like(a)
        return jnp.concatenate([jnp.concatenate([a, zero], axis=1), jnp.concatenate([zero, a], axis=1)], axis=0)

    def row2(v):
        v = pad(v[None, :], 1, hw)
        return jnp.concatenate([v, v], axis=1)

    def small(shape):
        return pl.BlockSpec(shape, lambda o, j: (0, 0))

    w4_top = pad(fw4, LANES, fw4.shape[1])
    w4_bot = jnp.concatenate([jnp.zeros((hw, fw4.shape[1]), F32), pad(fw4, hw, fw4.shape[1])], axis=0)
    out_sds = jax.ShapeDtypeStruct((l, 2 * D_MODEL), BF16)
    out_spec = pl.BlockSpec((l, tc), lambda o, j: (0, o * nj + j))
    side_f = pl.BlockSpec((LANES, tc), lambda o, j: (0, 2 * o * nj + j))
    side_b = pl.BlockSpec((LANES, tc), lambda o, j: (0, (2 * o + 1) * nj + j))
    sq, vec = small((LANES, LANES)), small((1, LANES))
    return pl.pallas_call(
        functools.partial(_filter_kernel, tc=tc),
        grid=(2, nj),
        in_specs=[small((l // 2, LANES)), sq, vec, sq, vec, sq, vec, vec, side_f, side_f, side_b, side_b,
                  small((8, d)),
                  pl.BlockSpec((None, d, ta), lambda o, j: (o, 0, j)),
                  pl.BlockSpec((None, 1, ta), lambda o, j: (o, 0, j))],
        out_specs=(out_spec, out_spec, pl.BlockSpec((None, 8, ta), lambda o, j: (o, 0, j))),
        out_shape=(out_sds, out_sds, jax.ShapeDtypeStruct((depth, 8, n_mod), F32)),
        scratch_shapes=[pltpu.VMEM((l // 2, 3 * LANES), BF16)],
        compiler_params=_params("arbitrary", "arbitrary"),
        name="hyena_filter",
    )(jnp.asarray(_filter_positions()), twice(fw1), row2(fb1), twice(fw2), row2(fb2), twice(fw3), row2(fb3),
      row2(freq), w4_top, w4_bot, w4_top, w4_bot, cond, ada_w, ada_b.reshape(depth, 1, n_mod))


def _trig_table(a_rows, b_cols, scale):
    n = (a_rows[:, None].astype(np.int64) * b_cols[None, :].astype(np.int64)) % (4 * SEQ)
    ang = n.astype(np.float64) * (math.pi / (2 * SEQ))
    return (np.cos(ang) * scale).astype(np.float32), (np.sin(ang) * scale).astype(np.float32)


@functools.lru_cache(maxsize=None)
def _dft_tables():
    idx = np.arange(QUARTER)
    ana_c, ana_s, syn_c, syn_s = [], [], [], []
    for r in range(4):
        c, s = _trig_table(2 * idx + 1, 4 * idx + r, 1.0)
        ana_c.append(c)
        ana_s.append(s)
        c, s = _trig_table(4 * idx + r, 2 * idx + 1, 1.0 / SEQ)
        syn_c.append(c)
        syn_s.append(s)
    fmap = np.concatenate([idx, HALF + idx])
    lag = np.arange(HALF)
    cee, see = _trig_table(2 * fmap + 1, 2 * lag, 1.0)
    ceo, seo = _trig_table(2 * fmap + 1, 2 * lag + 1, 1.0)
    return np.stack(ana_c + ana_s), np.stack(syn_c + syn_s), (cee, ceo, see, seo)


def _filter_dft_kernel(cee_ref, ceo_ref, see_ref, seo_ref, fs_ref, fd_ref, p_ref, pm_ref, q_ref, qm_ref):
    pe = _dot(cee_ref[...], fs_ref[0:HALF, :])
    po = _dot(ceo_ref[...], fs_ref[HALF:SEQ, :])
    qe = _dot(see_ref[...], fd_ref[0:HALF, :])
    qo = _dot(seo_ref[...], fd_ref[HALF:SEQ, :])
    p_ref[...] = pe + po
    pm_ref[...] = pe - po
    q_ref[...] = qe + qo
    qm_ref[...] = qo - qe


def _filter_dft(tables, taps_sum, taps_diff):
    tm, tn = HALF, 256
    n_cols = taps_sum.shape[1]
    mat = pl.BlockSpec((tm, HALF), lambda j, i: (i, 0), pipeline_mode=pl.Buffered(1))
    tap = pl.BlockSpec((SEQ, tn), lambda j, i: (0, j))
    out_spec = pl.BlockSpec((tm, tn), lambda j, i: (i, j))
    out_sds = jax.ShapeDtypeStruct((HALF, n_cols), F32)
    return pl.pallas_call(
        _filter_dft_kernel,
        grid=(n_cols // tn, HALF // tm),
        in_specs=[mat, mat, mat, mat, tap, tap],
        out_specs=(out_spec,) * 4,
        out_shape=(out_sds,) * 4,
        compiler_params=_params("arbitrary", "arbitrary"),
        name="filter_dft",
    )(*tables, taps_sum, taps_diff)


def _dft4_fwd_kernel(tab_ref, z_ref, p_ref, pm_ref, q_ref, qm_ref, x_ref, zs_ref, *, slab):
    panels = zs_ref.shape[0]
    for k in range(panels):
        zs_ref[k] = z_ref[:, k * LANES:(k + 1) * LANES].astype(F32)
    zr = [jnp.concatenate([zs_ref[k, pl.ds(r, QUARTER, stride=4), :] for k in range(panels)],
                          axis=1).astype(BF16) for r in range(4)]

    for f0 in range(0, QUARTER, slab):
        rows = slice(f0, f0 + slab)
        mirror = slice(QUARTER + f0, QUARTER + f0 + slab)
        c = [_dot(tab_ref[r, rows, :], zr[r]) for r in range(4)]
        s = [_dot(tab_ref[4 + r, rows, :], zr[r]) for r in range(4)]
        e, o, em, om = c[0] + c[2], c[1] + c[3], c[0] - c[2], c[1] - c[3]
        se, so, sem, som = s[0] + s[2], s[1] + s[3], s[0] - s[2], s[1] - s[3]
        a = (e + o, em + som, em - som, e - o)
        b = (se + so, om - sem, sem + om, so - se)
        pk = (p_ref[rows, :], pm_ref[mirror, :], p_ref[mirror, :], pm_ref[rows, :])
        qk = (q_ref[rows, :], qm_ref[mirror, :], q_ref[mirror, :], qm_ref[rows, :])
        u = [a[k] * pk[k] - b[k] * qk[k] for k in range(4)]
        v = [a[k] * qk[k] + b[k] * pk[k] for k in range(4)]
        up, um, wp, wm = u[0] + u[3], u[0] - u[3], u[1] + u[2], u[1] - u[2]
        vp, vm, yp, ym = v[0] + v[3], v[0] - v[3], v[1] + v[2], v[1] - v[2]
        outs = (up + wp, um + yp, up - wp, um - yp,
                vm - ym, vp + wm, vm + ym, vp - wm)
        for n, val in enumerate(outs):
            x_ref[n, rows, :] = val.astype(x_ref.dtype)


def _dft4_fwd(tab, z_arr, z_col0, spectra, order):
    tn, slab = 512, 256
    nj = D_MODEL // tn
    zc = z_col0 // tn
    spec = pl.BlockSpec((HALF, tn), lambda j, b: (0, order * nj + j))
    return pl.pallas_call(
        functools.partial(_dft4_fwd_kernel, slab=slab),
        grid=(nj, BATCH),
        in_specs=[pl.BlockSpec(tab.shape, lambda j, b: (0, 0, 0), pipeline_mode=pl.Buffered(1)),
                  pl.BlockSpec((SEQ, tn), lambda j, b: (b, zc + j)), spec, spec, spec, spec],
        out_specs=pl.BlockSpec((8, QUARTER, tn), lambda j, b: (0, b, j)),
        out_shape=jax.ShapeDtypeStruct((8, BATCH * QUARTER, D_MODEL), BF16),
        scratch_shapes=[pltpu.VMEM((tn // LANES, SEQ, LANES), F32)],
        compiler_params=_params("arbitrary", "arbitrary"),
        name="dft_analysis",
    )(tab, z_arr, *spectra)


def _dft4_inv_kernel(tab_ref, x_ref, z_ref, gate_ref, skip_ref, o_ref, ys_ref):
    panels = ys_ref.shape[0]
    for r in range(4):
        y_r = _dot(tab_ref[r], x_ref[r]) + _dot(tab_ref[4 + r], x_ref[4 + r])
        for k in range(panels):
            ys_ref[k, pl.ds(r, QUARTER, stride=4), :] = y_r[:, k * LANES:(k + 1) * LANES]
    y = jnp.concatenate([ys_ref[k] for k in range(panels)], axis=1)
    y = y + z_ref[...].astype(F32) * skip_ref[...]
    o_ref[...] = (gate_ref[...].astype(F32) * y).astype(o_ref.dtype)


def _dft4_inv(tab, x, z_arr, z_col0, gate_arr, gate_col0, skip):
    tn = 512
    nj = D_MODEL // tn
    zc, gc = z_col0 // tn, gate_col0 // tn
    return pl.pallas_call(
        _dft4_inv_kernel,
        grid=(nj, BATCH),
        in_specs=[pl.BlockSpec(tab.shape, lambda j, b: (0, 0, 0), pipeline_mode=pl.Buffered(1)),
                  pl.BlockSpec((8, QUARTER, tn), lambda j, b: (0, b, j)),
                  pl.BlockSpec((SEQ, tn), lambda j, b: (b, zc + j)),
                  pl.BlockSpec((SEQ, tn), lambda j, b: (b, gc + j)),
                  pl.BlockSpec((1, tn), lambda j, b: (0, j))],
        out_specs=pl.BlockSpec((SEQ, tn), lambda j, b: (b, j)),
        out_shape=jax.ShapeDtypeStruct((TOKENS, D_MODEL), BF16),
        scratch_shapes=[pltpu.VMEM((tn // LANES, SEQ, LANES), F32)],
        compiler_params=_params("arbitrary", "arbitrary"),
        name="dft_synthesis",
    )(tab, x, z_arr, gate_arr, skip.reshape(1, D_MODEL))


def _mlp_block(x2d, mod, l, norm_g, w1, w2):
    sh2, sc2, g2 = (mod[l, :BATCH, k * D_MODEL:(k + 1) * D_MODEL] for k in (3, 4, 5))
    h = _prep(x2d.reshape(BATCH, SEQ, D_MODEL), norm_g[l, 1], sc2, sh2, BF16, 1024).reshape(TOKENS, D_MODEL)
    hid = _mm_plain(h, w1, D_FF, BF16, _epi_relu2, 1024, 1024, "mlp_up", w_layer=l)
    return _mm_residual([hid], w2, l, x2d, g2, 512, 512, "mlp_down", w_single_buffer=True)


def kernel(x, c, ctx, c_ctx, ada_w, ada_b, norm_g, lb_logits, ab_w_in, ab_conv_w, ab_gnorm_g, ab_w_out,
           hy_in_w, hy_short_w, hy_out_w, hy_fw1, hy_fb1, hy_fw2, hy_fb2, hy_fw3, hy_fb3, hy_fw4, hy_freq,
           hy_skip, mlp_w1, mlp_w2, final_g):
    d = D_MODEL
    cond = jnp.concatenate([c, c_ctx[None, :], jnp.zeros((3, d), F32)], axis=0)
    taps_sum, taps_diff, mod = _filter_taps_and_modulation(
        hy_fw1[0], hy_fb1[0], hy_fw2[0], hy_fb2[0], hy_fw3[0], hy_fb3[0], hy_fw4[0], hy_freq[0],
        cond, ada_w, ada_b)
    x2d = x.reshape(TOKENS, d)

    sh1, sc1, g1 = (mod[0, :BATCH, k * d:(k + 1) * d] for k in (0, 1, 2))
    h = _prep(x, norm_g[0, 0], sc1, sh1, BF16, 1024).reshape(TOKENS, d)
    csh = jnp.broadcast_to(mod[0, BATCH:BATCH + 1, 0:d], (BATCH, d))
    csc = jnp.broadcast_to(mod[0, BATCH:BATCH + 1, d:2 * d], (BATCH, d))
    hc = _prep(ctx, norm_g[0, 0], csc, csh, BF16, CTX_LEN).reshape(BATCH * CTX_LEN, d)
    p = _mm_plain(h, ab_w_in, AB_IN_WIDTH, F32, _epi_store, 1024, 1024, "ab_in_proj", w_layer=0)
    cp = _mm_plain(hc, ab_w_in, 3 * A_WIDTH, F32, _epi_store, 1024, 1024, "ab_ctx_proj", w_layer=0)
    mix_a, mix_b = _even_mixer(p, cp, lb_logits, ab_gnorm_g[0], ab_conv_w[0], 0)
    x2d = _mm_residual([mix_a, mix_b], ab_w_out, 0, x2d, g1, 1024, 1024, "ab_out_proj")
    x2d = _mlp_block(x2d, mod, 0, norm_g, mlp_w1, mlp_w2)

    sh1, sc1, g1 = (mod[1, :BATCH, k * d:(k + 1) * d] for k in (0, 1, 2))
    h = _prep(x2d.reshape(BATCH, SEQ, d), norm_g[1, 0], sc1, sh1, BF16, 1024).reshape(TOKENS, d)
    pc = _mm([h], hy_in_w, w_layer=0, tm=1024, tn=1024, n_cols=3 * d, epi=_epi_conv3_grid,
             extras=(hy_short_w[0],), extra_specs=(pl.BlockSpec((3, 1024), lambda j, i: (0, j)),),
             out_shapes=jax.ShapeDtypeStruct((TOKENS, 3 * d), BF16),
             out_specs=pl.BlockSpec((1024, 1024), lambda j, i: (i, j)), name="hy_in_proj")
    analysis, synthesis, filter_tables = jax.tree.map(lambda t: jnp.asarray(t).astype(BF16), _dft_tables())
    spectra = _filter_dft(filter_tables, taps_sum, taps_diff)
    coef = _dft4_fwd(analysis, pc, 2 * d, spectra, 0)
    z = _dft4_inv(synthesis, coef, pc, 2 * d, pc, 0, hy_skip[0, 0])
    coef = _dft4_fwd(analysis, z, 0, spectra, 1)
    z = _dft4_inv(synthesis, coef, z, 0, pc, d, hy_skip[0, 1])
    x2d = _mm_residual([z], hy_out_w, 0, x2d, g1, 1024, 1024, "hy_out_proj")
    x2d = _mlp_block(x2d, mod, 1, norm_g, mlp_w1, mlp_w2)

    zeros = jnp.zeros((BATCH, d), F32)
    return _prep(x2d.reshape(BATCH, SEQ, d), final_g, zeros, zeros, F32, 1024)
```

```python
import functools
import math

import numpy as np
import jax
import jax.numpy as jnp
from jax import lax
from jax.experimental import pallas as pl
from jax.experimental.pallas import tpu as pltpu

F32 = jnp.float32
BF16 = jnp.bfloat16

D_MODEL = 2048
BATCH = 4
SEQ = 2048
CTX_LEN = 256
GRID_W = 64
EPS = 1e-6
A_HEAD_DIM = 128
A_WIDTH = D_MODEL // 2
A_HEADS = A_WIDTH // A_HEAD_DIM
B_WIDTH = D_MODEL - A_WIDTH
AB_IN_WIDTH = 5 * A_WIDTH + 3 * B_WIDTH
HY_EMB = 33
HY_BANDS = (HY_EMB - 1) // 2
HY_DECAY_TARGET = 1e-2
HY_FAST_PCT = 0.3
HY_SLOW_PCT = 1.5
D_FF = 4 * D_MODEL
TOKENS = BATCH * SEQ
HALF = SEQ // 2
QUARTER = SEQ // 4

SCAN_CHUNK = 64
SCAN_LEVELS = (1, 2, 4, 8, 16, 32)

SUBLANES = 8
LANES = 128

VMEM_LIMIT_BYTES = 56 * 1024 * 1024


def _params(*sem):
    return pltpu.CompilerParams(dimension_semantics=sem, vmem_limit_bytes=VMEM_LIMIT_BYTES)


def _dot(a, b):
    return jnp.dot(a, b, preferred_element_type=F32)


def _dot_nt(a, b):
    return lax.dot_general(a, b, (((1,), (1,)), ((), ())), preferred_element_type=F32)


def _dot_tn(a, b):
    return lax.dot_general(a, b, (((0,), (0,)), ((), ())), preferred_element_type=F32)


def _dot_f32(a, b):
    return jnp.dot(a, b, precision=lax.Precision.HIGHEST, preferred_element_type=F32)


def _conv3_rows(t, w, group):
    rows = lax.broadcasted_iota(jnp.int32, t.shape, 0) % group
    prev = jnp.where(rows == 0, 0.0, pltpu.roll(t, 1, 0))
    nxt = jnp.where(rows == group - 1, 0.0, pltpu.roll(t, t.shape[0] - 1, 0))
    return w[0:1] * prev + w[1:2] * t + w[2:3] * nxt


def _prep_kernel(x_ref, g_ref, sc_ref, sh_ref, o_ref):
    x = x_ref[...]
    ms = jnp.mean(x * x, axis=-1, keepdims=True)
    xn = x * lax.rsqrt(ms + EPS) * g_ref[...]
    o_ref[...] = (xn * (1.0 + sc_ref[...]) + sh_ref[...]).astype(o_ref.dtype)


def _prep(x, g, sc, sh, out_dtype, ts):
    b, l, d = x.shape
    return pl.pallas_call(
        _prep_kernel,
        grid=(b, l // ts),
        in_specs=[
            pl.BlockSpec((None, ts, d), lambda i, j: (i, j, 0)),
            pl.BlockSpec((1, d), lambda i, j: (0, 0)),
            pl.BlockSpec((None, 1, d), lambda i, j: (i, 0, 0)),
            pl.BlockSpec((None, 1, d), lambda i, j: (i, 0, 0)),
        ],
        out_specs=pl.BlockSpec((None, ts, d), lambda i, j: (i, j, 0)),
        out_shape=jax.ShapeDtypeStruct((b, l, d), out_dtype),
        compiler_params=_params("arbitrary", "arbitrary"),
        name="norm_mod",
    )(x, g.reshape(1, d), sc.reshape(b, 1, d), sh.reshape(b, 1, d))


def _mm_kernel(*refs, n_a, k_sizes, epi, cast_w):
    a_refs = refs[:n_a]
    w_ref = refs[n_a]
    rest = refs[n_a + 1:]
    if cast_w:
        w_ref, rest = rest[-1], rest[:-1]

        @pl.when(pl.program_id(1) == 0)
        def _():
            w_ref[...] = refs[n_a][...].astype(BF16)

    acc = None
    off = 0
    for a_ref, k in zip(a_refs, k_sizes):
        part = _dot(a_ref[...], w_ref[off:off + k, :])
        acc = part if acc is None else acc + part
        off += k
    epi(acc, *rest)


def _mm(a_list, w, *, tm, tn, n_cols, epi, extras, extra_specs, out_shapes, out_specs, name,
        w_layer=None, w_single_buffer=False):
    m = a_list[0].shape[0]
    k_sizes = tuple(a.shape[1] for a in a_list)
    k_total = sum(k_sizes)
    cast_w = w_layer is not None
    assert w.shape[-2] == k_total and m % tm == 0 and n_cols % tn == 0
    in_specs = [pl.BlockSpec((tm, k), lambda j, i: (i, 0)) for k in k_sizes]
    w_mode = dict(pipeline_mode=pl.Buffered(1)) if w_single_buffer else {}
    if cast_w:
        in_specs.append(pl.BlockSpec((None, k_total, tn), lambda j, i: (w_layer, 0, j), **w_mode))
    else:
        in_specs.append(pl.BlockSpec((k_total, tn), lambda j, i: (0, j), **w_mode))
    in_specs.extend(extra_specs)
    return pl.pallas_call(
        functools.partial(_mm_kernel, n_a=len(a_list), k_sizes=k_sizes, epi=epi, cast_w=cast_w),
        grid=(n_cols // tn, m // tm),
        in_specs=in_specs,
        out_specs=out_specs,
        out_shape=out_shapes,
        scratch_shapes=[pltpu.VMEM((k_total, tn), BF16)] if cast_w else [],
        compiler_params=_params("arbitrary", "arbitrary"),
        name=name,
    )(*a_list, w, *extras)


def _epi_store(acc, o_ref):
    o_ref[...] = acc.astype(o_ref.dtype)


def _epi_relu2(acc, o_ref):
    r = jnp.maximum(acc, 0.0)
    o_ref[...] = (r * r).astype(o_ref.dtype)


def _epi_residual(acc, x_ref, gate_ref, o_ref):
    o_ref[...] = x_ref[...] + gate_ref[...] * acc


def _epi_residual_norm(acc, x_ref, gate_ref, g_ref, sc_ref, sh_ref, o_ref, h_ref):
    x = x_ref[...] + gate_ref[...] * acc
    o_ref[...] = x
    ms = jnp.mean(x * x, axis=-1, keepdims=True)
    xn = x * lax.rsqrt(ms + EPS) * g_ref[...]
    h_ref[...] = (xn * (1.0 + sc_ref[...]) + sh_ref[...]).astype(h_ref.dtype)


def _epi_conv3_grid(acc, w_ref, o_ref):
    o_ref[...] = _conv3_rows(acc, w_ref[...], GRID_W).astype(o_ref.dtype)


def _mm_plain(a, w, n_cols, out_dtype, epi, tm, tn, name, w_layer=None):
    m = a.shape[0]
    return _mm([a], w, tm=tm, tn=tn, n_cols=n_cols, epi=epi, extras=(), extra_specs=(),
               out_shapes=jax.ShapeDtypeStruct((m, n_cols), out_dtype),
               out_specs=pl.BlockSpec((tm, tn), lambda j, i: (i, j)), name=name, w_layer=w_layer)


def _mm_residual(a_list, w, w_layer, x, gate, tm, tn, name, w_single_buffer=False):
    m, n = x.shape
    tiles_per_batch = SEQ // tm
    return _mm(a_list, w, tm=tm, tn=tn, n_cols=n, epi=_epi_residual,
               extras=(x, gate.reshape(BATCH, 1, n)),
               extra_specs=(pl.BlockSpec((tm, tn), lambda j, i: (i, j)),
                            pl.BlockSpec((None, 1, tn), lambda j, i: (i // tiles_per_batch, 0, j))),
               out_shapes=jax.ShapeDtypeStruct((m, n), F32),
               out_specs=pl.BlockSpec((tm, tn), lambda j, i: (i, j)), name=name,
               w_layer=w_layer, w_single_buffer=w_single_buffer)


def _mm_residual_norm(a_list, w, w_layer, x, gate, norm_g, sc, sh, tm, name):
    m, n = x.shape
    tiles_per_batch = SEQ // tm

    def per_batch(v):
        return v.reshape(BATCH, 1, n), pl.BlockSpec((None, 1, n), lambda j, i: (i // tiles_per_batch, 0, 0))

    (gate3, gate_spec), (sc3, sc_spec), (sh3, sh_spec) = per_batch(gate), per_batch(sc), per_batch(sh)
    rows = pl.BlockSpec((tm, n), lambda j, i: (i, 0))
    return _mm(a_list, w, tm=tm, tn=n, n_cols=n, epi=_epi_residual_norm,
               extras=(x, gate3, norm_g.reshape(1, n), sc3, sh3),
               extra_specs=(rows, gate_spec, pl.BlockSpec((1, n), lambda j, i: (0, 0)), sc_spec, sh_spec),
               out_shapes=(jax.ShapeDtypeStruct((m, n), F32), jax.ShapeDtypeStruct((m, n), BF16)),
               out_specs=(rows, rows), name=name, w_layer=w_layer, w_single_buffer=True)


def _row_groups(x):
    return [x[k:k + SUBLANES] for k in range(0, x.shape[0], SUBLANES)]


def _cumsum_groups(groups, sub):
    out, carry = [], None
    for g in groups:
        for s in (1, 2, 4):
            g = g + jnp.where(sub >= s, pltpu.roll(g, s, 0), 0.0)
        if carry is not None:
            g = g + carry
        out.append(g)
        carry = g[SUBLANES - 1:SUBLANES]
    return out


def _hold_boundary(cum, w, k, sub):
    shape = cum[k].shape
    if w >= SUBLANES // 2:
        r = (k * SUBLANES // (2 * w)) * 2 * w + w - 1
        return jnp.broadcast_to(cum[r // SUBLANES][r % SUBLANES:r % SUBLANES + 1], shape)
    assert w == 2
    return jnp.where(sub < 4, jnp.broadcast_to(cum[k][1:2], shape), jnp.broadcast_to(cum[k][5:6], shape))


def _scan_kernel(zf_ref, zb_ref, v_ref, q_ref, g_ref, u_ref, gb_ref, gc_ref, czf_ref, czb_ref, cv_ref,
                 lbl_ref, gn_ref, cw_ref, o_ref, oc_ref,
                 of_ref, ob_ref, st_ref, qs_ref, inc_ref, dec_ref, kl_ref, lv_ref, att_ref, *, layer):
    c = SCAN_CHUNK
    n_chunks = SEQ // c
    n_ctx = CTX_LEN // c
    z_refs = (zf_ref, zb_ref)
    cz_refs = (czf_ref, czb_ref)
    out_refs = (of_ref, ob_ref)

    logit_rows = [lbl_ref[:, k, :] for k in range(lbl_ref.shape[1])]
    top = functools.reduce(jnp.maximum, logit_rows)
    exps = [jnp.exp(r - top) for r in logit_rows]
    lbs = sum(exps[:layer + 1]) / sum(exps)

    sub = lax.broadcasted_iota(jnp.int32, (SUBLANES, A_HEAD_DIM), 0)
    n_groups = c // SUBLANES

    def rows(groups):
        return jnp.concatenate(groups, axis=0)

    def chunk(d, z, q, v, want_out):
        lb = lbs[d:d + 1]
        f = lb + (1.0 - lb) * jax.nn.sigmoid(z)
        kk = 1.0 - f
        fg, kg = _row_groups(f), _row_groups(kk)
        lg = [jnp.log2(g) for g in fg]
        cum = _cumsum_groups(lg, sub)
        total = cum[-1][SUBLANES - 1:SUBLANES]
        if d == 0:
            pos = cum
            q_in = [jnp.exp2(g) for g in cum]
            k_out = [jnp.exp2(total - g) for g in cum]
        else:
            pos = [a - b for a, b in zip(cum, lg)]
            q_in = [jnp.exp2(total - g) for g in pos]
            k_out = [jnp.exp2(g) for g in pos]
        k_leave = rows([a * b for a, b in zip(kg, k_out)]).astype(BF16)
        decay = jnp.exp2(total)
        if not want_out:
            return k_leave, decay, None, None, None
        qg = _row_groups(q)
        o_diag = jnp.sum(q * kk, axis=1, keepdims=True) * v
        upper, lower = (qg, kg) if d == 0 else (kg, qg)
        operands = []
        for w in SCAN_LEVELS:
            y = []
            for k in range(n_groups):
                if w == 1:
                    odd = (sub & 1) != 0
                    y.append(jnp.where(odd, qg[k] * fg[k], kg[k]) if d == 0
                             else jnp.where(odd, kg[k], qg[k] * fg[k]))
                    continue
                hold = _hold_boundary(cum, w, k, sub)
                if w >= SUBLANES:
                    bit_set = (k * SUBLANES) & w != 0
                    y.append(upper[k] * jnp.exp2(pos[k] - hold) if bit_set
                             else lower[k] * jnp.exp2(hold - pos[k]))
                else:
                    bit = (sub & w) != 0
                    sign = jnp.where(bit, 1.0, -1.0)
                    y.append(jnp.where(bit, upper[k], lower[k]) * jnp.exp2((pos[k] - hold) * sign))
            operands.append(rows(y).astype(BF16))
        qs = rows([a * b for a, b in zip(qg, q_in)]).astype(BF16)
        return k_leave, decay, qs, o_diag, operands

    def within_chunk(d, operands):
        t_idx = lax.broadcasted_iota(jnp.int32, (c, c), 0)
        s_idx = lax.broadcasted_iota(jnp.int32, (c, c), 1)
        split = t_idx ^ s_idx
        att = None
        for w, yb in reversed(list(zip(SCAN_LEVELS, operands))):
            scores = _dot_nt(yb, yb)
            att = scores if att is None else jnp.where(split < 2 * w, scores, att)
        return jnp.where(t_idx > s_idx if d == 0 else t_idx < s_idx, att, 0.0)

    st_ref[...] = jnp.zeros_like(st_ref)

    def ctx_body(i, carry):
        for d in range(2):
            ci = i if d == 0 else n_ctx - 1 - i
            rows_i = pl.ds(pl.multiple_of(ci * c, c), c)
            k_leave, decay, _, _, _ = chunk(d, cz_refs[d][rows_i, :], None, None, False)
            st_ref[d] = st_ref[d] * decay + _dot_tn(cv_ref[rows_i, :].astype(BF16), k_leave)
        return carry

    lax.fori_loop(0, n_ctx, ctx_body, 0, unroll=2)

    def chunk_rows(ci):
        start = ci * c
        return pl.ds(start if isinstance(ci, int) else pl.multiple_of(start, c), c)

    def prepare(ci):
        rows_i = chunk_rows(ci)
        for d in range(2):
            k_leave, decay, qs, o_diag, operands = chunk(d, z_refs[d][rows_i, :], q_ref[rows_i, :],
                                                         v_ref[rows_i, :], True)
            kl_ref[d, rows_i, :] = k_leave
            dec_ref[d, ci] = decay
            qs_ref[d, rows_i, :] = qs
            out_refs[d][rows_i, :] = o_diag
            for lvl, yb in enumerate(operands):
                lv_ref[d, lvl, rows_i, :] = yb

    def scores(ci):
        rows_i = chunk_rows(ci)
        vb = v_ref[rows_i, :].astype(BF16)
        for d in range(2):
            inc_ref[d, ci] = _dot_tn(vb, kl_ref[d, rows_i, :])
            operands = [lv_ref[d, lvl, rows_i, :] for lvl in range(len(SCAN_LEVELS))]
            att_ref[d, rows_i, :] = within_chunk(d, operands).astype(BF16)

    pair = 2
    for j in range(pair):
        prepare(j)

    def skewed_body(i, carry):
        for j in range(pair):
            scores((i - 1) * pair + j)
        for j in range(pair):
            prepare(i * pair + j)
        return carry

    lax.fori_loop(1, n_chunks // pair, skewed_body, 0)
    for j in range(pair):
        scores(n_chunks - pair + j)

    def carry_body(i, carry):
        for d in range(2):
            ci = i if d == 0 else n_chunks - 1 - i
            rows_i = pl.ds(pl.multiple_of(ci * c, c), c)
            state = st_ref[d]
            out_refs[d][rows_i, :] += (_dot(att_ref[d, rows_i, :], v_ref[rows_i, :].astype(BF16))
                                       + _dot_nt(qs_ref[d, rows_i, :], state.astype(BF16)))
            st_ref[d] = state * dec_ref[d, ci] + inc_ref[d, ci]
        return carry

    lax.fori_loop(0, n_chunks, carry_body, 0, unroll=8)

    rb = 256
    gn = gn_ref[...]
    cw = cw_ref[...]

    def read_body(i, carry):
        rows_i = pl.ds(pl.multiple_of(i * rb, rb), rb)
        o = of_ref[rows_i, :] + ob_ref[rows_i, :]
        ms = jnp.mean(o * o, axis=-1, keepdims=True)
        on = o * lax.rsqrt(ms + EPS) * gn
        g = g_ref[rows_i, :]
        o_ref[rows_i, :] = (on * (g * jax.nn.sigmoid(g))).astype(o_ref.dtype)
        conv = _conv3_rows(gc_ref[rows_i, :] * u_ref[rows_i, :], cw, GRID_W)
        oc_ref[rows_i, :] = (gb_ref[rows_i, :] * conv).astype(oc_ref.dtype)
        return carry

    lax.fori_loop(0, SEQ // rb, read_body, 0)


def _even_mixer(p, cp, lb_logits, gnorm_g, conv_w, layer):
    h = A_HEADS
    hd = A_HEAD_DIM
    n_lb = lb_logits.shape[1]

    def col(k):
        return pl.BlockSpec((SEQ, hd), lambda b, j, k=k: (b, k * h + j))

    def ccol(k):
        return pl.BlockSpec((CTX_LEN, hd), lambda b, j, k=k: (b, k * h + j))

    out_spec = pl.BlockSpec((SEQ, hd), lambda b, j: (b, j))
    return pl.pallas_call(
        functools.partial(_scan_kernel, layer=layer),
        grid=(BATCH, h),
        in_specs=[col(0), col(1), col(2), col(3), col(4), col(5), col(6), col(7),
                  ccol(0), ccol(1), ccol(2),
                  pl.BlockSpec((2, n_lb, hd), lambda b, j: (0, 0, j)),
                  pl.BlockSpec((1, hd), lambda b, j: (0, j)),
                  pl.BlockSpec((3, hd), lambda b, j: (0, j))],
        out_specs=(out_spec, out_spec),
        out_shape=(jax.ShapeDtypeStruct((TOKENS, A_WIDTH), BF16),
                   jax.ShapeDtypeStruct((TOKENS, B_WIDTH), BF16)),
        scratch_shapes=[pltpu.VMEM((SEQ, hd), F32), pltpu.VMEM((SEQ, hd), F32),
                        pltpu.VMEM((2, hd, hd), F32), pltpu.VMEM((2, SEQ, hd), BF16),
                        pltpu.VMEM((2, SEQ // SCAN_CHUNK, hd, hd), F32),
                        pltpu.VMEM((2, SEQ // SCAN_CHUNK, 1, hd), F32),
                        pltpu.VMEM((2, SEQ, hd), BF16),
                        pltpu.VMEM((2, len(SCAN_LEVELS), SEQ, hd), BF16),
                        pltpu.VMEM((2, SEQ, SCAN_CHUNK), BF16)],
        compiler_params=_params("arbitrary", "arbitrary"),
        name="hgrn_scan",
    )(p, p, p, p, p, p, p, p, cp, cp, cp, lb_logits, gnorm_g.reshape(1, A_WIDTH), conv_w)


def _split_bf16(x):
    hi = x.astype(BF16)
    return hi, (x - hi.astype(F32)).astype(BF16)


def _filter_kernel(z_ref, w1_ref, b1_ref, w2_ref, b2_ref, w3_ref, b3_ref, fr_ref,
                   w4f_top_ref, w4f_bot_ref, w4b_top_ref, w4b_bot_ref, cond_ref, aw_ref, ab_ref,
                   sum_ref, diff_ref, mod_ref, hid_ref, *, tc):
    o = pl.program_id(0)
    j = pl.program_id(1)

    cond = cond_ref[...]
    mod_ref[...] = _dot((cond * jax.nn.sigmoid(cond)).astype(BF16), aw_ref[...].astype(BF16)) + ab_ref[...]

    @pl.when((o == 0) & (j == 0))
    def _():
        fr = fr_ref[...]
        h = jnp.sin(fr * (_dot_f32(z_ref[...], w1_ref[...]) + b1_ref[...]))
        h = jnp.sin(fr * (_dot_f32(h, w2_ref[...]) + b2_ref[...]))
        h = jnp.sin(fr * (_dot_f32(h, w3_ref[...]) + b3_ref[...]))
        hi, lo = _split_bf16(h)
        hid_ref[...] = jnp.concatenate([hi, lo, hi], axis=1)

    def project(top_ref, bot_ref):
        halves = []
        for w_ref in (top_ref, bot_ref):
            w_hi, w_lo = _split_bf16(w_ref[...])
            halves.append(_dot(hid_ref[...], jnp.concatenate([w_hi, w_hi, w_lo], axis=0)))
        return jnp.concatenate(halves, axis=0)

    rows = lax.broadcasted_iota(jnp.int32, (SEQ, tc), 0)
    chan = lax.broadcasted_iota(jnp.int32, (SEQ, tc), 1) + j * tc
    lag = jnp.where(rows < HALF, 2 * rows, 2 * rows - (SEQ - 1))
    t = lag.astype(F32) * (1.0 / (SEQ - 1))
    max_decay = math.log(HY_DECAY_TARGET) / HY_FAST_PCT
    min_decay = math.log(HY_DECAY_TARGET) / HY_SLOW_PCT
    deltas = jnp.abs(min_decay + chan.astype(F32) * ((max_decay - min_decay) / (D_MODEL - 1)))
    window = jnp.exp(-t * deltas)
    fw = project(w4f_top_ref, w4f_bot_ref) * window
    bw = project(w4b_top_ref, w4b_bot_ref) * window
    first = rows == 0
    a = fw + jnp.where(first, bw, 0.0)
    bb = jnp.where(first, 0.0, bw)
    inv = 1.0 / jnp.sum(jnp.abs(a) + jnp.abs(bb), axis=0, keepdims=True)
    sum_ref[...] = ((a + bb) * inv).astype(sum_ref.dtype)
    diff_ref[...] = ((a - bb) * inv).astype(diff_ref.dtype)


@functools.lru_cache(maxsize=None)
def _filter_positions():
    l = SEQ
    t = np.linspace(0.0, 1.0, l, dtype=np.float32)
    w = (2.0 * math.pi * np.arange(l, dtype=np.float32) / l).astype(np.float32)
    bands = np.linspace(1e-4, HY_BANDS - 1, HY_BANDS, dtype=np.float32)
    ang = w[:, None] * bands[None, :]
    z = np.concatenate([t[:, None], np.cos(ang), -np.sin(ang)], axis=-1).astype(np.float32)
    z = np.concatenate([z[0::2], z[1::2]], axis=0)
    z = np.pad(z, ((0, 0), (0, LANES // 2 - HY_EMB)))
    return np.concatenate([z[:l // 2], z[l // 2:]], axis=1)


def _filter_taps_and_modulation(fw1, fb1, fw2, fb2, fw3, fb3, fw4, freq, cond, ada_w, ada_b):
    l = SEQ
    tc = 256
    nj = D_MODEL // tc
    depth, d, n_mod = ada_w.shape
    assert depth == 2 and n_mod % (nj * LANES) == 0
    ta = n_mod // nj
    hw = LANES // 2

    def pad(a, rows, cols):
        return jnp.pad(a, ((0, rows - a.shape[0]), (0, cols - a.shape[1])))

    def twice(a):
        a = pad(a, hw, hw)
        zero = jnp.zeros_like(a)
        return jnp.concatenate([jnp.concatenate([a, zero], axis=1), jnp.concatenate([zero, a], axis=1)], axis=0)

    def row2(v):
        v = pad(v[None, :], 1, hw)
        return jnp.concatenate([v, v], axis=1)

    def small(shape):
        return pl.BlockSpec(shape, lambda o, j: (0, 0))

    w4_top = pad(fw4, LANES, fw4.shape[1])
    w4_bot = jnp.concatenate([jnp.zeros((hw, fw4.shape[1]), F32), pad(fw4, hw, fw4.shape[1])], axis=0)
    out_sds = jax.ShapeDtypeStruct((l, 2 * D_MODEL), BF16)
    out_spec = pl.BlockSpec((l, tc), lambda o, j: (0, o * nj + j))
    side_f = pl.BlockSpec((LANES, tc), lambda o, j: (0, 2 * o * nj + j))
    side_b = pl.BlockSpec((LANES, tc), lambda o, j: (0, (2 * o + 1) * nj + j))
    sq, vec = small((LANES, LANES)), small((1, LANES))
    return pl.pallas_call(
        functools.partial(_filter_kernel, tc=tc),
        grid=(2, nj),
        in_specs=[small((l // 2, LANES)), sq, vec, sq, vec, sq, vec, vec, side_f, side_f, side_b, side_b,
                  small((8, d)),
                  pl.BlockSpec((None, d, ta), lambda o, j: (o, 0, j)),
                  pl.BlockSpec((None, 1, ta), lambda o, j: (o, 0, j))],
        out_specs=(out_spec, out_spec, pl.BlockSpec((None, 8, ta), lambda o, j: (o, 0, j))),
        out_shape=(out_sds, out_sds, jax.ShapeDtypeStruct((depth, 8, n_mod), F32)),
        scratch_shapes=[pltpu.VMEM((l // 2, 3 * LANES), BF16)],
        compiler_params=_params("arbitrary", "arbitrary"),
        name="hyena_filter",
    )(jnp.asarray(_filter_positions()), twice(fw1), row2(fb1), twice(fw2), row2(fb2), twice(fw3), row2(fb3),
      row2(freq), w4_top, w4_bot, w4_top, w4_bot, cond, ada_w, ada_b.reshape(depth, 1, n_mod))


def _trig_table(a_rows, b_cols, scale):
    n = (a_rows[:, None].astype(np.int64) * b_cols[None, :].astype(np.int64)) % (4 * SEQ)
    ang = n.astype(np.float64) * (math.pi / (2 * SEQ))
    return (np.cos(ang) * scale).astype(np.float32), (np.sin(ang) * scale).astype(np.float32)


@functools.lru_cache(maxsize=None)
def _dft_tables():
    idx = np.arange(QUARTER)
    ana_c, ana_s, syn_c, syn_s = [], [], [], []
    for r in range(4):
        c, s = _trig_table(2 * idx + 1, 4 * idx + r, 1.0)
        ana_c.append(c)
        ana_s.append(s)
        c, s = _trig_table(4 * idx + r, 2 * idx + 1, 1.0 / SEQ)
        syn_c.append(c)
        syn_s.append(s)
    fmap = np.concatenate([idx, HALF + idx])
    lag = np.arange(HALF)
    cee, see = _trig_table(2 * fmap + 1, 2 * lag, 1.0)
    ceo, seo = _trig_table(2 * fmap + 1, 2 * lag + 1, 1.0)
    return np.stack(ana_c + ana_s), np.stack(syn_c + syn_s), (cee, ceo, see, seo)


def _filter_dft_kernel(cee_ref, ceo_ref, see_ref, seo_ref, fs_ref, fd_ref, p_ref, pm_ref, q_ref, qm_ref):
    pe = _dot(cee_ref[...], fs_ref[0:HALF, :])
    po = _dot(ceo_ref[...], fs_ref[HALF:SEQ, :])
    qe = _dot(see_ref[...], fd_ref[0:HALF, :])
    qo = _dot(seo_ref[...], fd_ref[HALF:SEQ, :])
    p_ref[...] = pe + po
    pm_ref[...] = pe - po
    q_ref[...] = qe + qo
    qm_ref[...] = qo - qe


def _filter_dft(tables, taps_sum, taps_diff):
    tm, tn = HALF, 256
    n_cols = taps_sum.shape[1]
    mat = pl.BlockSpec((tm, HALF), lambda j, i: (i, 0), pipeline_mode=pl.Buffered(1))
    tap = pl.BlockSpec((SEQ, tn), lambda j, i: (0, j))
    out_spec = pl.BlockSpec((tm, tn), lambda j, i: (i, j))
    out_sds = jax.ShapeDtypeStruct((HALF, n_cols), F32)
    return pl.pallas_call(
        _filter_dft_kernel,
        grid=(n_cols // tn, HALF // tm),
        in_specs=[mat, mat, mat, mat, tap, tap],
        out_specs=(out_spec,) * 4,
        out_shape=(out_sds,) * 4,
        compiler_params=_params("arbitrary", "arbitrary"),
        name="filter_dft",
    )(*tables, taps_sum, taps_diff)


def _dft4_fwd_kernel(tab_ref, z_ref, p_ref, pm_ref, q_ref, qm_ref, x_ref, zs_ref, *, slab):
    panels = zs_ref.shape[0]
    for k in range(panels):
        zs_ref[k] = z_ref[:, k * LANES:(k + 1) * LANES].astype(F32)
    zr = [jnp.concatenate([zs_ref[k, pl.ds(r, QUARTER, stride=4), :] for k in range(panels)],
                          axis=1).astype(BF16) for r in range(4)]

    for f0 in range(0, QUARTER, slab):
        rows = slice(f0, f0 + slab)
        mirror = slice(QUARTER + f0, QUARTER + f0 + slab)
        c = [_dot(tab_ref[r, rows, :], zr[r]) for r in range(4)]
        s = [_dot(tab_ref[4 + r, rows, :], zr[r]) for r in range(4)]
        e, o, em, om = c[0] + c[2], c[1] + c[3], c[0] - c[2], c[1] - c[3]
        se, so, sem, som = s[0] + s[2], s[1] + s[3], s[0] - s[2], s[1] - s[3]
        a = (e + o, em + som, em - som, e - o)
        b = (se + so, om - sem, sem + om, so - se)
        pk = (p_ref[rows, :], pm_ref[mirror, :], p_ref[mirror, :], pm_ref[rows, :])
        qk = (q_ref[rows, :], qm_ref[mirror, :], q_ref[mirror, :], qm_ref[rows, :])
        u = [a[k] * pk[k] - b[k] * qk[k] for k in range(4)]
        v = [a[k] * qk[k] + b[k] * pk[k] for k in range(4)]
        up, um, wp, wm = u[0] + u[3], u[0] - u[3], u[1] + u[2], u[1] - u[2]
        vp, vm, yp, ym = v[0] + v[3], v[0] - v[3], v[1] + v[2], v[1] - v[2]
        outs = (up + wp, um + yp, up - wp, um - yp,
                vm - ym, vp + wm, vm + ym, vp - wm)
        for n, val in enumerate(outs):
            x_ref[n, rows, :] = val.astype(x_ref.dtype)


def _dft4_fwd(tab, z_arr, z_col0, spectra, order):
    tn, slab = 512, 256
    nj = D_MODEL // tn
    zc = z_col0 // tn
    spec = pl.BlockSpec((HALF, tn), lambda j, b: (0, order * nj + j))
    return pl.pallas_call(
        functools.partial(_dft4_fwd_kernel, slab=slab),
        grid=(nj, BATCH),
        in_specs=[pl.BlockSpec(tab.shape, lambda j, b: (0, 0, 0), pipeline_mode=pl.Buffered(1)),
                  pl.BlockSpec((SEQ, tn), lambda j, b: (b, zc + j)), spec, spec, spec, spec],
        out_specs=pl.BlockSpec((8, QUARTER, tn), lambda j, b: (0, b, j)),
        out_shape=jax.ShapeDtypeStruct((8, BATCH * QUARTER, D_MODEL), BF16),
        scratch_shapes=[pltpu.VMEM((tn // LANES, SEQ, LANES), F32)],
        compiler_params=_params("arbitrary", "arbitrary"),
        name="dft_analysis",
    )(tab, z_arr, *spectra)


def _dft4_inv_kernel(tab_ref, x_ref, z_ref, gate_ref, skip_ref, o_ref, ys_ref):
    panels = ys_ref.shape[0]
    for r in range(4):
        y_r = _dot(tab_ref[r], x_ref[r]) + _dot(tab_ref[4 + r], x_ref[4 + r])
        for k in range(panels):
            ys_ref[k, pl.ds(r, QUARTER, stride=4), :] = y_r[:, k * LANES:(k + 1) * LANES]
    y = jnp.concatenate([ys_ref[k] for k in range(panels)], axis=1)
    y = y + z_ref[...].astype(F32) * skip_ref[...]
    o_ref[...] = (gate_ref[...].astype(F32) * y).astype(o_ref.dtype)


def _dft4_inv(tab, x, z_arr, z_col0, gate_arr, gate_col0, skip):
    tn = 512
    nj = D_MODEL // tn
    zc, gc = z_col0 // tn, gate_col0 // tn
    return pl.pallas_call(
        _dft4_inv_kernel,
        grid=(nj, BATCH),
        in_specs=[pl.BlockSpec(tab.shape, lambda j, b: (0, 0, 0), pipeline_mode=pl.Buffered(1)),
                  pl.BlockSpec((8, QUARTER, tn), lambda j, b: (0, b, j)),
                  pl.BlockSpec((SEQ, tn), lambda j, b: (b, zc + j)),
                  pl.BlockSpec((SEQ, tn), lambda j, b: (b, gc + j)),
                  pl.BlockSpec((1, tn), lambda j, b: (0, j))],
        out_specs=pl.BlockSpec((SEQ, tn), lambda j, b: (b, j)),
        out_shape=jax.ShapeDtypeStruct((TOKENS, D_MODEL), BF16),
        scratch_shapes=[pltpu.VMEM((tn // LANES, SEQ, LANES), F32)],
        compiler_params=_params("arbitrary", "arbitrary"),
        name="dft_synthesis",
    )(tab, x, z_arr, gate_arr, skip.reshape(1, D_MODEL))


def _mod_chunk(mod, l, k):
    return mod[l, :BATCH, k * D_MODEL:(k + 1) * D_MODEL]


def _mlp_block(x2d, h, mod, l, w1, w2):
    hid = _mm_plain(h, w1, D_FF, BF16, _epi_relu2, 1024, 1024, "mlp_up", w_layer=l)
    return _mm_residual([hid], w2, l, x2d, _mod_chunk(mod, l, 5), 512, 512, "mlp_down", w_single_buffer=True)


def kernel(x, c, ctx, c_ctx, ada_w, ada_b, norm_g, lb_logits, ab_w_in, ab_conv_w, ab_gnorm_g, ab_w_out,
           hy_in_w, hy_short_w, hy_out_w, hy_fw1, hy_fb1, hy_fw2, hy_fb2, hy_fw3, hy_fb3, hy_fw4, hy_freq,
           hy_skip, mlp_w1, mlp_w2, final_g):
    d = D_MODEL
    cond = jnp.concatenate([c, c_ctx[None, :], jnp.zeros((3, d), F32)], axis=0)
    taps_sum, taps_diff, mod = _filter_taps_and_modulation(
        hy_fw1[0], hy_fb1[0], hy_fw2[0], hy_fb2[0], hy_fw3[0], hy_fb3[0], hy_fw4[0], hy_freq[0],
        cond, ada_w, ada_b)
    x2d = x.reshape(TOKENS, d)

    sh1, sc1, g1 = (mod[0, :BATCH, k * d:(k + 1) * d] for k in (0, 1, 2))
    h = _prep(x, norm_g[0, 0], sc1, sh1, BF16, 1024).reshape(TOKENS, d)
    csh = jnp.broadcast_to(mod[0, BATCH:BATCH + 1, 0:d], (BATCH, d))
    csc = jnp.broadcast_to(mod[0, BATCH:BATCH + 1, d:2 * d], (BATCH, d))
    hc = _prep(ctx, norm_g[0, 0], csc, csh, BF16, CTX_LEN).reshape(BATCH * CTX_LEN, d)
    p = _mm_plain(h, ab_w_in, AB_IN_WIDTH, F32, _epi_store, 1024, 1024, "ab_in_proj", w_layer=0)
    cp = _mm_plain(hc, ab_w_in, 3 * A_WIDTH, F32, _epi_store, 1024, 1024, "ab_ctx_proj", w_layer=0)
    mix_a, mix_b = _even_mixer(p, cp, lb_logits, ab_gnorm_g[0], ab_conv_w[0], 0)
    x2d, h = _mm_residual_norm([mix_a, mix_b], ab_w_out, 0, x2d, g1, norm_g[0, 1], _mod_chunk(mod, 0, 4),
                               _mod_chunk(mod, 0, 3), 512, "ab_out_proj")
    x2d = _mlp_block(x2d, h, mod, 0, mlp_w1, mlp_w2)

    sh1, sc1, g1 = (mod[1, :BATCH, k * d:(k + 1) * d] for k in (0, 1, 2))
    h = _prep(x2d.reshape(BATCH, SEQ, d), norm_g[1, 0], sc1, sh1, BF16, 1024).reshape(TOKENS, d)
    pc = _mm([h], hy_in_w, w_layer=0, tm=1024, tn=1024, n_cols=3 * d, epi=_epi_conv3_grid,
             extras=(hy_short_w[0],), extra_specs=(pl.BlockSpec((3, 1024), lambda j, i: (0, j)),),
             out_shapes=jax.ShapeDtypeStruct((TOKENS, 3 * d), BF16),
             out_specs=pl.BlockSpec((1024, 1024), lambda j, i: (i, j)), name="hy_in_proj")
    analysis, synthesis, filter_tables = jax.tree.map(lambda t: jnp.asarray(t).astype(BF16), _dft_tables())
    spectra = _filter_dft(filter_tables, taps_sum, taps_diff)
    coef = _dft4_fwd(analysis, pc, 2 * d, spectra, 0)
    z = _dft4_inv(synthesis, coef, pc, 2 * d, pc, 0, hy_skip[0, 0])
    coef = _dft4_fwd(analysis, z, 0, spectra, 1)
    z = _dft4_inv(synthesis, coef, z, 0, pc, d, hy_skip[0, 1])
    x2d, h = _mm_residual_norm([z], hy_out_w, 0, x2d, g1, norm_g[1, 1], _mod_chunk(mod, 1, 4),
                               _mod_chunk(mod, 1, 3), 512, "hy_out_proj")
    x2d = _mlp_block(x2d, h, mod, 1, mlp_w1, mlp_w2)

    zeros = jnp.zeros((BATCH, d), F32)
    return _prep(x2d.reshape(BATCH, SEQ, d), final_g, zeros, zeros, F32, 1024)
```

```python
import functools
import math

import numpy as np
import jax
import jax.numpy as jnp
from jax import lax
from jax.experimental import pallas as pl
from jax.experimental.pallas import tpu as pltpu

F32 = jnp.float32
BF16 = jnp.bfloat16

D_MODEL = 2048
BATCH = 4
SEQ = 2048
CTX_LEN = 256
GRID_W = 64
EPS = 1e-6
A_HEAD_DIM = 128
A_WIDTH = D_MODEL // 2
A_HEADS = A_WIDTH // A_HEAD_DIM
B_WIDTH = D_MODEL - A_WIDTH
AB_IN_WIDTH = 5 * A_WIDTH + 3 * B_WIDTH
HY_EMB = 33
HY_BANDS = (HY_EMB - 1) // 2
HY_DECAY_TARGET = 1e-2
HY_FAST_PCT = 0.3
HY_SLOW_PCT = 1.5
D_FF = 4 * D_MODEL
TOKENS = BATCH * SEQ
HALF = SEQ // 2
QUARTER = SEQ // 4

SCAN_CHUNK = 64
SCAN_LEVELS = (1, 2, 4, 8, 16, 32)

SUBLANES = 8
LANES = 128

VMEM_LIMIT_BYTES = 56 * 1024 * 1024


def _params(*sem):
    return pltpu.CompilerParams(dimension_semantics=sem, vmem_limit_bytes=VMEM_LIMIT_BYTES)


def _dot(a, b):
    return jnp.dot(a, b, preferred_element_type=F32)


def _dot_nt(a, b):
    return lax.dot_general(a, b, (((1,), (1,)), ((), ())), preferred_element_type=F32)


def _dot_tn(a, b):
    return lax.dot_general(a, b, (((0,), (0,)), ((), ())), preferred_element_type=F32)


def _dot_f32(a, b):
    return jnp.dot(a, b, precision=lax.Precision.HIGHEST, preferred_element_type=F32)


def _conv3_rows(t, w, group):
    rows = lax.broadcasted_iota(jnp.int32, t.shape, 0) % group
    prev = jnp.where(rows == 0, 0.0, pltpu.roll(t, 1, 0))
    nxt = jnp.where(rows == group - 1, 0.0, pltpu.roll(t, t.shape[0] - 1, 0))
    return w[0:1] * prev + w[1:2] * t + w[2:3] * nxt


def _prep_kernel(x_ref, g_ref, sc_ref, sh_ref, o_ref):
    x = x_ref[...]
    ms = jnp.mean(x * x, axis=-1, keepdims=True)
    xn = x * lax.rsqrt(ms + EPS) * g_ref[...]
    o_ref[...] = (xn * (1.0 + sc_ref[...]) + sh_ref[...]).astype(o_ref.dtype)


def _prep(x, g, sc, sh, out_dtype, ts):
    b, l, d = x.shape
    return pl.pallas_call(
        _prep_kernel,
        grid=(b, l // ts),
        in_specs=[
            pl.BlockSpec((None, ts, d), lambda i, j: (i, j, 0)),
            pl.BlockSpec((1, d), lambda i, j: (0, 0)),
            pl.BlockSpec((None, 1, d), lambda i, j: (i, 0, 0)),
            pl.BlockSpec((None, 1, d), lambda i, j: (i, 0, 0)),
        ],
        out_specs=pl.BlockSpec((None, ts, d), lambda i, j: (i, j, 0)),
        out_shape=jax.ShapeDtypeStruct((b, l, d), out_dtype),
        compiler_params=_params("arbitrary", "arbitrary"),
        name="norm_mod",
    )(x, g.reshape(1, d), sc.reshape(b, 1, d), sh.reshape(b, 1, d))


def _mm_kernel(*refs, n_a, k_sizes, epi, cast_w):
    a_refs = refs[:n_a]
    w_ref = refs[n_a]
    rest = refs[n_a + 1:]
    if cast_w:
        w_ref, rest = rest[-1], rest[:-1]

        @pl.when(pl.program_id(1) == 0)
        def _():
            w_ref[...] = refs[n_a][...].astype(BF16)

    acc = None
    off = 0
    for a_ref, k in zip(a_refs, k_sizes):
        part = _dot(a_ref[...], w_ref[off:off + k, :])
        acc = part if acc is None else acc + part
        off += k
    epi(acc, *rest)


def _mm(a_list, w, *, tm, tn, n_cols, epi, extras, extra_specs, out_shapes, out_specs, name,
        w_layer=None, w_single_buffer=False):
    m = a_list[0].shape[0]
    k_sizes = tuple(a.shape[1] for a in a_list)
    k_total = sum(k_sizes)
    cast_w = w_layer is not None
    assert w.shape[-2] == k_total and m % tm == 0 and n_cols % tn == 0
    in_specs = [pl.BlockSpec((tm, k), lambda j, i: (i, 0)) for k in k_sizes]
    w_mode = dict(pipeline_mode=pl.Buffered(1)) if w_single_buffer else {}
    if cast_w:
        in_specs.append(pl.BlockSpec((None, k_total, tn), lambda j, i: (w_layer, 0, j), **w_mode))
    else:
        in_specs.append(pl.BlockSpec((k_total, tn), lambda j, i: (0, j), **w_mode))
    in_specs.extend(extra_specs)
    return pl.pallas_call(
        functools.partial(_mm_kernel, n_a=len(a_list), k_sizes=k_sizes, epi=epi, cast_w=cast_w),
        grid=(n_cols // tn, m // tm),
        in_specs=in_specs,
        out_specs=out_specs,
        out_shape=out_shapes,
        scratch_shapes=[pltpu.VMEM((k_total, tn), BF16)] if cast_w else [],
        compiler_params=_params("arbitrary", "arbitrary"),
        name=name,
    )(*a_list, w, *extras)


def _epi_store(acc, o_ref):
    o_ref[...] = acc.astype(o_ref.dtype)


def _epi_relu2(acc, o_ref):
    r = jnp.maximum(acc, 0.0)
    o_ref[...] = (r * r).astype(o_ref.dtype)


def _epi_residual(acc, x_ref, gate_ref, o_ref):
    o_ref[...] = x_ref[...] + gate_ref[...] * acc


def _epi_residual_norm(acc, x_ref, gate_ref, g_ref, sc_ref, sh_ref, o_ref, h_ref):
    x = x_ref[...] + gate_ref[...] * acc
    o_ref[...] = x
    ms = jnp.mean(x * x, axis=-1, keepdims=True)
    xn = x * lax.rsqrt(ms + EPS) * g_ref[...]
    h_ref[...] = (xn * (1.0 + sc_ref[...]) + sh_ref[...]).astype(h_ref.dtype)


def _epi_conv3_grid(acc, w_ref, o_ref):
    o_ref[...] = _conv3_rows(acc, w_ref[...], GRID_W).astype(o_ref.dtype)


def _mm_plain(a, w, n_cols, out_dtype, epi, tm, tn, name, w_layer=None):
    m = a.shape[0]
    return _mm([a], w, tm=tm, tn=tn, n_cols=n_cols, epi=epi, extras=(), extra_specs=(),
               out_shapes=jax.ShapeDtypeStruct((m, n_cols), out_dtype),
               out_specs=pl.BlockSpec((tm, tn), lambda j, i: (i, j)), name=name, w_layer=w_layer)


def _mm_residual(a_list, w, w_layer, x, gate, tm, tn, name, w_single_buffer=False):
    m, n = x.shape
    tiles_per_batch = SEQ // tm
    return _mm(a_list, w, tm=tm, tn=tn, n_cols=n, epi=_epi_residual,
               extras=(x, gate.reshape(BATCH, 1, n)),
               extra_specs=(pl.BlockSpec((tm, tn), lambda j, i: (i, j)),
                            pl.BlockSpec((None, 1, tn), lambda j, i: (i // tiles_per_batch, 0, j))),
               out_shapes=jax.ShapeDtypeStruct((m, n), F32),
               out_specs=pl.BlockSpec((tm, tn), lambda j, i: (i, j)), name=name,
               w_layer=w_layer, w_single_buffer=w_single_buffer)


def _mm_residual_norm(a_list, w, w_layer, x, gate, norm_g, sc, sh, tm, name):
    m, n = x.shape
    tiles_per_batch = SEQ // tm

    def per_batch(v):
        return v.reshape(BATCH, 1, n), pl.BlockSpec((None, 1, n), lambda j, i: (i // tiles_per_batch, 0, 0))

    (gate3, gate_spec), (sc3, sc_spec), (sh3, sh_spec) = per_batch(gate), per_batch(sc), per_batch(sh)
    rows = pl.BlockSpec((tm, n), lambda j, i: (i, 0))
    return _mm(a_list, w, tm=tm, tn=n, n_cols=n, epi=_epi_residual_norm,
               extras=(x, gate3, norm_g.reshape(1, n), sc3, sh3),
               extra_specs=(rows, gate_spec, pl.BlockSpec((1, n), lambda j, i: (0, 0)), sc_spec, sh_spec),
               out_shapes=(jax.ShapeDtypeStruct((m, n), F32), jax.ShapeDtypeStruct((m, n), BF16)),
               out_specs=(rows, rows), name=name, w_layer=w_layer, w_single_buffer=True)


def _row_groups(x):
    return [x[k:k + SUBLANES] for k in range(0, x.shape[0], SUBLANES)]


def _cumsum_groups(groups, sub):
    out, carry = [], None
    for g in groups:
        for s in (1, 2, 4):
            g = g + jnp.where(sub >= s, pltpu.roll(g, s, 0), 0.0)
        if carry is not None:
            g = g + carry
        out.append(g)
        carry = g[SUBLANES - 1:SUBLANES]
    return out


def _hold_boundary(cum, w, k, sub):
    shape = cum[k].shape
    if w >= SUBLANES // 2:
        r = (k * SUBLANES // (2 * w)) * 2 * w + w - 1
        return jnp.broadcast_to(cum[r // SUBLANES][r % SUBLANES:r % SUBLANES + 1], shape)
    assert w == 2
    return jnp.where(sub < 4, jnp.broadcast_to(cum[k][1:2], shape), jnp.broadcast_to(cum[k][5:6], shape))


def _scan_kernel(zf_ref, zb_ref, v_ref, q_ref, g_ref, u_ref, gb_ref, gc_ref, czf_ref, czb_ref, cv_ref,
                 lbl_ref, gn_ref, cw_ref, o_ref, oc_ref,
                 of_ref, ob_ref, st_ref, qs_ref, inc_ref, dec_ref, kl_ref, lv_ref, att_ref, *, layer):
    c = SCAN_CHUNK
    n_chunks = SEQ // c
    n_ctx = CTX_LEN // c
    z_refs = (zf_ref, zb_ref)
    cz_refs = (czf_ref, czb_ref)
    out_refs = (of_ref, ob_ref)

    logit_rows = [lbl_ref[:, k, :] for k in range(lbl_ref.shape[1])]
    top = functools.reduce(jnp.maximum, logit_rows)
    exps = [jnp.exp(r - top) for r in logit_rows]
    lbs = sum(exps[:layer + 1]) / sum(exps)

    sub = lax.broadcasted_iota(jnp.int32, (SUBLANES, A_HEAD_DIM), 0)
    n_groups = c // SUBLANES

    def rows(groups):
        return jnp.concatenate(groups, axis=0)

    def chunk(d, z, q, v, want_out):
        lb = lbs[d:d + 1]
        f = lb + (1.0 - lb) * jax.nn.sigmoid(z)
        kk = 1.0 - f
        fg, kg = _row_groups(f), _row_groups(kk)
        lg = [jnp.log2(g) for g in fg]
        cum = _cumsum_groups(lg, sub)
        total = cum[-1][SUBLANES - 1:SUBLANES]
        if d == 0:
            pos = cum
            q_in = [jnp.exp2(g) for g in cum]
            k_out = [jnp.exp2(total - g) for g in cum]
        else:
            pos = [a - b for a, b in zip(cum, lg)]
            q_in = [jnp.exp2(total - g) for g in pos]
            k_out = [jnp.exp2(g) for g in pos]
        k_leave = rows([a * b for a, b in zip(kg, k_out)]).astype(BF16)
        decay = jnp.exp2(total)
        if not want_out:
            return k_leave, decay, None, None, None
        qg = _row_groups(q)
        o_diag = jnp.sum(q * kk, axis=1, keepdims=True) * v
        upper, lower = (qg, kg) if d == 0 else (kg, qg)
        operands = []
        for w in SCAN_LEVELS:
            y = []
            for k in range(n_groups):
                if w == 1:
                    odd = (sub & 1) != 0
                    y.append(jnp.where(odd, qg[k] * fg[k], kg[k]) if d == 0
                             else jnp.where(odd, kg[k], qg[k] * fg[k]))
                    continue
                hold = _hold_boundary(cum, w, k, sub)
                if w >= SUBLANES:
                    bit_set = (k * SUBLANES) & w != 0
                    y.append(upper[k] * jnp.exp2(pos[k] - hold) if bit_set
                             else lower[k] * jnp.exp2(hold - pos[k]))
                else:
                    bit = (sub & w) != 0
                    sign = jnp.where(bit, 1.0, -1.0)
                    y.append(jnp.where(bit, upper[k], lower[k]) * jnp.exp2((pos[k] - hold) * sign))
            operands.append(rows(y).astype(BF16))
        qs = rows([a * b for a, b in zip(qg, q_in)]).astype(BF16)
        return k_leave, decay, qs, o_diag, operands

    def within_chunk(d, operands):
        t_idx = lax.broadcasted_iota(jnp.int32, (c, c), 0)
        s_idx = lax.broadcasted_iota(jnp.int32, (c, c), 1)
        split = t_idx ^ s_idx
        att = None
        for w, yb in reversed(list(zip(SCAN_LEVELS, operands))):
            scores = _dot_nt(yb, yb)
            att = scores if att is None else jnp.where(split < 2 * w, scores, att)
        return jnp.where(t_idx > s_idx if d == 0 else t_idx < s_idx, att, 0.0)

    st_ref[...] = jnp.zeros_like(st_ref)

    def ctx_body(i, carry):
        for d in range(2):
            ci = i if d == 0 else n_ctx - 1 - i
            rows_i = pl.ds(pl.multiple_of(ci * c, c), c)
            k_leave, decay, _, _, _ = chunk(d, cz_refs[d][rows_i, :], None, None, False)
            st_ref[d] = st_ref[d] * decay + _dot_tn(cv_ref[rows_i, :].astype(BF16), k_leave)
        return carry

    lax.fori_loop(0, n_ctx, ctx_body, 0, unroll=2)

    def chunk_rows(ci):
        start = ci * c
        return pl.ds(start if isinstance(ci, int) else pl.multiple_of(start, c), c)

    def chunk_of(d, t):
        return t if d == 0 else n_chunks - 1 - t

    def prepare(t):
        for d in range(2):
            ci = chunk_of(d, t)
            rows_i = chunk_rows(ci)
            k_leave, decay, qs, o_diag, operands = chunk(d, z_refs[d][rows_i, :], q_ref[rows_i, :],
                                                         v_ref[rows_i, :], True)
            kl_ref[d, rows_i, :] = k_leave
            dec_ref[d, ci] = decay
            qs_ref[d, rows_i, :] = qs
            out_refs[d][rows_i, :] = o_diag
            for lvl, yb in enumerate(operands):
                lv_ref[d, lvl, rows_i, :] = yb

    def scores(t):
        for d in range(2):
            ci = chunk_of(d, t)
            rows_i = chunk_rows(ci)
            inc_ref[d, ci] = _dot_tn(v_ref[rows_i, :].astype(BF16), kl_ref[d, rows_i, :])
            operands = [lv_ref[d, lvl, rows_i, :] for lvl in range(len(SCAN_LEVELS))]
            att_ref[d, rows_i, :] = within_chunk(d, operands).astype(BF16)

    def carry_state(t):
        for d in range(2):
            ci = chunk_of(d, t)
            rows_i = chunk_rows(ci)
            state = st_ref[d]
            out_refs[d][rows_i, :] += (_dot(att_ref[d, rows_i, :], v_ref[rows_i, :].astype(BF16))
                                       + _dot_nt(qs_ref[d, rows_i, :], state.astype(BF16)))
            st_ref[d] = state * dec_ref[d, ci] + inc_ref[d, ci]

    pair = 2
    n_trips = n_chunks // pair

    def stage(fn, trip):
        for j in range(pair):
            fn(trip * pair + j)

    stage(prepare, 0)
    stage(scores, 0)
    stage(prepare, 1)

    def skewed_body(i, carry):
        stage(carry_state, i - 2)
        stage(scores, i - 1)
        stage(prepare, i)
        return carry

    lax.fori_loop(2, n_trips, skewed_body, 0)
    stage(carry_state, n_trips - 2)
    stage(scores, n_trips - 1)
    stage(carry_state, n_trips - 1)

    rb = 256
    gn = gn_ref[...]
    cw = cw_ref[...]

    def read_body(i, carry):
        rows_i = pl.ds(pl.multiple_of(i * rb, rb), rb)
        o = of_ref[rows_i, :] + ob_ref[rows_i, :]
        ms = jnp.mean(o * o, axis=-1, keepdims=True)
        on = o * lax.rsqrt(ms + EPS) * gn
        g = g_ref[rows_i, :]
        o_ref[rows_i, :] = (on * (g * jax.nn.sigmoid(g))).astype(o_ref.dtype)
        conv = _conv3_rows(gc_ref[rows_i, :] * u_ref[rows_i, :], cw, GRID_W)
        oc_ref[rows_i, :] = (gb_ref[rows_i, :] * conv).astype(oc_ref.dtype)
        return carry

    lax.fori_loop(0, SEQ // rb, read_body, 0)


def _even_mixer(p, cp, lb_logits, gnorm_g, conv_w, layer):
    h = A_HEADS
    hd = A_HEAD_DIM
    n_lb = lb_logits.shape[1]

    def col(k):
        return pl.BlockSpec((SEQ, hd), lambda b, j, k=k: (b, k * h + j))

    def ccol(k):
        return pl.BlockSpec((CTX_LEN, hd), lambda b, j, k=k: (b, k * h + j))

    out_spec = pl.BlockSpec((SEQ, hd), lambda b, j: (b, j))
    return pl.pallas_call(
        functools.partial(_scan_kernel, layer=layer),
        grid=(BATCH, h),
        in_specs=[col(0), col(1), col(2), col(3), col(4), col(5), col(6), col(7),
                  ccol(0), ccol(1), ccol(2),
                  pl.BlockSpec((2, n_lb, hd), lambda b, j: (0, 0, j)),
                  pl.BlockSpec((1, hd), lambda b, j: (0, j)),
                  pl.BlockSpec((3, hd), lambda b, j: (0, j))],
        out_specs=(out_spec, out_spec),
        out_shape=(jax.ShapeDtypeStruct((TOKENS, A_WIDTH), BF16),
                   jax.ShapeDtypeStruct((TOKENS, B_WIDTH), BF16)),
        scratch_shapes=[pltpu.VMEM((SEQ, hd), F32), pltpu.VMEM((SEQ, hd), F32),
                        pltpu.VMEM((2, hd, hd), F32), pltpu.VMEM((2, SEQ, hd), BF16),
                        pltpu.VMEM((2, SEQ // SCAN_CHUNK, hd, hd), F32),
                        pltpu.VMEM((2, SEQ // SCAN_CHUNK, 1, hd), F32),
                        pltpu.VMEM((2, SEQ, hd), BF16),
                        pltpu.VMEM((2, len(SCAN_LEVELS), SEQ, hd), BF16),
                        pltpu.VMEM((2, SEQ, SCAN_CHUNK), BF16)],
        compiler_params=_params("arbitrary", "arbitrary"),
        name="hgrn_scan",
    )(p, p, p, p, p, p, p, p, cp, cp, cp, lb_logits, gnorm_g.reshape(1, A_WIDTH), conv_w)


def _split_bf16(x):
    hi = x.astype(BF16)
    return hi, (x - hi.astype(F32)).astype(BF16)


def _filter_kernel(z_ref, w1_ref, b1_ref, w2_ref, b2_ref, w3_ref, b3_ref, fr_ref,
                   w4f_top_ref, w4f_bot_ref, w4b_top_ref, w4b_bot_ref, cond_ref, aw_ref, ab_ref,
                   sum_ref, diff_ref, mod_ref, hid_ref, *, tc):
    o = pl.program_id(0)
    j = pl.program_id(1)

    cond = cond_ref[...]
    mod_ref[...] = _dot((cond * jax.nn.sigmoid(cond)).astype(BF16), aw_ref[...].astype(BF16)) + ab_ref[...]

    @pl.when((o == 0) & (j == 0))
    def _():
        fr = fr_ref[...]
        h = jnp.sin(fr * (_dot_f32(z_ref[...], w1_ref[...]) + b1_ref[...]))
        h = jnp.sin(fr * (_dot_f32(h, w2_ref[...]) + b2_ref[...]))
        h = jnp.sin(fr * (_dot_f32(h, w3_ref[...]) + b3_ref[...]))
        hi, lo = _split_bf16(h)
        hid_ref[...] = jnp.concatenate([hi, lo, hi], axis=1)

    def project(top_ref, bot_ref):
        halves = []
        for w_ref in (top_ref, bot_ref):
            w_hi, w_lo = _split_bf16(w_ref[...])
            halves.append(_dot(hid_ref[...], jnp.concatenate([w_hi, w_hi, w_lo], axis=0)))
        return jnp.concatenate(halves, axis=0)

    rows = lax.broadcasted_iota(jnp.int32, (SEQ, tc), 0)
    chan = lax.broadcasted_iota(jnp.int32, (SEQ, tc), 1) + j * tc
    lag = jnp.where(rows < HALF, 2 * rows, 2 * rows - (SEQ - 1))
    t = lag.astype(F32) * (1.0 / (SEQ - 1))
    max_decay = math.log(HY_DECAY_TARGET) / HY_FAST_PCT
    min_decay = math.log(HY_DECAY_TARGET) / HY_SLOW_PCT
    deltas = jnp.abs(min_decay + chan.astype(F32) * ((max_decay - min_decay) / (D_MODEL - 1)))
    window = jnp.exp(-t * deltas)
    fw = project(w4f_top_ref, w4f_bot_ref) * window
    bw = project(w4b_top_ref, w4b_bot_ref) * window
    first = rows == 0
    a = fw + jnp.where(first, bw, 0.0)
    bb = jnp.where(first, 0.0, bw)
    inv = 1.0 / jnp.sum(jnp.abs(a) + jnp.abs(bb), axis=0, keepdims=True)
    sum_ref[...] = ((a + bb) * inv).astype(sum_ref.dtype)
    diff_ref[...] = ((a - bb) * inv).astype(diff_ref.dtype)


@functools.lru_cache(maxsize=None)
def _filter_positions():
    l = SEQ
    t = np.linspace(0.0, 1.0, l, dtype=np.float32)
    w = (2.0 * math.pi * np.arange(l, dtype=np.float32) / l).astype(np.float32)
    bands = np.linspace(1e-4, HY_BANDS - 1, HY_BANDS, dtype=np.float32)
    ang = w[:, None] * bands[None, :]
    z = np.concatenate([t[:, None], np.cos(ang), -np.sin(ang)], axis=-1).astype(np.float32)
    z = np.concatenate([z[0::2], z[1::2]], axis=0)
    z = np.pad(z, ((0, 0), (0, LANES // 2 - HY_EMB)))
    return np.concatenate([z[:l // 2], z[l // 2:]], axis=1)


def _filter_taps_and_modulation(fw1, fb1, fw2, fb2, fw3, fb3, fw4, freq, cond, ada_w, ada_b):
    l = SEQ
    tc = 256
    nj = D_MODEL // tc
    depth, d, n_mod = ada_w.shape
    assert depth == 2 and n_mod % (nj * LANES) == 0
    ta = n_mod // nj
    hw = LANES // 2

    def pad(a, rows, cols):
        return jnp.pad(a, ((0, rows - a.shape[0]), (0, cols - a.shape[1])))

    def twice(a):
        a = pad(a, hw, hw)
        zero = jnp.zeros_like(a)
        return jnp.concatenate([jnp.concatenate([a, zero], axis=1), jnp.concatenate([zero, a], axis=1)], axis=0)

    def row2(v):
        v = pad(v[None, :], 1, hw)
        return jnp.concatenate([v, v], axis=1)

    def small(shape):
        return pl.BlockSpec(shape, lambda o, j: (0, 0))

    w4_top = pad(fw4, LANES, fw4.shape[1])
    w4_bot = jnp.concatenate([jnp.zeros((hw, fw4.shape[1]), F32), pad(fw4, hw, fw4.shape[1])], axis=0)
    out_sds = jax.ShapeDtypeStruct((l, 2 * D_MODEL), BF16)
    out_spec = pl.BlockSpec((l, tc), lambda o, j: (0, o * nj + j))
    side_f = pl.BlockSpec((LANES, tc), lambda o, j: (0, 2 * o * nj + j))
    side_b = pl.BlockSpec((LANES, tc), lambda o, j: (0, (2 * o + 1) * nj + j))
    sq, vec = small((LANES, LANES)), small((1, LANES))
    return pl.pallas_call(
        functools.partial(_filter_kernel, tc=tc),
        grid=(2, nj),
        in_specs=[small((l // 2, LANES)), sq, vec, sq, vec, sq, vec, vec, side_f, side_f, side_b, side_b,
                  small((8, d)),
                  pl.BlockSpec((None, d, ta), lambda o, j: (o, 0, j)),
                  pl.BlockSpec((None, 1, ta), lambda o, j: (o, 0, j))],
        out_specs=(out_spec, out_spec, pl.BlockSpec((None, 8, ta), lambda o, j: (o, 0, j))),
        out_shape=(out_sds, out_sds, jax.ShapeDtypeStruct((depth, 8, n_mod), F32)),
        scratch_shapes=[pltpu.VMEM((l // 2, 3 * LANES), BF16)],
        compiler_params=_params("arbitrary", "arbitrary"),
        name="hyena_filter",
    )(jnp.asarray(_filter_positions()), twice(fw1), row2(fb1), twice(fw2), row2(fb2), twice(fw3), row2(fb3),
      row2(freq), w4_top, w4_bot, w4_top, w4_bot, cond, ada_w, ada_b.reshape(depth, 1, n_mod))


def _trig_table(a_rows, b_cols, scale):
    n = (a_rows[:, None].astype(np.int64) * b_cols[None, :].astype(np.int64)) % (4 * SEQ)
    ang = n.astype(np.float64) * (math.pi / (2 * SEQ))
    return (np.cos(ang) * scale).astype(np.float32), (np.sin(ang) * scale).astype(np.float32)


@functools.lru_cache(maxsize=None)
def _dft_tables():
    idx = np.arange(QUARTER)
    ana_c, ana_s, syn_c, syn_s = [], [], [], []
    for r in range(4):
        c, s = _trig_table(2 * idx + 1, 4 * idx + r, 1.0)
        ana_c.append(c)
        ana_s.append(s)
        c, s = _trig_table(4 * idx + r, 2 * idx + 1, 1.0 / SEQ)
        syn_c.append(c)
        syn_s.append(s)
    fmap = np.concatenate([idx, HALF + idx])
    lag = np.arange(HALF)
    cee, see = _trig_table(2 * fmap + 1, 2 * lag, 1.0)
    ceo, seo = _trig_table(2 * fmap + 1, 2 * lag + 1, 1.0)
    return np.stack(ana_c + ana_s), np.stack(syn_c + syn_s), (cee, ceo, see, seo)


def _filter_dft_kernel(cee_ref, ceo_ref, see_ref, seo_ref, fs_ref, fd_ref, p_ref, pm_ref, q_ref, qm_ref):
    pe = _dot(cee_ref[...], fs_ref[0:HALF, :])
    po = _dot(ceo_ref[...], fs_ref[HALF:SEQ, :])
    qe = _dot(see_ref[...], fd_ref[0:HALF, :])
    qo = _dot(seo_ref[...], fd_ref[HALF:SEQ, :])
    p_ref[...] = pe + po
    pm_ref[...] = pe - po
    q_ref[...] = qe + qo
    qm_ref[...] = qo - qe


def _filter_dft(tables, taps_sum, taps_diff):
    tm, tn = HALF, 256
    n_cols = taps_sum.shape[1]
    mat = pl.BlockSpec((tm, HALF), lambda j, i: (i, 0), pipeline_mode=pl.Buffered(1))
    tap = pl.BlockSpec((SEQ, tn), lambda j, i: (0, j))
    out_spec = pl.BlockSpec((tm, tn), lambda j, i: (i, j))
    out_sds = jax.ShapeDtypeStruct((HALF, n_cols), F32)
    return pl.pallas_call(
        _filter_dft_kernel,
        grid=(n_cols // tn, HALF // tm),
        in_specs=[mat, mat, mat, mat, tap, tap],
        out_specs=(out_spec,) * 4,
        out_shape=(out_sds,) * 4,
        compiler_params=_params("arbitrary", "arbitrary"),
        name="filter_dft",
    )(*tables, taps_sum, taps_diff)


def _dft4_fwd_kernel(tab_ref, z_ref, p_ref, pm_ref, q_ref, qm_ref, x_ref, zs_ref, *, slab):
    panels = zs_ref.shape[0]
    for k in range(panels):
        zs_ref[k] = z_ref[:, k * LANES:(k + 1) * LANES].astype(F32)
    zr = [jnp.concatenate([zs_ref[k, pl.ds(r, QUARTER, stride=4), :] for k in range(panels)],
                          axis=1).astype(BF16) for r in range(4)]

    for f0 in range(0, QUARTER, slab):
        rows = slice(f0, f0 + slab)
        mirror = slice(QUARTER + f0, QUARTER + f0 + slab)
        c = [_dot(tab_ref[r, rows, :], zr[r]) for r in range(4)]
        s = [_dot(tab_ref[4 + r, rows, :], zr[r]) for r in range(4)]
        e, o, em, om = c[0] + c[2], c[1] + c[3], c[0] - c[2], c[1] - c[3]
        se, so, sem, som = s[0] + s[2], s[1] + s[3], s[0] - s[2], s[1] - s[3]
        a = (e + o, em + som, em - som, e - o)
        b = (se + so, om - sem, sem + om, so - se)
        pk = (p_ref[rows, :], pm_ref[mirror, :], p_ref[mirror, :], pm_ref[rows, :])
        qk = (q_ref[rows, :], qm_ref[mirror, :], q_ref[mirror, :], qm_ref[rows, :])
        u = [a[k] * pk[k] - b[k] * qk[k] for k in range(4)]
        v = [a[k] * qk[k] + b[k] * pk[k] for k in range(4)]
        up, um, wp, wm = u[0] + u[3], u[0] - u[3], u[1] + u[2], u[1] - u[2]
        vp, vm, yp, ym = v[0] + v[3], v[0] - v[3], v[1] + v[2], v[1] - v[2]
        outs = (up + wp, um + yp, up - wp, um - yp,
                vm - ym, vp + wm, vm + ym, vp - wm)
        for n, val in enumerate(outs):
            x_ref[n, rows, :] = val.astype(x_ref.dtype)


def _dft4_fwd(tab, z_arr, z_col0, spectra, order):
    tn, slab = 512, 256
    nj = D_MODEL // tn
    zc = z_col0 // tn
    spec = pl.BlockSpec((HALF, tn), lambda j, b: (0, order * nj + j))
    return pl.pallas_call(
        functools.partial(_dft4_fwd_kernel, slab=slab),
        grid=(nj, BATCH),
        in_specs=[pl.BlockSpec(tab.shape, lambda j, b: (0, 0, 0), pipeline_mode=pl.Buffered(1)),
                  pl.BlockSpec((SEQ, tn), lambda j, b: (b, zc + j)), spec, spec, spec, spec],
        out_specs=pl.BlockSpec((8, QUARTER, tn), lambda j, b: (0, b, j)),
        out_shape=jax.ShapeDtypeStruct((8, BATCH * QUARTER, D_MODEL), BF16),
        scratch_shapes=[pltpu.VMEM((tn // LANES, SEQ, LANES), F32)],
        compiler_params=_params("arbitrary", "arbitrary"),
        name="dft_analysis",
    )(tab, z_arr, *spectra)


def _dft4_inv_kernel(tab_ref, x_ref, z_ref, gate_ref, skip_ref, o_ref, ys_ref):
    panels = ys_ref.shape[0]
    for r in range(4):
        y_r = _dot(tab_ref[r], x_ref[r]) + _dot(tab_ref[4 + r], x_ref[4 + r])
        for k in range(panels):
            ys_ref[k, pl.ds(r, QUARTER, stride=4), :] = y_r[:, k * LANES:(k + 1) * LANES]
    y = jnp.concatenate([ys_ref[k] for k in range(panels)], axis=1)
    y = y + z_ref[...].astype(F32) * skip_ref[...]
    o_ref[...] = (gate_ref[...].astype(F32) * y).astype(o_ref.dtype)


def _dft4_inv(tab, x, z_arr, z_col0, gate_arr, gate_col0, skip):
    tn = 512
    nj = D_MODEL // tn
    zc, gc = z_col0 // tn, gate_col0 // tn
    return pl.pallas_call(
        _dft4_inv_kernel,
        grid=(nj, BATCH),
        in_specs=[pl.BlockSpec(tab.shape, lambda j, b: (0, 0, 0), pipeline_mode=pl.Buffered(1)),
                  pl.BlockSpec((8, QUARTER, tn), lambda j, b: (0, b, j)),
                  pl.BlockSpec((SEQ, tn), lambda j, b: (b, zc + j)),
                  pl.BlockSpec((SEQ, tn), lambda j, b: (b, gc + j)),
                  pl.BlockSpec((1, tn), lambda j, b: (0, j))],
        out_specs=pl.BlockSpec((SEQ, tn), lambda j, b: (b, j)),
        out_shape=jax.ShapeDtypeStruct((TOKENS, D_MODEL), BF16),
        scratch_shapes=[pltpu.VMEM((tn // LANES, SEQ, LANES), F32)],
        compiler_params=_params("arbitrary", "arbitrary"),
        name="dft_synthesis",
    )(tab, x, z_arr, gate_arr, skip.reshape(1, D_MODEL))


def _mod_chunk(mod, l, k):
    return mod[l, :BATCH, k * D_MODEL:(k + 1) * D_MODEL]


def _mlp_block(x2d, h, mod, l, w1, w2):
    hid = _mm_plain(h, w1, D_FF, BF16, _epi_relu2, 1024, 1024, "mlp_up", w_layer=l)
    return _mm_residual([hid], w2, l, x2d, _mod_chunk(mod, l, 5), 512, 512, "mlp_down", w_single_buffer=True)


def kernel(x, c, ctx, c_ctx, ada_w, ada_b, norm_g, lb_logits, ab_w_in, ab_conv_w, ab_gnorm_g, ab_w_out,
           hy_in_w, hy_short_w, hy_out_w, hy_fw1, hy_fb1, hy_fw2, hy_fb2, hy_fw3, hy_fb3, hy_fw4, hy_freq,
           hy_skip, mlp_w1, mlp_w2, final_g):
    d = D_MODEL
    cond = jnp.concatenate([c, c_ctx[None, :], jnp.zeros((3, d), F32)], axis=0)
    taps_sum, taps_diff, mod = _filter_taps_and_modulation(
        hy_fw1[0], hy_fb1[0], hy_fw2[0], hy_fb2[0], hy_fw3[0], hy_fb3[0], hy_fw4[0], hy_freq[0],
        cond, ada_w, ada_b)
    x2d = x.reshape(TOKENS, d)

    sh1, sc1, g1 = (mod[0, :BATCH, k * d:(k + 1) * d] for k in (0, 1, 2))
    h = _prep(x, norm_g[0, 0], sc1, sh1, BF16, 1024).reshape(TOKENS, d)
    csh = jnp.broadcast_to(mod[0, BATCH:BATCH + 1, 0:d], (BATCH, d))
    csc = jnp.broadcast_to(mod[0, BATCH:BATCH + 1, d:2 * d], (BATCH, d))
    hc = _prep(ctx, norm_g[0, 0], csc, csh, BF16, CTX_LEN).reshape(BATCH * CTX_LEN, d)
    p = _mm_plain(h, ab_w_in, AB_IN_WIDTH, F32, _epi_store, 1024, 1024, "ab_in_proj", w_layer=0)
    cp = _mm_plain(hc, ab_w_in, 3 * A_WIDTH, F32, _epi_store, 1024, 1024, "ab_ctx_proj", w_layer=0)
    mix_a, mix_b = _even_mixer(p, cp, lb_logits, ab_gnorm_g[0], ab_conv_w[0], 0)
    x2d, h = _mm_residual_norm([mix_a, mix_b], ab_w_out, 0, x2d, g1, norm_g[0, 1], _mod_chunk(mod, 0, 4),
                               _mod_chunk(mod, 0, 3), 512, "ab_out_proj")
    x2d = _mlp_block(x2d, h, mod, 0, mlp_w1, mlp_w2)

    sh1, sc1, g1 = (mod[1, :BATCH, k * d:(k + 1) * d] for k in (0, 1, 2))
    h = _prep(x2d.reshape(BATCH, SEQ, d), norm_g[1, 0], sc1, sh1, BF16, 1024).reshape(TOKENS, d)
    pc = _mm([h], hy_in_w, w_layer=0, tm=1024, tn=1024, n_cols=3 * d, epi=_epi_conv3_grid,
             extras=(hy_short_w[0],), extra_specs=(pl.BlockSpec((3, 1024), lambda j, i: (0, j)),),
             out_shapes=jax.ShapeDtypeStruct((TOKENS, 3 * d), BF16),
             out_specs=pl.BlockSpec((1024, 1024), lambda j, i: (i, j)), name="hy_in_proj")
    analysis, synthesis, filter_tables = jax.tree.map(lambda t: jnp.asarray(t).astype(BF16), _dft_tables())
    spectra = _filter_dft(filter_tables, taps_sum, taps_diff)
    coef = _dft4_fwd(analysis, pc, 2 * d, spectra, 0)
    z = _dft4_inv(synthesis, coef, pc, 2 * d, pc, 0, hy_skip[0, 0])
    coef = _dft4_fwd(analysis, z, 0, spectra, 1)
    z = _dft4_inv(synthesis, coef, z, 0, pc, d, hy_skip[0, 1])
    x2d, h = _mm_residual_norm([z], hy_out_w, 0, x2d, g1, norm_g[1, 1], _mod_chunk(mod, 1, 4),
                               _mod_chunk(mod, 1, 3), 512, "hy_out_proj")
    x2d = _mlp_block(x2d, h, mod, 1, mlp_w1, mlp_w2)

    zeros = jnp.zeros((BATCH, d), F32)
    return _prep(x2d.reshape(BATCH, SEQ, d), final_g, zeros, zeros, F32, 1024)
```

```python
import functools
import math

import numpy as np
import jax
import jax.numpy as jnp
from jax import lax
from jax.experimental import pallas as pl
from jax.experimental.pallas import tpu as pltpu

F32 = jnp.float32
BF16 = jnp.bfloat16

D_MODEL = 2048
BATCH = 4
SEQ = 2048
CTX_LEN = 256
GRID_W = 64
EPS = 1e-6
A_HEAD_DIM = 128
A_WIDTH = D_MODEL // 2
A_HEADS = A_WIDTH // A_HEAD_DIM
B_WIDTH = D_MODEL - A_WIDTH
AB_IN_WIDTH = 5 * A_WIDTH + 3 * B_WIDTH
HY_EMB = 33
HY_BANDS = (HY_EMB - 1) // 2
HY_DECAY_TARGET = 1e-2
HY_FAST_PCT = 0.3
HY_SLOW_PCT = 1.5
D_FF = 4 * D_MODEL
TOKENS = BATCH * SEQ
HALF = SEQ // 2
QUARTER = SEQ // 4

SCAN_CHUNK = 64
SCAN_LEVELS = (1, 2, 4, 8, 16, 32)

SUBLANES = 8
LANES = 128

VMEM_LIMIT_BYTES = 56 * 1024 * 1024


def _params(*sem):
    return pltpu.CompilerParams(dimension_semantics=sem, vmem_limit_bytes=VMEM_LIMIT_BYTES)


def _dot(a, b):
    return jnp.dot(a, b, preferred_element_type=F32)


def _dot_nt(a, b):
    return lax.dot_general(a, b, (((1,), (1,)), ((), ())), preferred_element_type=F32)


def _dot_tn(a, b):
    return lax.dot_general(a, b, (((0,), (0,)), ((), ())), preferred_element_type=F32)


def _dot_f32(a, b):
    return jnp.dot(a, b, precision=lax.Precision.HIGHEST, preferred_element_type=F32)


def _conv3_rows(t, w, group):
    rows = lax.broadcasted_iota(jnp.int32, t.shape, 0) % group
    prev = jnp.where(rows == 0, 0.0, pltpu.roll(t, 1, 0))
    nxt = jnp.where(rows == group - 1, 0.0, pltpu.roll(t, t.shape[0] - 1, 0))
    return w[0:1] * prev + w[1:2] * t + w[2:3] * nxt


def _prep_kernel(x_ref, g_ref, sc_ref, sh_ref, o_ref):
    x = x_ref[...]
    ms = jnp.mean(x * x, axis=-1, keepdims=True)
    xn = x * lax.rsqrt(ms + EPS) * g_ref[...]
    o_ref[...] = (xn * (1.0 + sc_ref[...]) + sh_ref[...]).astype(o_ref.dtype)


def _prep(x, g, sc, sh, out_dtype, ts):
    b, l, d = x.shape
    return pl.pallas_call(
        _prep_kernel,
        grid=(b, l // ts),
        in_specs=[
            pl.BlockSpec((None, ts, d), lambda i, j: (i, j, 0)),
            pl.BlockSpec((1, d), lambda i, j: (0, 0)),
            pl.BlockSpec((None, 1, d), lambda i, j: (i, 0, 0)),
            pl.BlockSpec((None, 1, d), lambda i, j: (i, 0, 0)),
        ],
        out_specs=pl.BlockSpec((None, ts, d), lambda i, j: (i, j, 0)),
        out_shape=jax.ShapeDtypeStruct((b, l, d), out_dtype),
        compiler_params=_params("arbitrary", "arbitrary"),
        name="norm_mod",
    )(x, g.reshape(1, d), sc.reshape(b, 1, d), sh.reshape(b, 1, d))


def _mm_kernel(*refs, n_a, k_sizes, epi, cast_w):
    a_refs = refs[:n_a]
    w_ref = refs[n_a]
    rest = refs[n_a + 1:]
    if cast_w:
        w_ref, rest = rest[-1], rest[:-1]

        @pl.when(pl.program_id(1) == 0)
        def _():
            w_ref[...] = refs[n_a][...].astype(BF16)

    acc = None
    off = 0
    for a_ref, k in zip(a_refs, k_sizes):
        part = _dot(a_ref[...], w_ref[off:off + k, :])
        acc = part if acc is None else acc + part
        off += k
    epi(acc, *rest)


def _mm(a_list, w, *, tm, tn, n_cols, epi, extras, extra_specs, out_shapes, out_specs, name,
        w_layer=None, w_single_buffer=False):
    m = a_list[0].shape[0]
    k_sizes = tuple(a.shape[1] for a in a_list)
    k_total = sum(k_sizes)
    cast_w = w_layer is not None
    assert w.shape[-2] == k_total and m % tm == 0 and n_cols % tn == 0
    in_specs = [pl.BlockSpec((tm, k), lambda j, i: (i, 0)) for k in k_sizes]
    w_mode = dict(pipeline_mode=pl.Buffered(1)) if w_single_buffer else {}
    if cast_w:
        in_specs.append(pl.BlockSpec((None, k_total, tn), lambda j, i: (w_layer, 0, j), **w_mode))
    else:
        in_specs.append(pl.BlockSpec((k_total, tn), lambda j, i: (0, j), **w_mode))
    in_specs.extend(extra_specs)
    return pl.pallas_call(
        functools.partial(_mm_kernel, n_a=len(a_list), k_sizes=k_sizes, epi=epi, cast_w=cast_w),
        grid=(n_cols // tn, m // tm),
        in_specs=in_specs,
        out_specs=out_specs,
        out_shape=out_shapes,
        scratch_shapes=[pltpu.VMEM((k_total, tn), BF16)] if cast_w else [],
        compiler_params=_params("arbitrary", "arbitrary"),
        name=name,
    )(*a_list, w, *extras)


def _epi_store(acc, o_ref):
    o_ref[...] = acc.astype(o_ref.dtype)


def _epi_relu2(acc, o_ref):
    r = jnp.maximum(acc, 0.0)
    o_ref[...] = (r * r).astype(o_ref.dtype)


def _epi_residual(acc, x_ref, gate_ref, o_ref):
    o_ref[...] = x_ref[...] + gate_ref[...] * acc


def _epi_residual_norm(acc, x_ref, gate_ref, g_ref, sc_ref, sh_ref, o_ref, h_ref):
    x = x_ref[...] + gate_ref[...] * acc
    o_ref[...] = x
    ms = jnp.mean(x * x, axis=-1, keepdims=True)
    xn = x * lax.rsqrt(ms + EPS) * g_ref[...]
    h_ref[...] = (xn * (1.0 + sc_ref[...]) + sh_ref[...]).astype(h_ref.dtype)


def _epi_conv3_grid(acc, w_ref, o_ref):
    o_ref[...] = _conv3_rows(acc, w_ref[...], GRID_W).astype(o_ref.dtype)


def _mm_plain(a, w, n_cols, out_dtype, epi, tm, tn, name, w_layer=None):
    m = a.shape[0]
    return _mm([a], w, tm=tm, tn=tn, n_cols=n_cols, epi=epi, extras=(), extra_specs=(),
               out_shapes=jax.ShapeDtypeStruct((m, n_cols), out_dtype),
               out_specs=pl.BlockSpec((tm, tn), lambda j, i: (i, j)), name=name, w_layer=w_layer)


def _mm_residual(a_list, w, w_layer, x, gate, tm, tn, name, w_single_buffer=False):
    m, n = x.shape
    tiles_per_batch = SEQ // tm
    return _mm(a_list, w, tm=tm, tn=tn, n_cols=n, epi=_epi_residual,
               extras=(x, gate.reshape(BATCH, 1, n)),
               extra_specs=(pl.BlockSpec((tm, tn), lambda j, i: (i, j)),
                            pl.BlockSpec((None, 1, tn), lambda j, i: (i // tiles_per_batch, 0, j))),
               out_shapes=jax.ShapeDtypeStruct((m, n), F32),
               out_specs=pl.BlockSpec((tm, tn), lambda j, i: (i, j)), name=name,
               w_layer=w_layer, w_single_buffer=w_single_buffer)


def _mm_residual_norm(a_list, w, w_layer, x, gate, norm_g, sc, sh, tm, name):
    m, n = x.shape
    tiles_per_batch = SEQ // tm

    def per_batch(v):
        return v.reshape(BATCH, 1, n), pl.BlockSpec((None, 1, n), lambda j, i: (i // tiles_per_batch, 0, 0))

    (gate3, gate_spec), (sc3, sc_spec), (sh3, sh_spec) = per_batch(gate), per_batch(sc), per_batch(sh)
    rows = pl.BlockSpec((tm, n), lambda j, i: (i, 0))
    return _mm(a_list, w, tm=tm, tn=n, n_cols=n, epi=_epi_residual_norm,
               extras=(x, gate3, norm_g.reshape(1, n), sc3, sh3),
               extra_specs=(rows, gate_spec, pl.BlockSpec((1, n), lambda j, i: (0, 0)), sc_spec, sh_spec),
               out_shapes=(jax.ShapeDtypeStruct((m, n), F32), jax.ShapeDtypeStruct((m, n), BF16)),
               out_specs=(rows, rows), name=name, w_layer=w_layer, w_single_buffer=True)


def _row_groups(x):
    return [x[k:k + SUBLANES] for k in range(0, x.shape[0], SUBLANES)]


def _cumsum_groups(groups, sub):
    out, carry = [], None
    for g in groups:
        for s in (1, 2, 4):
            g = g + jnp.where(sub >= s, pltpu.roll(g, s, 0), 0.0)
        if carry is not None:
            g = g + carry
        out.append(g)
        carry = g[SUBLANES - 1:SUBLANES]
    return out


def _hold_boundary(cum, w, k, sub):
    shape = cum[k].shape
    if w >= SUBLANES // 2:
        r = (k * SUBLANES // (2 * w)) * 2 * w + w - 1
        return jnp.broadcast_to(cum[r // SUBLANES][r % SUBLANES:r % SUBLANES + 1], shape)
    assert w == 2
    return jnp.where(sub < 4, jnp.broadcast_to(cum[k][1:2], shape), jnp.broadcast_to(cum[k][5:6], shape))


def _scan_kernel(zf_ref, zb_ref, v_ref, q_ref, g_ref, u_ref, gb_ref, gc_ref, czf_ref, czb_ref, cv_ref,
                 lbl_ref, gn_ref, cw_ref, o_ref, oc_ref,
                 of_ref, ob_ref, st_ref, qs_ref, inc_ref, dec_ref, kl_ref, lv_ref, att_ref, *, layer):
    c = SCAN_CHUNK
    n_chunks = SEQ // c
    n_ctx = CTX_LEN // c
    z_refs = (zf_ref, zb_ref)
    cz_refs = (czf_ref, czb_ref)
    out_refs = (of_ref, ob_ref)

    logit_rows = [lbl_ref[:, k, :] for k in range(lbl_ref.shape[1])]
    top = functools.reduce(jnp.maximum, logit_rows)
    exps = [jnp.exp(r - top) for r in logit_rows]
    lbs = sum(exps[:layer + 1]) / sum(exps)

    sub = lax.broadcasted_iota(jnp.int32, (SUBLANES, A_HEAD_DIM), 0)
    n_groups = c // SUBLANES

    def rows(groups):
        return jnp.concatenate(groups, axis=0)

    def chunk(d, z, q, v, want_out):
        lb = lbs[d:d + 1]
        f = lb + (1.0 - lb) * jax.nn.sigmoid(z)
        kk = 1.0 - f
        fg, kg = _row_groups(f), _row_groups(kk)
        lg = [jnp.log2(g) for g in fg]
        cum = _cumsum_groups(lg, sub)
        total = cum[-1][SUBLANES - 1:SUBLANES]
        if d == 0:
            pos = cum
            q_in = [jnp.exp2(g) for g in cum]
            k_out = [jnp.exp2(total - g) for g in cum]
        else:
            pos = [a - b for a, b in zip(cum, lg)]
            q_in = [jnp.exp2(total - g) for g in pos]
            k_out = [jnp.exp2(g) for g in pos]
        k_leave = rows([a * b for a, b in zip(kg, k_out)]).astype(BF16)
        decay = jnp.exp2(total)
        if not want_out:
            return k_leave, decay, None, None, None
        qg = _row_groups(q)
        o_diag = jnp.sum(q * kk, axis=1, keepdims=True) * v
        upper, lower = (qg, kg) if d == 0 else (kg, qg)
        operands = []
        for w in SCAN_LEVELS:
            y = []
            for k in range(n_groups):
                if w == 1:
                    odd = (sub & 1) != 0
                    y.append(jnp.where(odd, qg[k] * fg[k], kg[k]) if d == 0
                             else jnp.where(odd, kg[k], qg[k] * fg[k]))
                    continue
                hold = _hold_boundary(cum, w, k, sub)
                if w >= SUBLANES:
                    bit_set = (k * SUBLANES) & w != 0
                    y.append(upper[k] * jnp.exp2(pos[k] - hold) if bit_set
                             else lower[k] * jnp.exp2(hold - pos[k]))
                else:
                    bit = (sub & w) != 0
                    sign = jnp.where(bit, 1.0, -1.0)
                    y.append(jnp.where(bit, upper[k], lower[k]) * jnp.exp2((pos[k] - hold) * sign))
            operands.append(rows(y).astype(BF16))
        qs = rows([a * b for a, b in zip(qg, q_in)]).astype(BF16)
        return k_leave, decay, qs, o_diag, operands

    def within_chunk(d, operands):
        t_idx = lax.broadcasted_iota(jnp.int32, (c, c), 0)
        s_idx = lax.broadcasted_iota(jnp.int32, (c, c), 1)
        split = t_idx ^ s_idx
        att = None
        for w, yb in reversed(list(zip(SCAN_LEVELS, operands))):
            scores = _dot_nt(yb, yb)
            att = scores if att is None else jnp.where(split < 2 * w, scores, att)
        return jnp.where(t_idx > s_idx if d == 0 else t_idx < s_idx, att, 0.0)

    st_ref[...] = jnp.zeros_like(st_ref)

    def ctx_body(i, carry):
        for d in range(2):
            ci = i if d == 0 else n_ctx - 1 - i
            rows_i = pl.ds(pl.multiple_of(ci * c, c), c)
            k_leave, decay, _, _, _ = chunk(d, cz_refs[d][rows_i, :], None, None, False)
            st_ref[d] = st_ref[d] * decay + _dot_tn(cv_ref[rows_i, :].astype(BF16), k_leave)
        return carry

    lax.fori_loop(0, n_ctx, ctx_body, 0, unroll=2)

    def chunk_rows(ci):
        start = ci * c
        return pl.ds(start if isinstance(ci, int) else pl.multiple_of(start, c), c)

    def chunk_of(d, t):
        return t if d == 0 else n_chunks - 1 - t

    def prepare(t):
        for d in range(2):
            ci = chunk_of(d, t)
            rows_i = chunk_rows(ci)
            k_leave, decay, qs, o_diag, operands = chunk(d, z_refs[d][rows_i, :], q_ref[rows_i, :],
                                                         v_ref[rows_i, :], True)
            kl_ref[d, rows_i, :] = k_leave
            dec_ref[d, ci] = decay
            qs_ref[d, rows_i, :] = qs
            out_refs[d][rows_i, :] = o_diag
            for lvl, yb in enumerate(operands):
                lv_ref[d, lvl, rows_i, :] = yb

    def scores(t):
        for d in range(2):
            ci = chunk_of(d, t)
            rows_i = chunk_rows(ci)
            inc_ref[d, ci] = _dot_tn(v_ref[rows_i, :].astype(BF16), kl_ref[d, rows_i, :])
            operands = [lv_ref[d, lvl, rows_i, :] for lvl in range(len(SCAN_LEVELS))]
            att_ref[d, rows_i, :] = within_chunk(d, operands).astype(BF16)

    def carry_state(t):
        for d in range(2):
            ci = chunk_of(d, t)
            rows_i = chunk_rows(ci)
            state = st_ref[d]
            out_refs[d][rows_i, :] += (_dot(att_ref[d, rows_i, :], v_ref[rows_i, :].astype(BF16))
                                       + _dot_nt(qs_ref[d, rows_i, :], state.astype(BF16)))
            st_ref[d] = state * dec_ref[d, ci] + inc_ref[d, ci]

    pair = 4
    n_trips = n_chunks // pair

    def stage(fn, trip):
        for j in range(pair):
            fn(trip * pair + j)

    stage(prepare, 0)
    stage(scores, 0)
    stage(prepare, 1)

    def skewed_body(i, carry):
        stage(carry_state, i - 2)
        stage(scores, i - 1)
        stage(prepare, i)
        return carry

    lax.fori_loop(2, n_trips, skewed_body, 0)
    stage(carry_state, n_trips - 2)
    stage(scores, n_trips - 1)
    stage(carry_state, n_trips - 1)

    rb = 256
    gn = gn_ref[...]
    cw = cw_ref[...]

    def read_body(i, carry):
        rows_i = pl.ds(pl.multiple_of(i * rb, rb), rb)
        o = of_ref[rows_i, :] + ob_ref[rows_i, :]
        ms = jnp.mean(o * o, axis=-1, keepdims=True)
        on = o * lax.rsqrt(ms + EPS) * gn
        g = g_ref[rows_i, :]
        o_ref[rows_i, :] = (on * (g * jax.nn.sigmoid(g))).astype(o_ref.dtype)
        conv = _conv3_rows(gc_ref[rows_i, :] * u_ref[rows_i, :], cw, GRID_W)
        oc_ref[rows_i, :] = (gb_ref[rows_i, :] * conv).astype(oc_ref.dtype)
        return carry

    lax.fori_loop(0, SEQ // rb, read_body, 0)


def _even_mixer(p, cp, lb_logits, gnorm_g, conv_w, layer):
    h = A_HEADS
    hd = A_HEAD_DIM
    n_lb = lb_logits.shape[1]

    def col(k):
        return pl.BlockSpec((SEQ, hd), lambda b, j, k=k: (b, k * h + j))

    def ccol(k):
        return pl.BlockSpec((CTX_LEN, hd), lambda b, j, k=k: (b, k * h + j))

    out_spec = pl.BlockSpec((SEQ, hd), lambda b, j: (b, j))
    return pl.pallas_call(
        functools.partial(_scan_kernel, layer=layer),
        grid=(BATCH, h),
        in_specs=[col(0), col(1), col(2), col(3), col(4), col(5), col(6), col(7),
                  ccol(0), ccol(1), ccol(2),
                  pl.BlockSpec((2, n_lb, hd), lambda b, j: (0, 0, j)),
                  pl.BlockSpec((1, hd), lambda b, j: (0, j)),
                  pl.BlockSpec((3, hd), lambda b, j: (0, j))],
        out_specs=(out_spec, out_spec),
        out_shape=(jax.ShapeDtypeStruct((TOKENS, A_WIDTH), BF16),
                   jax.ShapeDtypeStruct((TOKENS, B_WIDTH), BF16)),
        scratch_shapes=[pltpu.VMEM((SEQ, hd), F32), pltpu.VMEM((SEQ, hd), F32),
                        pltpu.VMEM((2, hd, hd), F32), pltpu.VMEM((2, SEQ, hd), BF16),
                        pltpu.VMEM((2, SEQ // SCAN_CHUNK, hd, hd), F32),
                        pltpu.VMEM((2, SEQ // SCAN_CHUNK, 1, hd), F32),
                        pltpu.VMEM((2, SEQ, hd), BF16),
                        pltpu.VMEM((2, len(SCAN_LEVELS), SEQ, hd), BF16),
                        pltpu.VMEM((2, SEQ, SCAN_CHUNK), BF16)],
        compiler_params=_params("arbitrary", "arbitrary"),
        name="hgrn_scan",
    )(p, p, p, p, p, p, p, p, cp, cp, cp, lb_logits, gnorm_g.reshape(1, A_WIDTH), conv_w)


def _split_bf16(x):
    hi = x.astype(BF16)
    return hi, (x - hi.astype(F32)).astype(BF16)


def _filter_kernel(z_ref, w1_ref, b1_ref, w2_ref, b2_ref, w3_ref, b3_ref, fr_ref,
                   w4f_ref, w4b_ref, cond_ref, aw_ref, ab_ref,
                   sum_ref, diff_ref, mod_ref, hid_ref, *, tc):
    o = pl.program_id(0)
    j = pl.program_id(1)

    cond = cond_ref[...]
    mod_ref[...] = _dot((cond * jax.nn.sigmoid(cond)).astype(BF16), aw_ref[...].astype(BF16)) + ab_ref[...]

    @pl.when((o == 0) & (j == 0))
    def _():
        fr = fr_ref[...]
        h = jnp.sin(fr * (_dot_f32(z_ref[...], w1_ref[...]) + b1_ref[...]))
        h = jnp.sin(fr * (_dot_f32(h, w2_ref[...]) + b2_ref[...]))
        h = jnp.sin(fr * (_dot_f32(h, w3_ref[...]) + b3_ref[...]))
        hi, lo = _split_bf16(h)
        hid_ref[...] = jnp.concatenate([hi, lo, hi], axis=1)

    def project(w_ref):
        w_hi, w_lo = _split_bf16(w_ref[...])
        zero = jnp.zeros_like(w_hi)
        first = _dot(hid_ref[...], jnp.concatenate([w_hi, zero, w_hi, zero, w_lo, zero], axis=0))
        second = _dot(hid_ref[...], jnp.concatenate([zero, w_hi, zero, w_hi, zero, w_lo], axis=0))
        return jnp.concatenate([first, second], axis=0)

    rows = lax.broadcasted_iota(jnp.int32, (SEQ, tc), 0)
    chan = lax.broadcasted_iota(jnp.int32, (SEQ, tc), 1) + j * tc
    lag = jnp.where(rows < HALF, 2 * rows, 2 * rows - (SEQ - 1))
    t = lag.astype(F32) * (1.0 / (SEQ - 1))
    max_decay = math.log(HY_DECAY_TARGET) / HY_FAST_PCT
    min_decay = math.log(HY_DECAY_TARGET) / HY_SLOW_PCT
    deltas = jnp.abs(min_decay + chan.astype(F32) * ((max_decay - min_decay) / (D_MODEL - 1)))
    window = jnp.exp(-t * deltas)
    fw = project(w4f_ref) * window
    bw = project(w4b_ref) * window
    first = rows == 0
    a = fw + jnp.where(first, bw, 0.0)
    bb = jnp.where(first, 0.0, bw)
    inv = 1.0 / jnp.sum(jnp.abs(a) + jnp.abs(bb), axis=0, keepdims=True)
    sum_ref[...] = ((a + bb) * inv).astype(sum_ref.dtype)
    diff_ref[...] = ((a - bb) * inv).astype(diff_ref.dtype)


@functools.lru_cache(maxsize=None)
def _filter_positions():
    l = SEQ
    t = np.linspace(0.0, 1.0, l, dtype=np.float32)
    w = (2.0 * math.pi * np.arange(l, dtype=np.float32) / l).astype(np.float32)
    bands = np.linspace(1e-4, HY_BANDS - 1, HY_BANDS, dtype=np.float32)
    ang = w[:, None] * bands[None, :]
    z = np.concatenate([t[:, None], np.cos(ang), -np.sin(ang)], axis=-1).astype(np.float32)
    z = np.concatenate([z[0::2], z[1::2]], axis=0)
    z = np.pad(z, ((0, 0), (0, LANES // 2 - HY_EMB)))
    return np.concatenate([z[:l // 2], z[l // 2:]], axis=1)


def _filter_taps_and_modulation(fw1, fb1, fw2, fb2, fw3, fb3, fw4, freq, cond, ada_w, ada_b):
    l = SEQ
    tc = 256
    nj = D_MODEL // tc
    depth, d, n_mod = ada_w.shape
    assert depth == 2 and n_mod % (nj * LANES) == 0
    ta = n_mod // nj
    hw = LANES // 2

    def pad(a, rows, cols):
        return jnp.pad(a, ((0, rows - a.shape[0]), (0, cols - a.shape[1])))

    def twice(a):
        a = pad(a, hw, hw)
        zero = jnp.zeros_like(a)
        return jnp.concatenate([jnp.concatenate([a, zero], axis=1), jnp.concatenate([zero, a], axis=1)], axis=0)

    def row2(v):
        v = pad(v[None, :], 1, hw)
        return jnp.concatenate([v, v], axis=1)

    def small(shape):
        return pl.BlockSpec(shape, lambda o, j: (0, 0))

    out_sds = jax.ShapeDtypeStruct((l, 2 * D_MODEL), BF16)
    out_spec = pl.BlockSpec((l, tc), lambda o, j: (0, o * nj + j))
    assert fw4.shape[0] == hw
    side_f = pl.BlockSpec((hw, tc), lambda o, j: (0, 2 * o * nj + j))
    side_b = pl.BlockSpec((hw, tc), lambda o, j: (0, (2 * o + 1) * nj + j))
    sq, vec = small((LANES, LANES)), small((1, LANES))
    return pl.pallas_call(
        functools.partial(_filter_kernel, tc=tc),
        grid=(2, nj),
        in_specs=[small((l // 2, LANES)), sq, vec, sq, vec, sq, vec, vec, side_f, side_b,
                  small((8, d)),
                  pl.BlockSpec((None, d, ta), lambda o, j: (o, 0, j)),
                  pl.BlockSpec((None, 1, ta), lambda o, j: (o, 0, j))],
        out_specs=(out_spec, out_spec, pl.BlockSpec((None, 8, ta), lambda o, j: (o, 0, j))),
        out_shape=(out_sds, out_sds, jax.ShapeDtypeStruct((depth, 8, n_mod), F32)),
        scratch_shapes=[pltpu.VMEM((l // 2, 3 * LANES), BF16)],
        compiler_params=_params("arbitrary", "arbitrary"),
        name="hyena_filter",
    )(jnp.asarray(_filter_positions()), twice(fw1), row2(fb1), twice(fw2), row2(fb2), twice(fw3), row2(fb3),
      row2(freq), fw4, fw4, cond, ada_w, ada_b.reshape(depth, 1, n_mod))


def _trig_table(a_rows, b_cols, scale):
    n = (a_rows[:, None].astype(np.int64) * b_cols[None, :].astype(np.int64)) % (4 * SEQ)
    ang = n.astype(np.float64) * (math.pi / (2 * SEQ))
    return (np.cos(ang) * scale).astype(np.float32), (np.sin(ang) * scale).astype(np.float32)


@functools.lru_cache(maxsize=None)
def _dft_tables():
    idx = np.arange(QUARTER)
    ana_c, ana_s, syn_c, syn_s = [], [], [], []
    for r in range(4):
        c, s = _trig_table(2 * idx + 1, 4 * idx + r, 1.0)
        ana_c.append(c)
        ana_s.append(s)
        c, s = _trig_table(4 * idx + r, 2 * idx + 1, 1.0 / SEQ)
        syn_c.append(c)
        syn_s.append(s)
    fmap = np.concatenate([idx, HALF + idx])
    lag = np.arange(HALF)
    cee, see = _trig_table(2 * fmap + 1, 2 * lag, 1.0)
    ceo, seo = _trig_table(2 * fmap + 1, 2 * lag + 1, 1.0)
    return np.stack(ana_c + ana_s), np.stack(syn_c + syn_s), (cee, ceo, see, seo)


def _filter_dft_kernel(cee_ref, ceo_ref, see_ref, seo_ref, fs_ref, fd_ref, p_ref, pm_ref, q_ref, qm_ref):
    pe = _dot(cee_ref[...], fs_ref[0:HALF, :])
    po = _dot(ceo_ref[...], fs_ref[HALF:SEQ, :])
    qe = _dot(see_ref[...], fd_ref[0:HALF, :])
    qo = _dot(seo_ref[...], fd_ref[HALF:SEQ, :])
    p_ref[...] = pe + po
    pm_ref[...] = pe - po
    q_ref[...] = qe + qo
    qm_ref[...] = qo - qe


def _filter_dft(tables, taps_sum, taps_diff):
    tm, tn = HALF, 256
    n_cols = taps_sum.shape[1]
    mat = pl.BlockSpec((tm, HALF), lambda j, i: (i, 0), pipeline_mode=pl.Buffered(1))
    tap = pl.BlockSpec((SEQ, tn), lambda j, i: (0, j))
    out_spec = pl.BlockSpec((tm, tn), lambda j, i: (i, j))
    out_sds = jax.ShapeDtypeStruct((HALF, n_cols), F32)
    return pl.pallas_call(
        _filter_dft_kernel,
        grid=(n_cols // tn, HALF // tm),
        in_specs=[mat, mat, mat, mat, tap, tap],
        out_specs=(out_spec,) * 4,
        out_shape=(out_sds,) * 4,
        compiler_params=_params("arbitrary", "arbitrary"),
        name="filter_dft",
    )(*tables, taps_sum, taps_diff)


def _dft4_fwd_kernel(tab_ref, z_ref, p_ref, pm_ref, q_ref, qm_ref, x_ref, zs_ref, *, slab):
    panels = zs_ref.shape[0]
    for k in range(panels):
        zs_ref[k] = z_ref[:, k * LANES:(k + 1) * LANES].astype(F32)
    zr = [jnp.concatenate([zs_ref[k, pl.ds(r, QUARTER, stride=4), :] for k in range(panels)],
                          axis=1).astype(BF16) for r in range(4)]

    for f0 in range(0, QUARTER, slab):
        rows = slice(f0, f0 + slab)
        mirror = slice(QUARTER + f0, QUARTER + f0 + slab)
        c = [_dot(tab_ref[r, rows, :], zr[r]) for r in range(4)]
        s = [_dot(tab_ref[4 + r, rows, :], zr[r]) for r in range(4)]
        e, o, em, om = c[0] + c[2], c[1] + c[3], c[0] - c[2], c[1] - c[3]
        se, so, sem, som = s[0] + s[2], s[1] + s[3], s[0] - s[2], s[1] - s[3]
        a = (e + o, em + som, em - som, e - o)
        b = (se + so, om - sem, sem + om, so - se)
        pk = (p_ref[rows, :], pm_ref[mirror, :], p_ref[mirror, :], pm_ref[rows, :])
        qk = (q_ref[rows, :], qm_ref[mirror, :], q_ref[mirror, :], qm_ref[rows, :])
        u = [a[k] * pk[k] - b[k] * qk[k] for k in range(4)]
        v = [a[k] * qk[k] + b[k] * pk[k] for k in range(4)]
        up, um, wp, wm = u[0] + u[3], u[0] - u[3], u[1] + u[2], u[1] - u[2]
        vp, vm, yp, ym = v[0] + v[3], v[0] - v[3], v[1] + v[2], v[1] - v[2]
        outs = (up + wp, um + yp, up - wp, um - yp,
                vm - ym, vp + wm, vm + ym, vp - wm)
        for n, val in enumerate(outs):
            x_ref[n, rows, :] = val.astype(x_ref.dtype)


def _dft4_fwd(tab, z_arr, z_col0, spectra, order):
    tn, slab = 512, 256
    nj = D_MODEL // tn
    zc = z_col0 // tn
    spec = pl.BlockSpec((HALF, tn), lambda j, b: (0, order * nj + j))
    return pl.pallas_call(
        functools.partial(_dft4_fwd_kernel, slab=slab),
        grid=(nj, BATCH),
        in_specs=[pl.BlockSpec(tab.shape, lambda j, b: (0, 0, 0), pipeline_mode=pl.Buffered(1)),
                  pl.BlockSpec((SEQ, tn), lambda j, b: (b, zc + j)), spec, spec, spec, spec],
        out_specs=pl.BlockSpec((8, QUARTER, tn), lambda j, b: (0, b, j)),
        out_shape=jax.ShapeDtypeStruct((8, BATCH * QUARTER, D_MODEL), BF16),
        scratch_shapes=[pltpu.VMEM((tn // LANES, SEQ, LANES), F32)],
        compiler_params=_params("arbitrary", "arbitrary"),
        name="dft_analysis",
    )(tab, z_arr, *spectra)


def _dft4_inv_kernel(tab_ref, x_ref, z_ref, gate_ref, skip_ref, o_ref, ys_ref):
    panels = ys_ref.shape[0]
    for r in range(4):
        y_r = _dot(tab_ref[r], x_ref[r]) + _dot(tab_ref[4 + r], x_ref[4 + r])
        for k in range(panels):
            ys_ref[k, pl.ds(r, QUARTER, stride=4), :] = y_r[:, k * LANES:(k + 1) * LANES]
    y = jnp.concatenate([ys_ref[k] for k in range(panels)], axis=1)
    y = y + z_ref[...].astype(F32) * skip_ref[...]
    o_ref[...] = (gate_ref[...].astype(F32) * y).astype(o_ref.dtype)


def _dft4_inv(tab, x, z_arr, z_col0, gate_arr, gate_col0, skip):
    tn = 512
    nj = D_MODEL // tn
    zc, gc = z_col0 // tn, gate_col0 // tn
    return pl.pallas_call(
        _dft4_inv_kernel,
        grid=(nj, BATCH),
        in_specs=[pl.BlockSpec(tab.shape, lambda j, b: (0, 0, 0), pipeline_mode=pl.Buffered(1)),
                  pl.BlockSpec((8, QUARTER, tn), lambda j, b: (0, b, j)),
                  pl.BlockSpec((SEQ, tn), lambda j, b: (b, zc + j)),
                  pl.BlockSpec((SEQ, tn), lambda j, b: (b, gc + j)),
                  pl.BlockSpec((1, tn), lambda j, b: (0, j))],
        out_specs=pl.BlockSpec((SEQ, tn), lambda j, b: (b, j)),
        out_shape=jax.ShapeDtypeStruct((TOKENS, D_MODEL), BF16),
        scratch_shapes=[pltpu.VMEM((tn // LANES, SEQ, LANES), F32)],
        compiler_params=_params("arbitrary", "arbitrary"),
        name="dft_synthesis",
    )(tab, x, z_arr, gate_arr, skip.reshape(1, D_MODEL))


def _mod_chunk(mod, l, k):
    return mod[l, :BATCH, k * D_MODEL:(k + 1) * D_MODEL]


def _mlp_block(x2d, h, mod, l, w1, w2):
    hid = _mm_plain(h, w1, D_FF, BF16, _epi_relu2, 1024, 1024, "mlp_up", w_layer=l)
    return _mm_residual([hid], w2, l, x2d, _mod_chunk(mod, l, 5), 512, 512, "mlp_down", w_single_buffer=True)


def kernel(x, c, ctx, c_ctx, ada_w, ada_b, norm_g, lb_logits, ab_w_in, ab_conv_w, ab_gnorm_g, ab_w_out,
           hy_in_w, hy_short_w, hy_out_w, hy_fw1, hy_fb1, hy_fw2, hy_fb2, hy_fw3, hy_fb3, hy_fw4, hy_freq,
           hy_skip, mlp_w1, mlp_w2, final_g):
    d = D_MODEL
    cond = jnp.concatenate([c, c_ctx[None, :], jnp.zeros((3, d), F32)], axis=0)
    taps_sum, taps_diff, mod = _filter_taps_and_modulation(
        hy_fw1[0], hy_fb1[0], hy_fw2[0], hy_fb2[0], hy_fw3[0], hy_fb3[0], hy_fw4[0], hy_freq[0],
        cond, ada_w, ada_b)
    x2d = x.reshape(TOKENS, d)

    sh1, sc1, g1 = (mod[0, :BATCH, k * d:(k + 1) * d] for k in (0, 1, 2))
    h = _prep(x, norm_g[0, 0], sc1, sh1, BF16, 1024).reshape(TOKENS, d)
    csh = jnp.broadcast_to(mod[0, BATCH:BATCH + 1, 0:d], (BATCH, d))
    csc = jnp.broadcast_to(mod[0, BATCH:BATCH + 1, d:2 * d], (BATCH, d))
    hc = _prep(ctx, norm_g[0, 0], csc, csh, BF16, CTX_LEN).reshape(BATCH * CTX_LEN, d)
    p = _mm_plain(h, ab_w_in, AB_IN_WIDTH, F32, _epi_store, 1024, 1024, "ab_in_proj", w_layer=0)
    cp = _mm_plain(hc, ab_w_in, 3 * A_WIDTH, F32, _epi_store, 1024, 1024, "ab_ctx_proj", w_layer=0)
    mix_a, mix_b = _even_mixer(p, cp, lb_logits, ab_gnorm_g[0], ab_conv_w[0], 0)
    x2d, h = _mm_residual_norm([mix_a, mix_b], ab_w_out, 0, x2d, g1, norm_g[0, 1], _mod_chunk(mod, 0, 4),
                               _mod_chunk(mod, 0, 3), 512, "ab_out_proj")
    x2d = _mlp_block(x2d, h, mod, 0, mlp_w1, mlp_w2)

    sh1, sc1, g1 = (mod[1, :BATCH, k * d:(k + 1) * d] for k in (0, 1, 2))
    h = _prep(x2d.reshape(BATCH, SEQ, d), norm_g[1, 0], sc1, sh1, BF16, 1024).reshape(TOKENS, d)
    pc = _mm([h], hy_in_w, w_layer=0, tm=1024, tn=1024, n_cols=3 * d, epi=_epi_conv3_grid,
             extras=(hy_short_w[0],), extra_specs=(pl.BlockSpec((3, 1024), lambda j, i: (0, j)),),
             out_shapes=jax.ShapeDtypeStruct((TOKENS, 3 * d), BF16),
             out_specs=pl.BlockSpec((1024, 1024), lambda j, i: (i, j)), name="hy_in_proj")
    analysis, synthesis, filter_tables = jax.tree.map(lambda t: jnp.asarray(t).astype(BF16), _dft_tables())
    spectra = _filter_dft(filter_tables, taps_sum, taps_diff)
    coef = _dft4_fwd(analysis, pc, 2 * d, spectra, 0)
    z = _dft4_inv(synthesis, coef, pc, 2 * d, pc, 0, hy_skip[0, 0])
    coef = _dft4_fwd(analysis, z, 0, spectra, 1)
    z = _dft4_inv(synthesis, coef, z, 0, pc, d, hy_skip[0, 1])
    x2d, h = _mm_residual_norm([z], hy_out_w, 0, x2d, g1, norm_g[1, 1], _mod_chunk(mod, 1, 4),
                               _mod_chunk(mod, 1, 3), 512, "hy_out_proj")
    x2d = _mlp_block(x2d, h, mod, 1, mlp_w1, mlp_w2)

    zeros = jnp.zeros((BATCH, d), F32)
    return _prep(x2d.reshape(BATCH, SEQ, d), final_g, zeros, zeros, F32, 1024)
```

```python
import functools
import math

import numpy as np
import jax
import jax.numpy as jnp
from jax import lax
from jax.experimental import pallas as pl
from jax.experimental.pallas import tpu as pltpu

F32 = jnp.float32
BF16 = jnp.bfloat16

D_MODEL = 2048
BATCH = 4
SEQ = 2048
CTX_LEN = 256
GRID_W = 64
EPS = 1e-6
A_HEAD_DIM = 128
A_WIDTH = D_MODEL // 2
A_HEADS = A_WIDTH // A_HEAD_DIM
B_WIDTH = D_MODEL - A_WIDTH
AB_IN_WIDTH = 5 * A_WIDTH + 3 * B_WIDTH
HY_EMB = 33
HY_BANDS = (HY_EMB - 1) // 2
HY_DECAY_TARGET = 1e-2
HY_FAST_PCT = 0.3
HY_SLOW_PCT = 1.5
D_FF = 4 * D_MODEL
TOKENS = BATCH * SEQ
HALF = SEQ // 2
QUARTER = SEQ // 4

SCAN_CHUNK = 64
SCAN_LEVELS = (1, 2, 4, 8, 16, 32)

SUBLANES = 8
LANES = 128

VMEM_LIMIT_BYTES = 56 * 1024 * 1024


def _params(*sem):
    return pltpu.CompilerParams(dimension_semantics=sem, vmem_limit_bytes=VMEM_LIMIT_BYTES)


def _dot(a, b):
    return jnp.dot(a, b, preferred_element_type=F32)


def _dot_nt(a, b):
    return lax.dot_general(a, b, (((1,), (1,)), ((), ())), preferred_element_type=F32)


def _dot_tn(a, b):
    return lax.dot_general(a, b, (((0,), (0,)), ((), ())), preferred_element_type=F32)


def _dot_f32(a, b):
    return jnp.dot(a, b, precision=lax.Precision.HIGHEST, preferred_element_type=F32)


def _conv3_rows(t, w, group):
    rows = lax.broadcasted_iota(jnp.int32, t.shape, 0) % group
    prev = jnp.where(rows == 0, 0.0, pltpu.roll(t, 1, 0))
    nxt = jnp.where(rows == group - 1, 0.0, pltpu.roll(t, t.shape[0] - 1, 0))
    return w[0:1] * prev + w[1:2] * t + w[2:3] * nxt


def _prep_kernel(x_ref, g_ref, sc_ref, sh_ref, o_ref):
    scale = g_ref[...] * (1.0 + sc_ref[...])
    shift = sh_ref[...]
    piece = 64

    def body(i, carry):
        rows = pl.ds(pl.multiple_of(i * piece, piece), piece)
        x = x_ref[rows, :]
        ms = jnp.mean(x * x, axis=-1, keepdims=True)
        o_ref[rows, :] = (x * lax.rsqrt(ms + EPS) * scale + shift).astype(o_ref.dtype)
        return carry

    lax.fori_loop(0, x_ref.shape[0] // piece, body, 0, unroll=4)


def _prep(x, g, sc, sh, out_dtype, ts):
    b, l, d = x.shape
    return pl.pallas_call(
        _prep_kernel,
        grid=(b, l // ts),
        in_specs=[
            pl.BlockSpec((None, ts, d), lambda i, j: (i, j, 0)),
            pl.BlockSpec((1, d), lambda i, j: (0, 0)),
            pl.BlockSpec((None, 1, d), lambda i, j: (i, 0, 0)),
            pl.BlockSpec((None, 1, d), lambda i, j: (i, 0, 0)),
        ],
        out_specs=pl.BlockSpec((None, ts, d), lambda i, j: (i, j, 0)),
        out_shape=jax.ShapeDtypeStruct((b, l, d), out_dtype),
        compiler_params=_params("arbitrary", "arbitrary"),
        name="norm_mod",
    )(x, g.reshape(1, d), sc.reshape(b, 1, d), sh.reshape(b, 1, d))


def _mm_kernel(*refs, n_a, k_sizes, epi, cast_w):
    a_refs = refs[:n_a]
    w_ref = refs[n_a]
    rest = refs[n_a + 1:]
    if cast_w:
        w_ref, rest = rest[-1], rest[:-1]

        @pl.when(pl.program_id(1) == 0)
        def _():
            w_ref[...] = refs[n_a][...].astype(BF16)

    acc = None
    off = 0
    for a_ref, k in zip(a_refs, k_sizes):
        part = _dot(a_ref[...], w_ref[off:off + k, :])
        acc = part if acc is None else acc + part
        off += k
    epi(acc, *rest)


def _mm(a_list, w, *, tm, tn, n_cols, epi, extras, extra_specs, out_shapes, out_specs, name,
        w_layer=None, w_single_buffer=False):
    m = a_list[0].shape[0]
    k_sizes = tuple(a.shape[1] for a in a_list)
    k_total = sum(k_sizes)
    cast_w = w_layer is not None
    assert w.shape[-2] == k_total and m % tm == 0 and n_cols % tn == 0
    in_specs = [pl.BlockSpec((tm, k), lambda j, i: (i, 0)) for k in k_sizes]
    w_mode = dict(pipeline_mode=pl.Buffered(1)) if w_single_buffer else {}
    if cast_w:
        in_specs.append(pl.BlockSpec((None, k_total, tn), lambda j, i: (w_layer, 0, j), **w_mode))
    else:
        in_specs.append(pl.BlockSpec((k_total, tn), lambda j, i: (0, j), **w_mode))
    in_specs.extend(extra_specs)
    return pl.pallas_call(
        functools.partial(_mm_kernel, n_a=len(a_list), k_sizes=k_sizes, epi=epi, cast_w=cast_w),
        grid=(n_cols // tn, m // tm),
        in_specs=in_specs,
        out_specs=out_specs,
        out_shape=out_shapes,
        scratch_shapes=[pltpu.VMEM((k_total, tn), BF16)] if cast_w else [],
        compiler_params=_params("arbitrary", "arbitrary"),
        name=name,
    )(*a_list, w, *extras)


def _epi_store(acc, o_ref):
    o_ref[...] = acc.astype(o_ref.dtype)


def _epi_relu2(acc, o_ref):
    r = jnp.maximum(acc, 0.0)
    o_ref[...] = (r * r).astype(o_ref.dtype)


def _epi_residual(acc, x_ref, gate_ref, o_ref):
    o_ref[...] = x_ref[...] + gate_ref[...] * acc


def _epi_residual_norm(acc, x_ref, gate_ref, g_ref, sc_ref, sh_ref, o_ref, h_ref):
    x = x_ref[...] + gate_ref[...] * acc
    o_ref[...] = x
    ms = jnp.mean(x * x, axis=-1, keepdims=True)
    xn = x * lax.rsqrt(ms + EPS) * g_ref[...]
    h_ref[...] = (xn * (1.0 + sc_ref[...]) + sh_ref[...]).astype(h_ref.dtype)


def _epi_conv3_grid(acc, w_ref, o_ref):
    o_ref[...] = _conv3_rows(acc, w_ref[...], GRID_W).astype(o_ref.dtype)


def _mm_plain(a, w, n_cols, out_dtype, epi, tm, tn, name, w_layer=None):
    m = a.shape[0]
    return _mm([a], w, tm=tm, tn=tn, n_cols=n_cols, epi=epi, extras=(), extra_specs=(),
               out_shapes=jax.ShapeDtypeStruct((m, n_cols), out_dtype),
               out_specs=pl.BlockSpec((tm, tn), lambda j, i: (i, j)), name=name, w_layer=w_layer)


def _mm_residual(a_list, w, w_layer, x, gate, tm, tn, name, w_single_buffer=False):
    m, n = x.shape
    tiles_per_batch = SEQ // tm
    return _mm(a_list, w, tm=tm, tn=tn, n_cols=n, epi=_epi_residual,
               extras=(x, gate.reshape(BATCH, 1, n)),
               extra_specs=(pl.BlockSpec((tm, tn), lambda j, i: (i, j)),
                            pl.BlockSpec((None, 1, tn), lambda j, i: (i // tiles_per_batch, 0, j))),
               out_shapes=jax.ShapeDtypeStruct((m, n), F32),
               out_specs=pl.BlockSpec((tm, tn), lambda j, i: (i, j)), name=name,
               w_layer=w_layer, w_single_buffer=w_single_buffer)


def _mm_residual_norm(a_list, w, w_layer, x, gate, norm_g, sc, sh, tm, name):
    m, n = x.shape
    tiles_per_batch = SEQ // tm

    def per_batch(v):
        return v.reshape(BATCH, 1, n), pl.BlockSpec((None, 1, n), lambda j, i: (i // tiles_per_batch, 0, 0))

    (gate3, gate_spec), (sc3, sc_spec), (sh3, sh_spec) = per_batch(gate), per_batch(sc), per_batch(sh)
    rows = pl.BlockSpec((tm, n), lambda j, i: (i, 0))
    return _mm(a_list, w, tm=tm, tn=n, n_cols=n, epi=_epi_residual_norm,
               extras=(x, gate3, norm_g.reshape(1, n), sc3, sh3),
               extra_specs=(rows, gate_spec, pl.BlockSpec((1, n), lambda j, i: (0, 0)), sc_spec, sh_spec),
               out_shapes=(jax.ShapeDtypeStruct((m, n), F32), jax.ShapeDtypeStruct((m, n), BF16)),
               out_specs=(rows, rows), name=name, w_layer=w_layer, w_single_buffer=True)


def _row_groups(x):
    return [x[k:k + SUBLANES] for k in range(0, x.shape[0], SUBLANES)]


def _cumsum_groups(groups, sub):
    out, carry = [], None
    for g in groups:
        for s in (1, 2, 4):
            g = g + jnp.where(sub >= s, pltpu.roll(g, s, 0), 0.0)
        if carry is not None:
            g = g + carry
        out.append(g)
        carry = g[SUBLANES - 1:SUBLANES]
    return out


def _hold_boundary(cum, w, k, sub):
    shape = cum[k].shape
    if w >= SUBLANES // 2:
        r = (k * SUBLANES // (2 * w)) * 2 * w + w - 1
        return jnp.broadcast_to(cum[r // SUBLANES][r % SUBLANES:r % SUBLANES + 1], shape)
    assert w == 2
    return jnp.where(sub < 4, jnp.broadcast_to(cum[k][1:2], shape), jnp.broadcast_to(cum[k][5:6], shape))


def _scan_kernel(zf_ref, zb_ref, v_ref, q_ref, g_ref, u_ref, gb_ref, gc_ref, czf_ref, czb_ref, cv_ref,
                 lbl_ref, gn_ref, cw_ref, o_ref, oc_ref,
                 of_ref, ob_ref, st_ref, qs_ref, inc_ref, dec_ref, kl_ref, lv_ref, att_ref, *, layer):
    c = SCAN_CHUNK
    n_chunks = SEQ // c
    n_ctx = CTX_LEN // c
    z_refs = (zf_ref, zb_ref)
    cz_refs = (czf_ref, czb_ref)
    out_refs = (of_ref, ob_ref)

    logit_rows = [lbl_ref[:, k, :] for k in range(lbl_ref.shape[1])]
    top = functools.reduce(jnp.maximum, logit_rows)
    exps = [jnp.exp(r - top) for r in logit_rows]
    lbs = sum(exps[:layer + 1]) / sum(exps)

    sub = lax.broadcasted_iota(jnp.int32, (SUBLANES, A_HEAD_DIM), 0)
    n_groups = c // SUBLANES

    def rows(groups):
        return jnp.concatenate(groups, axis=0)

    def chunk(d, z, q, v, want_out):
        lb = lbs[d:d + 1]
        f = lb + (1.0 - lb) * jax.nn.sigmoid(z)
        kk = 1.0 - f
        fg, kg = _row_groups(f), _row_groups(kk)
        lg = [jnp.log2(g) for g in fg]
        cum = _cumsum_groups(lg, sub)
        total = cum[-1][SUBLANES - 1:SUBLANES]
        if d == 0:
            pos = cum
            q_in = [jnp.exp2(g) for g in cum]
            k_out = [jnp.exp2(total - g) for g in cum]
        else:
            pos = [a - b for a, b in zip(cum, lg)]
            q_in = [jnp.exp2(total - g) for g in pos]
            k_out = [jnp.exp2(g) for g in pos]
        k_leave = rows([a * b for a, b in zip(kg, k_out)]).astype(BF16)
        decay = jnp.exp2(total)
        if not want_out:
            return k_leave, decay, None, None, None
        qg = _row_groups(q)
        o_diag = jnp.sum(q * kk, axis=1, keepdims=True) * v
        upper, lower = (qg, kg) if d == 0 else (kg, qg)
        operands = []
        for w in SCAN_LEVELS:
            y = []
            for k in range(n_groups):
                if w == 1:
                    odd = (sub & 1) != 0
                    y.append(jnp.where(odd, qg[k] * fg[k], kg[k]) if d == 0
                             else jnp.where(odd, kg[k], qg[k] * fg[k]))
                    continue
                hold = _hold_boundary(cum, w, k, sub)
                if w >= SUBLANES:
                    bit_set = (k * SUBLANES) & w != 0
                    y.append(upper[k] * jnp.exp2(pos[k] - hold) if bit_set
                             else lower[k] * jnp.exp2(hold - pos[k]))
                else:
                    bit = (sub & w) != 0
                    sign = jnp.where(bit, 1.0, -1.0)
                    y.append(jnp.where(bit, upper[k], lower[k]) * jnp.exp2((pos[k] - hold) * sign))
            operands.append(rows(y).astype(BF16))
        qs = rows([a * b for a, b in zip(qg, q_in)]).astype(BF16)
        return k_leave, decay, qs, o_diag, operands

    def within_chunk(d, operands):
        t_idx = lax.broadcasted_iota(jnp.int32, (c, c), 0)
        s_idx = lax.broadcasted_iota(jnp.int32, (c, c), 1)
        split = t_idx ^ s_idx
        att = None
        for w, yb in reversed(list(zip(SCAN_LEVELS, operands))):
            scores = _dot_nt(yb, yb)
            att = scores if att is None else jnp.where(split < 2 * w, scores, att)
        return jnp.where(t_idx > s_idx if d == 0 else t_idx < s_idx, att, 0.0)

    st_ref[...] = jnp.zeros_like(st_ref)

    def ctx_body(i, carry):
        for d in range(2):
            ci = i if d == 0 else n_ctx - 1 - i
            rows_i = pl.ds(pl.multiple_of(ci * c, c), c)
            k_leave, decay, _, _, _ = chunk(d, cz_refs[d][rows_i, :], None, None, False)
            st_ref[d] = st_ref[d] * decay + _dot_tn(cv_ref[rows_i, :].astype(BF16), k_leave)
        return carry

    lax.fori_loop(0, n_ctx, ctx_body, 0, unroll=2)

    def chunk_rows(ci):
        start = ci * c
        return pl.ds(start if isinstance(ci, int) else pl.multiple_of(start, c), c)

    def chunk_of(d, t):
        return t if d == 0 else n_chunks - 1 - t

    def prepare(t):
        for d in range(2):
            ci = chunk_of(d, t)
            rows_i = chunk_rows(ci)
            k_leave, decay, qs, o_diag, operands = chunk(d, z_refs[d][rows_i, :], q_ref[rows_i, :],
                                                         v_ref[rows_i, :], True)
            kl_ref[d, rows_i, :] = k_leave
            dec_ref[d, ci] = decay
            qs_ref[d, rows_i, :] = qs
            out_refs[d][rows_i, :] = o_diag
            for lvl, yb in enumerate(operands):
                lv_ref[d, lvl, rows_i, :] = yb

    def scores(t):
        for d in range(2):
            ci = chunk_of(d, t)
            rows_i = chunk_rows(ci)
            inc_ref[d, ci] = _dot_tn(v_ref[rows_i, :].astype(BF16), kl_ref[d, rows_i, :])
            operands = [lv_ref[d, lvl, rows_i, :] for lvl in range(len(SCAN_LEVELS))]
            att_ref[d, rows_i, :] = within_chunk(d, operands).astype(BF16)

    def carry_state(t):
        for d in range(2):
            ci = chunk_of(d, t)
            rows_i = chunk_rows(ci)
            state = st_ref[d]
            out_refs[d][rows_i, :] += (_dot(att_ref[d, rows_i, :], v_ref[rows_i, :].astype(BF16))
                                       + _dot_nt(qs_ref[d, rows_i, :], state.astype(BF16)))
            st_ref[d] = state * dec_ref[d, ci] + inc_ref[d, ci]

    pair = 4
    n_trips = n_chunks // pair

    def stage(fn, trip):
        for j in range(pair):
            fn(trip * pair + j)

    stage(prepare, 0)
    stage(scores, 0)
    stage(prepare, 1)

    def skewed_body(i, carry):
        stage(carry_state, i - 2)
        stage(scores, i - 1)
        stage(prepare, i)
        return carry

    lax.fori_loop(2, n_trips, skewed_body, 0)
    stage(carry_state, n_trips - 2)
    stage(scores, n_trips - 1)
    stage(carry_state, n_trips - 1)

    rb = 256
    gn = gn_ref[...]
    cw = cw_ref[...]

    def read_body(i, carry):
        rows_i = pl.ds(pl.multiple_of(i * rb, rb), rb)
        o = of_ref[rows_i, :] + ob_ref[rows_i, :]
        ms = jnp.mean(o * o, axis=-1, keepdims=True)
        on = o * lax.rsqrt(ms + EPS) * gn
        g = g_ref[rows_i, :]
        o_ref[rows_i, :] = (on * (g * jax.nn.sigmoid(g))).astype(o_ref.dtype)
        conv = _conv3_rows(gc_ref[rows_i, :] * u_ref[rows_i, :], cw, GRID_W)
        oc_ref[rows_i, :] = (gb_ref[rows_i, :] * conv).astype(oc_ref.dtype)
        return carry

    lax.fori_loop(0, SEQ // rb, read_body, 0)


def _even_mixer(p, cp, lb_logits, gnorm_g, conv_w, layer):
    h = A_HEADS
    hd = A_HEAD_DIM
    n_lb = lb_logits.shape[1]

    def col(k):
        return pl.BlockSpec((SEQ, hd), lambda b, j, k=k: (b, k * h + j))

    def ccol(k):
        return pl.BlockSpec((CTX_LEN, hd), lambda b, j, k=k: (b, k * h + j))

    out_spec = pl.BlockSpec((SEQ, hd), lambda b, j: (b, j))
    return pl.pallas_call(
        functools.partial(_scan_kernel, layer=layer),
        grid=(BATCH, h),
        in_specs=[col(0), col(1), col(2), col(3), col(4), col(5), col(6), col(7),
                  ccol(0), ccol(1), ccol(2),
                  pl.BlockSpec((2, n_lb, hd), lambda b, j: (0, 0, j)),
                  pl.BlockSpec((1, hd), lambda b, j: (0, j)),
                  pl.BlockSpec((3, hd), lambda b, j: (0, j))],
        out_specs=(out_spec, out_spec),
        out_shape=(jax.ShapeDtypeStruct((TOKENS, A_WIDTH), BF16),
                   jax.ShapeDtypeStruct((TOKENS, B_WIDTH), BF16)),
        scratch_shapes=[pltpu.VMEM((SEQ, hd), F32), pltpu.VMEM((SEQ, hd), F32),
                        pltpu.VMEM((2, hd, hd), F32), pltpu.VMEM((2, SEQ, hd), BF16),
                        pltpu.VMEM((2, SEQ // SCAN_CHUNK, hd, hd), F32),
                        pltpu.VMEM((2, SEQ // SCAN_CHUNK, 1, hd), F32),
                        pltpu.VMEM((2, SEQ, hd), BF16),
                        pltpu.VMEM((2, len(SCAN_LEVELS), SEQ, hd), BF16),
                        pltpu.VMEM((2, SEQ, SCAN_CHUNK), BF16)],
        compiler_params=_params("arbitrary", "arbitrary"),
        name="hgrn_scan",
    )(p, p, p, p, p, p, p, p, cp, cp, cp, lb_logits, gnorm_g.reshape(1, A_WIDTH), conv_w)


def _split_bf16(x):
    hi = x.astype(BF16)
    return hi, (x - hi.astype(F32)).astype(BF16)


def _filter_kernel(z_ref, w1_ref, b1_ref, w2_ref, b2_ref, w3_ref, b3_ref, fr_ref,
                   w4f_ref, w4b_ref, cond_ref, aw_ref, ab_ref,
                   sum_ref, diff_ref, mod_ref, hid_ref, *, tc):
    o = pl.program_id(0)
    j = pl.program_id(1)

    cond = cond_ref[...]
    mod_ref[...] = _dot((cond * jax.nn.sigmoid(cond)).astype(BF16), aw_ref[...].astype(BF16)) + ab_ref[...]

    @pl.when((o == 0) & (j == 0))
    def _():
        fr = fr_ref[...]
        h = jnp.sin(fr * (_dot_f32(z_ref[...], w1_ref[...]) + b1_ref[...]))
        h = jnp.sin(fr * (_dot_f32(h, w2_ref[...]) + b2_ref[...]))
        h = jnp.sin(fr * (_dot_f32(h, w3_ref[...]) + b3_ref[...]))
        hi, lo = _split_bf16(h)
        hid_ref[...] = jnp.concatenate([hi, lo, hi], axis=1)

    def project(w_ref):
        w_hi, w_lo = _split_bf16(w_ref[...])
        zero = jnp.zeros_like(w_hi)
        first = _dot(hid_ref[...], jnp.concatenate([w_hi, zero, w_hi, zero, w_lo, zero], axis=0))
        second = _dot(hid_ref[...], jnp.concatenate([zero, w_hi, zero, w_hi, zero, w_lo], axis=0))
        return jnp.concatenate([first, second], axis=0)

    rows = lax.broadcasted_iota(jnp.int32, (SEQ, tc), 0)
    chan = lax.broadcasted_iota(jnp.int32, (SEQ, tc), 1) + j * tc
    lag = jnp.where(rows < HALF, 2 * rows, 2 * rows - (SEQ - 1))
    t = lag.astype(F32) * (1.0 / (SEQ - 1))
    max_decay = math.log(HY_DECAY_TARGET) / HY_FAST_PCT
    min_decay = math.log(HY_DECAY_TARGET) / HY_SLOW_PCT
    deltas = jnp.abs(min_decay + chan.astype(F32) * ((max_decay - min_decay) / (D_MODEL - 1)))
    window = jnp.exp(-t * deltas)
    fw = project(w4f_ref) * window
    bw = project(w4b_ref) * window
    first = rows == 0
    a = fw + jnp.where(first, bw, 0.0)
    bb = jnp.where(first, 0.0, bw)
    inv = 1.0 / jnp.sum(jnp.abs(a) + jnp.abs(bb), axis=0, keepdims=True)
    sum_ref[...] = ((a + bb) * inv).astype(sum_ref.dtype)
    diff_ref[...] = ((a - bb) * inv).astype(diff_ref.dtype)


@functools.lru_cache(maxsize=None)
def _filter_positions():
    l = SEQ
    t = np.linspace(0.0, 1.0, l, dtype=np.float32)
    w = (2.0 * math.pi * np.arange(l, dtype=np.float32) / l).astype(np.float32)
    bands = np.linspace(1e-4, HY_BANDS - 1, HY_BANDS, dtype=np.float32)
    ang = w[:, None] * bands[None, :]
    z = np.concatenate([t[:, None], np.cos(ang), -np.sin(ang)], axis=-1).astype(np.float32)
    z = np.concatenate([z[0::2], z[1::2]], axis=0)
    z = np.pad(z, ((0, 0), (0, LANES // 2 - HY_EMB)))
    return np.concatenate([z[:l // 2], z[l // 2:]], axis=1)


def _filter_taps_and_modulation(fw1, fb1, fw2, fb2, fw3, fb3, fw4, freq, cond, ada_w, ada_b):
    l = SEQ
    tc = 256
    nj = D_MODEL // tc
    depth, d, n_mod = ada_w.shape
    assert depth == 2 and n_mod % (nj * LANES) == 0
    ta = n_mod // nj
    hw = LANES // 2

    def pad(a, rows, cols):
        return jnp.pad(a, ((0, rows - a.shape[0]), (0, cols - a.shape[1])))

    def twice(a):
        a = pad(a, hw, hw)
        zero = jnp.zeros_like(a)
        return jnp.concatenate([jnp.concatenate([a, zero], axis=1), jnp.concatenate([zero, a], axis=1)], axis=0)

    def row2(v):
        v = pad(v[None, :], 1, hw)
        return jnp.concatenate([v, v], axis=1)

    def small(shape):
        return pl.BlockSpec(shape, lambda o, j: (0, 0))

    out_sds = jax.ShapeDtypeStruct((l, 2 * D_MODEL), BF16)
    out_spec = pl.BlockSpec((l, tc), lambda o, j: (0, o * nj + j))
    assert fw4.shape[0] == hw
    side_f = pl.BlockSpec((hw, tc), lambda o, j: (0, 2 * o * nj + j))
    side_b = pl.BlockSpec((hw, tc), lambda o, j: (0, (2 * o + 1) * nj + j))
    sq, vec = small((LANES, LANES)), small((1, LANES))
    return pl.pallas_call(
        functools.partial(_filter_kernel, tc=tc),
        grid=(2, nj),
        in_specs=[small((l // 2, LANES)), sq, vec, sq, vec, sq, vec, vec, side_f, side_b,
                  small((8, d)),
                  pl.BlockSpec((None, d, ta), lambda o, j: (o, 0, j)),
                  pl.BlockSpec((None, 1, ta), lambda o, j: (o, 0, j))],
        out_specs=(out_spec, out_spec, pl.BlockSpec((None, 8, ta), lambda o, j: (o, 0, j))),
        out_shape=(out_sds, out_sds, jax.ShapeDtypeStruct((depth, 8, n_mod), F32)),
        scratch_shapes=[pltpu.VMEM((l // 2, 3 * LANES), BF16)],
        compiler_params=_params("arbitrary", "arbitrary"),
        name="hyena_filter",
    )(jnp.asarray(_filter_positions()), twice(fw1), row2(fb1), twice(fw2), row2(fb2), twice(fw3), row2(fb3),
      row2(freq), fw4, fw4, cond, ada_w, ada_b.reshape(depth, 1, n_mod))


def _trig_table(a_rows, b_cols, scale):
    n = (a_rows[:, None].astype(np.int64) * b_cols[None, :].astype(np.int64)) % (4 * SEQ)
    ang = n.astype(np.float64) * (math.pi / (2 * SEQ))
    return (np.cos(ang) * scale).astype(np.float32), (np.sin(ang) * scale).astype(np.float32)


@functools.lru_cache(maxsize=None)
def _dft_tables():
    idx = np.arange(QUARTER)
    ana_c, ana_s, syn_c, syn_s = [], [], [], []
    for r in range(4):
        c, s = _trig_table(2 * idx + 1, 4 * idx + r, 1.0)
        ana_c.append(c)
        ana_s.append(s)
        c, s = _trig_table(4 * idx + r, 2 * idx + 1, 1.0 / SEQ)
        syn_c.append(c)
        syn_s.append(s)
    fmap = np.concatenate([idx, HALF + idx])
    lag = np.arange(HALF)
    cee, see = _trig_table(2 * fmap + 1, 2 * lag, 1.0)
    ceo, seo = _trig_table(2 * fmap + 1, 2 * lag + 1, 1.0)
    return np.stack(ana_c + ana_s), np.stack(syn_c + syn_s), (cee, ceo, see, seo)


def _filter_dft_kernel(cee_ref, ceo_ref, see_ref, seo_ref, fs_ref, fd_ref, p_ref, pm_ref, q_ref, qm_ref):
    pe = _dot(cee_ref[...], fs_ref[0:HALF, :])
    po = _dot(ceo_ref[...], fs_ref[HALF:SEQ, :])
    qe = _dot(see_ref[...], fd_ref[0:HALF, :])
    qo = _dot(seo_ref[...], fd_ref[HALF:SEQ, :])
    p_ref[...] = pe + po
    pm_ref[...] = pe - po
    q_ref[...] = qe + qo
    qm_ref[...] = qo - qe


def _filter_dft(tables, taps_sum, taps_diff):
    tm, tn = HALF, 256
    n_cols = taps_sum.shape[1]
    mat = pl.BlockSpec((tm, HALF), lambda j, i: (i, 0), pipeline_mode=pl.Buffered(1))
    tap = pl.BlockSpec((SEQ, tn), lambda j, i: (0, j))
    out_spec = pl.BlockSpec((tm, tn), lambda j, i: (i, j))
    out_sds = jax.ShapeDtypeStruct((HALF, n_cols), F32)
    return pl.pallas_call(
        _filter_dft_kernel,
        grid=(n_cols // tn, HALF // tm),
        in_specs=[mat, mat, mat, mat, tap, tap],
        out_specs=(out_spec,) * 4,
        out_shape=(out_sds,) * 4,
        compiler_params=_params("arbitrary", "arbitrary"),
        name="filter_dft",
    )(*tables, taps_sum, taps_diff)


def _dft4_fwd_kernel(tab_ref, z_ref, p_ref, pm_ref, q_ref, qm_ref, x_ref, zs_ref, *, slab):
    panels = zs_ref.shape[0]
    for k in range(panels):
        zs_ref[k] = z_ref[:, k * LANES:(k + 1) * LANES].astype(F32)
    zr = [jnp.concatenate([zs_ref[k, pl.ds(r, QUARTER, stride=4), :] for k in range(panels)],
                          axis=1).astype(BF16) for r in range(4)]

    for f0 in range(0, QUARTER, slab):
        rows = slice(f0, f0 + slab)
        mirror = slice(QUARTER + f0, QUARTER + f0 + slab)
        c = [_dot(tab_ref[r, rows, :], zr[r]) for r in range(4)]
        s = [_dot(tab_ref[4 + r, rows, :], zr[r]) for r in range(4)]
        e, o, em, om = c[0] + c[2], c[1] + c[3], c[0] - c[2], c[1] - c[3]
        se, so, sem, som = s[0] + s[2], s[1] + s[3], s[0] - s[2], s[1] - s[3]
        a = (e + o, em + som, em - som, e - o)
        b = (se + so, om - sem, sem + om, so - se)
        pk = (p_ref[rows, :], pm_ref[mirror, :], p_ref[mirror, :], pm_ref[rows, :])
        qk = (q_ref[rows, :], qm_ref[mirror, :], q_ref[mirror, :], qm_ref[rows, :])
        u = [a[k] * pk[k] - b[k] * qk[k] for k in range(4)]
        v = [a[k] * qk[k] + b[k] * pk[k] for k in range(4)]
        up, um, wp, wm = u[0] + u[3], u[0] - u[3], u[1] + u[2], u[1] - u[2]
        vp, vm, yp, ym = v[0] + v[3], v[0] - v[3], v[1] + v[2], v[1] - v[2]
        outs = (up + wp, um + yp, up - wp, um - yp,
                vm - ym, vp + wm, vm + ym, vp - wm)
        for n, val in enumerate(outs):
            x_ref[n, rows, :] = val.astype(x_ref.dtype)


def _dft4_fwd(tab, z_arr, z_col0, spectra, order):
    tn, slab = 512, 256
    nj = D_MODEL // tn
    zc = z_col0 // tn
    spec = pl.BlockSpec((HALF, tn), lambda j, b: (0, order * nj + j))
    return pl.pallas_call(
        functools.partial(_dft4_fwd_kernel, slab=slab),
        grid=(nj, BATCH),
        in_specs=[pl.BlockSpec(tab.shape, lambda j, b: (0, 0, 0), pipeline_mode=pl.Buffered(1)),
                  pl.BlockSpec((SEQ, tn), lambda j, b: (b, zc + j)), spec, spec, spec, spec],
        out_specs=pl.BlockSpec((8, QUARTER, tn), lambda j, b: (0, b, j)),
        out_shape=jax.ShapeDtypeStruct((8, BATCH * QUARTER, D_MODEL), BF16),
        scratch_shapes=[pltpu.VMEM((tn // LANES, SEQ, LANES), F32)],
        compiler_params=_params("arbitrary", "arbitrary"),
        name="dft_analysis",
    )(tab, z_arr, *spectra)


def _dft4_inv_kernel(tab_ref, x_ref, z_ref, gate_ref, skip_ref, o_ref, ys_ref):
    panels = ys_ref.shape[0]
    for r in range(4):
        y_r = _dot(tab_ref[r], x_ref[r]) + _dot(tab_ref[4 + r], x_ref[4 + r])
        for k in range(panels):
            ys_ref[k, pl.ds(r, QUARTER, stride=4), :] = y_r[:, k * LANES:(k + 1) * LANES]
    y = jnp.concatenate([ys_ref[k] for k in range(panels)], axis=1)
    y = y + z_ref[...].astype(F32) * skip_ref[...]
    o_ref[...] = (gate_ref[...].astype(F32) * y).astype(o_ref.dtype)


def _dft4_inv(tab, x, z_arr, z_col0, gate_arr, gate_col0, skip):
    tn = 512
    nj = D_MODEL // tn
    zc, gc = z_col0 // tn, gate_col0 // tn
    return pl.pallas_call(
        _dft4_inv_kernel,
        grid=(nj, BATCH),
        in_specs=[pl.BlockSpec(tab.shape, lambda j, b: (0, 0, 0), pipeline_mode=pl.Buffered(1)),
                  pl.BlockSpec((8, QUARTER, tn), lambda j, b: (0, b, j)),
                  pl.BlockSpec((SEQ, tn), lambda j, b: (b, zc + j)),
                  pl.BlockSpec((SEQ, tn), lambda j, b: (b, gc + j)),
                  pl.BlockSpec((1, tn), lambda j, b: (0, j))],
        out_specs=pl.BlockSpec((SEQ, tn), lambda j, b: (b, j)),
        out_shape=jax.ShapeDtypeStruct((TOKENS, D_MODEL), BF16),
        scratch_shapes=[pltpu.VMEM((tn // LANES, SEQ, LANES), F32)],
        compiler_params=_params("arbitrary", "arbitrary"),
        name="dft_synthesis",
    )(tab, x, z_arr, gate_arr, skip.reshape(1, D_MODEL))


def _mod_chunk(mod, l, k):
    return mod[l, :BATCH, k * D_MODEL:(k + 1) * D_MODEL]


def _mlp_block(x2d, h, mod, l, w1, w2):
    hid = _mm_plain(h, w1, D_FF, BF16, _epi_relu2, 1024, 1024, "mlp_up", w_layer=l)
    return _mm_residual([hid], w2, l, x2d, _mod_chunk(mod, l, 5), 512, 512, "mlp_down", w_single_buffer=True)


def kernel(x, c, ctx, c_ctx, ada_w, ada_b, norm_g, lb_logits, ab_w_in, ab_conv_w, ab_gnorm_g, ab_w_out,
           hy_in_w, hy_short_w, hy_out_w, hy_fw1, hy_fb1, hy_fw2, hy_fb2, hy_fw3, hy_fb3, hy_fw4, hy_freq,
           hy_skip, mlp_w1, mlp_w2, final_g):
    d = D_MODEL
    cond = jnp.concatenate([c, c_ctx[None, :], jnp.zeros((3, d), F32)], axis=0)
    taps_sum, taps_diff, mod = _filter_taps_and_modulation(
        hy_fw1[0], hy_fb1[0], hy_fw2[0], hy_fb2[0], hy_fw3[0], hy_fb3[0], hy_fw4[0], hy_freq[0],
        cond, ada_w, ada_b)
    x2d = x.reshape(TOKENS, d)

    sh1, sc1, g1 = (mod[0, :BATCH, k * d:(k + 1) * d] for k in (0, 1, 2))
    h = _prep(x, norm_g[0, 0], sc1, sh1, BF16, 1024).reshape(TOKENS, d)
    csh = jnp.broadcast_to(mod[0, BATCH:BATCH + 1, 0:d], (BATCH, d))
    csc = jnp.broadcast_to(mod[0, BATCH:BATCH + 1, d:2 * d], (BATCH, d))
    hc = _prep(ctx, norm_g[0, 0], csc, csh, BF16, CTX_LEN).reshape(BATCH * CTX_LEN, d)
    p = _mm_plain(h, ab_w_in, AB_IN_WIDTH, F32, _epi_store, 1024, 1024, "ab_in_proj", w_layer=0)
    cp = _mm_plain(hc, ab_w_in, 3 * A_WIDTH, F32, _epi_store, 1024, 1024, "ab_ctx_proj", w_layer=0)
    mix_a, mix_b = _even_mixer(p, cp, lb_logits, ab_gnorm_g[0], ab_conv_w[0], 0)
    x2d, h = _mm_residual_norm([mix_a, mix_b], ab_w_out, 0, x2d, g1, norm_g[0, 1], _mod_chunk(mod, 0, 4),
                               _mod_chunk(mod, 0, 3), 512, "ab_out_proj")
    x2d = _mlp_block(x2d, h, mod, 0, mlp_w1, mlp_w2)

    sh1, sc1, g1 = (mod[1, :BATCH, k * d:(k + 1) * d] for k in (0, 1, 2))
    h = _prep(x2d.reshape(BATCH, SEQ, d), norm_g[1, 0], sc1, sh1, BF16, 1024).reshape(TOKENS, d)
    pc = _mm([h], hy_in_w, w_layer=0, tm=1024, tn=1024, n_cols=3 * d, epi=_epi_conv3_grid,
             extras=(hy_short_w[0],), extra_specs=(pl.BlockSpec((3, 1024), lambda j, i: (0, j)),),
             out_shapes=jax.ShapeDtypeStruct((TOKENS, 3 * d), BF16),
             out_specs=pl.BlockSpec((1024, 1024), lambda j, i: (i, j)), name="hy_in_proj")
    analysis, synthesis, filter_tables = jax.tree.map(lambda t: jnp.asarray(t).astype(BF16), _dft_tables())
    spectra = _filter_dft(filter_tables, taps_sum, taps_diff)
    coef = _dft4_fwd(analysis, pc, 2 * d, spectra, 0)
    z = _dft4_inv(synthesis, coef, pc, 2 * d, pc, 0, hy_skip[0, 0])
    coef = _dft4_fwd(analysis, z, 0, spectra, 1)
    z = _dft4_inv(synthesis, coef, z, 0, pc, d, hy_skip[0, 1])
    x2d, h = _mm_residual_norm([z], hy_out_w, 0, x2d, g1, norm_g[1, 1], _mod_chunk(mod, 1, 4),
                               _mod_chunk(mod, 1, 3), 512, "hy_out_proj")
    x2d = _mlp_block(x2d, h, mod, 1, mlp_w1, mlp_w2)

    zeros = jnp.zeros((BATCH, d), F32)
    return _prep(x2d.reshape(BATCH, SEQ, d), final_g, zeros, zeros, F32, 1024)
```

```python
import functools
import math

import numpy as np
import jax
import jax.numpy as jnp
from jax import lax
from jax.experimental import pallas as pl
from jax.experimental.pallas import tpu as pltpu

F32 = jnp.float32
BF16 = jnp.bfloat16

D_MODEL = 2048
BATCH = 4
SEQ = 2048
CTX_LEN = 256
GRID_W = 64
EPS = 1e-6
A_HEAD_DIM = 128
A_WIDTH = D_MODEL // 2
A_HEADS = A_WIDTH // A_HEAD_DIM
B_WIDTH = D_MODEL - A_WIDTH
AB_IN_WIDTH = 5 * A_WIDTH + 3 * B_WIDTH
HY_EMB = 33
HY_BANDS = (HY_EMB - 1) // 2
HY_DECAY_TARGET = 1e-2
HY_FAST_PCT = 0.3
HY_SLOW_PCT = 1.5
D_FF = 4 * D_MODEL
TOKENS = BATCH * SEQ
HALF = SEQ // 2
QUARTER = SEQ // 4

SCAN_CHUNK = 64
SCAN_LEVELS = (1, 2, 4, 8, 16, 32)

SUBLANES = 8
LANES = 128

VMEM_LIMIT_BYTES = 56 * 1024 * 1024


def _params(*sem):
    return pltpu.CompilerParams(dimension_semantics=sem, vmem_limit_bytes=VMEM_LIMIT_BYTES)


def _dot(a, b):
    return jnp.dot(a, b, preferred_element_type=F32)


def _dot_nt(a, b):
    return lax.dot_general(a, b, (((1,), (1,)), ((), ())), preferred_element_type=F32)


def _dot_tn(a, b):
    return lax.dot_general(a, b, (((0,), (0,)), ((), ())), preferred_element_type=F32)


def _dot_f32(a, b):
    return jnp.dot(a, b, precision=lax.Precision.HIGHEST, preferred_element_type=F32)


def _conv3_rows(t, w, group):
    rows = lax.broadcasted_iota(jnp.int32, t.shape, 0) % group
    prev = jnp.where(rows == 0, 0.0, pltpu.roll(t, 1, 0))
    nxt = jnp.where(rows == group - 1, 0.0, pltpu.roll(t, t.shape[0] - 1, 0))
    return w[0:1] * prev + w[1:2] * t + w[2:3] * nxt


def _prep_kernel(x_ref, g_ref, sc_ref, sh_ref, o_ref):
    scale = g_ref[...] * (1.0 + sc_ref[...])
    shift = sh_ref[...]
    piece = 64

    def body(i, carry):
        rows = pl.ds(pl.multiple_of(i * piece, piece), piece)
        x = x_ref[rows, :]
        ms = jnp.mean(x * x, axis=-1, keepdims=True)
        o_ref[rows, :] = (x * lax.rsqrt(ms + EPS) * scale + shift).astype(o_ref.dtype)
        return carry

    lax.fori_loop(0, x_ref.shape[0] // piece, body, 0, unroll=4)


def _prep(x, g, sc, sh, out_dtype, ts):
    b, l, d = x.shape
    return pl.pallas_call(
        _prep_kernel,
        grid=(b, l // ts),
        in_specs=[
            pl.BlockSpec((None, ts, d), lambda i, j: (i, j, 0)),
            pl.BlockSpec((1, d), lambda i, j: (0, 0)),
            pl.BlockSpec((None, 1, d), lambda i, j: (i, 0, 0)),
            pl.BlockSpec((None, 1, d), lambda i, j: (i, 0, 0)),
        ],
        out_specs=pl.BlockSpec((None, ts, d), lambda i, j: (i, j, 0)),
        out_shape=jax.ShapeDtypeStruct((b, l, d), out_dtype),
        compiler_params=_params("arbitrary", "arbitrary"),
        name="norm_mod",
    )(x, g.reshape(1, d), sc.reshape(b, 1, d), sh.reshape(b, 1, d))


def _mm_kernel(*refs, n_a, k_sizes, epi, cast_w):
    a_refs = refs[:n_a]
    w_ref = refs[n_a]
    rest = refs[n_a + 1:]
    if cast_w:
        w_ref, rest = rest[-1], rest[:-1]

        @pl.when(pl.program_id(1) == 0)
        def _():
            w_ref[...] = refs[n_a][...].astype(BF16)

    acc = None
    off = 0
    for a_ref, k in zip(a_refs, k_sizes):
        part = _dot(a_ref[...], w_ref[off:off + k, :])
        acc = part if acc is None else acc + part
        off += k
    epi(acc, *rest)


def _mm(a_list, w, *, tm, tn, n_cols, epi, extras, extra_specs, out_shapes, out_specs, name,
        w_layer=None, w_single_buffer=False):
    m = a_list[0].shape[0]
    k_sizes = tuple(a.shape[1] for a in a_list)
    k_total = sum(k_sizes)
    cast_w = w_layer is not None
    assert w.shape[-2] == k_total and m % tm == 0 and n_cols % tn == 0
    in_specs = [pl.BlockSpec((tm, k), lambda j, i: (i, 0)) for k in k_sizes]
    w_mode = dict(pipeline_mode=pl.Buffered(1)) if w_single_buffer else {}
    if cast_w:
        in_specs.append(pl.BlockSpec((None, k_total, tn), lambda j, i: (w_layer, 0, j), **w_mode))
    else:
        in_specs.append(pl.BlockSpec((k_total, tn), lambda j, i: (0, j), **w_mode))
    in_specs.extend(extra_specs)
    return pl.pallas_call(
        functools.partial(_mm_kernel, n_a=len(a_list), k_sizes=k_sizes, epi=epi, cast_w=cast_w),
        grid=(n_cols // tn, m // tm),
        in_specs=in_specs,
        out_specs=out_specs,
        out_shape=out_shapes,
        scratch_shapes=[pltpu.VMEM((k_total, tn), BF16)] if cast_w else [],
        compiler_params=_params("arbitrary", "arbitrary"),
        name=name,
    )(*a_list, w, *extras)


def _epi_store(acc, o_ref):
    o_ref[...] = acc.astype(o_ref.dtype)


def _epi_relu2_and_cast(acc, w_next_ref, o_ref, w_next_bf16_ref):
    r = jnp.maximum(acc, 0.0)
    o_ref[...] = (r * r).astype(o_ref.dtype)
    w_next_bf16_ref[...] = w_next_ref[...].astype(w_next_bf16_ref.dtype)


def _epi_residual(acc, x_ref, gate_ref, o_ref):
    o_ref[...] = x_ref[...] + gate_ref[...] * acc


def _epi_residual_norm(acc, x_ref, gate_ref, g_ref, sc_ref, sh_ref, o_ref, h_ref):
    x = x_ref[...] + gate_ref[...] * acc
    o_ref[...] = x
    ms = jnp.mean(x * x, axis=-1, keepdims=True)
    xn = x * lax.rsqrt(ms + EPS) * g_ref[...]
    h_ref[...] = (xn * (1.0 + sc_ref[...]) + sh_ref[...]).astype(h_ref.dtype)


def _epi_conv3_grid(acc, w_ref, o_ref):
    o_ref[...] = _conv3_rows(acc, w_ref[...], GRID_W).astype(o_ref.dtype)


def _mm_plain(a, w, n_cols, out_dtype, epi, tm, tn, name, w_layer=None):
    m = a.shape[0]
    return _mm([a], w, tm=tm, tn=tn, n_cols=n_cols, epi=epi, extras=(), extra_specs=(),
               out_shapes=jax.ShapeDtypeStruct((m, n_cols), out_dtype),
               out_specs=pl.BlockSpec((tm, tn), lambda j, i: (i, j)), name=name, w_layer=w_layer)


def _mm_residual(a_list, w, w_layer, x, gate, tm, tn, name, w_single_buffer=False):
    m, n = x.shape
    tiles_per_batch = SEQ // tm
    return _mm(a_list, w, tm=tm, tn=tn, n_cols=n, epi=_epi_residual,
               extras=(x, gate.reshape(BATCH, 1, n)),
               extra_specs=(pl.BlockSpec((tm, tn), lambda j, i: (i, j)),
                            pl.BlockSpec((None, 1, tn), lambda j, i: (i // tiles_per_batch, 0, j))),
               out_shapes=jax.ShapeDtypeStruct((m, n), F32),
               out_specs=pl.BlockSpec((tm, tn), lambda j, i: (i, j)), name=name,
               w_layer=w_layer, w_single_buffer=w_single_buffer)


def _mm_residual_norm(a_list, w, w_layer, x, gate, norm_g, sc, sh, tm, name):
    m, n = x.shape
    tiles_per_batch = SEQ // tm

    def per_batch(v):
        return v.reshape(BATCH, 1, n), pl.BlockSpec((None, 1, n), lambda j, i: (i // tiles_per_batch, 0, 0))

    (gate3, gate_spec), (sc3, sc_spec), (sh3, sh_spec) = per_batch(gate), per_batch(sc), per_batch(sh)
    rows = pl.BlockSpec((tm, n), lambda j, i: (i, 0))
    return _mm(a_list, w, tm=tm, tn=n, n_cols=n, epi=_epi_residual_norm,
               extras=(x, gate3, norm_g.reshape(1, n), sc3, sh3),
               extra_specs=(rows, gate_spec, pl.BlockSpec((1, n), lambda j, i: (0, 0)), sc_spec, sh_spec),
               out_shapes=(jax.ShapeDtypeStruct((m, n), F32), jax.ShapeDtypeStruct((m, n), BF16)),
               out_specs=(rows, rows), name=name, w_layer=w_layer, w_single_buffer=True)


def _row_groups(x):
    return [x[k:k + SUBLANES] for k in range(0, x.shape[0], SUBLANES)]


def _cumsum_groups(groups, sub):
    out, carry = [], None
    for g in groups:
        for s in (1, 2, 4):
            g = g + jnp.where(sub >= s, pltpu.roll(g, s, 0), 0.0)
        if carry is not None:
            g = g + carry
        out.append(g)
        carry = g[SUBLANES - 1:SUBLANES]
    return out


def _hold_boundary(cum, w, k, sub):
    shape = cum[k].shape
    if w >= SUBLANES // 2:
        r = (k * SUBLANES // (2 * w)) * 2 * w + w - 1
        return jnp.broadcast_to(cum[r // SUBLANES][r % SUBLANES:r % SUBLANES + 1], shape)
    assert w == 2
    return jnp.where(sub < 4, jnp.broadcast_to(cum[k][1:2], shape), jnp.broadcast_to(cum[k][5:6], shape))


def _scan_kernel(zf_ref, zb_ref, v_ref, q_ref, g_ref, u_ref, gb_ref, gc_ref, czf_ref, czb_ref, cv_ref,
                 lbl_ref, gn_ref, cw_ref, o_ref, oc_ref,
                 of_ref, ob_ref, st_ref, qs_ref, inc_ref, dec_ref, kl_ref, lv_ref, att_ref, *, layer):
    c = SCAN_CHUNK
    n_chunks = SEQ // c
    n_ctx = CTX_LEN // c
    z_refs = (zf_ref, zb_ref)
    cz_refs = (czf_ref, czb_ref)
    out_refs = (of_ref, ob_ref)

    logit_rows = [lbl_ref[:, k, :] for k in range(lbl_ref.shape[1])]
    top = functools.reduce(jnp.maximum, logit_rows)
    exps = [jnp.exp(r - top) for r in logit_rows]
    lbs = sum(exps[:layer + 1]) / sum(exps)

    sub = lax.broadcasted_iota(jnp.int32, (SUBLANES, A_HEAD_DIM), 0)
    n_groups = c // SUBLANES

    def rows(groups):
        return jnp.concatenate(groups, axis=0)

    def chunk(d, z, q, v, want_out):
        lb = lbs[d:d + 1]
        f = lb + (1.0 - lb) * jax.nn.sigmoid(z)
        kk = 1.0 - f
        fg, kg = _row_groups(f), _row_groups(kk)
        lg = [jnp.log2(g) for g in fg]
        cum = _cumsum_groups(lg, sub)
        total = cum[-1][SUBLANES - 1:SUBLANES]
        if d == 0:
            pos = cum
            q_in = [jnp.exp2(g) for g in cum]
            k_out = [jnp.exp2(total - g) for g in cum]
        else:
            pos = [a - b for a, b in zip(cum, lg)]
            q_in = [jnp.exp2(total - g) for g in pos]
            k_out = [jnp.exp2(g) for g in pos]
        k_leave = rows([a * b for a, b in zip(kg, k_out)]).astype(BF16)
        decay = jnp.exp2(total)
        if not want_out:
            return k_leave, decay, None, None, None
        qg = _row_groups(q)
        o_diag = jnp.sum(q * kk, axis=1, keepdims=True) * v
        upper, lower = (qg, kg) if d == 0 else (kg, qg)
        operands = []
        for w in SCAN_LEVELS:
            y = []
            for k in range(n_groups):
                if w == 1:
                    odd = (sub & 1) != 0
                    y.append(jnp.where(odd, qg[k] * fg[k], kg[k]) if d == 0
                             else jnp.where(odd, kg[k], qg[k] * fg[k]))
                    continue
                hold = _hold_boundary(cum, w, k, sub)
                if w >= SUBLANES:
                    bit_set = (k * SUBLANES) & w != 0
                    y.append(upper[k] * jnp.exp2(pos[k] - hold) if bit_set
                             else lower[k] * jnp.exp2(hold - pos[k]))
                else:
                    bit = (sub & w) != 0
                    sign = jnp.where(bit, 1.0, -1.0)
                    y.append(jnp.where(bit, upper[k], lower[k]) * jnp.exp2((pos[k] - hold) * sign))
            operands.append(rows(y).astype(BF16))
        qs = rows([a * b for a, b in zip(qg, q_in)]).astype(BF16)
        return k_leave, decay, qs, o_diag, operands

    def within_chunk(d, operands):
        t_idx = lax.broadcasted_iota(jnp.int32, (c, c), 0)
        s_idx = lax.broadcasted_iota(jnp.int32, (c, c), 1)
        split = t_idx ^ s_idx
        att = None
        for w, yb in reversed(list(zip(SCAN_LEVELS, operands))):
            scores = _dot_nt(yb, yb)
            att = scores if att is None else jnp.where(split < 2 * w, scores, att)
        return jnp.where(t_idx > s_idx if d == 0 else t_idx < s_idx, att, 0.0)

    st_ref[...] = jnp.zeros_like(st_ref)

    def ctx_body(i, carry):
        for d in range(2):
            ci = i if d == 0 else n_ctx - 1 - i
            rows_i = pl.ds(pl.multiple_of(ci * c, c), c)
            k_leave, decay, _, _, _ = chunk(d, cz_refs[d][rows_i, :], None, None, False)
            st_ref[d] = st_ref[d] * decay + _dot_tn(cv_ref[rows_i, :].astype(BF16), k_leave)
        return carry

    lax.fori_loop(0, n_ctx, ctx_body, 0, unroll=2)

    def chunk_rows(ci):
        start = ci * c
        return pl.ds(start if isinstance(ci, int) else pl.multiple_of(start, c), c)

    def chunk_of(d, t):
        return t if d == 0 else n_chunks - 1 - t

    def prepare(t):
        for d in range(2):
            ci = chunk_of(d, t)
            rows_i = chunk_rows(ci)
            k_leave, decay, qs, o_diag, operands = chunk(d, z_refs[d][rows_i, :], q_ref[rows_i, :],
                                                         v_ref[rows_i, :], True)
            kl_ref[d, rows_i, :] = k_leave
            dec_ref[d, ci] = decay
            qs_ref[d, rows_i, :] = qs
            out_refs[d][rows_i, :] = o_diag
            for lvl, yb in enumerate(operands):
                lv_ref[d, lvl, rows_i, :] = yb

    def scores(t):
        for d in range(2):
            ci = chunk_of(d, t)
            rows_i = chunk_rows(ci)
            inc_ref[d, ci] = _dot_tn(v_ref[rows_i, :].astype(BF16), kl_ref[d, rows_i, :])
            operands = [lv_ref[d, lvl, rows_i, :] for lvl in range(len(SCAN_LEVELS))]
            att_ref[d, rows_i, :] = within_chunk(d, operands).astype(BF16)

    def carry_state(t):
        for d in range(2):
            ci = chunk_of(d, t)
            rows_i = chunk_rows(ci)
            state = st_ref[d]
            out_refs[d][rows_i, :] += (_dot(att_ref[d, rows_i, :], v_ref[rows_i, :].astype(BF16))
                                       + _dot_nt(qs_ref[d, rows_i, :], state.astype(BF16)))
            st_ref[d] = state * dec_ref[d, ci] + inc_ref[d, ci]

    pair = 4
    n_trips = n_chunks // pair

    def stage(fn, trip):
        for j in range(pair):
            fn(trip * pair + j)

    stage(prepare, 0)
    stage(scores, 0)
    stage(prepare, 1)

    def skewed_body(i, carry):
        stage(carry_state, i - 2)
        stage(scores, i - 1)
        stage(prepare, i)
        return carry

    lax.fori_loop(2, n_trips, skewed_body, 0)
    stage(carry_state, n_trips - 2)
    stage(scores, n_trips - 1)
    stage(carry_state, n_trips - 1)

    rb = 256
    gn = gn_ref[...]
    cw = cw_ref[...]

    def read_body(i, carry):
        rows_i = pl.ds(pl.multiple_of(i * rb, rb), rb)
        o = of_ref[rows_i, :] + ob_ref[rows_i, :]
        ms = jnp.mean(o * o, axis=-1, keepdims=True)
        on = o * lax.rsqrt(ms + EPS) * gn
        g = g_ref[rows_i, :]
        o_ref[rows_i, :] = (on * (g * jax.nn.sigmoid(g))).astype(o_ref.dtype)
        conv = _conv3_rows(gc_ref[rows_i, :] * u_ref[rows_i, :], cw, GRID_W)
        oc_ref[rows_i, :] = (gb_ref[rows_i, :] * conv).astype(oc_ref.dtype)
        return carry

    lax.fori_loop(0, SEQ // rb, read_body, 0)


def _even_mixer(p, cp, lb_logits, gnorm_g, conv_w, layer):
    h = A_HEADS
    hd = A_HEAD_DIM
    n_lb = lb_logits.shape[1]

    def col(k):
        return pl.BlockSpec((SEQ, hd), lambda b, j, k=k: (b, k * h + j))

    def ccol(k):
        return pl.BlockSpec((CTX_LEN, hd), lambda b, j, k=k: (b, k * h + j))

    out_spec = pl.BlockSpec((SEQ, hd), lambda b, j: (b, j))
    return pl.pallas_call(
        functools.partial(_scan_kernel, layer=layer),
        grid=(BATCH, h),
        in_specs=[col(0), col(1), col(2), col(3), col(4), col(5), col(6), col(7),
                  ccol(0), ccol(1), ccol(2),
                  pl.BlockSpec((2, n_lb, hd), lambda b, j: (0, 0, j)),
                  pl.BlockSpec((1, hd), lambda b, j: (0, j)),
                  pl.BlockSpec((3, hd), lambda b, j: (0, j))],
        out_specs=(out_spec, out_spec),
        out_shape=(jax.ShapeDtypeStruct((TOKENS, A_WIDTH), BF16),
                   jax.ShapeDtypeStruct((TOKENS, B_WIDTH), BF16)),
        scratch_shapes=[pltpu.VMEM((SEQ, hd), F32), pltpu.VMEM((SEQ, hd), F32),
                        pltpu.VMEM((2, hd, hd), F32), pltpu.VMEM((2, SEQ, hd), BF16),
                        pltpu.VMEM((2, SEQ // SCAN_CHUNK, hd, hd), F32),
                        pltpu.VMEM((2, SEQ // SCAN_CHUNK, 1, hd), F32),
                        pltpu.VMEM((2, SEQ, hd), BF16),
                        pltpu.VMEM((2, len(SCAN_LEVELS), SEQ, hd), BF16),
                        pltpu.VMEM((2, SEQ, SCAN_CHUNK), BF16)],
        compiler_params=_params("arbitrary", "arbitrary"),
        name="hgrn_scan",
    )(p, p, p, p, p, p, p, p, cp, cp, cp, lb_logits, gnorm_g.reshape(1, A_WIDTH), conv_w)


def _split_bf16(x):
    hi = x.astype(BF16)
    return hi, (x - hi.astype(F32)).astype(BF16)


def _filter_kernel(z_ref, w1_ref, b1_ref, w2_ref, b2_ref, w3_ref, b3_ref, fr_ref,
                   w4f_ref, w4b_ref, cond_ref, aw_ref, ab_ref,
                   sum_ref, diff_ref, mod_ref, hid_ref, *, tc):
    o = pl.program_id(0)
    j = pl.program_id(1)

    cond = cond_ref[...]
    mod_ref[...] = _dot((cond * jax.nn.sigmoid(cond)).astype(BF16), aw_ref[...].astype(BF16)) + ab_ref[...]

    @pl.when((o == 0) & (j == 0))
    def _():
        fr = fr_ref[...]
        h = jnp.sin(fr * (_dot_f32(z_ref[...], w1_ref[...]) + b1_ref[...]))
        h = jnp.sin(fr * (_dot_f32(h, w2_ref[...]) + b2_ref[...]))
        h = jnp.sin(fr * (_dot_f32(h, w3_ref[...]) + b3_ref[...]))
        hi, lo = _split_bf16(h)
        hid_ref[...] = jnp.concatenate([hi, lo, hi], axis=1)

    def project(w_ref):
        w_hi, w_lo = _split_bf16(w_ref[...])
        zero = jnp.zeros_like(w_hi)
        first = _dot(hid_ref[...], jnp.concatenate([w_hi, zero, w_hi, zero, w_lo, zero], axis=0))
        second = _dot(hid_ref[...], jnp.concatenate([zero, w_hi, zero, w_hi, zero, w_lo], axis=0))
        return jnp.concatenate([first, second], axis=0)

    rows = lax.broadcasted_iota(jnp.int32, (SEQ, tc), 0)
    chan = lax.broadcasted_iota(jnp.int32, (SEQ, tc), 1) + j * tc
    lag = jnp.where(rows < HALF, 2 * rows, 2 * rows - (SEQ - 1))
    t = lag.astype(F32) * (1.0 / (SEQ - 1))
    max_decay = math.log(HY_DECAY_TARGET) / HY_FAST_PCT
    min_decay = math.log(HY_DECAY_TARGET) / HY_SLOW_PCT
    deltas = jnp.abs(min_decay + chan.astype(F32) * ((max_decay - min_decay) / (D_MODEL - 1)))
    window = jnp.exp(-t * deltas)
    fw = project(w4f_ref) * window
    bw = project(w4b_ref) * window
    first = rows == 0
    a = fw + jnp.where(first, bw, 0.0)
    bb = jnp.where(first, 0.0, bw)
    inv = 1.0 / jnp.sum(jnp.abs(a) + jnp.abs(bb), axis=0, keepdims=True)
    sum_ref[...] = ((a + bb) * inv).astype(sum_ref.dtype)
    diff_ref[...] = ((a - bb) * inv).astype(diff_ref.dtype)


@functools.lru_cache(maxsize=None)
def _filter_positions():
    l = SEQ
    t = np.linspace(0.0, 1.0, l, dtype=np.float32)
    w = (2.0 * math.pi * np.arange(l, dtype=np.float32) / l).astype(np.float32)
    bands = np.linspace(1e-4, HY_BANDS - 1, HY_BANDS, dtype=np.float32)
    ang = w[:, None] * bands[None, :]
    z = np.concatenate([t[:, None], np.cos(ang), -np.sin(ang)], axis=-1).astype(np.float32)
    z = np.concatenate([z[0::2], z[1::2]], axis=0)
    z = np.pad(z, ((0, 0), (0, LANES // 2 - HY_EMB)))
    return np.concatenate([z[:l // 2], z[l // 2:]], axis=1)


def _filter_taps_and_modulation(fw1, fb1, fw2, fb2, fw3, fb3, fw4, freq, cond, ada_w, ada_b):
    l = SEQ
    tc = 256
    nj = D_MODEL // tc
    depth, d, n_mod = ada_w.shape
    assert depth == 2 and n_mod % (nj * LANES) == 0
    ta = n_mod // nj
    hw = LANES // 2

    def pad(a, rows, cols):
        return jnp.pad(a, ((0, rows - a.shape[0]), (0, cols - a.shape[1])))

    def twice(a):
        a = pad(a, hw, hw)
        zero = jnp.zeros_like(a)
        return jnp.concatenate([jnp.concatenate([a, zero], axis=1), jnp.concatenate([zero, a], axis=1)], axis=0)

    def row2(v):
        v = pad(v[None, :], 1, hw)
        return jnp.concatenate([v, v], axis=1)

    def small(shape):
        return pl.BlockSpec(shape, lambda o, j: (0, 0))

    out_sds = jax.ShapeDtypeStruct((l, 2 * D_MODEL), BF16)
    out_spec = pl.BlockSpec((l, tc), lambda o, j: (0, o * nj + j))
    assert fw4.shape[0] == hw
    side_f = pl.BlockSpec((hw, tc), lambda o, j: (0, 2 * o * nj + j))
    side_b = pl.BlockSpec((hw, tc), lambda o, j: (0, (2 * o + 1) * nj + j))
    sq, vec = small((LANES, LANES)), small((1, LANES))
    return pl.pallas_call(
        functools.partial(_filter_kernel, tc=tc),
        grid=(2, nj),
        in_specs=[small((l // 2, LANES)), sq, vec, sq, vec, sq, vec, vec, side_f, side_b,
                  small((8, d)),
                  pl.BlockSpec((None, d, ta), lambda o, j: (o, 0, j)),
                  pl.BlockSpec((None, 1, ta), lambda o, j: (o, 0, j))],
        out_specs=(out_spec, out_spec, pl.BlockSpec((None, 8, ta), lambda o, j: (o, 0, j))),
        out_shape=(out_sds, out_sds, jax.ShapeDtypeStruct((depth, 8, n_mod), F32)),
        scratch_shapes=[pltpu.VMEM((l // 2, 3 * LANES), BF16)],
        compiler_params=_params("arbitrary", "arbitrary"),
        name="hyena_filter",
    )(jnp.asarray(_filter_positions()), twice(fw1), row2(fb1), twice(fw2), row2(fb2), twice(fw3), row2(fb3),
      row2(freq), fw4, fw4, cond, ada_w, ada_b.reshape(depth, 1, n_mod))


def _trig_table(a_rows, b_cols, scale):
    n = (a_rows[:, None].astype(np.int64) * b_cols[None, :].astype(np.int64)) % (4 * SEQ)
    ang = n.astype(np.float64) * (math.pi / (2 * SEQ))
    return (np.cos(ang) * scale).astype(np.float32), (np.sin(ang) * scale).astype(np.float32)


@functools.lru_cache(maxsize=None)
def _dft_tables():
    idx = np.arange(QUARTER)
    ana_c, ana_s, syn_c, syn_s = [], [], [], []
    for r in range(4):
        c, s = _trig_table(2 * idx + 1, 4 * idx + r, 1.0)
        ana_c.append(c)
        ana_s.append(s)
        c, s = _trig_table(4 * idx + r, 2 * idx + 1, 1.0 / SEQ)
        syn_c.append(c)
        syn_s.append(s)
    fmap = np.concatenate([idx, HALF + idx])
    lag = np.arange(HALF)
    cee, see = _trig_table(2 * fmap + 1, 2 * lag, 1.0)
    ceo, seo = _trig_table(2 * fmap + 1, 2 * lag + 1, 1.0)
    return np.stack(ana_c + ana_s), np.stack(syn_c + syn_s), (cee, ceo, see, seo)


def _filter_dft_kernel(cee_ref, ceo_ref, see_ref, seo_ref, fs_ref, fd_ref, p_ref, pm_ref, q_ref, qm_ref):
    pe = _dot(cee_ref[...], fs_ref[0:HALF, :])
    po = _dot(ceo_ref[...], fs_ref[HALF:SEQ, :])
    qe = _dot(see_ref[...], fd_ref[0:HALF, :])
    qo = _dot(seo_ref[...], fd_ref[HALF:SEQ, :])
    p_ref[...] = pe + po
    pm_ref[...] = pe - po
    q_ref[...] = qe + qo
    qm_ref[...] = qo - qe


def _filter_dft(tables, taps_sum, taps_diff):
    tm, tn = HALF, 256
    n_cols = taps_sum.shape[1]
    mat = pl.BlockSpec((tm, HALF), lambda j, i: (i, 0), pipeline_mode=pl.Buffered(1))
    tap = pl.BlockSpec((SEQ, tn), lambda j, i: (0, j))
    out_spec = pl.BlockSpec((tm, tn), lambda j, i: (i, j))
    out_sds = jax.ShapeDtypeStruct((HALF, n_cols), F32)
    return pl.pallas_call(
        _filter_dft_kernel,
        grid=(n_cols // tn, HALF // tm),
        in_specs=[mat, mat, mat, mat, tap, tap],
        out_specs=(out_spec,) * 4,
        out_shape=(out_sds,) * 4,
        compiler_params=_params("arbitrary", "arbitrary"),
        name="filter_dft",
    )(*tables, taps_sum, taps_diff)


def _dft4_fwd_kernel(tab_ref, z_ref, p_ref, pm_ref, q_ref, qm_ref, x_ref, zs_ref, *, slab):
    panels = zs_ref.shape[0]
    for k in range(panels):
        zs_ref[k] = z_ref[:, k * LANES:(k + 1) * LANES].astype(F32)
    zr = [jnp.concatenate([zs_ref[k, pl.ds(r, QUARTER, stride=4), :] for k in range(panels)],
                          axis=1).astype(BF16) for r in range(4)]

    for f0 in range(0, QUARTER, slab):
        rows = slice(f0, f0 + slab)
        mirror = slice(QUARTER + f0, QUARTER + f0 + slab)
        c = [_dot(tab_ref[r, rows, :], zr[r]) for r in range(4)]
        s = [_dot(tab_ref[4 + r, rows, :], zr[r]) for r in range(4)]
        e, o, em, om = c[0] + c[2], c[1] + c[3], c[0] - c[2], c[1] - c[3]
        se, so, sem, som = s[0] + s[2], s[1] + s[3], s[0] - s[2], s[1] - s[3]
        a = (e + o, em + som, em - som, e - o)
        b = (se + so, om - sem, sem + om, so - se)
        pk = (p_ref[rows, :], pm_ref[mirror, :], p_ref[mirror, :], pm_ref[rows, :])
        qk = (q_ref[rows, :], qm_ref[mirror, :], q_ref[mirror, :], qm_ref[rows, :])
        u = [a[k] * pk[k] - b[k] * qk[k] for k in range(4)]
        v = [a[k] * qk[k] + b[k] * pk[k] for k in range(4)]
        up, um, wp, wm = u[0] + u[3], u[0] - u[3], u[1] + u[2], u[1] - u[2]
        vp, vm, yp, ym = v[0] + v[3], v[0] - v[3], v[1] + v[2], v[1] - v[2]
        outs = (up + wp, um + yp, up - wp, um - yp,
                vm - ym, vp + wm, vm + ym, vp - wm)
        for n, val in enumerate(outs):
            x_ref[n, rows, :] = val.astype(x_ref.dtype)


def _dft4_fwd(tab, z_arr, z_col0, spectra, order):
    tn, slab = 512, 256
    nj = D_MODEL // tn
    zc = z_col0 // tn
    spec = pl.BlockSpec((HALF, tn), lambda j, b: (0, order * nj + j))
    return pl.pallas_call(
        functools.partial(_dft4_fwd_kernel, slab=slab),
        grid=(nj, BATCH),
        in_specs=[pl.BlockSpec(tab.shape, lambda j, b: (0, 0, 0), pipeline_mode=pl.Buffered(1)),
                  pl.BlockSpec((SEQ, tn), lambda j, b: (b, zc + j)), spec, spec, spec, spec],
        out_specs=pl.BlockSpec((8, QUARTER, tn), lambda j, b: (0, b, j)),
        out_shape=jax.ShapeDtypeStruct((8, BATCH * QUARTER, D_MODEL), BF16),
        scratch_shapes=[pltpu.VMEM((tn // LANES, SEQ, LANES), F32)],
        compiler_params=_params("arbitrary", "arbitrary"),
        name="dft_analysis",
    )(tab, z_arr, *spectra)


def _dft4_inv_kernel(tab_ref, x_ref, z_ref, gate_ref, skip_ref, o_ref, ys_ref):
    panels = ys_ref.shape[0]
    for r in range(4):
        y_r = _dot(tab_ref[r], x_ref[r]) + _dot(tab_ref[4 + r], x_ref[4 + r])
        for k in range(panels):
            ys_ref[k, pl.ds(r, QUARTER, stride=4), :] = y_r[:, k * LANES:(k + 1) * LANES]
    y = jnp.concatenate([ys_ref[k] for k in range(panels)], axis=1)
    y = y + z_ref[...].astype(F32) * skip_ref[...]
    o_ref[...] = (gate_ref[...].astype(F32) * y).astype(o_ref.dtype)


def _dft4_inv(tab, x, z_arr, z_col0, gate_arr, gate_col0, skip):
    tn = 512
    nj = D_MODEL // tn
    zc, gc = z_col0 // tn, gate_col0 // tn
    return pl.pallas_call(
        _dft4_inv_kernel,
        grid=(nj, BATCH),
        in_specs=[pl.BlockSpec(tab.shape, lambda j, b: (0, 0, 0), pipeline_mode=pl.Buffered(1)),
                  pl.BlockSpec((8, QUARTER, tn), lambda j, b: (0, b, j)),
                  pl.BlockSpec((SEQ, tn), lambda j, b: (b, zc + j)),
                  pl.BlockSpec((SEQ, tn), lambda j, b: (b, gc + j)),
                  pl.BlockSpec((1, tn), lambda j, b: (0, j))],
        out_specs=pl.BlockSpec((SEQ, tn), lambda j, b: (b, j)),
        out_shape=jax.ShapeDtypeStruct((TOKENS, D_MODEL), BF16),
        scratch_shapes=[pltpu.VMEM((tn // LANES, SEQ, LANES), F32)],
        compiler_params=_params("arbitrary", "arbitrary"),
        name="dft_synthesis",
    )(tab, x, z_arr, gate_arr, skip.reshape(1, D_MODEL))


def _mod_chunk(mod, l, k):
    return mod[l, :BATCH, k * D_MODEL:(k + 1) * D_MODEL]


def _mlp_block(x2d, h, mod, l, w1, w2):
    tm = tn = 1024
    n_i, n_j = TOKENS // tm, D_FF // tn
    slab = D_FF // (n_i * n_j)
    hid, w2_bf16 = _mm(
        [h], w1, w_layer=l, tm=tm, tn=tn, n_cols=D_FF, epi=_epi_relu2_and_cast,
        extras=(w2,), extra_specs=(pl.BlockSpec((None, slab, D_MODEL), lambda j, i: (l, j * n_i + i, 0)),),
        out_shapes=(jax.ShapeDtypeStruct((TOKENS, D_FF), BF16), jax.ShapeDtypeStruct((D_FF, D_MODEL), BF16)),
        out_specs=(pl.BlockSpec((tm, tn), lambda j, i: (i, j)),
                   pl.BlockSpec((slab, D_MODEL), lambda j, i: (j * n_i + i, 0))),
        name="mlp_up")
    return _mm_residual([hid], w2_bf16, None, x2d, _mod_chunk(mod, l, 5), 512, 512, "mlp_down")


def kernel(x, c, ctx, c_ctx, ada_w, ada_b, norm_g, lb_logits, ab_w_in, ab_conv_w, ab_gnorm_g, ab_w_out,
           hy_in_w, hy_short_w, hy_out_w, hy_fw1, hy_fb1, hy_fw2, hy_fb2, hy_fw3, hy_fb3, hy_fw4, hy_freq,
           hy_skip, mlp_w1, mlp_w2, final_g):
    d = D_MODEL
    cond = jnp.concatenate([c, c_ctx[None, :], jnp.zeros((3, d), F32)], axis=0)
    taps_sum, taps_diff, mod = _filter_taps_and_modulation(
        hy_fw1[0], hy_fb1[0], hy_fw2[0], hy_fb2[0], hy_fw3[0], hy_fb3[0], hy_fw4[0], hy_freq[0],
        cond, ada_w, ada_b)
    x2d = x.reshape(TOKENS, d)

    sh1, sc1, g1 = (mod[0, :BATCH, k * d:(k + 1) * d] for k in (0, 1, 2))
    h = _prep(x, norm_g[0, 0], sc1, sh1, BF16, 1024).reshape(TOKENS, d)
    csh = jnp.broadcast_to(mod[0, BATCH:BATCH + 1, 0:d], (BATCH, d))
    csc = jnp.broadcast_to(mod[0, BATCH:BATCH + 1, d:2 * d], (BATCH, d))
    hc = _prep(ctx, norm_g[0, 0], csc, csh, BF16, CTX_LEN).reshape(BATCH * CTX_LEN, d)
    p = _mm_plain(h, ab_w_in, AB_IN_WIDTH, F32, _epi_store, 1024, 1024, "ab_in_proj", w_layer=0)
    cp = _mm_plain(hc, ab_w_in, 3 * A_WIDTH, F32, _epi_store, 1024, 1024, "ab_ctx_proj", w_layer=0)
    mix_a, mix_b = _even_mixer(p, cp, lb_logits, ab_gnorm_g[0], ab_conv_w[0], 0)
    x2d, h = _mm_residual_norm([mix_a, mix_b], ab_w_out, 0, x2d, g1, norm_g[0, 1], _mod_chunk(mod, 0, 4),
                               _mod_chunk(mod, 0, 3), 512, "ab_out_proj")
    x2d = _mlp_block(x2d, h, mod, 0, mlp_w1, mlp_w2)

    sh1, sc1, g1 = (mod[1, :BATCH, k * d:(k + 1) * d] for k in (0, 1, 2))
    h = _prep(x2d.reshape(BATCH, SEQ, d), norm_g[1, 0], sc1, sh1, BF16, 1024).reshape(TOKENS, d)
    pc = _mm([h], hy_in_w, w_layer=0, tm=1024, tn=1024, n_cols=3 * d, epi=_epi_conv3_grid,
             extras=(hy_short_w[0],), extra_specs=(pl.BlockSpec((3, 1024), lambda j, i: (0, j)),),
             out_shapes=jax.ShapeDtypeStruct((TOKENS, 3 * d), BF16),
             out_specs=pl.BlockSpec((1024, 1024), lambda j, i: (i, j)), name="hy_in_proj")
    analysis, synthesis, filter_tables = jax.tree.map(lambda t: jnp.asarray(t).astype(BF16), _dft_tables())
    spectra = _filter_dft(filter_tables, taps_sum, taps_diff)
    coef = _dft4_fwd(analysis, pc, 2 * d, spectra, 0)
    z = _dft4_inv(synthesis, coef, pc, 2 * d, pc, 0, hy_skip[0, 0])
    coef = _dft4_fwd(analysis, z, 0, spectra, 1)
    z = _dft4_inv(synthesis, coef, z, 0, pc, d, hy_skip[0, 1])
    x2d, h = _mm_residual_norm([z], hy_out_w, 0, x2d, g1, norm_g[1, 1], _mod_chunk(mod, 1, 4),
                               _mod_chunk(mod, 1, 3), 512, "hy_out_proj")
    x2d = _mlp_block(x2d, h, mod, 1, mlp_w1, mlp_w2)

    zeros = jnp.zeros((BATCH, d), F32)
    return _prep(x2d.reshape(BATCH, SEQ, d), final_g, zeros, zeros, F32, 1024)
```

```python
import functools
import math

import numpy as np
import jax
import jax.numpy as jnp
from jax import lax
from jax.experimental import pallas as pl
from jax.experimental.pallas import tpu as pltpu

F32 = jnp.float32
BF16 = jnp.bfloat16

D_MODEL = 2048
BATCH = 4
SEQ = 2048
CTX_LEN = 256
GRID_W = 64
EPS = 1e-6
A_HEAD_DIM = 128
A_WIDTH = D_MODEL // 2
A_HEADS = A_WIDTH // A_HEAD_DIM
B_WIDTH = D_MODEL - A_WIDTH
AB_IN_WIDTH = 5 * A_WIDTH + 3 * B_WIDTH
HY_EMB = 33
HY_BANDS = (HY_EMB - 1) // 2
HY_DECAY_TARGET = 1e-2
HY_FAST_PCT = 0.3
HY_SLOW_PCT = 1.5
D_FF = 4 * D_MODEL
TOKENS = BATCH * SEQ
HALF = SEQ // 2
QUARTER = SEQ // 4

SCAN_CHUNK = 64
SCAN_LEVELS = (1, 2, 4, 8, 16, 32)

SUBLANES = 8
LANES = 128

VMEM_LIMIT_BYTES = 56 * 1024 * 1024


def _params(*sem):
    return pltpu.CompilerParams(dimension_semantics=sem, vmem_limit_bytes=VMEM_LIMIT_BYTES)


def _dot(a, b):
    return jnp.dot(a, b, preferred_element_type=F32)


def _dot_nt(a, b):
    return lax.dot_general(a, b, (((1,), (1,)), ((), ())), preferred_element_type=F32)


def _dot_tn(a, b):
    return lax.dot_general(a, b, (((0,), (0,)), ((), ())), preferred_element_type=F32)


def _dot_f32(a, b):
    return jnp.dot(a, b, precision=lax.Precision.HIGHEST, preferred_element_type=F32)


def _conv3_rows(t, w, group):
    rows = lax.broadcasted_iota(jnp.int32, t.shape, 0) % group
    prev = jnp.where(rows == 0, 0.0, pltpu.roll(t, 1, 0))
    nxt = jnp.where(rows == group - 1, 0.0, pltpu.roll(t, t.shape[0] - 1, 0))
    return w[0:1] * prev + w[1:2] * t + w[2:3] * nxt


def _norm_mod_rows(x_ref, g_ref, sc_ref, sh_ref, o_ref):
    scale = g_ref[...] * (1.0 + sc_ref[...])
    shift = sh_ref[...]
    piece = 64

    def body(i, carry):
        rows = pl.ds(pl.multiple_of(i * piece, piece), piece)
        x = x_ref[rows, :]
        ms = jnp.mean(x * x, axis=-1, keepdims=True)
        o_ref[rows, :] = (x * lax.rsqrt(ms + EPS) * scale + shift).astype(o_ref.dtype)
        return carry

    lax.fori_loop(0, x_ref.shape[0] // piece, body, 0, unroll=4)


def _prep_kernel(x_ref, g_ref, sc_ref, sh_ref, o_ref):
    _norm_mod_rows(x_ref, g_ref, sc_ref, sh_ref, o_ref)


def _prep(x, g, sc, sh, out_dtype, ts):
    b, l, d = x.shape
    return pl.pallas_call(
        _prep_kernel,
        grid=(b, l // ts),
        in_specs=[
            pl.BlockSpec((None, ts, d), lambda i, j: (i, j, 0)),
            pl.BlockSpec((1, d), lambda i, j: (0, 0)),
            pl.BlockSpec((None, 1, d), lambda i, j: (i, 0, 0)),
            pl.BlockSpec((None, 1, d), lambda i, j: (i, 0, 0)),
        ],
        out_specs=pl.BlockSpec((None, ts, d), lambda i, j: (i, j, 0)),
        out_shape=jax.ShapeDtypeStruct((b, l, d), out_dtype),
        compiler_params=_params("arbitrary", "arbitrary"),
        name="norm_mod",
    )(x, g.reshape(1, d), sc.reshape(b, 1, d), sh.reshape(b, 1, d))


def _mm_kernel(*refs, n_a, k_sizes, epi, cast_w):
    a_refs = refs[:n_a]
    w_ref = refs[n_a]
    rest = refs[n_a + 1:]
    if cast_w:
        w_ref, rest = rest[-1], rest[:-1]

        @pl.when(pl.program_id(1) == 0)
        def _():
            w_ref[...] = refs[n_a][...].astype(BF16)

    acc = None
    off = 0
    for a_ref, k in zip(a_refs, k_sizes):
        part = _dot(a_ref[...], w_ref[off:off + k, :])
        acc = part if acc is None else acc + part
        off += k
    epi(acc, *rest)


def _mm(a_list, w, *, tm, tn, n_cols, epi, extras, extra_specs, out_shapes, out_specs, name,
        w_layer=None, w_single_buffer=False):
    m = a_list[0].shape[0]
    k_sizes = tuple(a.shape[1] for a in a_list)
    k_total = sum(k_sizes)
    cast_w = w_layer is not None
    assert w.shape[-2] == k_total and m % tm == 0 and n_cols % tn == 0
    in_specs = [pl.BlockSpec((tm, k), lambda j, i: (i, 0)) for k in k_sizes]
    w_mode = dict(pipeline_mode=pl.Buffered(1)) if w_single_buffer else {}
    if cast_w:
        in_specs.append(pl.BlockSpec((None, k_total, tn), lambda j, i: (w_layer, 0, j), **w_mode))
    else:
        in_specs.append(pl.BlockSpec((k_total, tn), lambda j, i: (0, j), **w_mode))
    in_specs.extend(extra_specs)
    return pl.pallas_call(
        functools.partial(_mm_kernel, n_a=len(a_list), k_sizes=k_sizes, epi=epi, cast_w=cast_w),
        grid=(n_cols // tn, m // tm),
        in_specs=in_specs,
        out_specs=out_specs,
        out_shape=out_shapes,
        scratch_shapes=[pltpu.VMEM((k_total, tn), BF16)] if cast_w else [],
        compiler_params=_params("arbitrary", "arbitrary"),
        name=name,
    )(*a_list, w, *extras)


def _epi_store(acc, o_ref):
    o_ref[...] = acc.astype(o_ref.dtype)


def _epi_relu2_and_cast(acc, w_next_ref, o_ref, w_next_bf16_ref):
    r = jnp.maximum(acc, 0.0)
    o_ref[...] = (r * r).astype(o_ref.dtype)
    w_next_bf16_ref[...] = w_next_ref[...].astype(w_next_bf16_ref.dtype)


def _epi_residual(acc, x_ref, gate_ref, o_ref):
    o_ref[...] = x_ref[...] + gate_ref[...] * acc


def _epi_residual_norm(acc, x_ref, gate_ref, g_ref, sc_ref, sh_ref, o_ref, h_ref):
    x = x_ref[...] + gate_ref[...] * acc
    o_ref[...] = x
    ms = jnp.mean(x * x, axis=-1, keepdims=True)
    xn = x * lax.rsqrt(ms + EPS) * g_ref[...]
    h_ref[...] = (xn * (1.0 + sc_ref[...]) + sh_ref[...]).astype(h_ref.dtype)


def _epi_conv3_grid(acc, w_ref, o_ref):
    o_ref[...] = _conv3_rows(acc, w_ref[...], GRID_W).astype(o_ref.dtype)


def _mm_plain(a, w, n_cols, out_dtype, epi, tm, tn, name, w_layer=None):
    m = a.shape[0]
    return _mm([a], w, tm=tm, tn=tn, n_cols=n_cols, epi=epi, extras=(), extra_specs=(),
               out_shapes=jax.ShapeDtypeStruct((m, n_cols), out_dtype),
               out_specs=pl.BlockSpec((tm, tn), lambda j, i: (i, j)), name=name, w_layer=w_layer)


def _mm_residual(a_list, w, w_layer, x, gate, tm, tn, name, w_single_buffer=False):
    m, n = x.shape
    tiles_per_batch = SEQ // tm
    return _mm(a_list, w, tm=tm, tn=tn, n_cols=n, epi=_epi_residual,
               extras=(x, gate.reshape(BATCH, 1, n)),
               extra_specs=(pl.BlockSpec((tm, tn), lambda j, i: (i, j)),
                            pl.BlockSpec((None, 1, tn), lambda j, i: (i // tiles_per_batch, 0, j))),
               out_shapes=jax.ShapeDtypeStruct((m, n), F32),
               out_specs=pl.BlockSpec((tm, tn), lambda j, i: (i, j)), name=name,
               w_layer=w_layer, w_single_buffer=w_single_buffer)


def _mm_residual_norm(a_list, w, w_layer, x, gate, norm_g, sc, sh, tm, name):
    m, n = x.shape
    tiles_per_batch = SEQ // tm

    def per_batch(v):
        return v.reshape(BATCH, 1, n), pl.BlockSpec((None, 1, n), lambda j, i: (i // tiles_per_batch, 0, 0))

    (gate3, gate_spec), (sc3, sc_spec), (sh3, sh_spec) = per_batch(gate), per_batch(sc), per_batch(sh)
    rows = pl.BlockSpec((tm, n), lambda j, i: (i, 0))
    return _mm(a_list, w, tm=tm, tn=n, n_cols=n, epi=_epi_residual_norm,
               extras=(x, gate3, norm_g.reshape(1, n), sc3, sh3),
               extra_specs=(rows, gate_spec, pl.BlockSpec((1, n), lambda j, i: (0, 0)), sc_spec, sh_spec),
               out_shapes=(jax.ShapeDtypeStruct((m, n), F32), jax.ShapeDtypeStruct((m, n), BF16)),
               out_specs=(rows, rows), name=name, w_layer=w_layer, w_single_buffer=True)


def _row_groups(x):
    return [x[k:k + SUBLANES] for k in range(0, x.shape[0], SUBLANES)]


def _cumsum_groups(groups, sub):
    out, carry = [], None
    for g in groups:
        for s in (1, 2, 4):
            g = g + jnp.where(sub >= s, pltpu.roll(g, s, 0), 0.0)
        if carry is not None:
            g = g + carry
        out.append(g)
        carry = g[SUBLANES - 1:SUBLANES]
    return out


def _hold_boundary(cum, w, k, sub):
    shape = cum[k].shape
    if w >= SUBLANES // 2:
        r = (k * SUBLANES // (2 * w)) * 2 * w + w - 1
        return jnp.broadcast_to(cum[r // SUBLANES][r % SUBLANES:r % SUBLANES + 1], shape)
    assert w == 2
    return jnp.where(sub < 4, jnp.broadcast_to(cum[k][1:2], shape), jnp.broadcast_to(cum[k][5:6], shape))


def _scan_kernel(zf_ref, zb_ref, v_ref, q_ref, g_ref, u_ref, gb_ref, gc_ref, czf_ref, czb_ref, cv_ref,
                 lbl_ref, gn_ref, cw_ref, o_ref, oc_ref,
                 of_ref, ob_ref, st_ref, qs_ref, inc_ref, dec_ref, kl_ref, lv_ref, att_ref, *, layer):
    c = SCAN_CHUNK
    n_chunks = SEQ // c
    n_ctx = CTX_LEN // c
    z_refs = (zf_ref, zb_ref)
    cz_refs = (czf_ref, czb_ref)
    out_refs = (of_ref, ob_ref)

    logit_rows = [lbl_ref[:, k, :] for k in range(lbl_ref.shape[1])]
    top = functools.reduce(jnp.maximum, logit_rows)
    exps = [jnp.exp(r - top) for r in logit_rows]
    lbs = sum(exps[:layer + 1]) / sum(exps)

    sub = lax.broadcasted_iota(jnp.int32, (SUBLANES, A_HEAD_DIM), 0)
    n_groups = c // SUBLANES

    def rows(groups):
        return jnp.concatenate(groups, axis=0)

    def chunk(d, z, q, v, want_out):
        lb = lbs[d:d + 1]
        f = lb + (1.0 - lb) * jax.nn.sigmoid(z)
        kk = 1.0 - f
        fg, kg = _row_groups(f), _row_groups(kk)
        lg = [jnp.log2(g) for g in fg]
        cum = _cumsum_groups(lg, sub)
        total = cum[-1][SUBLANES - 1:SUBLANES]
        if d == 0:
            pos = cum
            q_in = [jnp.exp2(g) for g in cum]
            k_out = [jnp.exp2(total - g) for g in cum]
        else:
            pos = [a - b for a, b in zip(cum, lg)]
            q_in = [jnp.exp2(total - g) for g in pos]
            k_out = [jnp.exp2(g) for g in pos]
        k_leave = rows([a * b for a, b in zip(kg, k_out)]).astype(BF16)
        decay = jnp.exp2(total)
        if not want_out:
            return k_leave, decay, None, None, None
        qg = _row_groups(q)
        o_diag = jnp.sum(q * kk, axis=1, keepdims=True) * v
        upper, lower = (qg, kg) if d == 0 else (kg, qg)
        operands = []
        for w in SCAN_LEVELS:
            y = []
            for k in range(n_groups):
                if w == 1:
                    odd = (sub & 1) != 0
                    y.append(jnp.where(odd, qg[k] * fg[k], kg[k]) if d == 0
                             else jnp.where(odd, kg[k], qg[k] * fg[k]))
                    continue
                hold = _hold_boundary(cum, w, k, sub)
                if w >= SUBLANES:
                    bit_set = (k * SUBLANES) & w != 0
                    y.append(upper[k] * jnp.exp2(pos[k] - hold) if bit_set
                             else lower[k] * jnp.exp2(hold - pos[k]))
                else:
                    bit = (sub & w) != 0
                    sign = jnp.where(bit, 1.0, -1.0)
                    y.append(jnp.where(bit, upper[k], lower[k]) * jnp.exp2((pos[k] - hold) * sign))
            operands.append(rows(y).astype(BF16))
        qs = rows([a * b for a, b in zip(qg, q_in)]).astype(BF16)
        return k_leave, decay, qs, o_diag, operands

    def within_chunk(d, operands):
        t_idx = lax.broadcasted_iota(jnp.int32, (c, c), 0)
        s_idx = lax.broadcasted_iota(jnp.int32, (c, c), 1)
        split = t_idx ^ s_idx
        att = None
        for w, yb in reversed(list(zip(SCAN_LEVELS, operands))):
            scores = _dot_nt(yb, yb)
            att = scores if att is None else jnp.where(split < 2 * w, scores, att)
        return jnp.where(t_idx > s_idx if d == 0 else t_idx < s_idx, att, 0.0)

    st_ref[...] = jnp.zeros_like(st_ref)

    def ctx_body(i, carry):
        for d in range(2):
            ci = i if d == 0 else n_ctx - 1 - i
            rows_i = pl.ds(pl.multiple_of(ci * c, c), c)
            k_leave, decay, _, _, _ = chunk(d, cz_refs[d][rows_i, :], None, None, False)
            st_ref[d] = st_ref[d] * decay + _dot_tn(cv_ref[rows_i, :].astype(BF16), k_leave)
        return carry

    lax.fori_loop(0, n_ctx, ctx_body, 0, unroll=2)

    def chunk_rows(ci):
        start = ci * c
        return pl.ds(start if isinstance(ci, int) else pl.multiple_of(start, c), c)

    def chunk_of(d, t):
        return t if d == 0 else n_chunks - 1 - t

    def prepare(t):
        for d in range(2):
            ci = chunk_of(d, t)
            rows_i = chunk_rows(ci)
            k_leave, decay, qs, o_diag, operands = chunk(d, z_refs[d][rows_i, :], q_ref[rows_i, :],
                                                         v_ref[rows_i, :], True)
            kl_ref[d, rows_i, :] = k_leave
            dec_ref[d, ci] = decay
            qs_ref[d, rows_i, :] = qs
            out_refs[d][rows_i, :] = o_diag
            for lvl, yb in enumerate(operands):
                lv_ref[d, lvl, rows_i, :] = yb

    def scores(t):
        for d in range(2):
            ci = chunk_of(d, t)
            rows_i = chunk_rows(ci)
            inc_ref[d, ci] = _dot_tn(v_ref[rows_i, :].astype(BF16), kl_ref[d, rows_i, :])
            operands = [lv_ref[d, lvl, rows_i, :] for lvl in range(len(SCAN_LEVELS))]
            att_ref[d, rows_i, :] = within_chunk(d, operands).astype(BF16)

    def carry_state(t):
        for d in range(2):
            ci = chunk_of(d, t)
            rows_i = chunk_rows(ci)
            state = st_ref[d]
            out_refs[d][rows_i, :] += (_dot(att_ref[d, rows_i, :], v_ref[rows_i, :].astype(BF16))
                                       + _dot_nt(qs_ref[d, rows_i, :], state.astype(BF16)))
            st_ref[d] = state * dec_ref[d, ci] + inc_ref[d, ci]

    pair = 4
    n_trips = n_chunks // pair

    def stage(fn, trip):
        for j in range(pair):
            fn(trip * pair + j)

    stage(prepare, 0)
    stage(scores, 0)
    stage(prepare, 1)

    def skewed_body(i, carry):
        stage(carry_state, i - 2)
        stage(scores, i - 1)
        stage(prepare, i)
        return carry

    lax.fori_loop(2, n_trips, skewed_body, 0)
    stage(carry_state, n_trips - 2)
    stage(scores, n_trips - 1)
    stage(carry_state, n_trips - 1)

    rb = 256
    gn = gn_ref[...]
    cw = cw_ref[...]

    def read_body(i, carry):
        rows_i = pl.ds(pl.multiple_of(i * rb, rb), rb)
        o = of_ref[rows_i, :] + ob_ref[rows_i, :]
        ms = jnp.mean(o * o, axis=-1, keepdims=True)
        on = o * lax.rsqrt(ms + EPS) * gn
        g = g_ref[rows_i, :]
        o_ref[rows_i, :] = (on * (g * jax.nn.sigmoid(g))).astype(o_ref.dtype)
        conv = _conv3_rows(gc_ref[rows_i, :] * u_ref[rows_i, :], cw, GRID_W)
        oc_ref[rows_i, :] = (gb_ref[rows_i, :] * conv).astype(oc_ref.dtype)
        return carry

    lax.fori_loop(0, SEQ // rb, read_body, 0)


def _even_mixer(p, cp, lb_logits, gnorm_g, conv_w, layer):
    h = A_HEADS
    hd = A_HEAD_DIM
    n_lb = lb_logits.shape[1]

    def col(k):
        return pl.BlockSpec((SEQ, hd), lambda b, j, k=k: (b, k * h + j))

    def ccol(k):
        return pl.BlockSpec((CTX_LEN, hd), lambda b, j, k=k: (b, k * h + j))

    out_spec = pl.BlockSpec((SEQ, hd), lambda b, j: (b, j))
    return pl.pallas_call(
        functools.partial(_scan_kernel, layer=layer),
        grid=(BATCH, h),
        in_specs=[col(0), col(1), col(2), col(3), col(4), col(5), col(6), col(7),
                  ccol(0), ccol(1), ccol(2),
                  pl.BlockSpec((2, n_lb, hd), lambda b, j: (0, 0, j)),
                  pl.BlockSpec((1, hd), lambda b, j: (0, j)),
                  pl.BlockSpec((3, hd), lambda b, j: (0, j))],
        out_specs=(out_spec, out_spec),
        out_shape=(jax.ShapeDtypeStruct((TOKENS, A_WIDTH), BF16),
                   jax.ShapeDtypeStruct((TOKENS, B_WIDTH), BF16)),
        scratch_shapes=[pltpu.VMEM((SEQ, hd), F32), pltpu.VMEM((SEQ, hd), F32),
                        pltpu.VMEM((2, hd, hd), F32), pltpu.VMEM((2, SEQ, hd), BF16),
                        pltpu.VMEM((2, SEQ // SCAN_CHUNK, hd, hd), F32),
                        pltpu.VMEM((2, SEQ // SCAN_CHUNK, 1, hd), F32),
                        pltpu.VMEM((2, SEQ, hd), BF16),
                        pltpu.VMEM((2, len(SCAN_LEVELS), SEQ, hd), BF16),
                        pltpu.VMEM((2, SEQ, SCAN_CHUNK), BF16)],
        compiler_params=_params("arbitrary", "arbitrary"),
        name="hgrn_scan",
    )(p, p, p, p, p, p, p, p, cp, cp, cp, lb_logits, gnorm_g.reshape(1, A_WIDTH), conv_w)


def _split_bf16(x):
    hi = x.astype(BF16)
    return hi, (x - hi.astype(F32)).astype(BF16)


def _filter_kernel(z_ref, w1_ref, b1_ref, w2_ref, b2_ref, w3_ref, b3_ref, fr_ref,
                   w4f_ref, w4b_ref, cond_ref, aw_ref, ab_ref,
                   sum_ref, diff_ref, mod_ref, hid_ref, *, tc):
    o = pl.program_id(0)
    j = pl.program_id(1)

    cond = cond_ref[...]
    mod_ref[...] = _dot((cond * jax.nn.sigmoid(cond)).astype(BF16), aw_ref[...].astype(BF16)) + ab_ref[...]

    @pl.when((o == 0) & (j == 0))
    def _():
        fr = fr_ref[...]
        h = jnp.sin(fr * (_dot_f32(z_ref[...], w1_ref[...]) + b1_ref[...]))
        h = jnp.sin(fr * (_dot_f32(h, w2_ref[...]) + b2_ref[...]))
        h = jnp.sin(fr * (_dot_f32(h, w3_ref[...]) + b3_ref[...]))
        hi, lo = _split_bf16(h)
        hid_ref[...] = jnp.concatenate([hi, lo, hi], axis=1)

    def project(w_ref):
        w_hi, w_lo = _split_bf16(w_ref[...])
        zero = jnp.zeros_like(w_hi)
        first = _dot(hid_ref[...], jnp.concatenate([w_hi, zero, w_hi, zero, w_lo, zero], axis=0))
        second = _dot(hid_ref[...], jnp.concatenate([zero, w_hi, zero, w_hi, zero, w_lo], axis=0))
        return jnp.concatenate([first, second], axis=0)

    rows = lax.broadcasted_iota(jnp.int32, (SEQ, tc), 0)
    chan = lax.broadcasted_iota(jnp.int32, (SEQ, tc), 1) + j * tc
    lag = jnp.where(rows < HALF, 2 * rows, 2 * rows - (SEQ - 1))
    t = lag.astype(F32) * (1.0 / (SEQ - 1))
    max_decay = math.log(HY_DECAY_TARGET) / HY_FAST_PCT
    min_decay = math.log(HY_DECAY_TARGET) / HY_SLOW_PCT
    deltas = jnp.abs(min_decay + chan.astype(F32) * ((max_decay - min_decay) / (D_MODEL - 1)))
    window = jnp.exp(-t * deltas)
    fw = project(w4f_ref) * window
    bw = project(w4b_ref) * window
    first = rows == 0
    a = fw + jnp.where(first, bw, 0.0)
    bb = jnp.where(first, 0.0, bw)
    inv = 1.0 / jnp.sum(jnp.abs(a) + jnp.abs(bb), axis=0, keepdims=True)
    sum_ref[...] = ((a + bb) * inv).astype(sum_ref.dtype)
    diff_ref[...] = ((a - bb) * inv).astype(diff_ref.dtype)


@functools.lru_cache(maxsize=None)
def _filter_positions():
    l = SEQ
    t = np.linspace(0.0, 1.0, l, dtype=np.float32)
    w = (2.0 * math.pi * np.arange(l, dtype=np.float32) / l).astype(np.float32)
    bands = np.linspace(1e-4, HY_BANDS - 1, HY_BANDS, dtype=np.float32)
    ang = w[:, None] * bands[None, :]
    z = np.concatenate([t[:, None], np.cos(ang), -np.sin(ang)], axis=-1).astype(np.float32)
    z = np.concatenate([z[0::2], z[1::2]], axis=0)
    z = np.pad(z, ((0, 0), (0, LANES // 2 - HY_EMB)))
    return np.concatenate([z[:l // 2], z[l // 2:]], axis=1)


def _filter_taps_and_modulation(fw1, fb1, fw2, fb2, fw3, fb3, fw4, freq, cond, ada_w, ada_b):
    l = SEQ
    tc = 256
    nj = D_MODEL // tc
    depth, d, n_mod = ada_w.shape
    assert depth == 2 and n_mod % (nj * LANES) == 0
    ta = n_mod // nj
    hw = LANES // 2

    def pad(a, rows, cols):
        return jnp.pad(a, ((0, rows - a.shape[0]), (0, cols - a.shape[1])))

    def twice(a):
        a = pad(a, hw, hw)
        zero = jnp.zeros_like(a)
        return jnp.concatenate([jnp.concatenate([a, zero], axis=1), jnp.concatenate([zero, a], axis=1)], axis=0)

    def row2(v):
        v = pad(v[None, :], 1, hw)
        return jnp.concatenate([v, v], axis=1)

    def small(shape):
        return pl.BlockSpec(shape, lambda o, j: (0, 0))

    out_sds = jax.ShapeDtypeStruct((l, 2 * D_MODEL), BF16)
    out_spec = pl.BlockSpec((l, tc), lambda o, j: (0, o * nj + j))
    assert fw4.shape[0] == hw
    side_f = pl.BlockSpec((hw, tc), lambda o, j: (0, 2 * o * nj + j))
    side_b = pl.BlockSpec((hw, tc), lambda o, j: (0, (2 * o + 1) * nj + j))
    sq, vec = small((LANES, LANES)), small((1, LANES))
    return pl.pallas_call(
        functools.partial(_filter_kernel, tc=tc),
        grid=(2, nj),
        in_specs=[small((l // 2, LANES)), sq, vec, sq, vec, sq, vec, vec, side_f, side_b,
                  small((8, d)),
                  pl.BlockSpec((None, d, ta), lambda o, j: (o, 0, j)),
                  pl.BlockSpec((None, 1, ta), lambda o, j: (o, 0, j))],
        out_specs=(out_spec, out_spec, pl.BlockSpec((None, 8, ta), lambda o, j: (o, 0, j))),
        out_shape=(out_sds, out_sds, jax.ShapeDtypeStruct((depth, 8, n_mod), F32)),
        scratch_shapes=[pltpu.VMEM((l // 2, 3 * LANES), BF16)],
        compiler_params=_params("arbitrary", "arbitrary"),
        name="hyena_filter",
    )(jnp.asarray(_filter_positions()), twice(fw1), row2(fb1), twice(fw2), row2(fb2), twice(fw3), row2(fb3),
      row2(freq), fw4, fw4, cond, ada_w, ada_b.reshape(depth, 1, n_mod))


def _trig_table(a_rows, b_cols, scale):
    n = (a_rows[:, None].astype(np.int64) * b_cols[None, :].astype(np.int64)) % (4 * SEQ)
    ang = n.astype(np.float64) * (math.pi / (2 * SEQ))
    return (np.cos(ang) * scale).astype(np.float32), (np.sin(ang) * scale).astype(np.float32)


@functools.lru_cache(maxsize=None)
def _dft_tables():
    idx = np.arange(QUARTER)
    ana_c, ana_s, syn_c, syn_s = [], [], [], []
    for r in range(4):
        c, s = _trig_table(2 * idx + 1, 4 * idx + r, 1.0)
        ana_c.append(c)
        ana_s.append(s)
        c, s = _trig_table(4 * idx + r, 2 * idx + 1, 1.0 / SEQ)
        syn_c.append(c)
        syn_s.append(s)
    fmap = np.concatenate([idx, HALF + idx])
    lag = np.arange(HALF)
    cee, see = _trig_table(2 * fmap + 1, 2 * lag, 1.0)
    ceo, seo = _trig_table(2 * fmap + 1, 2 * lag + 1, 1.0)
    return np.stack(ana_c + ana_s), np.stack(syn_c + syn_s), (cee, ceo, see, seo)


def _filter_dft_kernel(cee_ref, ceo_ref, see_ref, seo_ref, fs_ref, fd_ref, x_ref, g_ref, sc_ref, sh_ref,
                       p_ref, pm_ref, q_ref, qm_ref, h_ref):
    _norm_mod_rows(x_ref, g_ref, sc_ref, sh_ref, h_ref)
    pe = _dot(cee_ref[...], fs_ref[0:HALF, :])
    po = _dot(ceo_ref[...], fs_ref[HALF:SEQ, :])
    qe = _dot(see_ref[...], fd_ref[0:HALF, :])
    qo = _dot(seo_ref[...], fd_ref[HALF:SEQ, :])
    p_ref[...] = pe + po
    pm_ref[...] = pe - po
    q_ref[...] = qe + qo
    qm_ref[...] = qo - qe


def _filter_dft_and_first_norm(tables, taps_sum, taps_diff, x, g, sc, sh):
    tm, tn = HALF, 256
    n_cols = taps_sum.shape[1]
    b, l, d = x.shape
    ts = b * l // (n_cols // tn)
    per_batch = l // ts
    assert per_batch * ts == l
    x_spec = pl.BlockSpec((None, ts, d), lambda j, i: (j // per_batch, j % per_batch, 0))
    mod_spec = pl.BlockSpec((None, 1, d), lambda j, i: (j // per_batch, 0, 0))
    mat = pl.BlockSpec((tm, HALF), lambda j, i: (i, 0), pipeline_mode=pl.Buffered(1))
    tap = pl.BlockSpec((SEQ, tn), lambda j, i: (0, j))
    out_spec = pl.BlockSpec((tm, tn), lambda j, i: (i, j))
    out_sds = jax.ShapeDtypeStruct((HALF, n_cols), F32)
    return pl.pallas_call(
        _filter_dft_kernel,
        grid=(n_cols // tn, HALF // tm),
        in_specs=[mat, mat, mat, mat, tap, tap, x_spec, pl.BlockSpec((1, d), lambda j, i: (0, 0)),
                  mod_spec, mod_spec],
        out_specs=(out_spec,) * 4 + (x_spec,),
        out_shape=(out_sds,) * 4 + (jax.ShapeDtypeStruct((b, l, d), BF16),),
        compiler_params=_params("arbitrary", "arbitrary"),
        name="filter_dft",
    )(*tables, taps_sum, taps_diff, x, g.reshape(1, d), sc.reshape(b, 1, d), sh.reshape(b, 1, d))


def _dft4_fwd_kernel(tab_ref, z_ref, p_ref, pm_ref, q_ref, qm_ref, x_ref, zs_ref, *, slab):
    panels = zs_ref.shape[0]
    for k in range(panels):
        zs_ref[k] = z_ref[:, k * LANES:(k + 1) * LANES].astype(F32)
    zr = [jnp.concatenate([zs_ref[k, pl.ds(r, QUARTER, stride=4), :] for k in range(panels)],
                          axis=1).astype(BF16) for r in range(4)]

    for f0 in range(0, QUARTER, slab):
        rows = slice(f0, f0 + slab)
        mirror = slice(QUARTER + f0, QUARTER + f0 + slab)
        c = [_dot(tab_ref[r, rows, :], zr[r]) for r in range(4)]
        s = [_dot(tab_ref[4 + r, rows, :], zr[r]) for r in range(4)]
        e, o, em, om = c[0] + c[2], c[1] + c[3], c[0] - c[2], c[1] - c[3]
        se, so, sem, som = s[0] + s[2], s[1] + s[3], s[0] - s[2], s[1] - s[3]
        a = (e + o, em + som, em - som, e - o)
        b = (se + so, om - sem, sem + om, so - se)
        pk = (p_ref[rows, :], pm_ref[mirror, :], p_ref[mirror, :], pm_ref[rows, :])
        qk = (q_ref[rows, :], qm_ref[mirror, :], q_ref[mirror, :], qm_ref[rows, :])
        u = [a[k] * pk[k] - b[k] * qk[k] for k in range(4)]
        v = [a[k] * qk[k] + b[k] * pk[k] for k in range(4)]
        up, um, wp, wm = u[0] + u[3], u[0] - u[3], u[1] + u[2], u[1] - u[2]
        vp, vm, yp, ym = v[0] + v[3], v[0] - v[3], v[1] + v[2], v[1] - v[2]
        outs = (up + wp, um + yp, up - wp, um - yp,
                vm - ym, vp + wm, vm + ym, vp - wm)
        for n, val in enumerate(outs):
            x_ref[n, rows, :] = val.astype(x_ref.dtype)


def _dft4_fwd(tab, z_arr, z_col0, spectra, order):
    tn, slab = 512, 256
    nj = D_MODEL // tn
    zc = z_col0 // tn
    spec = pl.BlockSpec((HALF, tn), lambda j, b: (0, order * nj + j))
    return pl.pallas_call(
        functools.partial(_dft4_fwd_kernel, slab=slab),
        grid=(nj, BATCH),
        in_specs=[pl.BlockSpec(tab.shape, lambda j, b: (0, 0, 0), pipeline_mode=pl.Buffered(1)),
                  pl.BlockSpec((SEQ, tn), lambda j, b: (b, zc + j)), spec, spec, spec, spec],
        out_specs=pl.BlockSpec((8, QUARTER, tn), lambda j, b: (0, b, j)),
        out_shape=jax.ShapeDtypeStruct((8, BATCH * QUARTER, D_MODEL), BF16),
        scratch_shapes=[pltpu.VMEM((tn // LANES, SEQ, LANES), F32)],
        compiler_params=_params("arbitrary", "arbitrary"),
        name="dft_analysis",
    )(tab, z_arr, *spectra)


def _dft4_inv_kernel(tab_ref, x_ref, z_ref, gate_ref, skip_ref, o_ref, ys_ref):
    panels = ys_ref.shape[0]
    for r in range(4):
        y_r = _dot(tab_ref[r], x_ref[r]) + _dot(tab_ref[4 + r], x_ref[4 + r])
        for k in range(panels):
            ys_ref[k, pl.ds(r, QUARTER, stride=4), :] = y_r[:, k * LANES:(k + 1) * LANES]
    y = jnp.concatenate([ys_ref[k] for k in range(panels)], axis=1)
    y = y + z_ref[...].astype(F32) * skip_ref[...]
    o_ref[...] = (gate_ref[...].astype(F32) * y).astype(o_ref.dtype)


def _dft4_inv(tab, x, z_arr, z_col0, gate_arr, gate_col0, skip):
    tn = 512
    nj = D_MODEL // tn
    zc, gc = z_col0 // tn, gate_col0 // tn
    return pl.pallas_call(
        _dft4_inv_kernel,
        grid=(nj, BATCH),
        in_specs=[pl.BlockSpec(tab.shape, lambda j, b: (0, 0, 0), pipeline_mode=pl.Buffered(1)),
                  pl.BlockSpec((8, QUARTER, tn), lambda j, b: (0, b, j)),
                  pl.BlockSpec((SEQ, tn), lambda j, b: (b, zc + j)),
                  pl.BlockSpec((SEQ, tn), lambda j, b: (b, gc + j)),
                  pl.BlockSpec((1, tn), lambda j, b: (0, j))],
        out_specs=pl.BlockSpec((SEQ, tn), lambda j, b: (b, j)),
        out_shape=jax.ShapeDtypeStruct((TOKENS, D_MODEL), BF16),
        scratch_shapes=[pltpu.VMEM((tn // LANES, SEQ, LANES), F32)],
        compiler_params=_params("arbitrary", "arbitrary"),
        name="dft_synthesis",
    )(tab, x, z_arr, gate_arr, skip.reshape(1, D_MODEL))


def _mod_chunk(mod, l, k):
    return mod[l, :BATCH, k * D_MODEL:(k + 1) * D_MODEL]


def _mlp_block(x2d, h, mod, l, w1, w2):
    tm = tn = 1024
    n_i, n_j = TOKENS // tm, D_FF // tn
    slab = D_FF // (n_i * n_j)
    hid, w2_bf16 = _mm(
        [h], w1, w_layer=l, tm=tm, tn=tn, n_cols=D_FF, epi=_epi_relu2_and_cast,
        extras=(w2,), extra_specs=(pl.BlockSpec((None, slab, D_MODEL), lambda j, i: (l, j * n_i + i, 0)),),
        out_shapes=(jax.ShapeDtypeStruct((TOKENS, D_FF), BF16), jax.ShapeDtypeStruct((D_FF, D_MODEL), BF16)),
        out_specs=(pl.BlockSpec((tm, tn), lambda j, i: (i, j)),
                   pl.BlockSpec((slab, D_MODEL), lambda j, i: (j * n_i + i, 0))),
        name="mlp_up")
    return _mm_residual([hid], w2_bf16, None, x2d, _mod_chunk(mod, l, 5), 512, 512, "mlp_down")


def kernel(x, c, ctx, c_ctx, ada_w, ada_b, norm_g, lb_logits, ab_w_in, ab_conv_w, ab_gnorm_g, ab_w_out,
           hy_in_w, hy_short_w, hy_out_w, hy_fw1, hy_fb1, hy_fw2, hy_fb2, hy_fw3, hy_fb3, hy_fw4, hy_freq,
           hy_skip, mlp_w1, mlp_w2, final_g):
    d = D_MODEL
    cond = jnp.concatenate([c, c_ctx[None, :], jnp.zeros((3, d), F32)], axis=0)
    taps_sum, taps_diff, mod = _filter_taps_and_modulation(
        hy_fw1[0], hy_fb1[0], hy_fw2[0], hy_fb2[0], hy_fw3[0], hy_fb3[0], hy_fw4[0], hy_freq[0],
        cond, ada_w, ada_b)
    x2d = x.reshape(TOKENS, d)

    sh1, sc1, g1 = (mod[0, :BATCH, k * d:(k + 1) * d] for k in (0, 1, 2))
    analysis, synthesis, filter_tables = jax.tree.map(lambda t: jnp.asarray(t).astype(BF16), _dft_tables())
    *spectra, h = _filter_dft_and_first_norm(filter_tables, taps_sum, taps_diff, x, norm_g[0, 0], sc1, sh1)
    h = h.reshape(TOKENS, d)
    csh = jnp.broadcast_to(mod[0, BATCH:BATCH + 1, 0:d], (BATCH, d))
    csc = jnp.broadcast_to(mod[0, BATCH:BATCH + 1, d:2 * d], (BATCH, d))
    hc = _prep(ctx, norm_g[0, 0], csc, csh, BF16, CTX_LEN).reshape(BATCH * CTX_LEN, d)
    p = _mm_plain(h, ab_w_in, AB_IN_WIDTH, F32, _epi_store, 1024, 1024, "ab_in_proj", w_layer=0)
    cp = _mm_plain(hc, ab_w_in, 3 * A_WIDTH, F32, _epi_store, 1024, 1024, "ab_ctx_proj", w_layer=0)
    mix_a, mix_b = _even_mixer(p, cp, lb_logits, ab_gnorm_g[0], ab_conv_w[0], 0)
    x2d, h = _mm_residual_norm([mix_a, mix_b], ab_w_out, 0, x2d, g1, norm_g[0, 1], _mod_chunk(mod, 0, 4),
                               _mod_chunk(mod, 0, 3), 512, "ab_out_proj")
    x2d = _mlp_block(x2d, h, mod, 0, mlp_w1, mlp_w2)

    sh1, sc1, g1 = (mod[1, :BATCH, k * d:(k + 1) * d] for k in (0, 1, 2))
    h = _prep(x2d.reshape(BATCH, SEQ, d), norm_g[1, 0], sc1, sh1, BF16, 1024).reshape(TOKENS, d)
    pc = _mm([h], hy_in_w, w_layer=0, tm=1024, tn=1024, n_cols=3 * d, epi=_epi_conv3_grid,
             extras=(hy_short_w[0],), extra_specs=(pl.BlockSpec((3, 1024), lambda j, i: (0, j)),),
             out_shapes=jax.ShapeDtypeStruct((TOKENS, 3 * d), BF16),
             out_specs=pl.BlockSpec((1024, 1024), lambda j, i: (i, j)), name="hy_in_proj")
    coef = _dft4_fwd(analysis, pc, 2 * d, spectra, 0)
    z = _dft4_inv(synthesis, coef, pc, 2 * d, pc, 0, hy_skip[0, 0])
    coef = _dft4_fwd(analysis, z, 0, spectra, 1)
    z = _dft4_inv(synthesis, coef, z, 0, pc, d, hy_skip[0, 1])
    x2d, h = _mm_residual_norm([z], hy_out_w, 0, x2d, g1, norm_g[1, 1], _mod_chunk(mod, 1, 4),
                               _mod_chunk(mod, 1, 3), 512, "hy_out_proj")
    x2d = _mlp_block(x2d, h, mod, 1, mlp_w1, mlp_w2)

    zeros = jnp.zeros((BATCH, d), F32)
    return _prep(x2d.reshape(BATCH, SEQ, d), final_g, zeros, zeros, F32, 1024)
```

```python
import functools
import math

import numpy as np
import jax
import jax.numpy as jnp
from jax import lax
from jax.experimental import pallas as pl
from jax.experimental.pallas import tpu as pltpu

F32 = jnp.float32
BF16 = jnp.bfloat16

D_MODEL = 2048
BATCH = 4
SEQ = 2048
CTX_LEN = 256
GRID_W = 64
EPS = 1e-6
A_HEAD_DIM = 128
A_WIDTH = D_MODEL // 2
A_HEADS = A_WIDTH // A_HEAD_DIM
B_WIDTH = D_MODEL - A_WIDTH
AB_IN_WIDTH = 5 * A_WIDTH + 3 * B_WIDTH
HY_EMB = 33
HY_BANDS = (HY_EMB - 1) // 2
HY_DECAY_TARGET = 1e-2
HY_FAST_PCT = 0.3
HY_SLOW_PCT = 1.5
D_FF = 4 * D_MODEL
TOKENS = BATCH * SEQ
HALF = SEQ // 2
QUARTER = SEQ // 4

SCAN_CHUNK = 64
SCAN_LEVELS = (1, 2, 4, 8, 16, 32)

SUBLANES = 8
LANES = 128

VMEM_LIMIT_BYTES = 56 * 1024 * 1024


def _params(*sem):
    return pltpu.CompilerParams(dimension_semantics=sem, vmem_limit_bytes=VMEM_LIMIT_BYTES)


def _dot(a, b):
    return jnp.dot(a, b, preferred_element_type=F32)


def _dot_nt(a, b):
    return lax.dot_general(a, b, (((1,), (1,)), ((), ())), preferred_element_type=F32)


def _dot_tn(a, b):
    return lax.dot_general(a, b, (((0,), (0,)), ((), ())), preferred_element_type=F32)


def _dot_f32(a, b):
    return jnp.dot(a, b, precision=lax.Precision.HIGHEST, preferred_element_type=F32)


def _conv3_rows(t, w, group):
    rows = lax.broadcasted_iota(jnp.int32, t.shape, 0) % group
    prev = jnp.where(rows == 0, 0.0, pltpu.roll(t, 1, 0))
    nxt = jnp.where(rows == group - 1, 0.0, pltpu.roll(t, t.shape[0] - 1, 0))
    return w[0:1] * prev + w[1:2] * t + w[2:3] * nxt


def _norm_mod_rows(x_ref, g_ref, sc_ref, sh_ref, o_ref):
    scale = g_ref[...] * (1.0 + sc_ref[...])
    shift = sh_ref[...]
    piece = 64

    def body(i, carry):
        rows = pl.ds(pl.multiple_of(i * piece, piece), piece)
        x = x_ref[rows, :]
        ms = jnp.mean(x * x, axis=-1, keepdims=True)
        o_ref[rows, :] = (x * lax.rsqrt(ms + EPS) * scale + shift).astype(o_ref.dtype)
        return carry

    lax.fori_loop(0, x_ref.shape[0] // piece, body, 0, unroll=4)


def _prep_kernel(x_ref, g_ref, sc_ref, sh_ref, o_ref):
    _norm_mod_rows(x_ref, g_ref, sc_ref, sh_ref, o_ref)


def _prep(x, g, sc, sh, out_dtype, ts):
    b, l, d = x.shape
    return pl.pallas_call(
        _prep_kernel,
        grid=(b, l // ts),
        in_specs=[
            pl.BlockSpec((None, ts, d), lambda i, j: (i, j, 0)),
            pl.BlockSpec((1, d), lambda i, j: (0, 0)),
            pl.BlockSpec((None, 1, d), lambda i, j: (i, 0, 0)),
            pl.BlockSpec((None, 1, d), lambda i, j: (i, 0, 0)),
        ],
        out_specs=pl.BlockSpec((None, ts, d), lambda i, j: (i, j, 0)),
        out_shape=jax.ShapeDtypeStruct((b, l, d), out_dtype),
        compiler_params=_params("arbitrary", "arbitrary"),
        name="norm_mod",
    )(x, g.reshape(1, d), sc.reshape(b, 1, d), sh.reshape(b, 1, d))


def _mm_kernel(*refs, n_a, k_sizes, epi, cast_w):
    a_refs = refs[:n_a]
    w_ref = refs[n_a]
    rest = refs[n_a + 1:]
    if cast_w:
        w_ref, rest = rest[-1], rest[:-1]

        @pl.when(pl.program_id(1) == 0)
        def _():
            w_ref[...] = refs[n_a][...].astype(BF16)

    acc = None
    off = 0
    for a_ref, k in zip(a_refs, k_sizes):
        part = _dot(a_ref[...], w_ref[off:off + k, :])
        acc = part if acc is None else acc + part
        off += k
    epi(acc, *rest)


def _mm(a_list, w, *, tm, tn, n_cols, epi, extras, extra_specs, out_shapes, out_specs, name,
        w_layer=None, w_single_buffer=False):
    m = a_list[0].shape[0]
    k_sizes = tuple(a.shape[1] for a in a_list)
    k_total = sum(k_sizes)
    cast_w = w_layer is not None
    assert w.shape[-2] == k_total and m % tm == 0 and n_cols % tn == 0
    in_specs = [pl.BlockSpec((tm, k), lambda j, i: (i, 0)) for k in k_sizes]
    w_mode = dict(pipeline_mode=pl.Buffered(1)) if w_single_buffer else {}
    if cast_w:
        in_specs.append(pl.BlockSpec((None, k_total, tn), lambda j, i: (w_layer, 0, j), **w_mode))
    else:
        in_specs.append(pl.BlockSpec((k_total, tn), lambda j, i: (0, j), **w_mode))
    in_specs.extend(extra_specs)
    return pl.pallas_call(
        functools.partial(_mm_kernel, n_a=len(a_list), k_sizes=k_sizes, epi=epi, cast_w=cast_w),
        grid=(n_cols // tn, m // tm),
        in_specs=in_specs,
        out_specs=out_specs,
        out_shape=out_shapes,
        scratch_shapes=[pltpu.VMEM((k_total, tn), BF16)] if cast_w else [],
        compiler_params=_params("arbitrary", "arbitrary"),
        name=name,
    )(*a_list, w, *extras)


def _epi_store(acc, o_ref):
    o_ref[...] = acc.astype(o_ref.dtype)


def _epi_relu2_and_cast(acc, w_next_ref, o_ref, w_next_bf16_ref):
    r = jnp.maximum(acc, 0.0)
    o_ref[...] = (r * r).astype(o_ref.dtype)
    w_next_bf16_ref[...] = w_next_ref[...].astype(w_next_bf16_ref.dtype)


def _epi_residual(acc, x_ref, gate_ref, o_ref):
    o_ref[...] = x_ref[...] + gate_ref[...] * acc


def _epi_residual_norm(acc, x_ref, gate_ref, g_ref, sc_ref, sh_ref, o_ref, h_ref):
    x = x_ref[...] + gate_ref[...] * acc
    o_ref[...] = x
    ms = jnp.mean(x * x, axis=-1, keepdims=True)
    xn = x * lax.rsqrt(ms + EPS) * g_ref[...]
    h_ref[...] = (xn * (1.0 + sc_ref[...]) + sh_ref[...]).astype(h_ref.dtype)


def _epi_conv3_grid(acc, w_ref, o_ref):
    o_ref[...] = _conv3_rows(acc, w_ref[...], GRID_W).astype(o_ref.dtype)


def _mm_plain(a, w, n_cols, out_dtype, epi, tm, tn, name, w_layer=None):
    m = a.shape[0]
    return _mm([a], w, tm=tm, tn=tn, n_cols=n_cols, epi=epi, extras=(), extra_specs=(),
               out_shapes=jax.ShapeDtypeStruct((m, n_cols), out_dtype),
               out_specs=pl.BlockSpec((tm, tn), lambda j, i: (i, j)), name=name, w_layer=w_layer)


def _mm_residual(a_list, w, w_layer, x, gate, tm, tn, name, w_single_buffer=False):
    m, n = x.shape
    tiles_per_batch = SEQ // tm
    return _mm(a_list, w, tm=tm, tn=tn, n_cols=n, epi=_epi_residual,
               extras=(x, gate.reshape(BATCH, 1, n)),
               extra_specs=(pl.BlockSpec((tm, tn), lambda j, i: (i, j)),
                            pl.BlockSpec((None, 1, tn), lambda j, i: (i // tiles_per_batch, 0, j))),
               out_shapes=jax.ShapeDtypeStruct((m, n), F32),
               out_specs=pl.BlockSpec((tm, tn), lambda j, i: (i, j)), name=name,
               w_layer=w_layer, w_single_buffer=w_single_buffer)


def _mm_residual_norm(a_list, w, w_layer, x, gate, norm_g, sc, sh, tm, name):
    m, n = x.shape
    tiles_per_batch = SEQ // tm

    def per_batch(v):
        return v.reshape(BATCH, 1, n), pl.BlockSpec((None, 1, n), lambda j, i: (i // tiles_per_batch, 0, 0))

    (gate3, gate_spec), (sc3, sc_spec), (sh3, sh_spec) = per_batch(gate), per_batch(sc), per_batch(sh)
    rows = pl.BlockSpec((tm, n), lambda j, i: (i, 0))
    return _mm(a_list, w, tm=tm, tn=n, n_cols=n, epi=_epi_residual_norm,
               extras=(x, gate3, norm_g.reshape(1, n), sc3, sh3),
               extra_specs=(rows, gate_spec, pl.BlockSpec((1, n), lambda j, i: (0, 0)), sc_spec, sh_spec),
               out_shapes=(jax.ShapeDtypeStruct((m, n), F32), jax.ShapeDtypeStruct((m, n), BF16)),
               out_specs=(rows, rows), name=name, w_layer=w_layer, w_single_buffer=True)


def _row_groups(x):
    return [x[k:k + SUBLANES] for k in range(0, x.shape[0], SUBLANES)]


def _cumsum_groups(groups, sub):
    out, carry = [], None
    for g in groups:
        for s in (1, 2, 4):
            g = g + jnp.where(sub >= s, pltpu.roll(g, s, 0), 0.0)
        if carry is not None:
            g = g + carry
        out.append(g)
        carry = g[SUBLANES - 1:SUBLANES]
    return out


def _hold_boundary(cum, w, k, sub):
    shape = cum[k].shape
    if w >= SUBLANES // 2:
        r = (k * SUBLANES // (2 * w)) * 2 * w + w - 1
        return jnp.broadcast_to(cum[r // SUBLANES][r % SUBLANES:r % SUBLANES + 1], shape)
    assert w == 2
    return jnp.where(sub < 4, jnp.broadcast_to(cum[k][1:2], shape), jnp.broadcast_to(cum[k][5:6], shape))


def _scan_kernel(zf_ref, zb_ref, v_ref, q_ref, g_ref, u_ref, gb_ref, gc_ref, czf_ref, czb_ref, cv_ref,
                 lbl_ref, gn_ref, cw_ref, o_ref, oc_ref,
                 of_ref, ob_ref, st_ref, qs_ref, inc_ref, dec_ref, kl_ref, lv_ref, att_ref, *, layer):
    c = SCAN_CHUNK
    n_chunks = SEQ // c
    n_ctx = CTX_LEN // c
    z_refs = (zf_ref, zb_ref)
    cz_refs = (czf_ref, czb_ref)
    out_refs = (of_ref, ob_ref)

    logit_rows = [lbl_ref[:, k, :] for k in range(lbl_ref.shape[1])]
    top = functools.reduce(jnp.maximum, logit_rows)
    exps = [jnp.exp(r - top) for r in logit_rows]
    lbs = sum(exps[:layer + 1]) / sum(exps)

    sub = lax.broadcasted_iota(jnp.int32, (SUBLANES, A_HEAD_DIM), 0)
    n_groups = c // SUBLANES

    def rows(groups):
        return jnp.concatenate(groups, axis=0)

    def chunk(d, z, q, v, want_out):
        lb = lbs[d:d + 1]
        f = lb + (1.0 - lb) * jax.nn.sigmoid(z)
        kk = 1.0 - f
        fg, kg = _row_groups(f), _row_groups(kk)
        lg = [jnp.log2(g) for g in fg]
        cum = _cumsum_groups(lg, sub)
        total = cum[-1][SUBLANES - 1:SUBLANES]
        if d == 0:
            pos = cum
            q_in = [jnp.exp2(g) for g in cum]
            k_out = [jnp.exp2(total - g) for g in cum]
        else:
            pos = [a - b for a, b in zip(cum, lg)]
            q_in = [jnp.exp2(total - g) for g in pos]
            k_out = [jnp.exp2(g) for g in pos]
        k_leave = rows([a * b for a, b in zip(kg, k_out)]).astype(BF16)
        decay = jnp.exp2(total)
        if not want_out:
            return k_leave, decay, None, None, None
        qg = _row_groups(q)
        o_diag = jnp.sum(q * kk, axis=1, keepdims=True) * v
        upper, lower = (qg, kg) if d == 0 else (kg, qg)
        operands = []
        for w in SCAN_LEVELS:
            y = []
            for k in range(n_groups):
                if w == 1:
                    odd = (sub & 1) != 0
                    y.append(jnp.where(odd, qg[k] * fg[k], kg[k]) if d == 0
                             else jnp.where(odd, kg[k], qg[k] * fg[k]))
                    continue
                hold = _hold_boundary(cum, w, k, sub)
                if w >= SUBLANES:
                    bit_set = (k * SUBLANES) & w != 0
                    y.append(upper[k] * jnp.exp2(pos[k] - hold) if bit_set
                             else lower[k] * jnp.exp2(hold - pos[k]))
                else:
                    bit = (sub & w) != 0
                    sign = jnp.where(bit, 1.0, -1.0)
                    y.append(jnp.where(bit, upper[k], lower[k]) * jnp.exp2((pos[k] - hold) * sign))
            operands.append(rows(y).astype(BF16))
        qs = rows([a * b for a, b in zip(qg, q_in)]).astype(BF16)
        return k_leave, decay, qs, o_diag, operands

    def within_chunk(d, operands):
        t_idx = lax.broadcasted_iota(jnp.int32, (c, c), 0)
        s_idx = lax.broadcasted_iota(jnp.int32, (c, c), 1)
        split = t_idx ^ s_idx
        att = None
        for w, yb in reversed(list(zip(SCAN_LEVELS, operands))):
            scores = _dot_nt(yb, yb)
            att = scores if att is None else jnp.where(split < 2 * w, scores, att)
        return jnp.where(t_idx > s_idx if d == 0 else t_idx < s_idx, att, 0.0)

    st_ref[...] = jnp.zeros_like(st_ref)

    def ctx_body(i, carry):
        for d in range(2):
            ci = i if d == 0 else n_ctx - 1 - i
            rows_i = pl.ds(pl.multiple_of(ci * c, c), c)
            k_leave, decay, _, _, _ = chunk(d, cz_refs[d][rows_i, :], None, None, False)
            st_ref[d] = st_ref[d] * decay + _dot_tn(cv_ref[rows_i, :].astype(BF16), k_leave)
        return carry

    lax.fori_loop(0, n_ctx, ctx_body, 0, unroll=2)

    def chunk_rows(ci):
        start = ci * c
        return pl.ds(start if isinstance(ci, int) else pl.multiple_of(start, c), c)

    def chunk_of(d, t):
        return t if d == 0 else n_chunks - 1 - t

    def prepare(t):
        for d in range(2):
            ci = chunk_of(d, t)
            rows_i = chunk_rows(ci)
            k_leave, decay, qs, o_diag, operands = chunk(d, z_refs[d][rows_i, :], q_ref[rows_i, :],
                                                         v_ref[rows_i, :], True)
            kl_ref[d, rows_i, :] = k_leave
            dec_ref[d, ci] = decay
            qs_ref[d, rows_i, :] = qs
            out_refs[d][rows_i, :] = o_diag
            for lvl, yb in enumerate(operands):
                lv_ref[d, lvl, rows_i, :] = yb

    def scores(t):
        for d in range(2):
            ci = chunk_of(d, t)
            rows_i = chunk_rows(ci)
            inc_ref[d, ci] = _dot_tn(v_ref[rows_i, :].astype(BF16), kl_ref[d, rows_i, :])
            operands = [lv_ref[d, lvl, rows_i, :] for lvl in range(len(SCAN_LEVELS))]
            att_ref[d, rows_i, :] = within_chunk(d, operands).astype(BF16)

    def carry_state(t):
        for d in range(2):
            ci = chunk_of(d, t)
            rows_i = chunk_rows(ci)
            state = st_ref[d]
            out_refs[d][rows_i, :] += (_dot(att_ref[d, rows_i, :], v_ref[rows_i, :].astype(BF16))
                                       + _dot_nt(qs_ref[d, rows_i, :], state.astype(BF16)))
            st_ref[d] = state * dec_ref[d, ci] + inc_ref[d, ci]

    pair = 8
    n_trips = n_chunks // pair

    def stage(fn, trip):
        for j in range(pair):
            fn(trip * pair + j)

    stage(prepare, 0)
    stage(scores, 0)
    stage(prepare, 1)

    def skewed_body(i, carry):
        stage(carry_state, i - 2)
        stage(scores, i - 1)
        stage(prepare, i)
        return carry

    lax.fori_loop(2, n_trips, skewed_body, 0)
    stage(carry_state, n_trips - 2)
    stage(scores, n_trips - 1)
    stage(carry_state, n_trips - 1)

    rb = 256
    gn = gn_ref[...]
    cw = cw_ref[...]

    def read_body(i, carry):
        rows_i = pl.ds(pl.multiple_of(i * rb, rb), rb)
        o = of_ref[rows_i, :] + ob_ref[rows_i, :]
        ms = jnp.mean(o * o, axis=-1, keepdims=True)
        on = o * lax.rsqrt(ms + EPS) * gn
        g = g_ref[rows_i, :]
        o_ref[rows_i, :] = (on * (g * jax.nn.sigmoid(g))).astype(o_ref.dtype)
        conv = _conv3_rows(gc_ref[rows_i, :] * u_ref[rows_i, :], cw, GRID_W)
        oc_ref[rows_i, :] = (gb_ref[rows_i, :] * conv).astype(oc_ref.dtype)
        return carry

    lax.fori_loop(0, SEQ // rb, read_body, 0, unroll=2)


def _even_mixer(p, cp, lb_logits, gnorm_g, conv_w, layer):
    h = A_HEADS
    hd = A_HEAD_DIM
    n_lb = lb_logits.shape[1]

    def col(k):
        return pl.BlockSpec((SEQ, hd), lambda b, j, k=k: (b, k * h + j))

    def ccol(k):
        return pl.BlockSpec((CTX_LEN, hd), lambda b, j, k=k: (b, k * h + j))

    out_spec = pl.BlockSpec((SEQ, hd), lambda b, j: (b, j))
    return pl.pallas_call(
        functools.partial(_scan_kernel, layer=layer),
        grid=(BATCH, h),
        in_specs=[col(0), col(1), col(2), col(3), col(4), col(5), col(6), col(7),
                  ccol(0), ccol(1), ccol(2),
                  pl.BlockSpec((2, n_lb, hd), lambda b, j: (0, 0, j)),
                  pl.BlockSpec((1, hd), lambda b, j: (0, j)),
                  pl.BlockSpec((3, hd), lambda b, j: (0, j))],
        out_specs=(out_spec, out_spec),
        out_shape=(jax.ShapeDtypeStruct((TOKENS, A_WIDTH), BF16),
                   jax.ShapeDtypeStruct((TOKENS, B_WIDTH), BF16)),
        scratch_shapes=[pltpu.VMEM((SEQ, hd), F32), pltpu.VMEM((SEQ, hd), F32),
                        pltpu.VMEM((2, hd, hd), F32), pltpu.VMEM((2, SEQ, hd), BF16),
                        pltpu.VMEM((2, SEQ // SCAN_CHUNK, hd, hd), F32),
                        pltpu.VMEM((2, SEQ // SCAN_CHUNK, 1, hd), F32),
                        pltpu.VMEM((2, SEQ, hd), BF16),
                        pltpu.VMEM((2, len(SCAN_LEVELS), SEQ, hd), BF16),
                        pltpu.VMEM((2, SEQ, SCAN_CHUNK), BF16)],
        compiler_params=_params("arbitrary", "arbitrary"),
        name="hgrn_scan",
    )(p, p, p, p, p, p, p, p, cp, cp, cp, lb_logits, gnorm_g.reshape(1, A_WIDTH), conv_w)


def _split_bf16(x):
    hi = x.astype(BF16)
    return hi, (x - hi.astype(F32)).astype(BF16)


def _filter_kernel(z_ref, w1_ref, b1_ref, w2_ref, b2_ref, w3_ref, b3_ref, fr_ref,
                   w4f_ref, w4b_ref, cond_ref, aw_ref, ab_ref,
                   sum_ref, diff_ref, mod_ref, hid_ref, *, tc):
    o = pl.program_id(0)
    j = pl.program_id(1)

    cond = cond_ref[...]
    mod_ref[...] = _dot((cond * jax.nn.sigmoid(cond)).astype(BF16), aw_ref[...].astype(BF16)) + ab_ref[...]

    @pl.when((o == 0) & (j == 0))
    def _():
        fr = fr_ref[...]
        h = jnp.sin(fr * (_dot_f32(z_ref[...], w1_ref[...]) + b1_ref[...]))
        h = jnp.sin(fr * (_dot_f32(h, w2_ref[...]) + b2_ref[...]))
        h = jnp.sin(fr * (_dot_f32(h, w3_ref[...]) + b3_ref[...]))
        hi, lo = _split_bf16(h)
        hid_ref[...] = jnp.concatenate([hi, lo, hi], axis=1)

    def project(w_ref):
        w_hi, w_lo = _split_bf16(w_ref[...])
        zero = jnp.zeros_like(w_hi)
        first = _dot(hid_ref[...], jnp.concatenate([w_hi, zero, w_hi, zero, w_lo, zero], axis=0))
        second = _dot(hid_ref[...], jnp.concatenate([zero, w_hi, zero, w_hi, zero, w_lo], axis=0))
        return jnp.concatenate([first, second], axis=0)

    rows = lax.broadcasted_iota(jnp.int32, (SEQ, tc), 0)
    chan = lax.broadcasted_iota(jnp.int32, (SEQ, tc), 1) + j * tc
    lag = jnp.where(rows < HALF, 2 * rows, 2 * rows - (SEQ - 1))
    t = lag.astype(F32) * (1.0 / (SEQ - 1))
    max_decay = math.log(HY_DECAY_TARGET) / HY_FAST_PCT
    min_decay = math.log(HY_DECAY_TARGET) / HY_SLOW_PCT
    deltas = jnp.abs(min_decay + chan.astype(F32) * ((max_decay - min_decay) / (D_MODEL - 1)))
    window = jnp.exp(-t * deltas)
    fw = project(w4f_ref) * window
    bw = project(w4b_ref) * window
    first = rows == 0
    a = fw + jnp.where(first, bw, 0.0)
    bb = jnp.where(first, 0.0, bw)
    inv = 1.0 / jnp.sum(jnp.abs(a) + jnp.abs(bb), axis=0, keepdims=True)
    sum_ref[...] = ((a + bb) * inv).astype(sum_ref.dtype)
    diff_ref[...] = ((a - bb) * inv).astype(diff_ref.dtype)


@functools.lru_cache(maxsize=None)
def _filter_positions():
    l = SEQ
    t = np.linspace(0.0, 1.0, l, dtype=np.float32)
    w = (2.0 * math.pi * np.arange(l, dtype=np.float32) / l).astype(np.float32)
    bands = np.linspace(1e-4, HY_BANDS - 1, HY_BANDS, dtype=np.float32)
    ang = w[:, None] * bands[None, :]
    z = np.concatenate([t[:, None], np.cos(ang), -np.sin(ang)], axis=-1).astype(np.float32)
    z = np.concatenate([z[0::2], z[1::2]], axis=0)
    z = np.pad(z, ((0, 0), (0, LANES // 2 - HY_EMB)))
    return np.concatenate([z[:l // 2], z[l // 2:]], axis=1)


def _filter_taps_and_modulation(fw1, fb1, fw2, fb2, fw3, fb3, fw4, freq, cond, ada_w, ada_b):
    l = SEQ
    tc = 256
    nj = D_MODEL // tc
    depth, d, n_mod = ada_w.shape
    assert depth == 2 and n_mod % (nj * LANES) == 0
    ta = n_mod // nj
    hw = LANES // 2

    def pad(a, rows, cols):
        return jnp.pad(a, ((0, rows - a.shape[0]), (0, cols - a.shape[1])))

    def twice(a):
        a = pad(a, hw, hw)
        zero = jnp.zeros_like(a)
        return jnp.concatenate([jnp.concatenate([a, zero], axis=1), jnp.concatenate([zero, a], axis=1)], axis=0)

    def row2(v):
        v = pad(v[None, :], 1, hw)
        return jnp.concatenate([v, v], axis=1)

    def small(shape):
        return pl.BlockSpec(shape, lambda o, j: (0, 0))

    out_sds = jax.ShapeDtypeStruct((l, 2 * D_MODEL), BF16)
    out_spec = pl.BlockSpec((l, tc), lambda o, j: (0, o * nj + j))
    assert fw4.shape[0] == hw
    side_f = pl.BlockSpec((hw, tc), lambda o, j: (0, 2 * o * nj + j))
    side_b = pl.BlockSpec((hw, tc), lambda o, j: (0, (2 * o + 1) * nj + j))
    sq, vec = small((LANES, LANES)), small((1, LANES))
    return pl.pallas_call(
        functools.partial(_filter_kernel, tc=tc),
        grid=(2, nj),
        in_specs=[small((l // 2, LANES)), sq, vec, sq, vec, sq, vec, vec, side_f, side_b,
                  small((8, d)),
                  pl.BlockSpec((None, d, ta), lambda o, j: (o, 0, j)),
                  pl.BlockSpec((None, 1, ta), lambda o, j: (o, 0, j))],
        out_specs=(out_spec, out_spec, pl.BlockSpec((None, 8, ta), lambda o, j: (o, 0, j))),
        out_shape=(out_sds, out_sds, jax.ShapeDtypeStruct((depth, 8, n_mod), F32)),
        scratch_shapes=[pltpu.VMEM((l // 2, 3 * LANES), BF16)],
        compiler_params=_params("arbitrary", "arbitrary"),
        name="hyena_filter",
    )(jnp.asarray(_filter_positions()), twice(fw1), row2(fb1), twice(fw2), row2(fb2), twice(fw3), row2(fb3),
      row2(freq), fw4, fw4, cond, ada_w, ada_b.reshape(depth, 1, n_mod))


def _trig_table(a_rows, b_cols, scale):
    n = (a_rows[:, None].astype(np.int64) * b_cols[None, :].astype(np.int64)) % (4 * SEQ)
    ang = n.astype(np.float64) * (math.pi / (2 * SEQ))
    return (np.cos(ang) * scale).astype(np.float32), (np.sin(ang) * scale).astype(np.float32)


@functools.lru_cache(maxsize=None)
def _dft_tables():
    idx = np.arange(QUARTER)
    ana_c, ana_s, syn_c, syn_s = [], [], [], []
    for r in range(4):
        c, s = _trig_table(2 * idx + 1, 4 * idx + r, 1.0)
        ana_c.append(c)
        ana_s.append(s)
        c, s = _trig_table(4 * idx + r, 2 * idx + 1, 1.0 / SEQ)
        syn_c.append(c)
        syn_s.append(s)
    fmap = np.concatenate([idx, HALF + idx])
    lag = np.arange(HALF)
    cee, see = _trig_table(2 * fmap + 1, 2 * lag, 1.0)
    ceo, seo = _trig_table(2 * fmap + 1, 2 * lag + 1, 1.0)
    return np.stack(ana_c + ana_s), np.stack(syn_c + syn_s), (cee, ceo, see, seo)


def _filter_dft_kernel(cee_ref, ceo_ref, see_ref, seo_ref, fs_ref, fd_ref, x_ref, g_ref, sc_ref, sh_ref,
                       p_ref, pm_ref, q_ref, qm_ref, h_ref):
    _norm_mod_rows(x_ref, g_ref, sc_ref, sh_ref, h_ref)
    pe = _dot(cee_ref[...], fs_ref[0:HALF, :])
    po = _dot(ceo_ref[...], fs_ref[HALF:SEQ, :])
    qe = _dot(see_ref[...], fd_ref[0:HALF, :])
    qo = _dot(seo_ref[...], fd_ref[HALF:SEQ, :])
    p_ref[...] = pe + po
    pm_ref[...] = pe - po
    q_ref[...] = qe + qo
    qm_ref[...] = qo - qe


def _filter_dft_and_first_norm(tables, taps_sum, taps_diff, x, g, sc, sh):
    tm, tn = HALF, 256
    n_cols = taps_sum.shape[1]
    b, l, d = x.shape
    ts = b * l // (n_cols // tn)
    per_batch = l // ts
    assert per_batch * ts == l
    x_spec = pl.BlockSpec((None, ts, d), lambda j, i: (j // per_batch, j % per_batch, 0))
    mod_spec = pl.BlockSpec((None, 1, d), lambda j, i: (j // per_batch, 0, 0))
    mat = pl.BlockSpec((tm, HALF), lambda j, i: (i, 0), pipeline_mode=pl.Buffered(1))
    tap = pl.BlockSpec((SEQ, tn), lambda j, i: (0, j))
    out_spec = pl.BlockSpec((tm, tn), lambda j, i: (i, j))
    out_sds = jax.ShapeDtypeStruct((HALF, n_cols), F32)
    return pl.pallas_call(
        _filter_dft_kernel,
        grid=(n_cols // tn, HALF // tm),
        in_specs=[mat, mat, mat, mat, tap, tap, x_spec, pl.BlockSpec((1, d), lambda j, i: (0, 0)),
                  mod_spec, mod_spec],
        out_specs=(out_spec,) * 4 + (x_spec,),
        out_shape=(out_sds,) * 4 + (jax.ShapeDtypeStruct((b, l, d), BF16),),
        compiler_params=_params("arbitrary", "arbitrary"),
        name="filter_dft",
    )(*tables, taps_sum, taps_diff, x, g.reshape(1, d), sc.reshape(b, 1, d), sh.reshape(b, 1, d))


def _dft4_fwd_kernel(tab_ref, z_ref, p_ref, pm_ref, q_ref, qm_ref, x_ref, zs_ref, *, slab):
    panels = zs_ref.shape[0]
    for k in range(panels):
        zs_ref[k] = z_ref[:, k * LANES:(k + 1) * LANES].astype(F32)
    zr = [jnp.concatenate([zs_ref[k, pl.ds(r, QUARTER, stride=4), :] for k in range(panels)],
                          axis=1).astype(BF16) for r in range(4)]

    for f0 in range(0, QUARTER, slab):
        rows = slice(f0, f0 + slab)
        mirror = slice(QUARTER + f0, QUARTER + f0 + slab)
        c = [_dot(tab_ref[r, rows, :], zr[r]) for r in range(4)]
        s = [_dot(tab_ref[4 + r, rows, :], zr[r]) for r in range(4)]
        e, o, em, om = c[0] + c[2], c[1] + c[3], c[0] - c[2], c[1] - c[3]
        se, so, sem, som = s[0] + s[2], s[1] + s[3], s[0] - s[2], s[1] - s[3]
        a = (e + o, em + som, em - som, e - o)
        b = (se + so, om - sem, sem + om, so - se)
        pk = (p_ref[rows, :], pm_ref[mirror, :], p_ref[mirror, :], pm_ref[rows, :])
        qk = (q_ref[rows, :], qm_ref[mirror, :], q_ref[mirror, :], qm_ref[rows, :])
        u = [a[k] * pk[k] - b[k] * qk[k] for k in range(4)]
        v = [a[k] * qk[k] + b[k] * pk[k] for k in range(4)]
        up, um, wp, wm = u[0] + u[3], u[0] - u[3], u[1] + u[2], u[1] - u[2]
        vp, vm, yp, ym = v[0] + v[3], v[0] - v[3], v[1] + v[2], v[1] - v[2]
        outs = (up + wp, um + yp, up - wp, um - yp,
                vm - ym, vp + wm, vm + ym, vp - wm)
        for n, val in enumerate(outs):
            x_ref[n, rows, :] = val.astype(x_ref.dtype)


def _dft4_fwd(tab, z_arr, z_col0, spectra, order):
    tn, slab = 512, 256
    nj = D_MODEL // tn
    zc = z_col0 // tn
    spec = pl.BlockSpec((HALF, tn), lambda j, b: (0, order * nj + j))
    return pl.pallas_call(
        functools.partial(_dft4_fwd_kernel, slab=slab),
        grid=(nj, BATCH),
        in_specs=[pl.BlockSpec(tab.shape, lambda j, b: (0, 0, 0), pipeline_mode=pl.Buffered(1)),
                  pl.BlockSpec((SEQ, tn), lambda j, b: (b, zc + j)), spec, spec, spec, spec],
        out_specs=pl.BlockSpec((8, QUARTER, tn), lambda j, b: (0, b, j)),
        out_shape=jax.ShapeDtypeStruct((8, BATCH * QUARTER, D_MODEL), BF16),
        scratch_shapes=[pltpu.VMEM((tn // LANES, SEQ, LANES), F32)],
        compiler_params=_params("arbitrary", "arbitrary"),
        name="dft_analysis",
    )(tab, z_arr, *spectra)


def _dft4_inv_kernel(tab_ref, x_ref, z_ref, gate_ref, skip_ref, o_ref, ys_ref):
    panels = ys_ref.shape[0]
    for r in range(4):
        y_r = _dot(tab_ref[r], x_ref[r]) + _dot(tab_ref[4 + r], x_ref[4 + r])
        for k in range(panels):
            ys_ref[k, pl.ds(r, QUARTER, stride=4), :] = y_r[:, k * LANES:(k + 1) * LANES]
    y = jnp.concatenate([ys_ref[k] for k in range(panels)], axis=1)
    y = y + z_ref[...].astype(F32) * skip_ref[...]
    o_ref[...] = (gate_ref[...].astype(F32) * y).astype(o_ref.dtype)


def _dft4_inv(tab, x, z_arr, z_col0, gate_arr, gate_col0, skip):
    tn = 512
    nj = D_MODEL // tn
    zc, gc = z_col0 // tn, gate_col0 // tn
    return pl.pallas_call(
        _dft4_inv_kernel,
        grid=(nj, BATCH),
        in_specs=[pl.BlockSpec(tab.shape, lambda j, b: (0, 0, 0), pipeline_mode=pl.Buffered(1)),
                  pl.BlockSpec((8, QUARTER, tn), lambda j, b: (0, b, j)),
                  pl.BlockSpec((SEQ, tn), lambda j, b: (b, zc + j)),
                  pl.BlockSpec((SEQ, tn), lambda j, b: (b, gc + j)),
                  pl.BlockSpec((1, tn), lambda j, b: (0, j))],
        out_specs=pl.BlockSpec((SEQ, tn), lambda j, b: (b, j)),
        out_shape=jax.ShapeDtypeStruct((TOKENS, D_MODEL), BF16),
        scratch_shapes=[pltpu.VMEM((tn // LANES, SEQ, LANES), F32)],
        compiler_params=_params("arbitrary", "arbitrary"),
        name="dft_synthesis",
    )(tab, x, z_arr, gate_arr, skip.reshape(1, D_MODEL))


def _mod_chunk(mod, l, k):
    return mod[l, :BATCH, k * D_MODEL:(k + 1) * D_MODEL]


def _mlp_block(x2d, h, mod, l, w1, w2):
    tm = tn = 1024
    n_i, n_j = TOKENS // tm, D_FF // tn
    slab = D_FF // (n_i * n_j)
    hid, w2_bf16 = _mm(
        [h], w1, w_layer=l, tm=tm, tn=tn, n_cols=D_FF, epi=_epi_relu2_and_cast,
        extras=(w2,), extra_specs=(pl.BlockSpec((None, slab, D_MODEL), lambda j, i: (l, j * n_i + i, 0)),),
        out_shapes=(jax.ShapeDtypeStruct((TOKENS, D_FF), BF16), jax.ShapeDtypeStruct((D_FF, D_MODEL), BF16)),
        out_specs=(pl.BlockSpec((tm, tn), lambda j, i: (i, j)),
                   pl.BlockSpec((slab, D_MODEL), lambda j, i: (j * n_i + i, 0))),
        name="mlp_up")
    return _mm_residual([hid], w2_bf16, None, x2d, _mod_chunk(mod, l, 5), 512, 512, "mlp_down")


def kernel(x, c, ctx, c_ctx, ada_w, ada_b, norm_g, lb_logits, ab_w_in, ab_conv_w, ab_gnorm_g, ab_w_out,
           hy_in_w, hy_short_w, hy_out_w, hy_fw1, hy_fb1, hy_fw2, hy_fb2, hy_fw3, hy_fb3, hy_fw4, hy_freq,
           hy_skip, mlp_w1, mlp_w2, final_g):
    d = D_MODEL
    cond = jnp.concatenate([c, c_ctx[None, :], jnp.zeros((3, d), F32)], axis=0)
    taps_sum, taps_diff, mod = _filter_taps_and_modulation(
        hy_fw1[0], hy_fb1[0], hy_fw2[0], hy_fb2[0], hy_fw3[0], hy_fb3[0], hy_fw4[0], hy_freq[0],
        cond, ada_w, ada_b)
    x2d = x.reshape(TOKENS, d)

    sh1, sc1, g1 = (mod[0, :BATCH, k * d:(k + 1) * d] for k in (0, 1, 2))
    analysis, synthesis, filter_tables = jax.tree.map(lambda t: jnp.asarray(t).astype(BF16), _dft_tables())
    *spectra, h = _filter_dft_and_first_norm(filter_tables, taps_sum, taps_diff, x, norm_g[0, 0], sc1, sh1)
    h = h.reshape(TOKENS, d)
    csh = jnp.broadcast_to(mod[0, BATCH:BATCH + 1, 0:d], (BATCH, d))
    csc = jnp.broadcast_to(mod[0, BATCH:BATCH + 1, d:2 * d], (BATCH, d))
    hc = _prep(ctx, norm_g[0, 0], csc, csh, BF16, CTX_LEN).reshape(BATCH * CTX_LEN, d)
    p = _mm_plain(h, ab_w_in, AB_IN_WIDTH, F32, _epi_store, 1024, 1024, "ab_in_proj", w_layer=0)
    cp = _mm_plain(hc, ab_w_in, 3 * A_WIDTH, F32, _epi_store, 1024, 1024, "ab_ctx_proj", w_layer=0)
    mix_a, mix_b = _even_mixer(p, cp, lb_logits, ab_gnorm_g[0], ab_conv_w[0], 0)
    x2d, h = _mm_residual_norm([mix_a, mix_b], ab_w_out, 0, x2d, g1, norm_g[0, 1], _mod_chunk(mod, 0, 4),
                               _mod_chunk(mod, 0, 3), 512, "ab_out_proj")
    x2d = _mlp_block(x2d, h, mod, 0, mlp_w1, mlp_w2)

    sh1, sc1, g1 = (mod[1, :BATCH, k * d:(k + 1) * d] for k in (0, 1, 2))
    h = _prep(x2d.reshape(BATCH, SEQ, d), norm_g[1, 0], sc1, sh1, BF16, 1024).reshape(TOKENS, d)
    pc = _mm([h], hy_in_w, w_layer=0, tm=1024, tn=1024, n_cols=3 * d, epi=_epi_conv3_grid,
             extras=(hy_short_w[0],), extra_specs=(pl.BlockSpec((3, 1024), lambda j, i: (0, j)),),
             out_shapes=jax.ShapeDtypeStruct((TOKENS, 3 * d), BF16),
             out_specs=pl.BlockSpec((1024, 1024), lambda j, i: (i, j)), name="hy_in_proj")
    coef = _dft4_fwd(analysis, pc, 2 * d, spectra, 0)
    z = _dft4_inv(synthesis, coef, pc, 2 * d, pc, 0, hy_skip[0, 0])
    coef = _dft4_fwd(analysis, z, 0, spectra, 1)
    z = _dft4_inv(synthesis, coef, z, 0, pc, d, hy_skip[0, 1])
    x2d, h = _mm_residual_norm([z], hy_out_w, 0, x2d, g1, norm_g[1, 1], _mod_chunk(mod, 1, 4),
                               _mod_chunk(mod, 1, 3), 512, "hy_out_proj")
    x2d = _mlp_block(x2d, h, mod, 1, mlp_w1, mlp_w2)

    zeros = jnp.zeros((BATCH, d), F32)
    return _prep(x2d.reshape(BATCH, SEQ, d), final_g, zeros, zeros, F32, 1024)
```

```python
import functools
import math

import numpy as np
import jax
import jax.numpy as jnp
from jax import lax
from jax.experimental import pallas as pl
from jax.experimental.pallas import tpu as pltpu

F32 = jnp.float32
BF16 = jnp.bfloat16

D_MODEL = 2048
BATCH = 4
SEQ = 2048
CTX_LEN = 256
GRID_W = 64
EPS = 1e-6
A_HEAD_DIM = 128
A_WIDTH = D_MODEL // 2
A_HEADS = A_WIDTH // A_HEAD_DIM
B_WIDTH = D_MODEL - A_WIDTH
AB_IN_WIDTH = 5 * A_WIDTH + 3 * B_WIDTH
HY_EMB = 33
HY_BANDS = (HY_EMB - 1) // 2
HY_DECAY_TARGET = 1e-2
HY_FAST_PCT = 0.3
HY_SLOW_PCT = 1.5
D_FF = 4 * D_MODEL
TOKENS = BATCH * SEQ
HALF = SEQ // 2
QUARTER = SEQ // 4

SCAN_CHUNK = 64
SCAN_LEVELS = (1, 2, 4, 8, 16, 32)

SUBLANES = 8
LANES = 128

VMEM_LIMIT_BYTES = 56 * 1024 * 1024


def _params(*sem):
    return pltpu.CompilerParams(dimension_semantics=sem, vmem_limit_bytes=VMEM_LIMIT_BYTES)


def _dot(a, b):
    return jnp.dot(a, b, preferred_element_type=F32)


def _dot_nt(a, b):
    return lax.dot_general(a, b, (((1,), (1,)), ((), ())), preferred_element_type=F32)


def _dot_tn(a, b):
    return lax.dot_general(a, b, (((0,), (0,)), ((), ())), preferred_element_type=F32)


def _dot_f32(a, b):
    return jnp.dot(a, b, precision=lax.Precision.HIGHEST, preferred_element_type=F32)


def _conv3_rows(t, w, group):
    rows = lax.broadcasted_iota(jnp.int32, t.shape, 0) % group
    prev = jnp.where(rows == 0, 0.0, pltpu.roll(t, 1, 0))
    nxt = jnp.where(rows == group - 1, 0.0, pltpu.roll(t, t.shape[0] - 1, 0))
    return w[0:1] * prev + w[1:2] * t + w[2:3] * nxt


def _norm_mod_rows(x_ref, g_ref, sc_ref, sh_ref, o_ref):
    scale = g_ref[...] * (1.0 + sc_ref[...])
    shift = sh_ref[...]
    piece = 64

    def body(i, carry):
        rows = pl.ds(pl.multiple_of(i * piece, piece), piece)
        x = x_ref[rows, :]
        ms = jnp.mean(x * x, axis=-1, keepdims=True)
        o_ref[rows, :] = (x * lax.rsqrt(ms + EPS) * scale + shift).astype(o_ref.dtype)
        return carry

    lax.fori_loop(0, x_ref.shape[0] // piece, body, 0, unroll=4)


def _prep_kernel(x_ref, g_ref, sc_ref, sh_ref, o_ref):
    _norm_mod_rows(x_ref, g_ref, sc_ref, sh_ref, o_ref)


def _prep(x, g, sc, sh, out_dtype, ts):
    b, l, d = x.shape
    return pl.pallas_call(
        _prep_kernel,
        grid=(b, l // ts),
        in_specs=[
            pl.BlockSpec((None, ts, d), lambda i, j: (i, j, 0)),
            pl.BlockSpec((1, d), lambda i, j: (0, 0)),
            pl.BlockSpec((None, 1, d), lambda i, j: (i, 0, 0)),
            pl.BlockSpec((None, 1, d), lambda i, j: (i, 0, 0)),
        ],
        out_specs=pl.BlockSpec((None, ts, d), lambda i, j: (i, j, 0)),
        out_shape=jax.ShapeDtypeStruct((b, l, d), out_dtype),
        compiler_params=_params("arbitrary", "arbitrary"),
        name="norm_mod",
    )(x, g.reshape(1, d), sc.reshape(b, 1, d), sh.reshape(b, 1, d))


def _mm_kernel(*refs, n_a, k_sizes, epi, cast_w):
    a_refs = refs[:n_a]
    w_ref = refs[n_a]
    rest = refs[n_a + 1:]
    if cast_w:
        w_ref, rest = rest[-1], rest[:-1]

        @pl.when(pl.program_id(1) == 0)
        def _():
            w_ref[...] = refs[n_a][...].astype(BF16)

    acc = None
    off = 0
    for a_ref, k in zip(a_refs, k_sizes):
        part = _dot(a_ref[...], w_ref[off:off + k, :])
        acc = part if acc is None else acc + part
        off += k
    epi(acc, *rest)


def _mm(a_list, w, *, tm, tn, n_cols, epi, extras, extra_specs, out_shapes, out_specs, name,
        w_layer=None, w_single_buffer=False):
    m = a_list[0].shape[0]
    k_sizes = tuple(a.shape[1] for a in a_list)
    k_total = sum(k_sizes)
    cast_w = w_layer is not None
    assert w.shape[-2] == k_total and m % tm == 0 and n_cols % tn == 0
    in_specs = [pl.BlockSpec((tm, k), lambda j, i: (i, 0)) for k in k_sizes]
    w_mode = dict(pipeline_mode=pl.Buffered(1)) if w_single_buffer else {}
    if cast_w:
        in_specs.append(pl.BlockSpec((None, k_total, tn), lambda j, i: (w_layer, 0, j), **w_mode))
    else:
        in_specs.append(pl.BlockSpec((k_total, tn), lambda j, i: (0, j), **w_mode))
    in_specs.extend(extra_specs)
    return pl.pallas_call(
        functools.partial(_mm_kernel, n_a=len(a_list), k_sizes=k_sizes, epi=epi, cast_w=cast_w),
        grid=(n_cols // tn, m // tm),
        in_specs=in_specs,
        out_specs=out_specs,
        out_shape=out_shapes,
        scratch_shapes=[pltpu.VMEM((k_total, tn), BF16)] if cast_w else [],
        compiler_params=_params("arbitrary", "arbitrary"),
        name=name,
    )(*a_list, w, *extras)


def _epi_store(acc, o_ref):
    o_ref[...] = acc.astype(o_ref.dtype)


def _epi_relu2_and_cast(acc, w_next_ref, o_ref, w_next_bf16_ref):
    r = jnp.maximum(acc, 0.0)
    o_ref[...] = (r * r).astype(o_ref.dtype)
    w_next_bf16_ref[...] = w_next_ref[...].astype(w_next_bf16_ref.dtype)


def _epi_residual(acc, x_ref, gate_ref, o_ref):
    o_ref[...] = x_ref[...] + gate_ref[...] * acc


def _epi_residual_norm(acc, x_ref, gate_ref, g_ref, sc_ref, sh_ref, o_ref, h_ref):
    x = x_ref[...] + gate_ref[...] * acc
    o_ref[...] = x
    ms = jnp.mean(x * x, axis=-1, keepdims=True)
    xn = x * lax.rsqrt(ms + EPS) * g_ref[...]
    h_ref[...] = (xn * (1.0 + sc_ref[...]) + sh_ref[...]).astype(h_ref.dtype)


def _epi_conv3_grid(acc, w_ref, o_ref):
    o_ref[...] = _conv3_rows(acc, w_ref[...], GRID_W).astype(o_ref.dtype)


def _mm_plain(a, w, n_cols, out_dtype, epi, tm, tn, name, w_layer=None):
    m = a.shape[0]
    return _mm([a], w, tm=tm, tn=tn, n_cols=n_cols, epi=epi, extras=(), extra_specs=(),
               out_shapes=jax.ShapeDtypeStruct((m, n_cols), out_dtype),
               out_specs=pl.BlockSpec((tm, tn), lambda j, i: (i, j)), name=name, w_layer=w_layer)


def _mm_residual(a_list, w, w_layer, x, gate, tm, tn, name, w_single_buffer=False):
    m, n = x.shape
    tiles_per_batch = SEQ // tm
    return _mm(a_list, w, tm=tm, tn=tn, n_cols=n, epi=_epi_residual,
               extras=(x, gate.reshape(BATCH, 1, n)),
               extra_specs=(pl.BlockSpec((tm, tn), lambda j, i: (i, j)),
                            pl.BlockSpec((None, 1, tn), lambda j, i: (i // tiles_per_batch, 0, j))),
               out_shapes=jax.ShapeDtypeStruct((m, n), F32),
               out_specs=pl.BlockSpec((tm, tn), lambda j, i: (i, j)), name=name,
               w_layer=w_layer, w_single_buffer=w_single_buffer)


def _mm_residual_norm(a_list, w, w_layer, x, gate, norm_g, sc, sh, tm, name):
    m, n = x.shape
    tiles_per_batch = SEQ // tm

    def per_batch(v):
        return v.reshape(BATCH, 1, n), pl.BlockSpec((None, 1, n), lambda j, i: (i // tiles_per_batch, 0, 0))

    (gate3, gate_spec), (sc3, sc_spec), (sh3, sh_spec) = per_batch(gate), per_batch(sc), per_batch(sh)
    rows = pl.BlockSpec((tm, n), lambda j, i: (i, 0))
    return _mm(a_list, w, tm=tm, tn=n, n_cols=n, epi=_epi_residual_norm,
               extras=(x, gate3, norm_g.reshape(1, n), sc3, sh3),
               extra_specs=(rows, gate_spec, pl.BlockSpec((1, n), lambda j, i: (0, 0)), sc_spec, sh_spec),
               out_shapes=(jax.ShapeDtypeStruct((m, n), F32), jax.ShapeDtypeStruct((m, n), BF16)),
               out_specs=(rows, rows), name=name, w_layer=w_layer, w_single_buffer=True)


def _row_groups(x):
    return [x[k:k + SUBLANES] for k in range(0, x.shape[0], SUBLANES)]


def _cumsum_groups(groups, sub):
    out, carry = [], None
    for g in groups:
        for s in (1, 2, 4):
            g = g + jnp.where(sub >= s, pltpu.roll(g, s, 0), 0.0)
        if carry is not None:
            g = g + carry
        out.append(g)
        carry = g[SUBLANES - 1:SUBLANES]
    return out


def _hold_boundary(cum, w, k, sub):
    shape = cum[k].shape
    if w >= SUBLANES // 2:
        r = (k * SUBLANES // (2 * w)) * 2 * w + w - 1
        return jnp.broadcast_to(cum[r // SUBLANES][r % SUBLANES:r % SUBLANES + 1], shape)
    assert w == 2
    return jnp.where(sub < 4, jnp.broadcast_to(cum[k][1:2], shape), jnp.broadcast_to(cum[k][5:6], shape))


def _scan_kernel(zf_ref, zb_ref, v_ref, q_ref, g_ref, u_ref, gb_ref, gc_ref, czf_ref, czb_ref, cv_ref,
                 lbl_ref, gn_ref, cw_ref, o_ref, oc_ref,
                 of_ref, ob_ref, st_ref, qs_ref, inc_ref, dec_ref, kl_ref, lv_ref, att_ref, *, layer):
    c = SCAN_CHUNK
    n_chunks = SEQ // c
    n_ctx = CTX_LEN // c
    z_refs = (zf_ref, zb_ref)
    cz_refs = (czf_ref, czb_ref)
    out_refs = (of_ref, ob_ref)

    logit_rows = [lbl_ref[:, k, :] for k in range(lbl_ref.shape[1])]
    top = functools.reduce(jnp.maximum, logit_rows)
    exps = [jnp.exp(r - top) for r in logit_rows]
    lbs = sum(exps[:layer + 1]) / sum(exps)

    sub = lax.broadcasted_iota(jnp.int32, (SUBLANES, A_HEAD_DIM), 0)
    n_groups = c // SUBLANES

    def rows(groups):
        return jnp.concatenate(groups, axis=0)

    def chunk(d, z, q, v, want_out):
        lb = lbs[d:d + 1]
        f = lb + (1.0 - lb) * jax.nn.sigmoid(z)
        kk = 1.0 - f
        fg, kg = _row_groups(f), _row_groups(kk)
        lg = [jnp.log2(g) for g in fg]
        cum = _cumsum_groups(lg, sub)
        total = cum[-1][SUBLANES - 1:SUBLANES]
        if d == 0:
            pos = cum
            q_in = [jnp.exp2(g) for g in cum]
            k_out = [jnp.exp2(total - g) for g in cum]
        else:
            pos = [a - b for a, b in zip(cum, lg)]
            q_in = [jnp.exp2(total - g) for g in pos]
            k_out = [jnp.exp2(g) for g in pos]
        k_leave = rows([a * b for a, b in zip(kg, k_out)]).astype(BF16)
        decay = jnp.exp2(total)
        if not want_out:
            return k_leave, decay, None, None, None
        qg = _row_groups(q)
        o_diag = jnp.sum(q * kk, axis=1, keepdims=True) * v
        upper, lower = (qg, kg) if d == 0 else (kg, qg)
        operands = []
        for w in SCAN_LEVELS:
            y = []
            for k in range(n_groups):
                if w == 1:
                    odd = (sub & 1) != 0
                    y.append(jnp.where(odd, qg[k] * fg[k], kg[k]) if d == 0
                             else jnp.where(odd, kg[k], qg[k] * fg[k]))
                    continue
                hold = _hold_boundary(cum, w, k, sub)
                if w >= SUBLANES:
                    bit_set = (k * SUBLANES) & w != 0
                    y.append(upper[k] * jnp.exp2(pos[k] - hold) if bit_set
                             else lower[k] * jnp.exp2(hold - pos[k]))
                else:
                    bit = (sub & w) != 0
                    sign = jnp.where(bit, 1.0, -1.0)
                    y.append(jnp.where(bit, upper[k], lower[k]) * jnp.exp2((pos[k] - hold) * sign))
            operands.append(rows(y).astype(BF16))
        qs = rows([a * b for a, b in zip(qg, q_in)]).astype(BF16)
        return k_leave, decay, qs, o_diag, operands

    def within_chunk(d, operands):
        t_idx = lax.broadcasted_iota(jnp.int32, (c, c), 0)
        s_idx = lax.broadcasted_iota(jnp.int32, (c, c), 1)
        split = t_idx ^ s_idx
        att = None
        for w, yb in reversed(list(zip(SCAN_LEVELS, operands))):
            scores = _dot_nt(yb, yb)
            att = scores if att is None else jnp.where(split < 2 * w, scores, att)
        return jnp.where(t_idx > s_idx if d == 0 else t_idx < s_idx, att, 0.0)

    st_ref[...] = jnp.zeros_like(st_ref)

    def ctx_body(i, carry):
        for d in range(2):
            ci = i if d == 0 else n_ctx - 1 - i
            rows_i = pl.ds(pl.multiple_of(ci * c, c), c)
            k_leave, decay, _, _, _ = chunk(d, cz_refs[d][rows_i, :], None, None, False)
            st_ref[d] = st_ref[d] * decay + _dot_tn(cv_ref[rows_i, :].astype(BF16), k_leave)
        return carry

    lax.fori_loop(0, n_ctx, ctx_body, 0, unroll=2)

    def chunk_rows(ci):
        start = ci * c
        return pl.ds(start if isinstance(ci, int) else pl.multiple_of(start, c), c)

    def chunk_of(d, t):
        return t if d == 0 else n_chunks - 1 - t

    def prepare(t):
        for d in range(2):
            ci = chunk_of(d, t)
            rows_i = chunk_rows(ci)
            k_leave, decay, qs, o_diag, operands = chunk(d, z_refs[d][rows_i, :], q_ref[rows_i, :],
                                                         v_ref[rows_i, :], True)
            kl_ref[d, rows_i, :] = k_leave
            dec_ref[d, ci] = decay
            qs_ref[d, rows_i, :] = qs
            out_refs[d][rows_i, :] = o_diag
            for lvl, yb in enumerate(operands):
                lv_ref[d, lvl, rows_i, :] = yb

    def scores(t):
        for d in range(2):
            ci = chunk_of(d, t)
            rows_i = chunk_rows(ci)
            inc_ref[d, ci] = _dot_tn(v_ref[rows_i, :].astype(BF16), kl_ref[d, rows_i, :])
            operands = [lv_ref[d, lvl, rows_i, :] for lvl in range(len(SCAN_LEVELS))]
            att_ref[d, rows_i, :] = within_chunk(d, operands).astype(BF16)

    def carry_state(t):
        for d in range(2):
            ci = chunk_of(d, t)
            rows_i = chunk_rows(ci)
            state = st_ref[d]
            out_refs[d][rows_i, :] += (_dot(att_ref[d, rows_i, :], v_ref[rows_i, :].astype(BF16))
                                       + _dot_nt(qs_ref[d, rows_i, :], state.astype(BF16)))
            st_ref[d] = state * dec_ref[d, ci] + inc_ref[d, ci]

    pair = 16
    n_trips = n_chunks // pair

    def stage(fn, trip):
        for j in range(pair):
            fn(trip * pair + j)

    stage(prepare, 0)
    stage(scores, 0)
    stage(prepare, 1)

    def skewed_body(i, carry):
        stage(carry_state, i - 2)
        stage(scores, i - 1)
        stage(prepare, i)
        return carry

    lax.fori_loop(2, n_trips, skewed_body, 0)
    stage(carry_state, n_trips - 2)
    stage(scores, n_trips - 1)
    stage(carry_state, n_trips - 1)

    rb = 256
    gn = gn_ref[...]
    cw = cw_ref[...]

    def read_body(i, carry):
        rows_i = pl.ds(pl.multiple_of(i * rb, rb), rb)
        o = of_ref[rows_i, :] + ob_ref[rows_i, :]
        ms = jnp.mean(o * o, axis=-1, keepdims=True)
        on = o * lax.rsqrt(ms + EPS) * gn
        g = g_ref[rows_i, :]
        o_ref[rows_i, :] = (on * (g * jax.nn.sigmoid(g))).astype(o_ref.dtype)
        conv = _conv3_rows(gc_ref[rows_i, :] * u_ref[rows_i, :], cw, GRID_W)
        oc_ref[rows_i, :] = (gb_ref[rows_i, :] * conv).astype(oc_ref.dtype)
        return carry

    lax.fori_loop(0, SEQ // rb, read_body, 0, unroll=2)


def _even_mixer(p, cp, lb_logits, gnorm_g, conv_w, layer):
    h = A_HEADS
    hd = A_HEAD_DIM
    n_lb = lb_logits.shape[1]

    def col(k):
        return pl.BlockSpec((SEQ, hd), lambda b, j, k=k: (b, k * h + j))

    def ccol(k):
        return pl.BlockSpec((CTX_LEN, hd), lambda b, j, k=k: (b, k * h + j))

    out_spec = pl.BlockSpec((SEQ, hd), lambda b, j: (b, j))
    return pl.pallas_call(
        functools.partial(_scan_kernel, layer=layer),
        grid=(BATCH, h),
        in_specs=[col(0), col(1), col(2), col(3), col(4), col(5), col(6), col(7),
                  ccol(0), ccol(1), ccol(2),
                  pl.BlockSpec((2, n_lb, hd), lambda b, j: (0, 0, j)),
                  pl.BlockSpec((1, hd), lambda b, j: (0, j)),
                  pl.BlockSpec((3, hd), lambda b, j: (0, j))],
        out_specs=(out_spec, out_spec),
        out_shape=(jax.ShapeDtypeStruct((TOKENS, A_WIDTH), BF16),
                   jax.ShapeDtypeStruct((TOKENS, B_WIDTH), BF16)),
        scratch_shapes=[pltpu.VMEM((SEQ, hd), F32), pltpu.VMEM((SEQ, hd), F32),
                        pltpu.VMEM((2, hd, hd), F32), pltpu.VMEM((2, SEQ, hd), BF16),
                        pltpu.VMEM((2, SEQ // SCAN_CHUNK, hd, hd), F32),
                        pltpu.VMEM((2, SEQ // SCAN_CHUNK, 1, hd), F32),
                        pltpu.VMEM((2, SEQ, hd), BF16),
                        pltpu.VMEM((2, len(SCAN_LEVELS), SEQ, hd), BF16),
                        pltpu.VMEM((2, SEQ, SCAN_CHUNK), BF16)],
        compiler_params=_params("arbitrary", "arbitrary"),
        name="hgrn_scan",
    )(p, p, p, p, p, p, p, p, cp, cp, cp, lb_logits, gnorm_g.reshape(1, A_WIDTH), conv_w)


def _split_bf16(x):
    hi = x.astype(BF16)
    return hi, (x - hi.astype(F32)).astype(BF16)


def _filter_kernel(z_ref, w1_ref, b1_ref, w2_ref, b2_ref, w3_ref, b3_ref, fr_ref,
                   w4f_ref, w4b_ref, cond_ref, aw_ref, ab_ref,
                   sum_ref, diff_ref, mod_ref, hid_ref, *, tc):
    o = pl.program_id(0)
    j = pl.program_id(1)

    cond = cond_ref[...]
    mod_ref[...] = _dot((cond * jax.nn.sigmoid(cond)).astype(BF16), aw_ref[...].astype(BF16)) + ab_ref[...]

    @pl.when((o == 0) & (j == 0))
    def _():
        fr = fr_ref[...]
        h = jnp.sin(fr * (_dot_f32(z_ref[...], w1_ref[...]) + b1_ref[...]))
        h = jnp.sin(fr * (_dot_f32(h, w2_ref[...]) + b2_ref[...]))
        h = jnp.sin(fr * (_dot_f32(h, w3_ref[...]) + b3_ref[...]))
        hi, lo = _split_bf16(h)
        hid_ref[...] = jnp.concatenate([hi, lo, hi], axis=1)

    def project(w_ref):
        w_hi, w_lo = _split_bf16(w_ref[...])
        zero = jnp.zeros_like(w_hi)
        first = _dot(hid_ref[...], jnp.concatenate([w_hi, zero, w_hi, zero, w_lo, zero], axis=0))
        second = _dot(hid_ref[...], jnp.concatenate([zero, w_hi, zero, w_hi, zero, w_lo], axis=0))
        return jnp.concatenate([first, second], axis=0)

    rows = lax.broadcasted_iota(jnp.int32, (SEQ, tc), 0)
    chan = lax.broadcasted_iota(jnp.int32, (SEQ, tc), 1) + j * tc
    lag = jnp.where(rows < HALF, 2 * rows, 2 * rows - (SEQ - 1))
    t = lag.astype(F32) * (1.0 / (SEQ - 1))
    max_decay = math.log(HY_DECAY_TARGET) / HY_FAST_PCT
    min_decay = math.log(HY_DECAY_TARGET) / HY_SLOW_PCT
    deltas = jnp.abs(min_decay + chan.astype(F32) * ((max_decay - min_decay) / (D_MODEL - 1)))
    window = jnp.exp(-t * deltas)
    fw = project(w4f_ref) * window
    bw = project(w4b_ref) * window
    first = rows == 0
    a = fw + jnp.where(first, bw, 0.0)
    bb = jnp.where(first, 0.0, bw)
    inv = 1.0 / jnp.sum(jnp.abs(a) + jnp.abs(bb), axis=0, keepdims=True)
    sum_ref[...] = ((a + bb) * inv).astype(sum_ref.dtype)
    diff_ref[...] = ((a - bb) * inv).astype(diff_ref.dtype)


@functools.lru_cache(maxsize=None)
def _filter_positions():
    l = SEQ
    t = np.linspace(0.0, 1.0, l, dtype=np.float32)
    w = (2.0 * math.pi * np.arange(l, dtype=np.float32) / l).astype(np.float32)
    bands = np.linspace(1e-4, HY_BANDS - 1, HY_BANDS, dtype=np.float32)
    ang = w[:, None] * bands[None, :]
    z = np.concatenate([t[:, None], np.cos(ang), -np.sin(ang)], axis=-1).astype(np.float32)
    z = np.concatenate([z[0::2], z[1::2]], axis=0)
    z = np.pad(z, ((0, 0), (0, LANES // 2 - HY_EMB)))
    return np.concatenate([z[:l // 2], z[l // 2:]], axis=1)


def _filter_taps_and_modulation(fw1, fb1, fw2, fb2, fw3, fb3, fw4, freq, cond, ada_w, ada_b):
    l = SEQ
    tc = 256
    nj = D_MODEL // tc
    depth, d, n_mod = ada_w.shape
    assert depth == 2 and n_mod % (nj * LANES) == 0
    ta = n_mod // nj
    hw = LANES // 2

    def pad(a, rows, cols):
        return jnp.pad(a, ((0, rows - a.shape[0]), (0, cols - a.shape[1])))

    def twice(a):
        a = pad(a, hw, hw)
        zero = jnp.zeros_like(a)
        return jnp.concatenate([jnp.concatenate([a, zero], axis=1), jnp.concatenate([zero, a], axis=1)], axis=0)

    def row2(v):
        v = pad(v[None, :], 1, hw)
        return jnp.concatenate([v, v], axis=1)

    def small(shape):
        return pl.BlockSpec(shape, lambda o, j: (0, 0))

    out_sds = jax.ShapeDtypeStruct((l, 2 * D_MODEL), BF16)
    out_spec = pl.BlockSpec((l, tc), lambda o, j: (0, o * nj + j))
    assert fw4.shape[0] == hw
    side_f = pl.BlockSpec((hw, tc), lambda o, j: (0, 2 * o * nj + j))
    side_b = pl.BlockSpec((hw, tc), lambda o, j: (0, (2 * o + 1) * nj + j))
    sq, vec = small((LANES, LANES)), small((1, LANES))
    return pl.pallas_call(
        functools.partial(_filter_kernel, tc=tc),
        grid=(2, nj),
        in_specs=[small((l // 2, LANES)), sq, vec, sq, vec, sq, vec, vec, side_f, side_b,
                  small((8, d)),
                  pl.BlockSpec((None, d, ta), lambda o, j: (o, 0, j)),
                  pl.BlockSpec((None, 1, ta), lambda o, j: (o, 0, j))],
        out_specs=(out_spec, out_spec, pl.BlockSpec((None, 8, ta), lambda o, j: (o, 0, j))),
        out_shape=(out_sds, out_sds, jax.ShapeDtypeStruct((depth, 8, n_mod), F32)),
        scratch_shapes=[pltpu.VMEM((l // 2, 3 * LANES), BF16)],
        compiler_params=_params("arbitrary", "arbitrary"),
        name="hyena_filter",
    )(jnp.asarray(_filter_positions()), twice(fw1), row2(fb1), twice(fw2), row2(fb2), twice(fw3), row2(fb3),
      row2(freq), fw4, fw4, cond, ada_w, ada_b.reshape(depth, 1, n_mod))


def _trig_table(a_rows, b_cols, scale):
    n = (a_rows[:, None].astype(np.int64) * b_cols[None, :].astype(np.int64)) % (4 * SEQ)
    ang = n.astype(np.float64) * (math.pi / (2 * SEQ))
    return (np.cos(ang) * scale).astype(np.float32), (np.sin(ang) * scale).astype(np.float32)


@functools.lru_cache(maxsize=None)
def _dft_tables():
    idx = np.arange(QUARTER)
    ana_c, ana_s, syn_c, syn_s = [], [], [], []
    for r in range(4):
        c, s = _trig_table(2 * idx + 1, 4 * idx + r, 1.0)
        ana_c.append(c)
        ana_s.append(s)
        c, s = _trig_table(4 * idx + r, 2 * idx + 1, 1.0 / SEQ)
        syn_c.append(c)
        syn_s.append(s)
    fmap = np.concatenate([idx, HALF + idx])
    lag = np.arange(HALF)
    cee, see = _trig_table(2 * fmap + 1, 2 * lag, 1.0)
    ceo, seo = _trig_table(2 * fmap + 1, 2 * lag + 1, 1.0)
    return np.stack(ana_c + ana_s), np.stack(syn_c + syn_s), (cee, ceo, see, seo)


def _filter_dft_kernel(cee_ref, ceo_ref, see_ref, seo_ref, fs_ref, fd_ref, x_ref, g_ref, sc_ref, sh_ref,
                       p_ref, pm_ref, q_ref, qm_ref, h_ref):
    _norm_mod_rows(x_ref, g_ref, sc_ref, sh_ref, h_ref)
    pe = _dot(cee_ref[...], fs_ref[0:HALF, :])
    po = _dot(ceo_ref[...], fs_ref[HALF:SEQ, :])
    qe = _dot(see_ref[...], fd_ref[0:HALF, :])
    qo = _dot(seo_ref[...], fd_ref[HALF:SEQ, :])
    p_ref[...] = pe + po
    pm_ref[...] = pe - po
    q_ref[...] = qe + qo
    qm_ref[...] = qo - qe


def _filter_dft_and_first_norm(tables, taps_sum, taps_diff, x, g, sc, sh):
    tm, tn = HALF, 256
    n_cols = taps_sum.shape[1]
    b, l, d = x.shape
    ts = b * l // (n_cols // tn)
    per_batch = l // ts
    assert per_batch * ts == l
    x_spec = pl.BlockSpec((None, ts, d), lambda j, i: (j // per_batch, j % per_batch, 0))
    mod_spec = pl.BlockSpec((None, 1, d), lambda j, i: (j // per_batch, 0, 0))
    mat = pl.BlockSpec((tm, HALF), lambda j, i: (i, 0), pipeline_mode=pl.Buffered(1))
    tap = pl.BlockSpec((SEQ, tn), lambda j, i: (0, j))
    out_spec = pl.BlockSpec((tm, tn), lambda j, i: (i, j))
    out_sds = jax.ShapeDtypeStruct((HALF, n_cols), F32)
    return pl.pallas_call(
        _filter_dft_kernel,
        grid=(n_cols // tn, HALF // tm),
        in_specs=[mat, mat, mat, mat, tap, tap, x_spec, pl.BlockSpec((1, d), lambda j, i: (0, 0)),
                  mod_spec, mod_spec],
        out_specs=(out_spec,) * 4 + (x_spec,),
        out_shape=(out_sds,) * 4 + (jax.ShapeDtypeStruct((b, l, d), BF16),),
        compiler_params=_params("arbitrary", "arbitrary"),
        name="filter_dft",
    )(*tables, taps_sum, taps_diff, x, g.reshape(1, d), sc.reshape(b, 1, d), sh.reshape(b, 1, d))


def _dft4_fwd_kernel(tab_ref, z_ref, p_ref, pm_ref, q_ref, qm_ref, x_ref, zs_ref, *, slab):
    panels = zs_ref.shape[0]
    for k in range(panels):
        zs_ref[k] = z_ref[:, k * LANES:(k + 1) * LANES].astype(F32)
    zr = [jnp.concatenate([zs_ref[k, pl.ds(r, QUARTER, stride=4), :] for k in range(panels)],
                          axis=1).astype(BF16) for r in range(4)]

    for f0 in range(0, QUARTER, slab):
        rows = slice(f0, f0 + slab)
        mirror = slice(QUARTER + f0, QUARTER + f0 + slab)
        c = [_dot(tab_ref[r, rows, :], zr[r]) for r in range(4)]
        s = [_dot(tab_ref[4 + r, rows, :], zr[r]) for r in range(4)]
        e, o, em, om = c[0] + c[2], c[1] + c[3], c[0] - c[2], c[1] - c[3]
        se, so, sem, som = s[0] + s[2], s[1] + s[3], s[0] - s[2], s[1] - s[3]
        a = (e + o, em + som, em - som, e - o)
        b = (se + so, om - sem, sem + om, so - se)
        pk = (p_ref[rows, :], pm_ref[mirror, :], p_ref[mirror, :], pm_ref[rows, :])
        qk = (q_ref[rows, :], qm_ref[mirror, :], q_ref[mirror, :], qm_ref[rows, :])
        u = [a[k] * pk[k] - b[k] * qk[k] for k in range(4)]
        v = [a[k] * qk[k] + b[k] * pk[k] for k in range(4)]
        up, um, wp, wm = u[0] + u[3], u[0] - u[3], u[1] + u[2], u[1] - u[2]
        vp, vm, yp, ym = v[0] + v[3], v[0] - v[3], v[1] + v[2], v[1] - v[2]
        outs = (up + wp, um + yp, up - wp, um - yp,
                vm - ym, vp + wm, vm + ym, vp - wm)
        for n, val in enumerate(outs):
            x_ref[n, rows, :] = val.astype(x_ref.dtype)


def _dft4_fwd(tab, z_arr, z_col0, spectra, order):
    tn, slab = 512, 256
    nj = D_MODEL // tn
    zc = z_col0 // tn
    spec = pl.BlockSpec((HALF, tn), lambda j, b: (0, order * nj + j))
    return pl.pallas_call(
        functools.partial(_dft4_fwd_kernel, slab=slab),
        grid=(nj, BATCH),
        in_specs=[pl.BlockSpec(tab.shape, lambda j, b: (0, 0, 0), pipeline_mode=pl.Buffered(1)),
                  pl.BlockSpec((SEQ, tn), lambda j, b: (b, zc + j)), spec, spec, spec, spec],
        out_specs=pl.BlockSpec((8, QUARTER, tn), lambda j, b: (0, b, j)),
        out_shape=jax.ShapeDtypeStruct((8, BATCH * QUARTER, D_MODEL), BF16),
        scratch_shapes=[pltpu.VMEM((tn // LANES, SEQ, LANES), F32)],
        compiler_params=_params("arbitrary", "arbitrary"),
        name="dft_analysis",
    )(tab, z_arr, *spectra)


def _dft4_inv_kernel(tab_ref, x_ref, z_ref, gate_ref, skip_ref, o_ref, ys_ref):
    panels = ys_ref.shape[0]
    for r in range(4):
        y_r = _dot(tab_ref[r], x_ref[r]) + _dot(tab_ref[4 + r], x_ref[4 + r])
        for k in range(panels):
            ys_ref[k, pl.ds(r, QUARTER, stride=4), :] = y_r[:, k * LANES:(k + 1) * LANES]
    y = jnp.concatenate([ys_ref[k] for k in range(panels)], axis=1)
    y = y + z_ref[...].astype(F32) * skip_ref[...]
    o_ref[...] = (gate_ref[...].astype(F32) * y).astype(o_ref.dtype)


def _dft4_inv(tab, x, z_arr, z_col0, gate_arr, gate_col0, skip):
    tn = 512
    nj = D_MODEL // tn
    zc, gc = z_col0 // tn, gate_col0 // tn
    return pl.pallas_call(
        _dft4_inv_kernel,
        grid=(nj, BATCH),
        in_specs=[pl.BlockSpec(tab.shape, lambda j, b: (0, 0, 0), pipeline_mode=pl.Buffered(1)),
                  pl.BlockSpec((8, QUARTER, tn), lambda j, b: (0, b, j)),
                  pl.BlockSpec((SEQ, tn), lambda j, b: (b, zc + j)),
                  pl.BlockSpec((SEQ, tn), lambda j, b: (b, gc + j)),
                  pl.BlockSpec((1, tn), lambda j, b: (0, j))],
        out_specs=pl.BlockSpec((SEQ, tn), lambda j, b: (b, j)),
        out_shape=jax.ShapeDtypeStruct((TOKENS, D_MODEL), BF16),
        scratch_shapes=[pltpu.VMEM((tn // LANES, SEQ, LANES), F32)],
        compiler_params=_params("arbitrary", "arbitrary"),
        name="dft_synthesis",
    )(tab, x, z_arr, gate_arr, skip.reshape(1, D_MODEL))


def _mod_chunk(mod, l, k):
    return mod[l, :BATCH, k * D_MODEL:(k + 1) * D_MODEL]


def _mlp_block(x2d, h, mod, l, w1, w2):
    tm = tn = 1024
    n_i, n_j = TOKENS // tm, D_FF // tn
    slab = D_FF // (n_i * n_j)
    hid, w2_bf16 = _mm(
        [h], w1, w_layer=l, tm=tm, tn=tn, n_cols=D_FF, epi=_epi_relu2_and_cast,
        extras=(w2,), extra_specs=(pl.BlockSpec((None, slab, D_MODEL), lambda j, i: (l, j * n_i + i, 0)),),
        out_shapes=(jax.ShapeDtypeStruct((TOKENS, D_FF), BF16), jax.ShapeDtypeStruct((D_FF, D_MODEL), BF16)),
        out_specs=(pl.BlockSpec((tm, tn), lambda j, i: (i, j)),
                   pl.BlockSpec((slab, D_MODEL), lambda j, i: (j * n_i + i, 0))),
        name="mlp_up")
    return _mm_residual([hid], w2_bf16, None, x2d, _mod_chunk(mod, l, 5), 512, 512, "mlp_down")


def kernel(x, c, ctx, c_ctx, ada_w, ada_b, norm_g, lb_logits, ab_w_in, ab_conv_w, ab_gnorm_g, ab_w_out,
           hy_in_w, hy_short_w, hy_out_w, hy_fw1, hy_fb1, hy_fw2, hy_fb2, hy_fw3, hy_fb3, hy_fw4, hy_freq,
           hy_skip, mlp_w1, mlp_w2, final_g):
    d = D_MODEL
    cond = jnp.concatenate([c, c_ctx[None, :], jnp.zeros((3, d), F32)], axis=0)
    taps_sum, taps_diff, mod = _filter_taps_and_modulation(
        hy_fw1[0], hy_fb1[0], hy_fw2[0], hy_fb2[0], hy_fw3[0], hy_fb3[0], hy_fw4[0], hy_freq[0],
        cond, ada_w, ada_b)
    x2d = x.reshape(TOKENS, d)

    sh1, sc1, g1 = (mod[0, :BATCH, k * d:(k + 1) * d] for k in (0, 1, 2))
    analysis, synthesis, filter_tables = jax.tree.map(lambda t: jnp.asarray(t).astype(BF16), _dft_tables())
    *spectra, h = _filter_dft_and_first_norm(filter_tables, taps_sum, taps_diff, x, norm_g[0, 0], sc1, sh1)
    h = h.reshape(TOKENS, d)
    csh = jnp.broadcast_to(mod[0, BATCH:BATCH + 1, 0:d], (BATCH, d))
    csc = jnp.broadcast_to(mod[0, BATCH:BATCH + 1, d:2 * d], (BATCH, d))
    hc = _prep(ctx, norm_g[0, 0], csc, csh, BF16, CTX_LEN).reshape(BATCH * CTX_LEN, d)
    p = _mm_plain(h, ab_w_in, AB_IN_WIDTH, F32, _epi_store, 1024, 1024, "ab_in_proj", w_layer=0)
    cp = _mm_plain(hc, ab_w_in, 3 * A_WIDTH, F32, _epi_store, 1024, 1024, "ab_ctx_proj", w_layer=0)
    mix_a, mix_b = _even_mixer(p, cp, lb_logits, ab_gnorm_g[0], ab_conv_w[0], 0)
    x2d, h = _mm_residual_norm([mix_a, mix_b], ab_w_out, 0, x2d, g1, norm_g[0, 1], _mod_chunk(mod, 0, 4),
                               _mod_chunk(mod, 0, 3), 512, "ab_out_proj")
    x2d = _mlp_block(x2d, h, mod, 0, mlp_w1, mlp_w2)

    sh1, sc1, g1 = (mod[1, :BATCH, k * d:(k + 1) * d] for k in (0, 1, 2))
    h = _prep(x2d.reshape(BATCH, SEQ, d), norm_g[1, 0], sc1, sh1, BF16, 1024).reshape(TOKENS, d)
    pc = _mm([h], hy_in_w, w_layer=0, tm=1024, tn=1024, n_cols=3 * d, epi=_epi_conv3_grid,
             extras=(hy_short_w[0],), extra_specs=(pl.BlockSpec((3, 1024), lambda j, i: (0, j)),),
             out_shapes=jax.ShapeDtypeStruct((TOKENS, 3 * d), BF16),
             out_specs=pl.BlockSpec((1024, 1024), lambda j, i: (i, j)), name="hy_in_proj")
    coef = _dft4_fwd(analysis, pc, 2 * d, spectra, 0)
    z = _dft4_inv(synthesis, coef, pc, 2 * d, pc, 0, hy_skip[0, 0])
    coef = _dft4_fwd(analysis, z, 0, spectra, 1)
    z = _dft4_inv(synthesis, coef, z, 0, pc, d, hy_skip[0, 1])
    x2d, h = _mm_residual_norm([z], hy_out_w, 0, x2d, g1, norm_g[1, 1], _mod_chunk(mod, 1, 4),
                               _mod_chunk(mod, 1, 3), 512, "hy_out_proj")
    x2d = _mlp_block(x2d, h, mod, 1, mlp_w1, mlp_w2)

    zeros = jnp.zeros((BATCH, d), F32)
    return _prep(x2d.reshape(BATCH, SEQ, d), final_g, zeros, zeros, F32, 1024)
```

```python
import functools
import math

import numpy as np
import jax
import jax.numpy as jnp
from jax import lax
from jax.experimental import pallas as pl
from jax.experimental.pallas import tpu as pltpu

F32 = jnp.float32
BF16 = jnp.bfloat16

D_MODEL = 2048
BATCH = 4
SEQ = 2048
CTX_LEN = 256
GRID_W = 64
EPS = 1e-6
A_HEAD_DIM = 128
A_WIDTH = D_MODEL // 2
A_HEADS = A_WIDTH // A_HEAD_DIM
B_WIDTH = D_MODEL - A_WIDTH
AB_IN_WIDTH = 5 * A_WIDTH + 3 * B_WIDTH
HY_EMB = 33
HY_BANDS = (HY_EMB - 1) // 2
HY_DECAY_TARGET = 1e-2
HY_FAST_PCT = 0.3
HY_SLOW_PCT = 1.5
D_FF = 4 * D_MODEL
TOKENS = BATCH * SEQ
HALF = SEQ // 2
QUARTER = SEQ // 4

SCAN_CHUNK = 64
SCAN_LEVELS = (1, 2, 4, 8, 16, 32)

SUBLANES = 8
LANES = 128

VMEM_LIMIT_BYTES = 56 * 1024 * 1024


def _params(*sem):
    return pltpu.CompilerParams(dimension_semantics=sem, vmem_limit_bytes=VMEM_LIMIT_BYTES)


def _dot(a, b):
    return jnp.dot(a, b, preferred_element_type=F32)


def _dot_nt(a, b):
    return lax.dot_general(a, b, (((1,), (1,)), ((), ())), preferred_element_type=F32)


def _dot_tn(a, b):
    return lax.dot_general(a, b, (((0,), (0,)), ((), ())), preferred_element_type=F32)


def _dot_f32(a, b):
    return jnp.dot(a, b, precision=lax.Precision.HIGHEST, preferred_element_type=F32)


def _conv3_rows(t, w, group):
    rows = lax.broadcasted_iota(jnp.int32, t.shape, 0) % group
    prev = jnp.where(rows == 0, 0.0, pltpu.roll(t, 1, 0))
    nxt = jnp.where(rows == group - 1, 0.0, pltpu.roll(t, t.shape[0] - 1, 0))
    return w[0:1] * prev + w[1:2] * t + w[2:3] * nxt


def _norm_mod_rows(x_ref, g_ref, sc_ref, sh_ref, o_ref):
    scale = g_ref[...] * (1.0 + sc_ref[...])
    shift = sh_ref[...]
    piece = 64

    def body(i, carry):
        rows = pl.ds(pl.multiple_of(i * piece, piece), piece)
        x = x_ref[rows, :]
        ms = jnp.mean(x * x, axis=-1, keepdims=True)
        o_ref[rows, :] = (x * lax.rsqrt(ms + EPS) * scale + shift).astype(o_ref.dtype)
        return carry

    lax.fori_loop(0, x_ref.shape[0] // piece, body, 0, unroll=4)


def _prep_kernel(x_ref, g_ref, sc_ref, sh_ref, o_ref):
    _norm_mod_rows(x_ref, g_ref, sc_ref, sh_ref, o_ref)


def _prep(x, g, sc, sh, out_dtype, ts):
    b, l, d = x.shape
    return pl.pallas_call(
        _prep_kernel,
        grid=(b, l // ts),
        in_specs=[
            pl.BlockSpec((None, ts, d), lambda i, j: (i, j, 0)),
            pl.BlockSpec((1, d), lambda i, j: (0, 0)),
            pl.BlockSpec((None, 1, d), lambda i, j: (i, 0, 0)),
            pl.BlockSpec((None, 1, d), lambda i, j: (i, 0, 0)),
        ],
        out_specs=pl.BlockSpec((None, ts, d), lambda i, j: (i, j, 0)),
        out_shape=jax.ShapeDtypeStruct((b, l, d), out_dtype),
        compiler_params=_params("arbitrary", "arbitrary"),
        name="norm_mod",
    )(x, g.reshape(1, d), sc.reshape(b, 1, d), sh.reshape(b, 1, d))


def _mm_kernel(*refs, n_a, k_sizes, epi, cast_w):
    a_refs = refs[:n_a]
    w_ref = refs[n_a]
    rest = refs[n_a + 1:]
    if cast_w:
        w_ref, rest = rest[-1], rest[:-1]

        @pl.when(pl.program_id(1) == 0)
        def _():
            w_ref[...] = refs[n_a][...].astype(BF16)

    acc = None
    off = 0
    for a_ref, k in zip(a_refs, k_sizes):
        part = _dot(a_ref[...], w_ref[off:off + k, :])
        acc = part if acc is None else acc + part
        off += k
    epi(acc, *rest)


def _mm(a_list, w, *, tm, tn, n_cols, epi, extras, extra_specs, out_shapes, out_specs, name,
        w_layer=None, w_single_buffer=False):
    m = a_list[0].shape[0]
    k_sizes = tuple(a.shape[1] for a in a_list)
    k_total = sum(k_sizes)
    cast_w = w_layer is not None
    assert w.shape[-2] == k_total and m % tm == 0 and n_cols % tn == 0
    in_specs = [pl.BlockSpec((tm, k), lambda j, i: (i, 0)) for k in k_sizes]
    w_mode = dict(pipeline_mode=pl.Buffered(1)) if w_single_buffer else {}
    if cast_w:
        in_specs.append(pl.BlockSpec((None, k_total, tn), lambda j, i: (w_layer, 0, j), **w_mode))
    else:
        in_specs.append(pl.BlockSpec((k_total, tn), lambda j, i: (0, j), **w_mode))
    in_specs.extend(extra_specs)
    return pl.pallas_call(
        functools.partial(_mm_kernel, n_a=len(a_list), k_sizes=k_sizes, epi=epi, cast_w=cast_w),
        grid=(n_cols // tn, m // tm),
        in_specs=in_specs,
        out_specs=out_specs,
        out_shape=out_shapes,
        scratch_shapes=[pltpu.VMEM((k_total, tn), BF16)] if cast_w else [],
        compiler_params=_params("arbitrary", "arbitrary"),
        name=name,
    )(*a_list, w, *extras)


def _epi_store(acc, o_ref):
    o_ref[...] = acc.astype(o_ref.dtype)


def _epi_relu2_and_cast(acc, w_next_ref, o_ref, w_next_bf16_ref):
    r = jnp.maximum(acc, 0.0)
    o_ref[...] = (r * r).astype(o_ref.dtype)
    w_next_bf16_ref[...] = w_next_ref[...].astype(w_next_bf16_ref.dtype)


def _epi_residual(acc, x_ref, gate_ref, o_ref):
    o_ref[...] = x_ref[...] + gate_ref[...] * acc


def _epi_residual_norm(acc, x_ref, gate_ref, g_ref, sc_ref, sh_ref, o_ref, h_ref):
    x = x_ref[...] + gate_ref[...] * acc
    o_ref[...] = x
    ms = jnp.mean(x * x, axis=-1, keepdims=True)
    xn = x * lax.rsqrt(ms + EPS) * g_ref[...]
    h_ref[...] = (xn * (1.0 + sc_ref[...]) + sh_ref[...]).astype(h_ref.dtype)


def _epi_conv3_grid(acc, w_ref, o_ref):
    o_ref[...] = _conv3_rows(acc, w_ref[...], GRID_W).astype(o_ref.dtype)


def _mm_plain(a, w, n_cols, out_dtype, epi, tm, tn, name, w_layer=None):
    m = a.shape[0]
    return _mm([a], w, tm=tm, tn=tn, n_cols=n_cols, epi=epi, extras=(), extra_specs=(),
               out_shapes=jax.ShapeDtypeStruct((m, n_cols), out_dtype),
               out_specs=pl.BlockSpec((tm, tn), lambda j, i: (i, j)), name=name, w_layer=w_layer)


def _mm_residual(a_list, w, w_layer, x, gate, tm, tn, name, w_single_buffer=False):
    m, n = x.shape
    tiles_per_batch = SEQ // tm
    return _mm(a_list, w, tm=tm, tn=tn, n_cols=n, epi=_epi_residual,
               extras=(x, gate.reshape(BATCH, 1, n)),
               extra_specs=(pl.BlockSpec((tm, tn), lambda j, i: (i, j)),
                            pl.BlockSpec((None, 1, tn), lambda j, i: (i // tiles_per_batch, 0, j))),
               out_shapes=jax.ShapeDtypeStruct((m, n), F32),
               out_specs=pl.BlockSpec((tm, tn), lambda j, i: (i, j)), name=name,
               w_layer=w_layer, w_single_buffer=w_single_buffer)


def _mm_residual_norm(a_list, w, w_layer, x, gate, norm_g, sc, sh, tm, name):
    m, n = x.shape
    tiles_per_batch = SEQ // tm

    def per_batch(v):
        return v.reshape(BATCH, 1, n), pl.BlockSpec((None, 1, n), lambda j, i: (i // tiles_per_batch, 0, 0))

    (gate3, gate_spec), (sc3, sc_spec), (sh3, sh_spec) = per_batch(gate), per_batch(sc), per_batch(sh)
    rows = pl.BlockSpec((tm, n), lambda j, i: (i, 0))
    return _mm(a_list, w, tm=tm, tn=n, n_cols=n, epi=_epi_residual_norm,
               extras=(x, gate3, norm_g.reshape(1, n), sc3, sh3),
               extra_specs=(rows, gate_spec, pl.BlockSpec((1, n), lambda j, i: (0, 0)), sc_spec, sh_spec),
               out_shapes=(jax.ShapeDtypeStruct((m, n), F32), jax.ShapeDtypeStruct((m, n), BF16)),
               out_specs=(rows, rows), name=name, w_layer=w_layer, w_single_buffer=True)


def _row_groups(x):
    return [x[k:k + SUBLANES] for k in range(0, x.shape[0], SUBLANES)]


def _cumsum_groups(groups, sub):
    out, carry = [], None
    for g in groups:
        for s in (1, 2, 4):
            g = g + jnp.where(sub >= s, pltpu.roll(g, s, 0), 0.0)
        if carry is not None:
            g = g + carry
        out.append(g)
        carry = g[SUBLANES - 1:SUBLANES]
    return out


def _hold_boundary(cum, w, k, sub):
    shape = cum[k].shape
    if w >= SUBLANES // 2:
        r = (k * SUBLANES // (2 * w)) * 2 * w + w - 1
        return jnp.broadcast_to(cum[r // SUBLANES][r % SUBLANES:r % SUBLANES + 1], shape)
    assert w == 2
    return jnp.where(sub < 4, jnp.broadcast_to(cum[k][1:2], shape), jnp.broadcast_to(cum[k][5:6], shape))


def _scan_kernel(zf_ref, zb_ref, v_ref, q_ref, g_ref, u_ref, gb_ref, gc_ref, czf_ref, czb_ref, cv_ref,
                 lbl_ref, gn_ref, cw_ref, o_ref, oc_ref,
                 of_ref, ob_ref, st_ref, qs_ref, inc_ref, dec_ref, kl_ref, lv_ref, att_ref, *, layer):
    c = SCAN_CHUNK
    n_chunks = SEQ // c
    n_ctx = CTX_LEN // c
    z_refs = (zf_ref, zb_ref)
    cz_refs = (czf_ref, czb_ref)
    out_refs = (of_ref, ob_ref)

    logit_rows = [lbl_ref[:, k, :] for k in range(lbl_ref.shape[1])]
    top = functools.reduce(jnp.maximum, logit_rows)
    exps = [jnp.exp(r - top) for r in logit_rows]
    lbs = sum(exps[:layer + 1]) / sum(exps)

    sub = lax.broadcasted_iota(jnp.int32, (SUBLANES, A_HEAD_DIM), 0)
    n_groups = c // SUBLANES

    def rows(groups):
        return jnp.concatenate(groups, axis=0)

    def chunk(d, z, q, v, want_out):
        lb = lbs[d:d + 1]
        f = lb + (1.0 - lb) * jax.nn.sigmoid(z)
        kk = 1.0 - f
        fg, kg = _row_groups(f), _row_groups(kk)
        lg = [jnp.log2(g) for g in fg]
        cum = _cumsum_groups(lg, sub)
        total = cum[-1][SUBLANES - 1:SUBLANES]
        if d == 0:
            pos = cum
            q_in = [jnp.exp2(g) for g in cum]
            k_out = [jnp.exp2(total - g) for g in cum]
        else:
            pos = [a - b for a, b in zip(cum, lg)]
            q_in = [jnp.exp2(total - g) for g in pos]
            k_out = [jnp.exp2(g) for g in pos]
        k_leave = rows([a * b for a, b in zip(kg, k_out)]).astype(BF16)
        decay = jnp.exp2(total)
        if not want_out:
            return k_leave, decay, None, None, None
        qg = _row_groups(q)
        o_diag = jnp.sum(q * kk, axis=1, keepdims=True) * v
        upper, lower = (qg, kg) if d == 0 else (kg, qg)
        operands = []
        for w in SCAN_LEVELS:
            y = []
            for k in range(n_groups):
                if w == 1:
                    odd = (sub & 1) != 0
                    y.append(jnp.where(odd, qg[k] * fg[k], kg[k]) if d == 0
                             else jnp.where(odd, kg[k], qg[k] * fg[k]))
                    continue
                hold = _hold_boundary(cum, w, k, sub)
                if w >= SUBLANES:
                    bit_set = (k * SUBLANES) & w != 0
                    y.append(upper[k] * jnp.exp2(pos[k] - hold) if bit_set
                             else lower[k] * jnp.exp2(hold - pos[k]))
                else:
                    bit = (sub & w) != 0
                    sign = jnp.where(bit, 1.0, -1.0)
                    y.append(jnp.where(bit, upper[k], lower[k]) * jnp.exp2((pos[k] - hold) * sign))
            operands.append(rows(y).astype(BF16))
        qs = rows([a * b for a, b in zip(qg, q_in)]).astype(BF16)
        return k_leave, decay, qs, o_diag, operands

    def within_chunk(d, operands):
        t_idx = lax.broadcasted_iota(jnp.int32, (c, c), 0)
        s_idx = lax.broadcasted_iota(jnp.int32, (c, c), 1)
        split = t_idx ^ s_idx
        att = None
        for w, yb in reversed(list(zip(SCAN_LEVELS, operands))):
            scores = _dot_nt(yb, yb)
            att = scores if att is None else jnp.where(split < 2 * w, scores, att)
        return jnp.where(t_idx > s_idx if d == 0 else t_idx < s_idx, att, 0.0)

    st_ref[...] = jnp.zeros_like(st_ref)

    def ctx_body(i, carry):
        for d in range(2):
            ci = i if d == 0 else n_ctx - 1 - i
            rows_i = pl.ds(pl.multiple_of(ci * c, c), c)
            k_leave, decay, _, _, _ = chunk(d, cz_refs[d][rows_i, :], None, None, False)
            st_ref[d] = st_ref[d] * decay + _dot_tn(cv_ref[rows_i, :].astype(BF16), k_leave)
        return carry

    lax.fori_loop(0, n_ctx, ctx_body, 0, unroll=True)

    def chunk_rows(ci):
        start = ci * c
        return pl.ds(start if isinstance(ci, int) else pl.multiple_of(start, c), c)

    def chunk_of(d, t):
        return t if d == 0 else n_chunks - 1 - t

    def prepare(t):
        for d in range(2):
            ci = chunk_of(d, t)
            rows_i = chunk_rows(ci)
            k_leave, decay, qs, o_diag, operands = chunk(d, z_refs[d][rows_i, :], q_ref[rows_i, :],
                                                         v_ref[rows_i, :], True)
            kl_ref[d, rows_i, :] = k_leave
            dec_ref[d, ci] = decay
            qs_ref[d, rows_i, :] = qs
            out_refs[d][rows_i, :] = o_diag
            for lvl, yb in enumerate(operands):
                lv_ref[d, lvl, rows_i, :] = yb

    def scores(t):
        for d in range(2):
            ci = chunk_of(d, t)
            rows_i = chunk_rows(ci)
            inc_ref[d, ci] = _dot_tn(v_ref[rows_i, :].astype(BF16), kl_ref[d, rows_i, :])
            operands = [lv_ref[d, lvl, rows_i, :] for lvl in range(len(SCAN_LEVELS))]
            att_ref[d, rows_i, :] = within_chunk(d, operands).astype(BF16)

    def carry_state(t):
        for d in range(2):
            ci = chunk_of(d, t)
            rows_i = chunk_rows(ci)
            state = st_ref[d]
            out_refs[d][rows_i, :] += (_dot(att_ref[d, rows_i, :], v_ref[rows_i, :].astype(BF16))
                                       + _dot_nt(qs_ref[d, rows_i, :], state.astype(BF16)))
            st_ref[d] = state * dec_ref[d, ci] + inc_ref[d, ci]

    pair = 16
    n_trips = n_chunks // pair

    def stage(fn, trip):
        for j in range(pair):
            fn(trip * pair + j)

    stage(prepare, 0)
    stage(scores, 0)
    stage(prepare, 1)

    def skewed_body(i, carry):
        stage(carry_state, i - 2)
        stage(scores, i - 1)
        stage(prepare, i)
        return carry

    lax.fori_loop(2, n_trips, skewed_body, 0)
    stage(carry_state, n_trips - 2)
    stage(scores, n_trips - 1)
    stage(carry_state, n_trips - 1)

    rb = 256
    gn = gn_ref[...]
    cw = cw_ref[...]

    def read_body(i, carry):
        rows_i = pl.ds(pl.multiple_of(i * rb, rb), rb)
        o = of_ref[rows_i, :] + ob_ref[rows_i, :]
        ms = jnp.mean(o * o, axis=-1, keepdims=True)
        on = o * lax.rsqrt(ms + EPS) * gn
        g = g_ref[rows_i, :]
        o_ref[rows_i, :] = (on * (g * jax.nn.sigmoid(g))).astype(o_ref.dtype)
        conv = _conv3_rows(gc_ref[rows_i, :] * u_ref[rows_i, :], cw, GRID_W)
        oc_ref[rows_i, :] = (gb_ref[rows_i, :] * conv).astype(oc_ref.dtype)
        return carry

    lax.fori_loop(0, SEQ // rb, read_body, 0, unroll=True)


def _even_mixer(p, cp, lb_logits, gnorm_g, conv_w, layer):
    h = A_HEADS
    hd = A_HEAD_DIM
    n_lb = lb_logits.shape[1]

    def col(k):
        return pl.BlockSpec((SEQ, hd), lambda b, j, k=k: (b, k * h + j))

    def ccol(k):
        return pl.BlockSpec((CTX_LEN, hd), lambda b, j, k=k: (b, k * h + j))

    out_spec = pl.BlockSpec((SEQ, hd), lambda b, j: (b, j))
    return pl.pallas_call(
        functools.partial(_scan_kernel, layer=layer),
        grid=(BATCH, h),
        in_specs=[col(0), col(1), col(2), col(3), col(4), col(5), col(6), col(7),
                  ccol(0), ccol(1), ccol(2),
                  pl.BlockSpec((2, n_lb, hd), lambda b, j: (0, 0, j)),
                  pl.BlockSpec((1, hd), lambda b, j: (0, j)),
                  pl.BlockSpec((3, hd), lambda b, j: (0, j))],
        out_specs=(out_spec, out_spec),
        out_shape=(jax.ShapeDtypeStruct((TOKENS, A_WIDTH), BF16),
                   jax.ShapeDtypeStruct((TOKENS, B_WIDTH), BF16)),
        scratch_shapes=[pltpu.VMEM((SEQ, hd), F32), pltpu.VMEM((SEQ, hd), F32),
                        pltpu.VMEM((2, hd, hd), F32), pltpu.VMEM((2, SEQ, hd), BF16),
                        pltpu.VMEM((2, SEQ // SCAN_CHUNK, hd, hd), F32),
                        pltpu.VMEM((2, SEQ // SCAN_CHUNK, 1, hd), F32),
                        pltpu.VMEM((2, SEQ, hd), BF16),
                        pltpu.VMEM((2, len(SCAN_LEVELS), SEQ, hd), BF16),
                        pltpu.VMEM((2, SEQ, SCAN_CHUNK), BF16)],
        compiler_params=_params("arbitrary", "arbitrary"),
        name="hgrn_scan",
    )(p, p, p, p, p, p, p, p, cp, cp, cp, lb_logits, gnorm_g.reshape(1, A_WIDTH), conv_w)


def _split_bf16(x):
    hi = x.astype(BF16)
    return hi, (x - hi.astype(F32)).astype(BF16)


def _filter_kernel(z_ref, w1_ref, b1_ref, w2_ref, b2_ref, w3_ref, b3_ref, fr_ref,
                   w4f_ref, w4b_ref, cond_ref, aw_ref, ab_ref,
                   sum_ref, diff_ref, mod_ref, hid_ref, *, tc):
    o = pl.program_id(0)
    j = pl.program_id(1)

    cond = cond_ref[...]
    mod_ref[...] = _dot((cond * jax.nn.sigmoid(cond)).astype(BF16), aw_ref[...].astype(BF16)) + ab_ref[...]

    @pl.when((o == 0) & (j == 0))
    def _():
        fr = fr_ref[...]
        h = jnp.sin(fr * (_dot_f32(z_ref[...], w1_ref[...]) + b1_ref[...]))
        h = jnp.sin(fr * (_dot_f32(h, w2_ref[...]) + b2_ref[...]))
        h = jnp.sin(fr * (_dot_f32(h, w3_ref[...]) + b3_ref[...]))
        hi, lo = _split_bf16(h)
        hid_ref[...] = jnp.concatenate([hi, lo, hi], axis=1)

    def project(w_ref):
        w_hi, w_lo = _split_bf16(w_ref[...])
        zero = jnp.zeros_like(w_hi)
        first = _dot(hid_ref[...], jnp.concatenate([w_hi, zero, w_hi, zero, w_lo, zero], axis=0))
        second = _dot(hid_ref[...], jnp.concatenate([zero, w_hi, zero, w_hi, zero, w_lo], axis=0))
        return jnp.concatenate([first, second], axis=0)

    rows = lax.broadcasted_iota(jnp.int32, (SEQ, tc), 0)
    chan = lax.broadcasted_iota(jnp.int32, (SEQ, tc), 1) + j * tc
    lag = jnp.where(rows < HALF, 2 * rows, 2 * rows - (SEQ - 1))
    t = lag.astype(F32) * (1.0 / (SEQ - 1))
    max_decay = math.log(HY_DECAY_TARGET) / HY_FAST_PCT
    min_decay = math.log(HY_DECAY_TARGET) / HY_SLOW_PCT
    deltas = jnp.abs(min_decay + chan.astype(F32) * ((max_decay - min_decay) / (D_MODEL - 1)))
    window = jnp.exp(-t * deltas)
    fw = project(w4f_ref) * window
    bw = project(w4b_ref) * window
    first = rows == 0
    a = fw + jnp.where(first, bw, 0.0)
    bb = jnp.where(first, 0.0, bw)
    inv = 1.0 / jnp.sum(jnp.abs(a) + jnp.abs(bb), axis=0, keepdims=True)
    sum_ref[...] = ((a + bb) * inv).astype(sum_ref.dtype)
    diff_ref[...] = ((a - bb) * inv).astype(diff_ref.dtype)


@functools.lru_cache(maxsize=None)
def _filter_positions():
    l = SEQ
    t = np.linspace(0.0, 1.0, l, dtype=np.float32)
    w = (2.0 * math.pi * np.arange(l, dtype=np.float32) / l).astype(np.float32)
    bands = np.linspace(1e-4, HY_BANDS - 1, HY_BANDS, dtype=np.float32)
    ang = w[:, None] * bands[None, :]
    z = np.concatenate([t[:, None], np.cos(ang), -np.sin(ang)], axis=-1).astype(np.float32)
    z = np.concatenate([z[0::2], z[1::2]], axis=0)
    z = np.pad(z, ((0, 0), (0, LANES // 2 - HY_EMB)))
    return np.concatenate([z[:l // 2], z[l // 2:]], axis=1)


def _filter_taps_and_modulation(fw1, fb1, fw2, fb2, fw3, fb3, fw4, freq, cond, ada_w, ada_b):
    l = SEQ
    tc = 256
    nj = D_MODEL // tc
    depth, d, n_mod = ada_w.shape
    assert depth == 2 and n_mod % (nj * LANES) == 0
    ta = n_mod // nj
    hw = LANES // 2

    def pad(a, rows, cols):
        return jnp.pad(a, ((0, rows - a.shape[0]), (0, cols - a.shape[1])))

    def twice(a):
        a = pad(a, hw, hw)
        zero = jnp.zeros_like(a)
        return jnp.concatenate([jnp.concatenate([a, zero], axis=1), jnp.concatenate([zero, a], axis=1)], axis=0)

    def row2(v):
        v = pad(v[None, :], 1, hw)
        return jnp.concatenate([v, v], axis=1)

    def small(shape):
        return pl.BlockSpec(shape, lambda o, j: (0, 0))

    out_sds = jax.ShapeDtypeStruct((l, 2 * D_MODEL), BF16)
    out_spec = pl.BlockSpec((l, tc), lambda o, j: (0, o * nj + j))
    assert fw4.shape[0] == hw
    side_f = pl.BlockSpec((hw, tc), lambda o, j: (0, 2 * o * nj + j))
    side_b = pl.BlockSpec((hw, tc), lambda o, j: (0, (2 * o + 1) * nj + j))
    sq, vec = small((LANES, LANES)), small((1, LANES))
    return pl.pallas_call(
        functools.partial(_filter_kernel, tc=tc),
        grid=(2, nj),
        in_specs=[small((l // 2, LANES)), sq, vec, sq, vec, sq, vec, vec, side_f, side_b,
                  small((8, d)),
                  pl.BlockSpec((None, d, ta), lambda o, j: (o, 0, j)),
                  pl.BlockSpec((None, 1, ta), lambda o, j: (o, 0, j))],
        out_specs=(out_spec, out_spec, pl.BlockSpec((None, 8, ta), lambda o, j: (o, 0, j))),
        out_shape=(out_sds, out_sds, jax.ShapeDtypeStruct((depth, 8, n_mod), F32)),
        scratch_shapes=[pltpu.VMEM((l // 2, 3 * LANES), BF16)],
        compiler_params=_params("arbitrary", "arbitrary"),
        name="hyena_filter",
    )(jnp.asarray(_filter_positions()), twice(fw1), row2(fb1), twice(fw2), row2(fb2), twice(fw3), row2(fb3),
      row2(freq), fw4, fw4, cond, ada_w, ada_b.reshape(depth, 1, n_mod))


def _trig_table(a_rows, b_cols, scale):
    n = (a_rows[:, None].astype(np.int64) * b_cols[None, :].astype(np.int64)) % (4 * SEQ)
    ang = n.astype(np.float64) * (math.pi / (2 * SEQ))
    return (np.cos(ang) * scale).astype(np.float32), (np.sin(ang) * scale).astype(np.float32)


@functools.lru_cache(maxsize=None)
def _dft_tables():
    idx = np.arange(QUARTER)
    ana_c, ana_s, syn_c, syn_s = [], [], [], []
    for r in range(4):
        c, s = _trig_table(2 * idx + 1, 4 * idx + r, 1.0)
        ana_c.append(c)
        ana_s.append(s)
        c, s = _trig_table(4 * idx + r, 2 * idx + 1, 1.0 / SEQ)
        syn_c.append(c)
        syn_s.append(s)
    fmap = np.concatenate([idx, HALF + idx])
    lag = np.arange(HALF)
    cee, see = _trig_table(2 * fmap + 1, 2 * lag, 1.0)
    ceo, seo = _trig_table(2 * fmap + 1, 2 * lag + 1, 1.0)
    return np.stack(ana_c + ana_s), np.stack(syn_c + syn_s), (cee, ceo, see, seo)


def _filter_dft_kernel(cee_ref, ceo_ref, see_ref, seo_ref, fs_ref, fd_ref, x_ref, g_ref, sc_ref, sh_ref,
                       p_ref, pm_ref, q_ref, qm_ref, h_ref):
    _norm_mod_rows(x_ref, g_ref, sc_ref, sh_ref, h_ref)
    pe = _dot(cee_ref[...], fs_ref[0:HALF, :])
    po = _dot(ceo_ref[...], fs_ref[HALF:SEQ, :])
    qe = _dot(see_ref[...], fd_ref[0:HALF, :])
    qo = _dot(seo_ref[...], fd_ref[HALF:SEQ, :])
    p_ref[...] = pe + po
    pm_ref[...] = pe - po
    q_ref[...] = qe + qo
    qm_ref[...] = qo - qe


def _filter_dft_and_first_norm(tables, taps_sum, taps_diff, x, g, sc, sh):
    tm, tn = HALF, 256
    n_cols = taps_sum.shape[1]
    b, l, d = x.shape
    ts = b * l // (n_cols // tn)
    per_batch = l // ts
    assert per_batch * ts == l
    x_spec = pl.BlockSpec((None, ts, d), lambda j, i: (j // per_batch, j % per_batch, 0))
    mod_spec = pl.BlockSpec((None, 1, d), lambda j, i: (j // per_batch, 0, 0))
    mat = pl.BlockSpec((tm, HALF), lambda j, i: (i, 0), pipeline_mode=pl.Buffered(1))
    tap = pl.BlockSpec((SEQ, tn), lambda j, i: (0, j))
    out_spec = pl.BlockSpec((tm, tn), lambda j, i: (i, j))
    out_sds = jax.ShapeDtypeStruct((HALF, n_cols), F32)
    return pl.pallas_call(
        _filter_dft_kernel,
        grid=(n_cols // tn, HALF // tm),
        in_specs=[mat, mat, mat, mat, tap, tap, x_spec, pl.BlockSpec((1, d), lambda j, i: (0, 0)),
                  mod_spec, mod_spec],
        out_specs=(out_spec,) * 4 + (x_spec,),
        out_shape=(out_sds,) * 4 + (jax.ShapeDtypeStruct((b, l, d), BF16),),
        compiler_params=_params("arbitrary", "arbitrary"),
        name="filter_dft",
    )(*tables, taps_sum, taps_diff, x, g.reshape(1, d), sc.reshape(b, 1, d), sh.reshape(b, 1, d))


def _dft4_fwd_kernel(tab_ref, z_ref, p_ref, pm_ref, q_ref, qm_ref, x_ref, zs_ref, *, slab):
    panels = zs_ref.shape[0]
    for k in range(panels):
        zs_ref[k] = z_ref[:, k * LANES:(k + 1) * LANES].astype(F32)
    zr = [jnp.concatenate([zs_ref[k, pl.ds(r, QUARTER, stride=4), :] for k in range(panels)],
                          axis=1).astype(BF16) for r in range(4)]

    for f0 in range(0, QUARTER, slab):
        rows = slice(f0, f0 + slab)
        mirror = slice(QUARTER + f0, QUARTER + f0 + slab)
        c = [_dot(tab_ref[r, rows, :], zr[r]) for r in range(4)]
        s = [_dot(tab_ref[4 + r, rows, :], zr[r]) for r in range(4)]
        e, o, em, om = c[0] + c[2], c[1] + c[3], c[0] - c[2], c[1] - c[3]
        se, so, sem, som = s[0] + s[2], s[1] + s[3], s[0] - s[2], s[1] - s[3]
        a = (e + o, em + som, em - som, e - o)
        b = (se + so, om - sem, sem + om, so - se)
        pk = (p_ref[rows, :], pm_ref[mirror, :], p_ref[mirror, :], pm_ref[rows, :])
        qk = (q_ref[rows, :], qm_ref[mirror, :], q_ref[mirror, :], qm_ref[rows, :])
        u = [a[k] * pk[k] - b[k] * qk[k] for k in range(4)]
        v = [a[k] * qk[k] + b[k] * pk[k] for k in range(4)]
        up, um, wp, wm = u[0] + u[3], u[0] - u[3], u[1] + u[2], u[1] - u[2]
        vp, vm, yp, ym = v[0] + v[3], v[0] - v[3], v[1] + v[2], v[1] - v[2]
        outs = (up + wp, um + yp, up - wp, um - yp,
                vm - ym, vp + wm, vm + ym, vp - wm)
        for n, val in enumerate(outs):
            x_ref[n, rows, :] = val.astype(x_ref.dtype)


def _dft4_fwd(tab, z_arr, z_col0, spectra, order):
    tn, slab = 512, 256
    nj = D_MODEL // tn
    zc = z_col0 // tn
    spec = pl.BlockSpec((HALF, tn), lambda j, b: (0, order * nj + j))
    return pl.pallas_call(
        functools.partial(_dft4_fwd_kernel, slab=slab),
        grid=(nj, BATCH),
        in_specs=[pl.BlockSpec(tab.shape, lambda j, b: (0, 0, 0), pipeline_mode=pl.Buffered(1)),
                  pl.BlockSpec((SEQ, tn), lambda j, b: (b, zc + j)), spec, spec, spec, spec],
        out_specs=pl.BlockSpec((8, QUARTER, tn), lambda j, b: (0, b, j)),
        out_shape=jax.ShapeDtypeStruct((8, BATCH * QUARTER, D_MODEL), BF16),
        scratch_shapes=[pltpu.VMEM((tn // LANES, SEQ, LANES), F32)],
        compiler_params=_params("arbitrary", "arbitrary"),
        name="dft_analysis",
    )(tab, z_arr, *spectra)


def _dft4_inv_kernel(tab_ref, x_ref, z_ref, gate_ref, skip_ref, o_ref, ys_ref):
    panels = ys_ref.shape[0]
    for r in range(4):
        y_r = _dot(tab_ref[r], x_ref[r]) + _dot(tab_ref[4 + r], x_ref[4 + r])
        for k in range(panels):
            ys_ref[k, pl.ds(r, QUARTER, stride=4), :] = y_r[:, k * LANES:(k + 1) * LANES]
    y = jnp.concatenate([ys_ref[k] for k in range(panels)], axis=1)
    y = y + z_ref[...].astype(F32) * skip_ref[...]
    o_ref[...] = (gate_ref[...].astype(F32) * y).astype(o_ref.dtype)


def _dft4_inv(tab, x, z_arr, z_col0, gate_arr, gate_col0, skip):
    tn = 512
    nj = D_MODEL // tn
    zc, gc = z_col0 // tn, gate_col0 // tn
    return pl.pallas_call(
        _dft4_inv_kernel,
        grid=(nj, BATCH),
        in_specs=[pl.BlockSpec(tab.shape, lambda j, b: (0, 0, 0), pipeline_mode=pl.Buffered(1)),
                  pl.BlockSpec((8, QUARTER, tn), lambda j, b: (0, b, j)),
                  pl.BlockSpec((SEQ, tn), lambda j, b: (b, zc + j)),
                  pl.BlockSpec((SEQ, tn), lambda j, b: (b, gc + j)),
                  pl.BlockSpec((1, tn), lambda j, b: (0, j))],
        out_specs=pl.BlockSpec((SEQ, tn), lambda j, b: (b, j)),
        out_shape=jax.ShapeDtypeStruct((TOKENS, D_MODEL), BF16),
        scratch_shapes=[pltpu.VMEM((tn // LANES, SEQ, LANES), F32)],
        compiler_params=_params("arbitrary", "arbitrary"),
        name="dft_synthesis",
    )(tab, x, z_arr, gate_arr, skip.reshape(1, D_MODEL))


def _mod_chunk(mod, l, k):
    return mod[l, :BATCH, k * D_MODEL:(k + 1) * D_MODEL]


def _mlp_block(x2d, h, mod, l, w1, w2):
    tm = tn = 1024
    n_i, n_j = TOKENS // tm, D_FF // tn
    slab = D_FF // (n_i * n_j)
    hid, w2_bf16 = _mm(
        [h], w1, w_layer=l, tm=tm, tn=tn, n_cols=D_FF, epi=_epi_relu2_and_cast,
        extras=(w2,), extra_specs=(pl.BlockSpec((None, slab, D_MODEL), lambda j, i: (l, j * n_i + i, 0)),),
        out_shapes=(jax.ShapeDtypeStruct((TOKENS, D_FF), BF16), jax.ShapeDtypeStruct((D_FF, D_MODEL), BF16)),
        out_specs=(pl.BlockSpec((tm, tn), lambda j, i: (i, j)),
                   pl.BlockSpec((slab, D_MODEL), lambda j, i: (j * n_i + i, 0))),
        name="mlp_up")
    return _mm_residual([hid], w2_bf16, None, x2d, _mod_chunk(mod, l, 5), 512, 512, "mlp_down")


def kernel(x, c, ctx, c_ctx, ada_w, ada_b, norm_g, lb_logits, ab_w_in, ab_conv_w, ab_gnorm_g, ab_w_out,
           hy_in_w, hy_short_w, hy_out_w, hy_fw1, hy_fb1, hy_fw2, hy_fb2, hy_fw3, hy_fb3, hy_fw4, hy_freq,
           hy_skip, mlp_w1, mlp_w2, final_g):
    d = D_MODEL
    cond = jnp.concatenate([c, c_ctx[None, :], jnp.zeros((3, d), F32)], axis=0)
    taps_sum, taps_diff, mod = _filter_taps_and_modulation(
        hy_fw1[0], hy_fb1[0], hy_fw2[0], hy_fb2[0], hy_fw3[0], hy_fb3[0], hy_fw4[0], hy_freq[0],
        cond, ada_w, ada_b)
    x2d = x.reshape(TOKENS, d)

    sh1, sc1, g1 = (mod[0, :BATCH, k * d:(k + 1) * d] for k in (0, 1, 2))
    analysis, synthesis, filter_tables = jax.tree.map(lambda t: jnp.asarray(t).astype(BF16), _dft_tables())
    *spectra, h = _filter_dft_and_first_norm(filter_tables, taps_sum, taps_diff, x, norm_g[0, 0], sc1, sh1)
    h = h.reshape(TOKENS, d)
    csh = jnp.broadcast_to(mod[0, BATCH:BATCH + 1, 0:d], (BATCH, d))
    csc = jnp.broadcast_to(mod[0, BATCH:BATCH + 1, d:2 * d], (BATCH, d))
    hc = _prep(ctx, norm_g[0, 0], csc, csh, BF16, CTX_LEN).reshape(BATCH * CTX_LEN, d)
    p = _mm_plain(h, ab_w_in, AB_IN_WIDTH, F32, _epi_store, 1024, 1024, "ab_in_proj", w_layer=0)
    cp = _mm_plain(hc, ab_w_in, 3 * A_WIDTH, F32, _epi_store, 1024, 1024, "ab_ctx_proj", w_layer=0)
    mix_a, mix_b = _even_mixer(p, cp, lb_logits, ab_gnorm_g[0], ab_conv_w[0], 0)
    x2d, h = _mm_residual_norm([mix_a, mix_b], ab_w_out, 0, x2d, g1, norm_g[0, 1], _mod_chunk(mod, 0, 4),
                               _mod_chunk(mod, 0, 3), 512, "ab_out_proj")
    x2d = _mlp_block(x2d, h, mod, 0, mlp_w1, mlp_w2)

    sh1, sc1, g1 = (mod[1, :BATCH, k * d:(k + 1) * d] for k in (0, 1, 2))
    h = _prep(x2d.reshape(BATCH, SEQ, d), norm_g[1, 0], sc1, sh1, BF16, 1024).reshape(TOKENS, d)
    pc = _mm([h], hy_in_w, w_layer=0, tm=1024, tn=1024, n_cols=3 * d, epi=_epi_conv3_grid,
             extras=(hy_short_w[0],), extra_specs=(pl.BlockSpec((3, 1024), lambda j, i: (0, j)),),
             out_shapes=jax.ShapeDtypeStruct((TOKENS, 3 * d), BF16),
             out_specs=pl.BlockSpec((1024, 1024), lambda j, i: (i, j)), name="hy_in_proj")
    coef = _dft4_fwd(analysis, pc, 2 * d, spectra, 0)
    z = _dft4_inv(synthesis, coef, pc, 2 * d, pc, 0, hy_skip[0, 0])
    coef = _dft4_fwd(analysis, z, 0, spectra, 1)
    z = _dft4_inv(synthesis, coef, z, 0, pc, d, hy_skip[0, 1])
    x2d, h = _mm_residual_norm([z], hy_out_w, 0, x2d, g1, norm_g[1, 1], _mod_chunk(mod, 1, 4),
                               _mod_chunk(mod, 1, 3), 512, "hy_out_proj")
    x2d = _mlp_block(x2d, h, mod, 1, mlp_w1, mlp_w2)

    zeros = jnp.zeros((BATCH, d), F32)
    return _prep(x2d.reshape(BATCH, SEQ, d), final_g, zeros, zeros, F32, 1024)
```
